```python
import jax, jax.numpy as jnp
from jax import lax
import numpy as np

D_MODEL = 1024
BATCH = 8
SEQ = 2048
DEPTH = 1

N_META = 16
HEAD_DIM = 64
N_Q_HEADS = 8
N_KV_HEADS = 2
GROUP = N_Q_HEADS // N_KV_HEADS
ATTN_W = N_Q_HEADS * HEAD_DIM
KV_W = N_KV_HEADS * HEAD_DIM
CONV_W = D_MODEL - ATTN_W
IN_W = ATTN_W + 2 * KV_W + 2 * CONV_W
WINDOW = 128
BLOCK = 128
CONV_K = 31
CONV_PAD = CONV_K // 2
N_EXPERTS = 32
TOP_K = 4
D_FF = D_MODEL
SWIGLU_LIMIT = 7.0
SWIGLU_ALPHA = 1.702
RMS_EPS = 1e-6
LN_EPS = 1e-5
NEG_INF = -1e30

kernel_name = 'hymba_conformer_swa_moe_encoder_layer'


def rms_norm(x, g):
    xf = x.astype(jnp.float32)
    y = xf * lax.rsqrt(jnp.mean(xf * xf, axis=-1, keepdims=True) + RMS_EPS)
    return (y * g.astype(jnp.float32)).astype(x.dtype)


def layer_norm(x, g, b):
    xf = x.astype(jnp.float32)
    mu = jnp.mean(xf, axis=-1, keepdims=True)
    var = jnp.mean(jnp.square(xf - mu), axis=-1, keepdims=True)
    y = (xf - mu) * lax.rsqrt(var + LN_EPS)
    return (y * g.astype(jnp.float32) + b.astype(jnp.float32)).astype(x.dtype)


def alibi_slopes():
    return jnp.exp2(-8.0 * (jnp.arange(N_Q_HEADS, dtype=jnp.float32) + 1.0) / N_Q_HEADS)


def windowed_attention(q, k, v, sink):
    B, L = q.shape[0], q.shape[1]
    S = L - N_META
    nb = S // BLOCK
    scale = HEAD_DIM ** -0.5
    q5 = q.reshape(B, L, N_KV_HEADS, GROUP, HEAD_DIM)
    qm, qr = q5[:, :N_META], q5[:, N_META:]
    km, kr = k[:, :N_META], k[:, N_META:]
    vm, vr = v[:, :N_META], v[:, N_META:]
    sink_f = sink.astype(jnp.float32).reshape(N_KV_HEADS, GROUP)

    qb = qr.reshape(B, nb, BLOCK, N_KV_HEADS, GROUP, HEAD_DIM)
    pad = ((0, 0), (BLOCK, BLOCK), (0, 0), (0, 0))
    kb = jnp.pad(kr, pad).reshape(B, nb + 2, BLOCK, N_KV_HEADS, HEAD_DIM)
    vb = jnp.pad(vr, pad).reshape(B, nb + 2, BLOCK, N_KV_HEADS, HEAD_DIM)
    kwin = jnp.concatenate([kb[:, :-2], kb[:, 1:-1], kb[:, 2:]], axis=2)
    vwin = jnp.concatenate([vb[:, :-2], vb[:, 1:-1], vb[:, 2:]], axis=2)
    s_win = jnp.einsum('bnqhgd,bnkhd->bhgnqk', qb, kwin).astype(jnp.float32) * scale
    s_meta = jnp.einsum('bnqhgd,bmhd->bhgnqm', qb, km).astype(jnp.float32) * scale

    qpos = jnp.arange(nb)[:, None] * BLOCK + jnp.arange(BLOCK)[None, :]
    kpos = (jnp.arange(nb)[:, None] - 1) * BLOCK + jnp.arange(3 * BLOCK)[None, :]
    dist = jnp.abs(qpos[:, :, None] - kpos[:, None, :])
    valid = (kpos >= 0)[:, None, :] & (kpos < S)[:, None, :] & (dist <= WINDOW)
    slopes = alibi_slopes().reshape(N_KV_HEADS, GROUP)
    alibi = slopes[:, :, None, None, None] * dist.astype(jnp.float32)[None, None]
    s_win = jnp.where(valid, s_win - alibi, NEG_INF)
    sink_col = jnp.broadcast_to(sink_f[None, :, :, None, None, None], s_win.shape[:-1] + (1,))
    p = jax.nn.softmax(jnp.concatenate([s_meta, s_win, sink_col], axis=-1), axis=-1)
    p_meta = p[..., :N_META].astype(v.dtype)
    p_win = p[..., N_META:N_META + 3 * BLOCK].astype(v.dtype)
    o_real = (jnp.einsum('bhgnqm,bmhd->bnqhgd', p_meta, vm)
              + jnp.einsum('bhgnqk,bnkhd->bnqhgd', p_win, vwin))
    o_real = o_real.reshape(B, S, ATTN_W)

    s_g = jnp.einsum('bmhgd,blhd->bhgml', qm, k).astype(jnp.float32) * scale
    sink_g = jnp.broadcast_to(sink_f[None, :, :, None, None], s_g.shape[:-1] + (1,))
    p_g = jax.nn.softmax(jnp.concatenate([s_g, sink_g], axis=-1), axis=-1)[..., :L].astype(v.dtype)
    o_meta = jnp.einsum('bhgml,blhd->bmhgd', p_g, v).reshape(B, N_META, ATTN_W)
    return jnp.concatenate([o_meta, o_real], axis=1)


def conv_group(u, dw_w, dw_b, ln_g, ln_b):
    a, g = u[..., :CONV_W], u[..., CONV_W:]
    hc = a * jax.nn.sigmoid(g)
    hc = lax.conv_general_dilated(hc, dw_w, window_strides=(1,), padding=[(CONV_PAD, CONV_PAD)],
                                  dimension_numbers=('NWC', 'WIO', 'NWC'),
                                  feature_group_count=CONV_W) + dw_b
    hc = layer_norm(hc, ln_g, ln_b)
    return jax.nn.silu(hc)


def moe(hn, router_w, router_b, w_gate_up, b_gate_up, w_down, b_down):
    B, L, D = hn.shape
    t = hn.reshape(B * L, D)
    n_tok = t.shape[0]
    logits = t.astype(jnp.float32) @ router_w.astype(jnp.float32) + router_b.astype(jnp.float32)
    top_val, top_idx = lax.top_k(logits, TOP_K)
    gates = jax.nn.softmax(top_val, axis=-1)
    flat_e = top_idx.reshape(-1)
    order = jnp.argsort(flat_e)
    tok = order // TOP_K
    e_sorted = flat_e[order]
    group_sizes = jnp.bincount(flat_e, length=N_EXPERTS).astype(jnp.int32)
    xs = t[tok]
    gu = lax.ragged_dot(xs, w_gate_up, group_sizes) + b_gate_up[e_sorted]
    gate = jnp.minimum(gu[:, :D_FF], SWIGLU_LIMIT)
    up = jnp.clip(gu[:, D_FF:], -SWIGLU_LIMIT, SWIGLU_LIMIT)
    act = gate * jax.nn.sigmoid(SWIGLU_ALPHA * gate) * (up + 1.0)
    y = lax.ragged_dot(act, w_down, group_sizes) + b_down[e_sorted]
    y = y * gates.reshape(-1)[order][:, None].astype(y.dtype)
    out = jax.ops.segment_sum(y, tok, num_segments=n_tok)
    return out.reshape(B, L, D)


def setup_inputs(seed: int = 0) -> dict:
    key = jax.random.key(seed)
    ks = jax.random.split(key, 20)
    f = jnp.float32
    nrm = lambda k, s, sc: jax.random.normal(k, s, f) * sc
    return {
        'x': nrm(ks[0], (BATCH, SEQ, D_MODEL), 1.0),
        'meta_tokens': nrm(ks[1], (N_META, D_MODEL), 1.0),
        'norm1_g': 1.0 + nrm(ks[2], (DEPTH, D_MODEL), 0.02),
        'w_in': nrm(ks[3], (DEPTH, D_MODEL, IN_W), D_MODEL ** -0.5),
        'q_norm_g': 1.0 + nrm(ks[4], (DEPTH, HEAD_DIM), 0.02),
        'k_norm_g': 1.0 + nrm(ks[5], (DEPTH, HEAD_DIM), 0.02),
        'attn_sink': nrm(ks[6], (DEPTH, N_Q_HEADS), 0.5),
        'dw_w': nrm(ks[7], (DEPTH, CONV_K, 1, CONV_W), CONV_K ** -0.5),
        'dw_b': nrm(ks[8], (DEPTH, CONV_W), 0.02),
        'conv_ln_g': 1.0 + nrm(ks[9], (DEPTH, CONV_W), 0.02),
        'conv_ln_b': nrm(ks[10], (DEPTH, CONV_W), 0.02),
        'w_out': nrm(ks[11], (DEPTH, D_MODEL, D_MODEL), D_MODEL ** -0.5),
        'norm2_g': 1.0 + nrm(ks[12], (DEPTH, D_MODEL), 0.02),
        'router_w': nrm(ks[13], (DEPTH, D_MODEL, N_EXPERTS), D_MODEL ** -0.5),
        'router_b': nrm(ks[14], (DEPTH, N_EXPERTS), 0.01),
        'w_gate_up': nrm(ks[15], (DEPTH, N_EXPERTS, D_MODEL, 2 * D_FF), D_MODEL ** -0.5),
        'b_gate_up': nrm(ks[16], (DEPTH, N_EXPERTS, 2 * D_FF), 0.02),
        'w_down': nrm(ks[17], (DEPTH, N_EXPERTS, D_FF, D_MODEL), D_FF ** -0.5),
        'b_down': nrm(ks[18], (DEPTH, N_EXPERTS, D_MODEL), 0.02),
    }


def reference(x, meta_tokens, norm1_g, w_in, q_norm_g, k_norm_g, attn_sink, dw_w, dw_b,
              conv_ln_g, conv_ln_b, w_out, norm2_g, router_w, router_b, w_gate_up, b_gate_up,
              w_down, b_down):
    B = x.shape[0]
    meta = jnp.broadcast_to(meta_tokens[None].astype(x.dtype), (B, N_META, D_MODEL))
    h = jnp.concatenate([meta, x], axis=1)
    L = h.shape[1]
    for l in range(DEPTH):
        n = rms_norm(h, norm1_g[l])
        proj = n @ w_in[l]
        q = proj[..., :ATTN_W].reshape(B, L, N_Q_HEADS, HEAD_DIM)
        k = proj[..., ATTN_W:ATTN_W + KV_W].reshape(B, L, N_KV_HEADS, HEAD_DIM)
        v = proj[..., ATTN_W + KV_W:ATTN_W + 2 * KV_W].reshape(B, L, N_KV_HEADS, HEAD_DIM)
        u = proj[..., ATTN_W + 2 * KV_W:]
        q = rms_norm(q, q_norm_g[l])
        k = rms_norm(k, k_norm_g[l])
        attn = windowed_attention(q, k, v, attn_sink[l])
        conv = conv_group(u, dw_w[l], dw_b[l], conv_ln_g[l], conv_ln_b[l])
        h = h + jnp.concatenate([attn, conv], axis=-1) @ w_out[l]
        h = h + moe(rms_norm(h, norm2_g[l]), router_w[l], router_b[l], w_gate_up[l],
                    b_gate_up[l], w_down[l], b_down[l])
    return h[:, N_META:]
```

```python
import functools

import numpy as np
import jax
import jax.numpy as jnp
from jax import lax
from jax.experimental import pallas as pl
from jax.experimental.pallas import tpu as pltpu

F32 = jnp.float32
BF16 = jnp.bfloat16

D_MODEL = 1024
N_META = 16
HEAD_DIM = 64
N_Q_HEADS = 8
N_KV_HEADS = 2
GROUP = N_Q_HEADS // N_KV_HEADS
ATTN_W = N_Q_HEADS * HEAD_DIM
KV_W = N_KV_HEADS * HEAD_DIM
CONV_W = D_MODEL - ATTN_W
IN_W = ATTN_W + 2 * KV_W + 2 * CONV_W
WINDOW = 128
BLOCK = 128
CONV_K = 31
CONV_PAD = CONV_K // 2
N_EXPERTS = 32
TOP_K = 4
D_FF = D_MODEL
SWIGLU_LIMIT = 7.0
SWIGLU_ALPHA = 1.702
RMS_EPS = 1e-6
LN_EPS = 1e-5
NEG_INF = -1e30

LANES = 128
ROW_TILE = 512
EXPERT_TILE = 512
DMA_TOKENS = 256
CONV_ROWS = 32
FF_CHUNK = 256
VMEM_LIMIT = 56 * 1024 * 1024


def _rms(x, eps):
    return x * lax.rsqrt(jnp.mean(x * x, axis=-1, keepdims=True) + eps)


def _inproj_kernel(x_ref, g1_ref, w_ref, qg_ref, kg_ref, pq_ref, pk_ref,
                   q_ref, k_ref, v_ref, hc_ref):
    x = x_ref[...]
    n = (_rms(x, RMS_EPS) * g1_ref[...]).astype(BF16)
    proj = jnp.dot(n, w_ref[...], preferred_element_type=F32)
    q = proj[:, :ATTN_W]
    k = proj[:, ATTN_W:ATTN_W + KV_W]
    v = proj[:, ATTN_W + KV_W:ATTN_W + 2 * KV_W]
    a = proj[:, ATTN_W + 2 * KV_W:ATTN_W + 2 * KV_W + CONV_W]
    g = proj[:, ATTN_W + 2 * KV_W + CONV_W:]
    qms = jnp.dot((q * q).astype(BF16), pq_ref[...], preferred_element_type=F32)
    kms = jnp.dot((k * k).astype(BF16), pk_ref[...], preferred_element_type=F32)
    q_ref[...] = (q * lax.rsqrt(qms + RMS_EPS) * qg_ref[...]).astype(BF16)
    kn = k * lax.rsqrt(kms + RMS_EPS) * kg_ref[...]
    lo = lax.broadcasted_iota(jnp.int32, kn.shape, 1) < HEAD_DIM
    ksw = pltpu.roll(kn, HEAD_DIM, 1)
    k_ref[...] = jnp.concatenate([jnp.where(lo, kn, ksw), jnp.where(lo, ksw, kn)], axis=1).astype(BF16)
    vsw = pltpu.roll(v, HEAD_DIM, 1)
    v_ref[...] = jnp.concatenate([jnp.where(lo, v, vsw), jnp.where(lo, vsw, v)], axis=1).astype(BF16)
    hc_ref[...] = a * jax.nn.sigmoid(g)


def _inproj(x2d, g1, w_in_b, qg, kg, pq, pk, tile):
    n = x2d.shape[0]
    const = lambda i: (0, 0)
    row = lambda i: (i, 0)
    return pl.pallas_call(
        _inproj_kernel,
        grid=(n // tile,),
        in_specs=[
            pl.BlockSpec((tile, D_MODEL), row),
            pl.BlockSpec((1, D_MODEL), const),
            pl.BlockSpec((D_MODEL, IN_W), const),
            pl.BlockSpec((1, ATTN_W), const),
            pl.BlockSpec((1, KV_W), const),
            pl.BlockSpec((ATTN_W, ATTN_W), const),
            pl.BlockSpec((KV_W, KV_W), const),
        ],
        out_specs=[
            pl.BlockSpec((tile, ATTN_W), row),
            pl.BlockSpec((tile, 2 * KV_W), row),
            pl.BlockSpec((tile, 2 * KV_W), row),
            pl.BlockSpec((tile, CONV_W), row),
        ],
        out_shape=[
            jax.ShapeDtypeStruct((n, ATTN_W), BF16),
            jax.ShapeDtypeStruct((n, 2 * KV_W), BF16),
            jax.ShapeDtypeStruct((n, 2 * KV_W), BF16),
            jax.ShapeDtypeStruct((n, CONV_W), F32),
        ],
        compiler_params=pltpu.CompilerParams(dimension_semantics=("arbitrary",),
                                             vmem_limit_bytes=VMEM_LIMIT),
        name="inproj",
    )(x2d, g1, w_in_b, qg, kg, pq, pk)


def _alibi_bias():
    qi = np.arange(BLOCK)[:, None]
    kj = np.arange(BLOCK)[None, :]
    dists = [qi + BLOCK - kj, np.abs(qi - kj), kj + BLOCK - qi]
    out = np.zeros((N_KV_HEADS, 3, GROUP * BLOCK, BLOCK), np.float32)
    for h in range(N_KV_HEADS):
        for g in range(GROUP):
            slope = 2.0 ** (-8.0 * (h * GROUP + g + 1) / N_Q_HEADS)
            for p, d in enumerate(dists):
                out[h, p, g * BLOCK:(g + 1) * BLOCK] = np.where(d <= WINDOW, -slope * d, NEG_INF)
    return out


def _attn_kernel(sink_ref, q_ref, kp_ref, kc_ref, kn_ref, vp_ref, vc_ref, vn_ref,
                 km_ref, vm_ref, bias_ref, o_ref, *, n_blocks):
    i = pl.program_id(1)
    lo = lax.broadcasted_iota(jnp.int32, (BLOCK, LANES), 1) < HEAD_DIM
    edge_p = jnp.where(i == 0, NEG_INF, 0.0).astype(F32)
    edge_n = jnp.where(i == n_blocks - 1, NEG_INF, 0.0).astype(F32)
    nt = (((1,), (1,)), ((), ()))
    zero = jnp.zeros((BLOCK, LANES), BF16)
    for h in range(N_KV_HEADS):
        ks = slice(h * LANES, (h + 1) * LANES)
        rows = []
        for j in range(2):
            pair = q_ref[0, :, (2 * h + j) * LANES:(2 * h + j + 1) * LANES]
            rows.append(jnp.where(lo, pair, zero))
            rows.append(jnp.where(lo, zero, pair))
        qs = jnp.concatenate(rows, axis=0)
        s_p = lax.dot_general(qs, kp_ref[0, :, ks], nt, preferred_element_type=F32) + bias_ref[h, 0] + edge_p
        s_c = lax.dot_general(qs, kc_ref[0, :, ks], nt, preferred_element_type=F32) + bias_ref[h, 1]
        s_n = lax.dot_general(qs, kn_ref[0, :, ks], nt, preferred_element_type=F32) + bias_ref[h, 2] + edge_n
        s_m = lax.dot_general(qs, km_ref[:, ks], nt, preferred_element_type=F32)
        sink = jnp.concatenate(
            [jnp.full((BLOCK, 1), sink_ref[h * GROUP + g], F32) for g in range(GROUP)], axis=0)
        m = jnp.maximum(
            jnp.maximum(jnp.max(s_p, axis=-1, keepdims=True), jnp.max(s_c, axis=-1, keepdims=True)),
            jnp.maximum(jnp.max(s_n, axis=-1, keepdims=True), jnp.max(s_m, axis=-1, keepdims=True)))
        m = jnp.maximum(m, sink)
        p_p = jnp.exp(s_p - m)
        p_c = jnp.exp(s_c - m)
        p_n = jnp.exp(s_n - m)
        p_m = jnp.exp(s_m - m)
        denom = (jnp.sum(p_p, axis=-1, keepdims=True) + jnp.sum(p_c, axis=-1, keepdims=True)
                 + jnp.sum(p_n, axis=-1, keepdims=True) + jnp.sum(p_m, axis=-1, keepdims=True)
                 + jnp.exp(sink - m))
        o = (jnp.dot(p_p.astype(BF16), vp_ref[0, :, ks], preferred_element_type=F32)
             + jnp.dot(p_c.astype(BF16), vc_ref[0, :, ks], preferred_element_type=F32)
             + jnp.dot(p_n.astype(BF16), vn_ref[0, :, ks], preferred_element_type=F32)
             + jnp.dot(p_m.astype(BF16), vm_ref[:, ks], preferred_element_type=F32))
        o = o / denom
        for j in range(2):
            even = o[(2 * j) * BLOCK:(2 * j + 1) * BLOCK]
            odd = o[(2 * j + 1) * BLOCK:(2 * j + 2) * BLOCK]
            o_ref[0, :, (2 * h + j) * LANES:(2 * h + j + 1) * LANES] = jnp.where(lo, even, odd).astype(BF16)


def _attention(sink, q, kd, vd, kmeta, vmeta, bias):
    b, s, _ = q.shape
    nb = s // BLOCK
    cur = lambda bi, i: (bi, i, 0)
    prev = lambda bi, i: (bi, jnp.maximum(i - 1, 0), 0)
    nxt = lambda bi, i: (bi, jnp.minimum(i + 1, nb - 1), 0)
    const2 = lambda bi, i: (0, 0)
    kv_blk = (1, BLOCK, 2 * KV_W)
    return pl.pallas_call(
        functools.partial(_attn_kernel, n_blocks=nb),
        grid=(b, nb),
        in_specs=[
            pl.BlockSpec(memory_space=pltpu.SMEM),
            pl.BlockSpec((1, BLOCK, ATTN_W), cur),
            pl.BlockSpec(kv_blk, prev), pl.BlockSpec(kv_blk, cur), pl.BlockSpec(kv_blk, nxt),
            pl.BlockSpec(kv_blk, prev), pl.BlockSpec(kv_blk, cur), pl.BlockSpec(kv_blk, nxt),
            pl.BlockSpec((N_META, 2 * KV_W), const2),
            pl.BlockSpec((N_META, 2 * KV_W), const2),
            pl.BlockSpec(bias.shape, lambda bi, i: (0, 0, 0, 0)),
        ],
        out_specs=pl.BlockSpec((1, BLOCK, ATTN_W), cur),
        out_shape=jax.ShapeDtypeStruct((b, s, ATTN_W), BF16),
        compiler_params=pltpu.CompilerParams(dimension_semantics=("arbitrary", "arbitrary"),
                                             vmem_limit_bytes=VMEM_LIMIT),
        name="attention",
    )(sink, q, kd, kd, kd, vd, vd, vd, kmeta, vmeta, bias)


def _conv_kernel(hc_ref, hm_ref, w_ref, b_ref, lg_ref, lb_ref, o_ref, pad_ref, *, seq):
    tail = pad_ref.shape[0] - (N_META + seq)
    pad_ref[0:N_META, :] = hm_ref[...]
    pad_ref[N_META:N_META + seq, :] = hc_ref[0]
    pad_ref[N_META + seq:, :] = jnp.zeros((tail, CONV_W), F32)
    first = N_META - CONV_PAD
    span = CONV_ROWS + 24

    def chunk(c, carry):
        base = pl.multiple_of(c * CONV_ROWS, CONV_ROWS)
        win = pad_ref[pl.ds(base, CONV_ROWS + 32), :]
        acc = jnp.zeros((CONV_ROWS, CONV_W), F32)
        for sub in range(8):
            shifted = win[sub:sub + span]
            for al in range(4):
                k = 8 * al + sub - first
                if 0 <= k < CONV_K:
                    acc = acc + shifted[8 * al:8 * al + CONV_ROWS] * w_ref[k:k + 1, :]
        y = acc + b_ref[...]
        mu = jnp.mean(y, axis=-1, keepdims=True)
        yc = y - mu
        var = jnp.mean(yc * yc, axis=-1, keepdims=True)
        z = yc * lax.rsqrt(var + LN_EPS) * lg_ref[...] + lb_ref[...]
        o_ref[0, pl.ds(base, CONV_ROWS), :] = (z * jax.nn.sigmoid(z)).astype(BF16)
        return carry

    lax.fori_loop(0, seq // CONV_ROWS, chunk, 0)


def _conv(hc, hc_meta, dw_w, dw_b, ln_g, ln_b):
    b, s, _ = hc.shape
    const = lambda bi: (0, 0)
    return pl.pallas_call(
        functools.partial(_conv_kernel, seq=s),
        grid=(b,),
        in_specs=[
            pl.BlockSpec((1, s, CONV_W), lambda bi: (bi, 0, 0)),
            pl.BlockSpec((N_META, CONV_W), const),
            pl.BlockSpec((CONV_K, CONV_W), const),
            pl.BlockSpec((1, CONV_W), const),
            pl.BlockSpec((1, CONV_W), const),
            pl.BlockSpec((1, CONV_W), const),
        ],
        out_specs=pl.BlockSpec((1, s, CONV_W), lambda bi: (bi, 0, 0)),
        out_shape=jax.ShapeDtypeStruct((b, s, CONV_W), BF16),
        scratch_shapes=[pltpu.VMEM((N_META + s + 32, CONV_W), F32)],
        compiler_params=pltpu.CompilerParams(dimension_semantics=("arbitrary",),
                                             vmem_limit_bytes=VMEM_LIMIT),
        name="conv",
    )(hc, hc_meta, dw_w, dw_b, ln_g, ln_b)


def _outproj_kernel(attn_ref, conv_ref, x_ref, wa_ref, wc_ref, g2_ref, rwh_ref, rwl_ref, rb_ref,
                    tri_ref, h2_ref, xp_ref, route_ref, cnt_ref, run_ref):
    step = pl.program_id(0)

    @pl.when(step == 0)
    def _():
        run_ref[...] = jnp.zeros_like(run_ref)

    mix = (jnp.dot(attn_ref[...], wa_ref[...], preferred_element_type=F32)
           + jnp.dot(conv_ref[...], wc_ref[...], preferred_element_type=F32))
    h2 = x_ref[...] + mix
    h2_ref[...] = h2
    hn = _rms(h2, RMS_EPS) * g2_ref[...]
    hn_hi = hn.astype(BF16)
    half = D_MODEL // 2
    bits = lax.bitcast_convert_type(hn_hi.astype(F32), jnp.uint32)
    xp_ref[...] = (bits[:, :half] >> 16) | (bits[:, half:] & jnp.uint32(0xFFFF0000))
    hn_lo = (hn - hn_hi.astype(F32)).astype(BF16)
    logits = (jnp.dot(hn_hi, rwh_ref[...], preferred_element_type=F32)
              + jnp.dot(hn_lo, rwh_ref[...], preferred_element_type=F32)
              + jnp.dot(hn_hi, rwl_ref[...], preferred_element_type=F32)) + rb_ref[...]
    lane = lax.broadcasted_iota(jnp.int32, logits.shape, 1).astype(F32)
    work = logits
    vals, idxs = [], []
    onehot = jnp.zeros(logits.shape, F32)
    for _ in range(TOP_K):
        m = jnp.max(work, axis=-1, keepdims=True)
        idx = jnp.min(jnp.where(work == m, lane, float(LANES)), axis=-1, keepdims=True)
        sel = lane == idx
        onehot = jnp.where(sel, 1.0, onehot)
        work = jnp.where(sel, -jnp.inf, work)
        vals.append(m)
        idxs.append(idx)
    exps = [jnp.exp(v - vals[0]) for v in vals]
    tot = exps[0] + exps[1] + exps[2] + exps[3]
    before = jnp.dot(tri_ref[...], onehot.astype(BF16), preferred_element_type=F32) + run_ref[...]
    route = jnp.zeros(logits.shape, F32)
    for r in range(TOP_K):
        rank = jnp.sum(jnp.where(lane == idxs[r], before, 0.0), axis=-1, keepdims=True)
        route = jnp.where(lane == r, idxs[r], route)
        route = jnp.where(lane == TOP_K + r, exps[r] / tot, route)
        route = jnp.where(lane == 2 * TOP_K + r, rank, route)
    route_ref[...] = route
    run_ref[...] = run_ref[...] + jnp.sum(onehot, axis=0, keepdims=True)
    cnt_ref[...] = run_ref[...]


def _outproj(attn2d, conv2d, x2d, wa, wc, g2, rwh, rwl, rb, tri):
    n = x2d.shape[0]
    tile = ROW_TILE
    const = lambda i: (0, 0)
    row = lambda i: (i, 0)
    half = D_MODEL // 2
    return pl.pallas_call(
        _outproj_kernel,
        grid=(n // tile,),
        in_specs=[
            pl.BlockSpec((tile, ATTN_W), row),
            pl.BlockSpec((tile, CONV_W), row),
            pl.BlockSpec((tile, D_MODEL), row),
            pl.BlockSpec((ATTN_W, D_MODEL), const),
            pl.BlockSpec((CONV_W, D_MODEL), const),
            pl.BlockSpec((1, D_MODEL), const),
            pl.BlockSpec((D_MODEL, LANES), const),
            pl.BlockSpec((D_MODEL, LANES), const),
            pl.BlockSpec((1, LANES), const),
            pl.BlockSpec((tile, tile), const),
        ],
        out_specs=[
            pl.BlockSpec((tile, D_MODEL), row),
            pl.BlockSpec((tile, half), row),
            pl.BlockSpec((tile, LANES), row),
            pl.BlockSpec((1, LANES), const),
        ],
        out_shape=[
            jax.ShapeDtypeStruct((n, D_MODEL), F32),
            jax.ShapeDtypeStruct((n, half), jnp.uint32),
            jax.ShapeDtypeStruct((n, LANES), F32),
            jax.ShapeDtypeStruct((1, LANES), F32),
        ],
        scratch_shapes=[pltpu.VMEM((1, LANES), F32)],
        compiler_params=pltpu.CompilerParams(dimension_semantics=("arbitrary",),
                                             vmem_limit_bytes=VMEM_LIMIT),
        name="outproj",
    )(attn2d, conv2d, x2d, wa, wc, g2, rwh, rwl, rb, tri)


def _row_copy(src_hbm, dst_hbm, sem, src_row, dst_row):
    return pltpu.make_async_copy(src_hbm.at[pl.ds(src_row, 1)], dst_hbm.at[pl.ds(dst_row, 1)], sem)


def _dispatch_kernel(pos_ref, xp_hbm, xs_hbm, sem):
    base = pl.program_id(0) * DMA_TOKENS

    def issue(t, carry):
        for k in range(TOP_K):
            _row_copy(xp_hbm, xs_hbm, sem, base + t, pos_ref[t * TOP_K + k]).start()
        return carry

    lax.fori_loop(0, DMA_TOKENS, issue, 0)

    def drain(t, carry):
        for k in range(TOP_K):
            _row_copy(xp_hbm, xs_hbm, sem, 0, 0).wait()
        return carry

    lax.fori_loop(0, DMA_TOKENS, drain, 0)


def _dispatch(pos_flat, xp):
    n, half = xp.shape
    return pl.pallas_call(
        _dispatch_kernel,
        grid=(n // DMA_TOKENS,),
        in_specs=[
            pl.BlockSpec((DMA_TOKENS * TOP_K,), lambda i: (i,), memory_space=pltpu.SMEM),
            pl.BlockSpec(memory_space=pl.ANY),
        ],
        out_specs=pl.BlockSpec(memory_space=pl.ANY),
        out_shape=jax.ShapeDtypeStruct((n * TOP_K, half), jnp.uint32),
        scratch_shapes=[pltpu.SemaphoreType.DMA],
        compiler_params=pltpu.CompilerParams(dimension_semantics=("arbitrary",)),
        name="dispatch",
    )(pos_flat, xp)


def _expert_kernel(tile_ref, exp_ref, valid_ref, start_ref, end_ref,
                   xs_ref, wgu_ref, bgu_ref, wd_ref, bd_ref, y_ref):
    w = pl.program_id(0)
    tile = tile_ref[w]
    e = exp_ref[w]
    first = jnp.logical_or(w == 0, tile_ref[jnp.maximum(w - 1, 0)] != tile)

    @pl.when(valid_ref[w] == 1)
    def _():
        half = D_MODEL // 2
        packed = xs_ref[...]
        x_lo = lax.bitcast_convert_type(packed << 16, F32).astype(BF16)
        x_hi = lax.bitcast_convert_type(packed & jnp.uint32(0xFFFF0000), F32).astype(BF16)
        acc = jnp.zeros((EXPERT_TILE, D_MODEL), F32)
        for c in range(D_FF // FF_CHUNK):
            gc = slice(c * FF_CHUNK, (c + 1) * FF_CHUNK)
            uc = slice(D_FF + c * FF_CHUNK, D_FF + (c + 1) * FF_CHUNK)
            gate = (jnp.dot(x_lo, wgu_ref[0, :half, gc].astype(BF16), preferred_element_type=F32)
                    + jnp.dot(x_hi, wgu_ref[0, half:, gc].astype(BF16), preferred_element_type=F32)
                    + bgu_ref[0, :, gc])
            up = (jnp.dot(x_lo, wgu_ref[0, :half, uc].astype(BF16), preferred_element_type=F32)
                  + jnp.dot(x_hi, wgu_ref[0, half:, uc].astype(BF16), preferred_element_type=F32)
                  + bgu_ref[0, :, uc])
            gate = jnp.minimum(gate, SWIGLU_LIMIT)
            up = jnp.clip(up, -SWIGLU_LIMIT, SWIGLU_LIMIT)
            act = gate * jax.nn.sigmoid(SWIGLU_ALPHA * gate) * (up + 1.0)
            acc = acc + jnp.dot(act.astype(BF16), wd_ref[0, gc, :].astype(BF16),
                                preferred_element_type=F32)
        y = acc + bd_ref[0]
        rows = tile * EXPERT_TILE + lax.broadcasted_iota(jnp.int32, (EXPERT_TILE, 1), 0)
        mine = jnp.logical_and(rows >= start_ref[e], rows < end_ref[e])

        @pl.when(first)
        def _():
            y_ref[...] = jnp.where(mine, y, 0.0)

        @pl.when(jnp.logical_not(first))
        def _():
            y_ref[...] = jnp.where(mine, y, y_ref[...])


def _experts(tile_of, expert_of, valid, starts, ends, xs, wgu, bgu, wd, bd):
    rows, half = xs.shape
    n_work = tile_of.shape[0]
    grid_spec = pltpu.PrefetchScalarGridSpec(
        num_scalar_prefetch=5,
        grid=(n_work,),
        in_specs=[
            pl.BlockSpec((EXPERT_TILE, half), lambda w, t, e, v, s, n: (t[w], 0)),
            pl.BlockSpec((1, D_MODEL, 2 * D_FF), lambda w, t, e, v, s, n: (e[w], 0, 0)),
            pl.BlockSpec((1, 1, 2 * D_FF), lambda w, t, e, v, s, n: (e[w], 0, 0)),
            pl.BlockSpec((1, D_FF, D_MODEL), lambda w, t, e, v, s, n: (e[w], 0, 0)),
            pl.BlockSpec((1, 1, D_MODEL), lambda w, t, e, v, s, n: (e[w], 0, 0)),
        ],
        out_specs=pl.BlockSpec((EXPERT_TILE, D_MODEL), lambda w, t, e, v, s, n: (t[w], 0)),
    )
    return pl.pallas_call(
        _expert_kernel,
        grid_spec=grid_spec,
        out_shape=jax.ShapeDtypeStruct((rows, D_MODEL), F32),
        compiler_params=pltpu.CompilerParams(dimension_semantics=("arbitrary",),
                                             vmem_limit_bytes=VMEM_LIMIT),
        name="experts",
    )(tile_of, expert_of, valid, starts, ends, xs, wgu, bgu, wd, bd)


def _combine_kernel(pos_ref, y_hbm, h2_ref, route_ref, o_ref, buf_ref, sem):
    def issue(t, carry):
        for k in range(TOP_K):
            pltpu.make_async_copy(y_hbm.at[pl.ds(pos_ref[t * TOP_K + k], 1)],
                                  buf_ref.at[k, pl.ds(t, 1)], sem).start()
        return carry

    lax.fori_loop(0, DMA_TOKENS, issue, 0)

    def drain(t, carry):
        for k in range(TOP_K):
            pltpu.make_async_copy(y_hbm.at[pl.ds(0, 1)], buf_ref.at[k, pl.ds(0, 1)], sem).wait()
        return carry

    lax.fori_loop(0, DMA_TOKENS, drain, 0)
    route = route_ref[...]
    out = h2_ref[...]
    for k in range(TOP_K):
        out = out + route[:, TOP_K + k:TOP_K + k + 1] * buf_ref[k]
    o_ref[...] = out


def _combine(pos_flat, y, h2, route):
    n = h2.shape[0]
    row = lambda i: (i, 0)
    return pl.pallas_call(
        _combine_kernel,
        grid=(n // DMA_TOKENS,),
        in_specs=[
            pl.BlockSpec((DMA_TOKENS * TOP_K,), lambda i: (i,), memory_space=pltpu.SMEM),
            pl.BlockSpec(memory_space=pl.ANY),
            pl.BlockSpec((DMA_TOKENS, D_MODEL), row),
            pl.BlockSpec((DMA_TOKENS, LANES), row),
        ],
        out_specs=pl.BlockSpec((DMA_TOKENS, D_MODEL), row),
        out_shape=jax.ShapeDtypeStruct((n, D_MODEL), F32),
        scratch_shapes=[pltpu.VMEM((TOP_K, DMA_TOKENS, D_MODEL), F32), pltpu.SemaphoreType.DMA],
        compiler_params=pltpu.CompilerParams(dimension_semantics=("arbitrary",),
                                             vmem_limit_bytes=VMEM_LIMIT),
        name="combine",
    )(pos_flat, y, h2, route)


def _work_schedule(counts, n_rows):
    n_tiles = n_rows // EXPERT_TILE
    n_work = n_tiles + N_EXPERTS - 1
    ends = jnp.cumsum(counts)
    starts = ends - counts
    tile_lo = jnp.arange(n_tiles, dtype=jnp.int32) * EXPERT_TILE
    first_e = jnp.searchsorted(ends, tile_lo, side="right").astype(jnp.int32)
    last_e = (jnp.searchsorted(starts, tile_lo + (EXPERT_TILE - 1), side="right") - 1).astype(jnp.int32)
    per_tile = last_e - first_e + 1
    w_end = jnp.cumsum(per_tile)
    w_start = w_end - per_tile
    total = w_end[-1]
    w = jnp.arange(n_work, dtype=jnp.int32)
    wc = jnp.minimum(w, total - 1)
    tile_of = (jnp.searchsorted(w_end, wc, side="right")).astype(jnp.int32)
    expert_of = first_e[tile_of] + (wc - w_start[tile_of])
    valid = (w < total).astype(jnp.int32)
    return tile_of, expert_of.astype(jnp.int32), valid, starts.astype(jnp.int32), ends.astype(jnp.int32)


def kernel(x, meta_tokens, norm1_g, w_in, q_norm_g, k_norm_g, attn_sink, dw_w, dw_b, conv_ln_g,
           conv_ln_b, w_out, norm2_g, router_w, router_b, w_gate_up, b_gate_up, w_down, b_down):
    assert norm1_g.shape[0] == 1, "single-layer trunk: meta-token query rows are not materialised"
    b, s, d = x.shape
    n = b * s
    x2d = x.reshape(n, d)

    scale = HEAD_DIM ** -0.5
    qg = (jnp.tile(q_norm_g[0], N_Q_HEADS) * scale).reshape(1, ATTN_W)
    kg = jnp.tile(k_norm_g[0], N_KV_HEADS).reshape(1, KV_W)
    head_of = np.arange(ATTN_W) // HEAD_DIM
    pq = jnp.asarray((head_of[:, None] == head_of[None, :]) / HEAD_DIM, BF16)
    pk = pq[:KV_W, :KV_W]
    g1 = norm1_g[0].reshape(1, d)
    w_in_b = w_in[0].astype(BF16)

    q, kd, vd, hc = _inproj(x2d, g1, w_in_b, qg, kg, pq, pk, ROW_TILE)
    _, kmeta, vmeta, hc_meta = _inproj(meta_tokens, g1, w_in_b, qg, kg, pq, pk, N_META)

    attn = _attention(attn_sink[0], q.reshape(b, s, ATTN_W), kd.reshape(b, s, 2 * KV_W),
                      vd.reshape(b, s, 2 * KV_W), kmeta, vmeta, jnp.asarray(_alibi_bias()))
    conv = _conv(hc.reshape(b, s, CONV_W), hc_meta, dw_w[0].reshape(CONV_K, CONV_W),
                 dw_b[0].reshape(1, CONV_W), conv_ln_g[0].reshape(1, CONV_W),
                 conv_ln_b[0].reshape(1, CONV_W))

    w_out_b = w_out[0].astype(BF16)
    rw = jnp.pad(router_w[0], ((0, 0), (0, LANES - N_EXPERTS)))
    rwh = rw.astype(BF16)
    rwl = (rw - rwh.astype(F32)).astype(BF16)
    rb = jnp.pad(router_b[0], (0, LANES - N_EXPERTS), constant_values=NEG_INF).reshape(1, LANES)
    tri = jnp.asarray(np.tril(np.ones((ROW_TILE, ROW_TILE), np.float32), -1), BF16)
    h2, xp, route, counts = _outproj(attn.reshape(n, ATTN_W), conv.reshape(n, CONV_W), x2d,
                                     w_out_b[:ATTN_W], w_out_b[ATTN_W:], norm2_g[0].reshape(1, d),
                                     rwh, rwl, rb, tri)

    counts_i = counts[0, :N_EXPERTS].astype(jnp.int32)
    idx = route[:, :TOP_K].astype(jnp.int32)
    rank = route[:, 2 * TOP_K:3 * TOP_K].astype(jnp.int32)
    tile_of, expert_of, valid, starts, ends = _work_schedule(counts_i, n * TOP_K)
    pos_flat = (starts[idx] + rank).reshape(n * TOP_K)

    xs = _dispatch(pos_flat, xp)
    y = _experts(tile_of, expert_of, valid, starts, ends, xs, w_gate_up[0],
                 b_gate_up[0].reshape(N_EXPERTS, 1, 2 * D_FF), w_down[0],
                 b_down[0].reshape(N_EXPERTS, 1, D_MODEL))
    out = _combine(pos_flat, y, h2, route)
    return out.reshape(b, s, d)
```

```python
import functools

import numpy as np
import jax
import jax.numpy as jnp
from jax import lax
from jax.experimental import pallas as pl
from jax.experimental.pallas import tpu as pltpu

F32 = jnp.float32
BF16 = jnp.bfloat16

D_MODEL = 1024
N_META = 16
HEAD_DIM = 64
N_Q_HEADS = 8
N_KV_HEADS = 2
GROUP = N_Q_HEADS // N_KV_HEADS
ATTN_W = N_Q_HEADS * HEAD_DIM
KV_W = N_KV_HEADS * HEAD_DIM
CONV_W = D_MODEL - ATTN_W
IN_W = ATTN_W + 2 * KV_W + 2 * CONV_W
WINDOW = 128
BLOCK = 128
CONV_K = 31
CONV_PAD = CONV_K // 2
N_EXPERTS = 32
TOP_K = 4
D_FF = D_MODEL
SWIGLU_LIMIT = 7.0
SWIGLU_ALPHA = 1.702
RMS_EPS = 1e-6
LN_EPS = 1e-5
NEG_INF = -1e30

LANES = 128
ROW_TILE = 512
EXPERT_TILE = 512
DMA_TOKENS = 256
CONV_ROWS = 32
FF_CHUNK = 256
VMEM_LIMIT = 56 * 1024 * 1024


def _rms(x, eps):
    return x * lax.rsqrt(jnp.mean(x * x, axis=-1, keepdims=True) + eps)


def _inproj_kernel(x_ref, g1_ref, w_ref, qg_ref, kg_ref, pq_ref, pk_ref,
                   q_ref, k_ref, v_ref, hc_ref):
    x = x_ref[...]
    n = (_rms(x, RMS_EPS) * g1_ref[...]).astype(BF16)
    proj = jnp.dot(n, w_ref[...], preferred_element_type=F32)
    q = proj[:, :ATTN_W]
    k = proj[:, ATTN_W:ATTN_W + KV_W]
    v = proj[:, ATTN_W + KV_W:ATTN_W + 2 * KV_W]
    a = proj[:, ATTN_W + 2 * KV_W:ATTN_W + 2 * KV_W + CONV_W]
    g = proj[:, ATTN_W + 2 * KV_W + CONV_W:]
    qms = jnp.dot((q * q).astype(BF16), pq_ref[...], preferred_element_type=F32)
    kms = jnp.dot((k * k).astype(BF16), pk_ref[...], preferred_element_type=F32)
    q_ref[...] = (q * lax.rsqrt(qms + RMS_EPS) * qg_ref[...]).astype(BF16)
    kn = k * lax.rsqrt(kms + RMS_EPS) * kg_ref[...]
    lo = lax.broadcasted_iota(jnp.int32, kn.shape, 1) < HEAD_DIM
    ksw = pltpu.roll(kn, HEAD_DIM, 1)
    k_ref[...] = jnp.concatenate([jnp.where(lo, kn, ksw), jnp.where(lo, ksw, kn)], axis=1).astype(BF16)
    vsw = pltpu.roll(v, HEAD_DIM, 1)
    v_ref[...] = jnp.concatenate([jnp.where(lo, v, vsw), jnp.where(lo, vsw, v)], axis=1).astype(BF16)
    hc_ref[...] = a * jax.nn.sigmoid(g)


def _inproj(x2d, g1, w_in_b, qg, kg, pq, pk, tile):
    n = x2d.shape[0]
    const = lambda i: (0, 0)
    row = lambda i: (i, 0)
    return pl.pallas_call(
        _inproj_kernel,
        grid=(n // tile,),
        in_specs=[
            pl.BlockSpec((tile, D_MODEL), row),
            pl.BlockSpec((1, D_MODEL), const),
            pl.BlockSpec((D_MODEL, IN_W), const),
            pl.BlockSpec((1, ATTN_W), const),
            pl.BlockSpec((1, KV_W), const),
            pl.BlockSpec((ATTN_W, ATTN_W), const),
            pl.BlockSpec((KV_W, KV_W), const),
        ],
        out_specs=[
            pl.BlockSpec((tile, ATTN_W), row),
            pl.BlockSpec((tile, 2 * KV_W), row),
            pl.BlockSpec((tile, 2 * KV_W), row),
            pl.BlockSpec((tile, CONV_W), row),
        ],
        out_shape=[
            jax.ShapeDtypeStruct((n, ATTN_W), BF16),
            jax.ShapeDtypeStruct((n, 2 * KV_W), BF16),
            jax.ShapeDtypeStruct((n, 2 * KV_W), BF16),
            jax.ShapeDtypeStruct((n, CONV_W), F32),
        ],
        compiler_params=pltpu.CompilerParams(dimension_semantics=("arbitrary",),
                                             vmem_limit_bytes=VMEM_LIMIT),
        name="inproj",
    )(x2d, g1, w_in_b, qg, kg, pq, pk)


def _alibi_bias():
    qi = np.arange(BLOCK)[:, None]
    kj = np.arange(BLOCK)[None, :]
    dists = [qi + BLOCK - kj, np.abs(qi - kj), kj + BLOCK - qi]
    out = np.zeros((N_KV_HEADS, 3, GROUP * BLOCK, BLOCK), np.float32)
    for h in range(N_KV_HEADS):
        for g in range(GROUP):
            slope = 2.0 ** (-8.0 * (h * GROUP + g + 1) / N_Q_HEADS)
            for p, d in enumerate(dists):
                out[h, p, g * BLOCK:(g + 1) * BLOCK] = np.where(d <= WINDOW, -slope * d, NEG_INF)
    return out


def _attn_kernel(sink_ref, q_ref, kp_ref, kc_ref, kn_ref, vp_ref, vc_ref, vn_ref,
                 km_ref, vm_ref, bias_ref, o_ref, *, n_blocks):
    i = pl.program_id(1)
    lo = lax.broadcasted_iota(jnp.int32, (BLOCK, LANES), 1) < HEAD_DIM
    edge_p = jnp.where(i == 0, NEG_INF, 0.0).astype(F32)
    edge_n = jnp.where(i == n_blocks - 1, NEG_INF, 0.0).astype(F32)
    nt = (((1,), (1,)), ((), ()))
    zero = jnp.zeros((BLOCK, LANES), BF16)
    for h in range(N_KV_HEADS):
        ks = slice(h * LANES, (h + 1) * LANES)
        rows = []
        for j in range(2):
            pair = q_ref[0, :, (2 * h + j) * LANES:(2 * h + j + 1) * LANES]
            rows.append(jnp.where(lo, pair, zero))
            rows.append(jnp.where(lo, zero, pair))
        qs = jnp.concatenate(rows, axis=0)
        s_p = lax.dot_general(qs, kp_ref[0, :, ks], nt, preferred_element_type=F32) + bias_ref[h, 0] + edge_p
        s_c = lax.dot_general(qs, kc_ref[0, :, ks], nt, preferred_element_type=F32) + bias_ref[h, 1]
        s_n = lax.dot_general(qs, kn_ref[0, :, ks], nt, preferred_element_type=F32) + bias_ref[h, 2] + edge_n
        s_m = lax.dot_general(qs, km_ref[:, ks], nt, preferred_element_type=F32)
        sink = jnp.concatenate(
            [jnp.full((BLOCK, 1), sink_ref[h * GROUP + g], F32) for g in range(GROUP)], axis=0)
        m = jnp.maximum(
            jnp.maximum(jnp.max(s_p, axis=-1, keepdims=True), jnp.max(s_c, axis=-1, keepdims=True)),
            jnp.maximum(jnp.max(s_n, axis=-1, keepdims=True), jnp.max(s_m, axis=-1, keepdims=True)))
        m = jnp.maximum(m, sink)
        p_p = jnp.exp(s_p - m)
        p_c = jnp.exp(s_c - m)
        p_n = jnp.exp(s_n - m)
        p_m = jnp.exp(s_m - m)
        denom = (jnp.sum(p_p, axis=-1, keepdims=True) + jnp.sum(p_c, axis=-1, keepdims=True)
                 + jnp.sum(p_n, axis=-1, keepdims=True) + jnp.sum(p_m, axis=-1, keepdims=True)
                 + jnp.exp(sink - m))
        o = (jnp.dot(p_p.astype(BF16), vp_ref[0, :, ks], preferred_element_type=F32)
             + jnp.dot(p_c.astype(BF16), vc_ref[0, :, ks], preferred_element_type=F32)
             + jnp.dot(p_n.astype(BF16), vn_ref[0, :, ks], preferred_element_type=F32)
             + jnp.dot(p_m.astype(BF16), vm_ref[:, ks], preferred_element_type=F32))
        o = o / denom
        for j in range(2):
            even = o[(2 * j) * BLOCK:(2 * j + 1) * BLOCK]
            odd = o[(2 * j + 1) * BLOCK:(2 * j + 2) * BLOCK]
            o_ref[0, :, (2 * h + j) * LANES:(2 * h + j + 1) * LANES] = jnp.where(lo, even, odd).astype(BF16)


def _attention(sink, q, kd, vd, kmeta, vmeta, bias):
    b, s, _ = q.shape
    nb = s // BLOCK
    cur = lambda bi, i: (bi, i, 0)
    prev = lambda bi, i: (bi, jnp.maximum(i - 1, 0), 0)
    nxt = lambda bi, i: (bi, jnp.minimum(i + 1, nb - 1), 0)
    const2 = lambda bi, i: (0, 0)
    kv_blk = (1, BLOCK, 2 * KV_W)
    return pl.pallas_call(
        functools.partial(_attn_kernel, n_blocks=nb),
        grid=(b, nb),
        in_specs=[
            pl.BlockSpec(memory_space=pltpu.SMEM),
            pl.BlockSpec((1, BLOCK, ATTN_W), cur),
            pl.BlockSpec(kv_blk, prev), pl.BlockSpec(kv_blk, cur), pl.BlockSpec(kv_blk, nxt),
            pl.BlockSpec(kv_blk, prev), pl.BlockSpec(kv_blk, cur), pl.BlockSpec(kv_blk, nxt),
            pl.BlockSpec((N_META, 2 * KV_W), const2),
            pl.BlockSpec((N_META, 2 * KV_W), const2),
            pl.BlockSpec(bias.shape, lambda bi, i: (0, 0, 0, 0)),
        ],
        out_specs=pl.BlockSpec((1, BLOCK, ATTN_W), cur),
        out_shape=jax.ShapeDtypeStruct((b, s, ATTN_W), BF16),
        compiler_params=pltpu.CompilerParams(dimension_semantics=("arbitrary", "arbitrary"),
                                             vmem_limit_bytes=VMEM_LIMIT),
        name="attention",
    )(sink, q, kd, kd, kd, vd, vd, vd, kmeta, vmeta, bias)


def _conv_kernel(hc_ref, hm_ref, w_ref, b_ref, lg_ref, lb_ref, o_ref, pad_ref, *, seq):
    tail = pad_ref.shape[0] - (N_META + seq)
    pad_ref[0:N_META, :] = hm_ref[...]
    pad_ref[N_META:N_META + seq, :] = hc_ref[0]
    pad_ref[N_META + seq:, :] = jnp.zeros((tail, CONV_W), F32)
    first = N_META - CONV_PAD
    span = CONV_ROWS + 24

    def chunk(c, carry):
        base = pl.multiple_of(c * CONV_ROWS, CONV_ROWS)
        win = pad_ref[pl.ds(base, CONV_ROWS + 32), :]
        acc = jnp.zeros((CONV_ROWS, CONV_W), F32)
        for sub in range(8):
            shifted = win[sub:sub + span]
            for al in range(4):
                k = 8 * al + sub - first
                if 0 <= k < CONV_K:
                    acc = acc + shifted[8 * al:8 * al + CONV_ROWS] * w_ref[k:k + 1, :]
        y = acc + b_ref[...]
        mu = jnp.mean(y, axis=-1, keepdims=True)
        yc = y - mu
        var = jnp.mean(yc * yc, axis=-1, keepdims=True)
        z = yc * lax.rsqrt(var + LN_EPS) * lg_ref[...] + lb_ref[...]
        o_ref[0, pl.ds(base, CONV_ROWS), :] = (z * jax.nn.sigmoid(z)).astype(BF16)
        return carry

    lax.fori_loop(0, seq // CONV_ROWS, chunk, 0)


def _conv(hc, hc_meta, dw_w, dw_b, ln_g, ln_b):
    b, s, _ = hc.shape
    const = lambda bi: (0, 0)
    return pl.pallas_call(
        functools.partial(_conv_kernel, seq=s),
        grid=(b,),
        in_specs=[
            pl.BlockSpec((1, s, CONV_W), lambda bi: (bi, 0, 0)),
            pl.BlockSpec((N_META, CONV_W), const),
            pl.BlockSpec((CONV_K, CONV_W), const),
            pl.BlockSpec((1, CONV_W), const),
            pl.BlockSpec((1, CONV_W), const),
            pl.BlockSpec((1, CONV_W), const),
        ],
        out_specs=pl.BlockSpec((1, s, CONV_W), lambda bi: (bi, 0, 0)),
        out_shape=jax.ShapeDtypeStruct((b, s, CONV_W), BF16),
        scratch_shapes=[pltpu.VMEM((N_META + s + 32, CONV_W), F32)],
        compiler_params=pltpu.CompilerParams(dimension_semantics=("arbitrary",),
                                             vmem_limit_bytes=VMEM_LIMIT),
        name="conv",
    )(hc, hc_meta, dw_w, dw_b, ln_g, ln_b)


def _outproj_kernel(attn_ref, conv_ref, x_ref, wa_ref, wc_ref, g2_ref, rwh_ref, rwl_ref, rb_ref,
                    tri_ref, h2_ref, xp_ref, route_ref, cnt_ref, run_ref):
    step = pl.program_id(0)

    @pl.when(step == 0)
    def _():
        run_ref[...] = jnp.zeros_like(run_ref)

    mix = (jnp.dot(attn_ref[...], wa_ref[...], preferred_element_type=F32)
           + jnp.dot(conv_ref[...], wc_ref[...], preferred_element_type=F32))
    h2 = x_ref[...] + mix
    h2_ref[...] = h2
    hn = _rms(h2, RMS_EPS) * g2_ref[...]
    hn_hi = hn.astype(BF16)
    half = D_MODEL // 2
    bits = lax.bitcast_convert_type(hn_hi.astype(F32), jnp.uint32)
    xp_ref[...] = (bits[:, :half] >> 16) | (bits[:, half:] & jnp.uint32(0xFFFF0000))
    hn_lo = (hn - hn_hi.astype(F32)).astype(BF16)
    logits = (jnp.dot(hn_hi, rwh_ref[...], preferred_element_type=F32)
              + jnp.dot(hn_lo, rwh_ref[...], preferred_element_type=F32)
              + jnp.dot(hn_hi, rwl_ref[...], preferred_element_type=F32)) + rb_ref[...]
    lane = lax.broadcasted_iota(jnp.int32, logits.shape, 1).astype(F32)
    work = logits
    vals, idxs = [], []
    onehot = jnp.zeros(logits.shape, F32)
    for _ in range(TOP_K):
        m = jnp.max(work, axis=-1, keepdims=True)
        idx = jnp.min(jnp.where(work == m, lane, float(LANES)), axis=-1, keepdims=True)
        sel = lane == idx
        onehot = jnp.where(sel, 1.0, onehot)
        work = jnp.where(sel, -jnp.inf, work)
        vals.append(m)
        idxs.append(idx)
    exps = [jnp.exp(v - vals[0]) for v in vals]
    tot = exps[0] + exps[1] + exps[2] + exps[3]
    before = jnp.dot(tri_ref[...], onehot.astype(BF16), preferred_element_type=F32) + run_ref[...]
    route = jnp.zeros(logits.shape, F32)
    for r in range(TOP_K):
        rank = jnp.sum(jnp.where(lane == idxs[r], before, 0.0), axis=-1, keepdims=True)
        route = jnp.where(lane == r, idxs[r], route)
        route = jnp.where(lane == TOP_K + r, exps[r] / tot, route)
        route = jnp.where(lane == 2 * TOP_K + r, rank, route)
    route_ref[...] = route
    run_ref[...] = run_ref[...] + jnp.sum(onehot, axis=0, keepdims=True)
    cnt_ref[...] = run_ref[...]


def _outproj(attn2d, conv2d, x2d, wa, wc, g2, rwh, rwl, rb, tri):
    n = x2d.shape[0]
    tile = ROW_TILE
    const = lambda i: (0, 0)
    row = lambda i: (i, 0)
    half = D_MODEL // 2
    return pl.pallas_call(
        _outproj_kernel,
        grid=(n // tile,),
        in_specs=[
            pl.BlockSpec((tile, ATTN_W), row),
            pl.BlockSpec((tile, CONV_W), row),
            pl.BlockSpec((tile, D_MODEL), row),
            pl.BlockSpec((ATTN_W, D_MODEL), const),
            pl.BlockSpec((CONV_W, D_MODEL), const),
            pl.BlockSpec((1, D_MODEL), const),
            pl.BlockSpec((D_MODEL, LANES), const),
            pl.BlockSpec((D_MODEL, LANES), const),
            pl.BlockSpec((1, LANES), const),
            pl.BlockSpec((tile, tile), const),
        ],
        out_specs=[
            pl.BlockSpec((tile, D_MODEL), row),
            pl.BlockSpec((tile, half), row),
            pl.BlockSpec((tile, LANES), row),
            pl.BlockSpec((1, LANES), const),
        ],
        out_shape=[
            jax.ShapeDtypeStruct((n, D_MODEL), F32),
            jax.ShapeDtypeStruct((n, half), jnp.uint32),
            jax.ShapeDtypeStruct((n, LANES), F32),
            jax.ShapeDtypeStruct((1, LANES), F32),
        ],
        scratch_shapes=[pltpu.VMEM((1, LANES), F32)],
        compiler_params=pltpu.CompilerParams(dimension_semantics=("arbitrary",),
                                             vmem_limit_bytes=VMEM_LIMIT),
        name="outproj",
    )(attn2d, conv2d, x2d, wa, wc, g2, rwh, rwl, rb, tri)


def _row_copy(src_ref, dst_hbm, sem, src_row, dst_row):
    return pltpu.make_async_copy(src_ref.at[pl.ds(src_row, 1)], dst_hbm.at[pl.ds(dst_row, 1)], sem)


def _dispatch_kernel(pos_ref, xp_ref, xs_hbm, sem):
    def issue(t, carry):
        for k in range(TOP_K):
            _row_copy(xp_ref, xs_hbm, sem, t, pos_ref[t * TOP_K + k]).start()
        return carry

    lax.fori_loop(0, DMA_TOKENS, issue, 0)

    def drain(t, carry):
        for k in range(TOP_K):
            _row_copy(xp_ref, xs_hbm, sem, 0, 0).wait()
        return carry

    lax.fori_loop(0, DMA_TOKENS, drain, 0)


def _dispatch(pos_flat, xp):
    n, half = xp.shape
    return pl.pallas_call(
        _dispatch_kernel,
        grid=(n // DMA_TOKENS,),
        in_specs=[
            pl.BlockSpec((DMA_TOKENS * TOP_K,), lambda i: (i,), memory_space=pltpu.SMEM),
            pl.BlockSpec((DMA_TOKENS, half), lambda i: (i, 0)),
        ],
        out_specs=pl.BlockSpec(memory_space=pl.ANY),
        out_shape=jax.ShapeDtypeStruct((n * TOP_K, half), jnp.uint32),
        scratch_shapes=[pltpu.SemaphoreType.DMA],
        compiler_params=pltpu.CompilerParams(dimension_semantics=("arbitrary",)),
        name="dispatch",
    )(pos_flat, xp)


def _expert_kernel(tile_ref, exp_ref, valid_ref, start_ref, end_ref,
                   xs_ref, wgu_ref, bgu_ref, wd_ref, bd_ref, y_ref):
    w = pl.program_id(0)
    tile = tile_ref[w]
    e = exp_ref[w]
    first = jnp.logical_or(w == 0, tile_ref[jnp.maximum(w - 1, 0)] != tile)

    @pl.when(valid_ref[w] == 1)
    def _():
        half = D_MODEL // 2
        packed = xs_ref[...]
        x_lo = lax.bitcast_convert_type(packed << 16, F32).astype(BF16)
        x_hi = lax.bitcast_convert_type(packed & jnp.uint32(0xFFFF0000), F32).astype(BF16)
        acc = jnp.zeros((EXPERT_TILE, D_MODEL), F32)
        for c in range(D_FF // FF_CHUNK):
            gc = slice(c * FF_CHUNK, (c + 1) * FF_CHUNK)
            uc = slice(D_FF + c * FF_CHUNK, D_FF + (c + 1) * FF_CHUNK)
            gate = (jnp.dot(x_lo, wgu_ref[0, :half, gc].astype(BF16), preferred_element_type=F32)
                    + jnp.dot(x_hi, wgu_ref[0, half:, gc].astype(BF16), preferred_element_type=F32)
                    + bgu_ref[0, :, gc])
            up = (jnp.dot(x_lo, wgu_ref[0, :half, uc].astype(BF16), preferred_element_type=F32)
                  + jnp.dot(x_hi, wgu_ref[0, half:, uc].astype(BF16), preferred_element_type=F32)
                  + bgu_ref[0, :, uc])
            gate = jnp.minimum(gate, SWIGLU_LIMIT)
            up = jnp.clip(up, -SWIGLU_LIMIT, SWIGLU_LIMIT)
            act = gate * jax.nn.sigmoid(SWIGLU_ALPHA * gate) * (up + 1.0)
            acc = acc + jnp.dot(act.astype(BF16), wd_ref[0, gc, :].astype(BF16),
                                preferred_element_type=F32)
        y = acc + bd_ref[0]
        rows = tile * EXPERT_TILE + lax.broadcasted_iota(jnp.int32, (EXPERT_TILE, 1), 0)
        mine = jnp.logical_and(rows >= start_ref[e], rows < end_ref[e])

        @pl.when(first)
        def _():
            y_ref[...] = jnp.where(mine, y, 0.0)

        @pl.when(jnp.logical_not(first))
        def _():
            y_ref[...] = jnp.where(mine, y, y_ref[...])


def _experts(tile_of, expert_of, valid, starts, ends, xs, wgu, bgu, wd, bd):
    rows, half = xs.shape
    n_work = tile_of.shape[0]
    grid_spec = pltpu.PrefetchScalarGridSpec(
        num_scalar_prefetch=5,
        grid=(n_work,),
        in_specs=[
            pl.BlockSpec((EXPERT_TILE, half), lambda w, t, e, v, s, n: (t[w], 0)),
            pl.BlockSpec((1, D_MODEL, 2 * D_FF), lambda w, t, e, v, s, n: (e[w], 0, 0)),
            pl.BlockSpec((1, 1, 2 * D_FF), lambda w, t, e, v, s, n: (e[w], 0, 0)),
            pl.BlockSpec((1, D_FF, D_MODEL), lambda w, t, e, v, s, n: (e[w], 0, 0)),
            pl.BlockSpec((1, 1, D_MODEL), lambda w, t, e, v, s, n: (e[w], 0, 0)),
        ],
        out_specs=pl.BlockSpec((EXPERT_TILE, D_MODEL), lambda w, t, e, v, s, n: (t[w], 0)),
    )
    return pl.pallas_call(
        _expert_kernel,
        grid_spec=grid_spec,
        out_shape=jax.ShapeDtypeStruct((rows, D_MODEL), F32),
        compiler_params=pltpu.CompilerParams(dimension_semantics=("arbitrary",),
                                             vmem_limit_bytes=VMEM_LIMIT),
        name="experts",
    )(tile_of, expert_of, valid, starts, ends, xs, wgu, bgu, wd, bd)


def _combine_kernel(pos_ref, y_hbm, h2_ref, route_ref, o_ref, buf_ref, sem):
    def issue(t, carry):
        for k in range(TOP_K):
            pltpu.make_async_copy(y_hbm.at[pl.ds(pos_ref[t * TOP_K + k], 1)],
                                  buf_ref.at[k, pl.ds(t, 1)], sem).start()
        return carry

    lax.fori_loop(0, DMA_TOKENS, issue, 0)

    def drain(t, carry):
        for k in range(TOP_K):
            pltpu.make_async_copy(y_hbm.at[pl.ds(0, 1)], buf_ref.at[k, pl.ds(0, 1)], sem).wait()
        return carry

    lax.fori_loop(0, DMA_TOKENS, drain, 0)
    route = route_ref[...]
    out = h2_ref[...]
    for k in range(TOP_K):
        out = out + route[:, TOP_K + k:TOP_K + k + 1] * buf_ref[k]
    o_ref[...] = out


def _combine(pos_flat, y, h2, route):
    n = h2.shape[0]
    row = lambda i: (i, 0)
    return pl.pallas_call(
        _combine_kernel,
        grid=(n // DMA_TOKENS,),
        in_specs=[
            pl.BlockSpec((DMA_TOKENS * TOP_K,), lambda i: (i,), memory_space=pltpu.SMEM),
            pl.BlockSpec(memory_space=pl.ANY),
            pl.BlockSpec((DMA_TOKENS, D_MODEL), row),
            pl.BlockSpec((DMA_TOKENS, LANES), row),
        ],
        out_specs=pl.BlockSpec((DMA_TOKENS, D_MODEL), row),
        out_shape=jax.ShapeDtypeStruct((n, D_MODEL), F32),
        scratch_shapes=[pltpu.VMEM((TOP_K, DMA_TOKENS, D_MODEL), F32), pltpu.SemaphoreType.DMA],
        compiler_params=pltpu.CompilerParams(dimension_semantics=("arbitrary",),
                                             vmem_limit_bytes=VMEM_LIMIT),
        name="combine",
    )(pos_flat, y, h2, route)


def _work_schedule(counts, n_rows):
    n_tiles = n_rows // EXPERT_TILE
    n_work = n_tiles + N_EXPERTS - 1
    ends = jnp.cumsum(counts)
    starts = ends - counts
    tile_lo = jnp.arange(n_tiles, dtype=jnp.int32) * EXPERT_TILE
    first_e = jnp.searchsorted(ends, tile_lo, side="right").astype(jnp.int32)
    last_e = (jnp.searchsorted(starts, tile_lo + (EXPERT_TILE - 1), side="right") - 1).astype(jnp.int32)
    per_tile = last_e - first_e + 1
    w_end = jnp.cumsum(per_tile)
    w_start = w_end - per_tile
    total = w_end[-1]
    w = jnp.arange(n_work, dtype=jnp.int32)
    wc = jnp.minimum(w, total - 1)
    tile_of = (jnp.searchsorted(w_end, wc, side="right")).astype(jnp.int32)
    expert_of = first_e[tile_of] + (wc - w_start[tile_of])
    valid = (w < total).astype(jnp.int32)
    return tile_of, expert_of.astype(jnp.int32), valid, starts.astype(jnp.int32), ends.astype(jnp.int32)


def kernel(x, meta_tokens, norm1_g, w_in, q_norm_g, k_norm_g, attn_sink, dw_w, dw_b, conv_ln_g,
           conv_ln_b, w_out, norm2_g, router_w, router_b, w_gate_up, b_gate_up, w_down, b_down):
    assert norm1_g.shape[0] == 1, "single-layer trunk: meta-token query rows are not materialised"
    b, s, d = x.shape
    n = b * s
    x2d = x.reshape(n, d)

    scale = HEAD_DIM ** -0.5
    qg = (jnp.tile(q_norm_g[0], N_Q_HEADS) * scale).reshape(1, ATTN_W)
    kg = jnp.tile(k_norm_g[0], N_KV_HEADS).reshape(1, KV_W)
    head_of = np.arange(ATTN_W) // HEAD_DIM
    pq = jnp.asarray((head_of[:, None] == head_of[None, :]) / HEAD_DIM, BF16)
    pk = pq[:KV_W, :KV_W]
    g1 = norm1_g[0].reshape(1, d)
    w_in_b = w_in[0].astype(BF16)

    q, kd, vd, hc = _inproj(x2d, g1, w_in_b, qg, kg, pq, pk, ROW_TILE)
    _, kmeta, vmeta, hc_meta = _inproj(meta_tokens, g1, w_in_b, qg, kg, pq, pk, N_META)

    attn = _attention(attn_sink[0], q.reshape(b, s, ATTN_W), kd.reshape(b, s, 2 * KV_W),
                      vd.reshape(b, s, 2 * KV_W), kmeta, vmeta, jnp.asarray(_alibi_bias()))
    conv = _conv(hc.reshape(b, s, CONV_W), hc_meta, dw_w[0].reshape(CONV_K, CONV_W),
                 dw_b[0].reshape(1, CONV_W), conv_ln_g[0].reshape(1, CONV_W),
                 conv_ln_b[0].reshape(1, CONV_W))

    w_out_b = w_out[0].astype(BF16)
    rw = jnp.pad(router_w[0], ((0, 0), (0, LANES - N_EXPERTS)))
    rwh = rw.astype(BF16)
    rwl = (rw - rwh.astype(F32)).astype(BF16)
    rb = jnp.pad(router_b[0], (0, LANES - N_EXPERTS), constant_values=NEG_INF).reshape(1, LANES)
    tri = jnp.asarray(np.tril(np.ones((ROW_TILE, ROW_TILE), np.float32), -1), BF16)
    h2, xp, route, counts = _outproj(attn.reshape(n, ATTN_W), conv.reshape(n, CONV_W), x2d,
                                     w_out_b[:ATTN_W], w_out_b[ATTN_W:], norm2_g[0].reshape(1, d),
                                     rwh, rwl, rb, tri)

    counts_i = counts[0, :N_EXPERTS].astype(jnp.int32)
    idx = route[:, :TOP_K].astype(jnp.int32)
    rank = route[:, 2 * TOP_K:3 * TOP_K].astype(jnp.int32)
    tile_of, expert_of, valid, starts, ends = _work_schedule(counts_i, n * TOP_K)
    pos_flat = (starts[idx] + rank).reshape(n * TOP_K)

    xs = _dispatch(pos_flat, xp)
    y = _experts(tile_of, expert_of, valid, starts, ends, xs, w_gate_up[0],
                 b_gate_up[0].reshape(N_EXPERTS, 1, 2 * D_FF), w_down[0],
                 b_down[0].reshape(N_EXPERTS, 1, D_MODEL))
    out = _combine(pos_flat, y, h2, route)
    return out.reshape(b, s, d)
```

```python
import functools

import numpy as np
import jax
import jax.numpy as jnp
from jax import lax
from jax.experimental import pallas as pl
from jax.experimental.pallas import tpu as pltpu
from jax.experimental.pallas import tpu_sc as plsc

F32 = jnp.float32
BF16 = jnp.bfloat16

D_MODEL = 1024
N_META = 16
HEAD_DIM = 64
N_Q_HEADS = 8
N_KV_HEADS = 2
GROUP = N_Q_HEADS // N_KV_HEADS
ATTN_W = N_Q_HEADS * HEAD_DIM
KV_W = N_KV_HEADS * HEAD_DIM
CONV_W = D_MODEL - ATTN_W
IN_W = ATTN_W + 2 * KV_W + 2 * CONV_W
WINDOW = 128
BLOCK = 128
CONV_K = 31
CONV_PAD = CONV_K // 2
N_EXPERTS = 32
TOP_K = 4
D_FF = D_MODEL
SWIGLU_LIMIT = 7.0
SWIGLU_ALPHA = 1.702
RMS_EPS = 1e-6
LN_EPS = 1e-5
NEG_INF = -1e30

LANES = 128
ROW_TILE = 512
EXPERT_TILE = 512
SC_CORES = 2
SC_SUBCORES = 16
SC_WORKERS = SC_CORES * SC_SUBCORES
SC_CHUNK = 64
CONV_ROWS = 32
FF_CHUNK = 256
VMEM_LIMIT = 56 * 1024 * 1024


def _rms(x, eps):
    return x * lax.rsqrt(jnp.mean(x * x, axis=-1, keepdims=True) + eps)


def _inproj_kernel(x_ref, g1_ref, w_ref, qg_ref, kg_ref, pq_ref, pk_ref,
                   q_ref, k_ref, v_ref, hc_ref):
    x = x_ref[...]
    n = (_rms(x, RMS_EPS) * g1_ref[...]).astype(BF16)
    proj = jnp.dot(n, w_ref[...], preferred_element_type=F32)
    q = proj[:, :ATTN_W]
    k = proj[:, ATTN_W:ATTN_W + KV_W]
    v = proj[:, ATTN_W + KV_W:ATTN_W + 2 * KV_W]
    a = proj[:, ATTN_W + 2 * KV_W:ATTN_W + 2 * KV_W + CONV_W]
    g = proj[:, ATTN_W + 2 * KV_W + CONV_W:]
    qms = jnp.dot((q * q).astype(BF16), pq_ref[...], preferred_element_type=F32)
    kms = jnp.dot((k * k).astype(BF16), pk_ref[...], preferred_element_type=F32)
    q_ref[...] = (q * lax.rsqrt(qms + RMS_EPS) * qg_ref[...]).astype(BF16)
    kn = k * lax.rsqrt(kms + RMS_EPS) * kg_ref[...]
    lo = lax.broadcasted_iota(jnp.int32, kn.shape, 1) < HEAD_DIM
    ksw = pltpu.roll(kn, HEAD_DIM, 1)
    k_ref[...] = jnp.concatenate([jnp.where(lo, kn, ksw), jnp.where(lo, ksw, kn)], axis=1).astype(BF16)
    vsw = pltpu.roll(v, HEAD_DIM, 1)
    v_ref[...] = jnp.concatenate([jnp.where(lo, v, vsw), jnp.where(lo, vsw, v)], axis=1).astype(BF16)
    hc_ref[...] = a * jax.nn.sigmoid(g)


def _inproj(x2d, g1, w_in_b, qg, kg, pq, pk, tile):
    n = x2d.shape[0]
    const = lambda i: (0, 0)
    row = lambda i: (i, 0)
    return pl.pallas_call(
        _inproj_kernel,
        grid=(n // tile,),
        in_specs=[
            pl.BlockSpec((tile, D_MODEL), row),
            pl.BlockSpec((1, D_MODEL), const),
            pl.BlockSpec((D_MODEL, IN_W), const),
            pl.BlockSpec((1, ATTN_W), const),
            pl.BlockSpec((1, KV_W), const),
            pl.BlockSpec((ATTN_W, ATTN_W), const),
            pl.BlockSpec((KV_W, KV_W), const),
        ],
        out_specs=[
            pl.BlockSpec((tile, ATTN_W), row),
            pl.BlockSpec((tile, 2 * KV_W), row),
            pl.BlockSpec((tile, 2 * KV_W), row),
            pl.BlockSpec((tile, CONV_W), row),
        ],
        out_shape=[
            jax.ShapeDtypeStruct((n, ATTN_W), BF16),
            jax.ShapeDtypeStruct((n, 2 * KV_W), BF16),
            jax.ShapeDtypeStruct((n, 2 * KV_W), BF16),
            jax.ShapeDtypeStruct((n, CONV_W), F32),
        ],
        compiler_params=pltpu.CompilerParams(dimension_semantics=("arbitrary",),
                                             vmem_limit_bytes=VMEM_LIMIT),
        name="inproj",
    )(x2d, g1, w_in_b, qg, kg, pq, pk)


def _alibi_bias():
    qi = np.arange(BLOCK)[:, None]
    kj = np.arange(BLOCK)[None, :]
    dists = [qi + BLOCK - kj, np.abs(qi - kj), kj + BLOCK - qi]
    out = np.zeros((N_KV_HEADS, 3, GROUP * BLOCK, BLOCK), np.float32)
    for h in range(N_KV_HEADS):
        for g in range(GROUP):
            slope = 2.0 ** (-8.0 * (h * GROUP + g + 1) / N_Q_HEADS)
            for p, d in enumerate(dists):
                out[h, p, g * BLOCK:(g + 1) * BLOCK] = np.where(d <= WINDOW, -slope * d, NEG_INF)
    return out


def _attn_kernel(sink_ref, q_ref, kp_ref, kc_ref, kn_ref, vp_ref, vc_ref, vn_ref,
                 km_ref, vm_ref, bias_ref, o_ref, *, n_blocks):
    i = pl.program_id(1)
    lo = lax.broadcasted_iota(jnp.int32, (BLOCK, LANES), 1) < HEAD_DIM
    edge_p = jnp.where(i == 0, NEG_INF, 0.0).astype(F32)
    edge_n = jnp.where(i == n_blocks - 1, NEG_INF, 0.0).astype(F32)
    nt = (((1,), (1,)), ((), ()))
    zero = jnp.zeros((BLOCK, LANES), BF16)
    for h in range(N_KV_HEADS):
        ks = slice(h * LANES, (h + 1) * LANES)
        rows = []
        for j in range(2):
            pair = q_ref[0, :, (2 * h + j) * LANES:(2 * h + j + 1) * LANES]
            rows.append(jnp.where(lo, pair, zero))
            rows.append(jnp.where(lo, zero, pair))
        qs = jnp.concatenate(rows, axis=0)
        s_p = lax.dot_general(qs, kp_ref[0, :, ks], nt, preferred_element_type=F32) + bias_ref[h, 0] + edge_p
        s_c = lax.dot_general(qs, kc_ref[0, :, ks], nt, preferred_element_type=F32) + bias_ref[h, 1]
        s_n = lax.dot_general(qs, kn_ref[0, :, ks], nt, preferred_element_type=F32) + bias_ref[h, 2] + edge_n
        s_m = lax.dot_general(qs, km_ref[:, ks], nt, preferred_element_type=F32)
        sink = jnp.concatenate(
            [jnp.full((BLOCK, 1), sink_ref[h * GROUP + g], F32) for g in range(GROUP)], axis=0)
        m = jnp.maximum(
            jnp.maximum(jnp.max(s_p, axis=-1, keepdims=True), jnp.max(s_c, axis=-1, keepdims=True)),
            jnp.maximum(jnp.max(s_n, axis=-1, keepdims=True), jnp.max(s_m, axis=-1, keepdims=True)))
        m = jnp.maximum(m, sink)
        p_p = jnp.exp(s_p - m)
        p_c = jnp.exp(s_c - m)
        p_n = jnp.exp(s_n - m)
        p_m = jnp.exp(s_m - m)
        denom = (jnp.sum(p_p, axis=-1, keepdims=True) + jnp.sum(p_c, axis=-1, keepdims=True)
                 + jnp.sum(p_n, axis=-1, keepdims=True) + jnp.sum(p_m, axis=-1, keepdims=True)
                 + jnp.exp(sink - m))
        o = (jnp.dot(p_p.astype(BF16), vp_ref[0, :, ks], preferred_element_type=F32)
             + jnp.dot(p_c.astype(BF16), vc_ref[0, :, ks], preferred_element_type=F32)
             + jnp.dot(p_n.astype(BF16), vn_ref[0, :, ks], preferred_element_type=F32)
             + jnp.dot(p_m.astype(BF16), vm_ref[:, ks], preferred_element_type=F32))
        o = o / denom
        for j in range(2):
            even = o[(2 * j) * BLOCK:(2 * j + 1) * BLOCK]
            odd = o[(2 * j + 1) * BLOCK:(2 * j + 2) * BLOCK]
            o_ref[0, :, (2 * h + j) * LANES:(2 * h + j + 1) * LANES] = jnp.where(lo, even, odd).astype(BF16)


def _attention(sink, q, kd, vd, kmeta, vmeta, bias):
    b, s, _ = q.shape
    nb = s // BLOCK
    cur = lambda bi, i: (bi, i, 0)
    prev = lambda bi, i: (bi, jnp.maximum(i - 1, 0), 0)
    nxt = lambda bi, i: (bi, jnp.minimum(i + 1, nb - 1), 0)
    const2 = lambda bi, i: (0, 0)
    kv_blk = (1, BLOCK, 2 * KV_W)
    return pl.pallas_call(
        functools.partial(_attn_kernel, n_blocks=nb),
        grid=(b, nb),
        in_specs=[
            pl.BlockSpec(memory_space=pltpu.SMEM),
            pl.BlockSpec((1, BLOCK, ATTN_W), cur),
            pl.BlockSpec(kv_blk, prev), pl.BlockSpec(kv_blk, cur), pl.BlockSpec(kv_blk, nxt),
            pl.BlockSpec(kv_blk, prev), pl.BlockSpec(kv_blk, cur), pl.BlockSpec(kv_blk, nxt),
            pl.BlockSpec((N_META, 2 * KV_W), const2),
            pl.BlockSpec((N_META, 2 * KV_W), const2),
            pl.BlockSpec(bias.shape, lambda bi, i: (0, 0, 0, 0)),
        ],
        out_specs=pl.BlockSpec((1, BLOCK, ATTN_W), cur),
        out_shape=jax.ShapeDtypeStruct((b, s, ATTN_W), BF16),
        compiler_params=pltpu.CompilerParams(dimension_semantics=("arbitrary", "arbitrary"),
                                             vmem_limit_bytes=VMEM_LIMIT),
        name="attention",
    )(sink, q, kd, kd, kd, vd, vd, vd, kmeta, vmeta, bias)


def _conv_kernel(hc_ref, hm_ref, w_ref, b_ref, lg_ref, lb_ref, o_ref, pad_ref, *, seq):
    tail = pad_ref.shape[0] - (N_META + seq)
    pad_ref[0:N_META, :] = hm_ref[...]
    pad_ref[N_META:N_META + seq, :] = hc_ref[0]
    pad_ref[N_META + seq:, :] = jnp.zeros((tail, CONV_W), F32)
    first = N_META - CONV_PAD
    span = CONV_ROWS + 24

    def chunk(c, carry):
        base = pl.multiple_of(c * CONV_ROWS, CONV_ROWS)
        win = pad_ref[pl.ds(base, CONV_ROWS + 32), :]
        acc = jnp.zeros((CONV_ROWS, CONV_W), F32)
        for sub in range(8):
            shifted = win[sub:sub + span]
            for al in range(4):
                k = 8 * al + sub - first
                if 0 <= k < CONV_K:
                    acc = acc + shifted[8 * al:8 * al + CONV_ROWS] * w_ref[k:k + 1, :]
        y = acc + b_ref[...]
        mu = jnp.mean(y, axis=-1, keepdims=True)
        yc = y - mu
        var = jnp.mean(yc * yc, axis=-1, keepdims=True)
        z = yc * lax.rsqrt(var + LN_EPS) * lg_ref[...] + lb_ref[...]
        o_ref[0, pl.ds(base, CONV_ROWS), :] = (z * jax.nn.sigmoid(z)).astype(BF16)
        return carry

    lax.fori_loop(0, seq // CONV_ROWS, chunk, 0)


def _conv(hc, hc_meta, dw_w, dw_b, ln_g, ln_b):
    b, s, _ = hc.shape
    const = lambda bi: (0, 0)
    return pl.pallas_call(
        functools.partial(_conv_kernel, seq=s),
        grid=(b,),
        in_specs=[
            pl.BlockSpec((1, s, CONV_W), lambda bi: (bi, 0, 0)),
            pl.BlockSpec((N_META, CONV_W), const),
            pl.BlockSpec((CONV_K, CONV_W), const),
            pl.BlockSpec((1, CONV_W), const),
            pl.BlockSpec((1, CONV_W), const),
            pl.BlockSpec((1, CONV_W), const),
        ],
        out_specs=pl.BlockSpec((1, s, CONV_W), lambda bi: (bi, 0, 0)),
        out_shape=jax.ShapeDtypeStruct((b, s, CONV_W), BF16),
        scratch_shapes=[pltpu.VMEM((N_META + s + 32, CONV_W), F32)],
        compiler_params=pltpu.CompilerParams(dimension_semantics=("arbitrary",),
                                             vmem_limit_bytes=VMEM_LIMIT),
        name="conv",
    )(hc, hc_meta, dw_w, dw_b, ln_g, ln_b)


def _outproj_kernel(attn_ref, conv_ref, x_ref, wa_ref, wc_ref, g2_ref, rwh_ref, rwl_ref, rb_ref,
                    tri_ref, h2_ref, xp_ref, route_ref, cnt_ref, run_ref):
    step = pl.program_id(0)

    @pl.when(step == 0)
    def _():
        run_ref[...] = jnp.zeros_like(run_ref)

    mix = (jnp.dot(attn_ref[...], wa_ref[...], preferred_element_type=F32)
           + jnp.dot(conv_ref[...], wc_ref[...], preferred_element_type=F32))
    h2 = x_ref[...] + mix
    h2_ref[...] = h2
    hn = _rms(h2, RMS_EPS) * g2_ref[...]
    hn_hi = hn.astype(BF16)
    half = D_MODEL // 2
    xp_ref[...] = _pack_pair(hn[:, :half], hn[:, half:])
    hn_lo = (hn - hn_hi.astype(F32)).astype(BF16)
    logits = (jnp.dot(hn_hi, rwh_ref[...], preferred_element_type=F32)
              + jnp.dot(hn_lo, rwh_ref[...], preferred_element_type=F32)
              + jnp.dot(hn_hi, rwl_ref[...], preferred_element_type=F32)) + rb_ref[...]
    lane = lax.broadcasted_iota(jnp.int32, logits.shape, 1).astype(F32)
    work = logits
    vals, idxs = [], []
    onehot = jnp.zeros(logits.shape, F32)
    for _ in range(TOP_K):
        m = jnp.max(work, axis=-1, keepdims=True)
        idx = jnp.min(jnp.where(work == m, lane, float(LANES)), axis=-1, keepdims=True)
        sel = lane == idx
        onehot = jnp.where(sel, 1.0, onehot)
        work = jnp.where(sel, -jnp.inf, work)
        vals.append(m)
        idxs.append(idx)
    exps = [jnp.exp(v - vals[0]) for v in vals]
    tot = exps[0] + exps[1] + exps[2] + exps[3]
    before = jnp.dot(tri_ref[...], onehot.astype(BF16), preferred_element_type=F32) + run_ref[...]
    route = jnp.zeros(logits.shape, F32)
    for r in range(TOP_K):
        rank = jnp.sum(jnp.where(lane == idxs[r], before, 0.0), axis=-1, keepdims=True)
        route = jnp.where(lane == r, idxs[r], route)
        route = jnp.where(lane == TOP_K + r, exps[r] / tot, route)
        route = jnp.where(lane == 2 * TOP_K + r, rank, route)
    route_ref[...] = route
    run_ref[...] = run_ref[...] + jnp.sum(onehot, axis=0, keepdims=True)
    cnt_ref[...] = run_ref[...]


def _outproj(attn2d, conv2d, x2d, wa, wc, g2, rwh, rwl, rb, tri):
    n = x2d.shape[0]
    tile = ROW_TILE
    const = lambda i: (0, 0)
    row = lambda i: (i, 0)
    half = D_MODEL // 2
    return pl.pallas_call(
        _outproj_kernel,
        grid=(n // tile,),
        in_specs=[
            pl.BlockSpec((tile, ATTN_W), row),
            pl.BlockSpec((tile, CONV_W), row),
            pl.BlockSpec((tile, D_MODEL), row),
            pl.BlockSpec((ATTN_W, D_MODEL), const),
            pl.BlockSpec((CONV_W, D_MODEL), const),
            pl.BlockSpec((1, D_MODEL), const),
            pl.BlockSpec((D_MODEL, LANES), const),
            pl.BlockSpec((D_MODEL, LANES), const),
            pl.BlockSpec((1, LANES), const),
            pl.BlockSpec((tile, tile), const),
        ],
        out_specs=[
            pl.BlockSpec((tile, D_MODEL), row),
            pl.BlockSpec((tile, half), row),
            pl.BlockSpec((tile, LANES), row),
            pl.BlockSpec((1, LANES), const),
        ],
        out_shape=[
            jax.ShapeDtypeStruct((n, D_MODEL), F32),
            jax.ShapeDtypeStruct((n, half), jnp.uint32),
            jax.ShapeDtypeStruct((n, LANES), F32),
            jax.ShapeDtypeStruct((1, LANES), F32),
        ],
        scratch_shapes=[pltpu.VMEM((1, LANES), F32)],
        compiler_params=pltpu.CompilerParams(dimension_semantics=("arbitrary",),
                                             vmem_limit_bytes=VMEM_LIMIT),
        name="outproj",
    )(attn2d, conv2d, x2d, wa, wc, g2, rwh, rwl, rb, tri)


def _sc_mesh():
    return plsc.VectorSubcoreMesh(core_axis_name="c", subcore_axis_name="s",
                                  num_cores=SC_CORES, num_subcores=SC_SUBCORES)


def _sc_worker():
    return lax.axis_index("s") * SC_CORES + lax.axis_index("c")


def _sc_dispatch(pos3, xp):
    n, half = xp.shape
    per_worker = n // SC_CHUNK // SC_WORKERS

    @functools.partial(
        pl.kernel, mesh=_sc_mesh(),
        out_type=jax.ShapeDtypeStruct((n * TOP_K, half), jnp.uint32),
        scratch_types=[pltpu.VMEM((TOP_K, SC_CHUNK), jnp.int32),
                       pltpu.VMEM((SC_CHUNK, half), jnp.uint32),
                       pltpu.SemaphoreType.DMA],
        name="sc_dispatch")
    def run(pos_hbm, xp_hbm, xs_hbm, idx_v, rows_v, sem):
        first = _sc_worker() * per_worker

        @pl.loop(0, per_worker)
        def _(j):
            c = first + j
            pltpu.sync_copy(pos_hbm.at[c], idx_v)
            pltpu.sync_copy(xp_hbm.at[pl.ds(c * SC_CHUNK, SC_CHUNK)], rows_v)
            copies = [pltpu.async_copy(rows_v, xs_hbm.at[idx_v.at[k]], sem) for k in range(TOP_K)]
            for cp in copies:
                cp.wait()

    return run(pos3, xp)


def _sc_collect(pos3, y):
    rows, half = y.shape
    n = rows // TOP_K
    per_worker = n // SC_CHUNK // SC_WORKERS

    @functools.partial(
        pl.kernel, mesh=_sc_mesh(),
        out_type=jax.ShapeDtypeStruct((TOP_K, n, half), jnp.uint32),
        scratch_types=[pltpu.VMEM((TOP_K, SC_CHUNK), jnp.int32),
                       pltpu.VMEM((SC_CHUNK, half), jnp.uint32),
                       pltpu.SemaphoreType.DMA],
        name="sc_collect")
    def run(pos_hbm, y_hbm, yg_hbm, idx_v, rows_v, sem):
        first = _sc_worker() * per_worker

        @pl.loop(0, per_worker)
        def _(j):
            c = first + j
            pltpu.sync_copy(pos_hbm.at[c], idx_v)
            for k in range(TOP_K):
                pltpu.async_copy(y_hbm.at[idx_v.at[k]], rows_v, sem).wait()
                pltpu.sync_copy(rows_v, yg_hbm.at[k, pl.ds(c * SC_CHUNK, SC_CHUNK)])

    return run(pos3, y)


def _expert_kernel(tile_ref, exp_ref, valid_ref, start_ref, end_ref,
                   xs_ref, wgu_ref, bgu_ref, wd_ref, bd_ref, y_ref):
    w = pl.program_id(0)
    tile = tile_ref[w]
    e = exp_ref[w]
    first = jnp.logical_or(w == 0, tile_ref[jnp.maximum(w - 1, 0)] != tile)

    @pl.when(valid_ref[w] == 1)
    def _():
        half = D_MODEL // 2
        x_lo, x_hi = _unpack_pair(xs_ref[...])
        x_lo = x_lo.astype(BF16)
        x_hi = x_hi.astype(BF16)
        acc = jnp.zeros((EXPERT_TILE, D_MODEL), F32)
        for c in range(D_FF // FF_CHUNK):
            gc = slice(c * FF_CHUNK, (c + 1) * FF_CHUNK)
            uc = slice(D_FF + c * FF_CHUNK, D_FF + (c + 1) * FF_CHUNK)
            gate = (jnp.dot(x_lo, wgu_ref[0, :half, gc].astype(BF16), preferred_element_type=F32)
                    + jnp.dot(x_hi, wgu_ref[0, half:, gc].astype(BF16), preferred_element_type=F32)
                    + bgu_ref[0, :, gc])
            up = (jnp.dot(x_lo, wgu_ref[0, :half, uc].astype(BF16), preferred_element_type=F32)
                  + jnp.dot(x_hi, wgu_ref[0, half:, uc].astype(BF16), preferred_element_type=F32)
                  + bgu_ref[0, :, uc])
            gate = jnp.minimum(gate, SWIGLU_LIMIT)
            up = jnp.clip(up, -SWIGLU_LIMIT, SWIGLU_LIMIT)
            act = gate * jax.nn.sigmoid(SWIGLU_ALPHA * gate) * (up + 1.0)
            acc = acc + jnp.dot(act.astype(BF16), wd_ref[0, gc, :].astype(BF16),
                                preferred_element_type=F32)
        y = acc + bd_ref[0]
        y = _pack_pair(y[:, :half], y[:, half:])
        rows = tile * EXPERT_TILE + lax.broadcasted_iota(jnp.int32, (EXPERT_TILE, 1), 0)
        mine = jnp.logical_and(rows >= start_ref[e], rows < end_ref[e])

        @pl.when(first)
        def _():
            y_ref[...] = jnp.where(mine, y, jnp.uint32(0))

        @pl.when(jnp.logical_not(first))
        def _():
            y_ref[...] = jnp.where(mine, y, y_ref[...])


def _experts(tile_of, expert_of, valid, starts, ends, xs, wgu, bgu, wd, bd):
    rows, half = xs.shape
    n_work = tile_of.shape[0]
    grid_spec = pltpu.PrefetchScalarGridSpec(
        num_scalar_prefetch=5,
        grid=(n_work,),
        in_specs=[
            pl.BlockSpec((EXPERT_TILE, half), lambda w, t, e, v, s, n: (t[w], 0)),
            pl.BlockSpec((1, D_MODEL, 2 * D_FF), lambda w, t, e, v, s, n: (e[w], 0, 0)),
            pl.BlockSpec((1, 1, 2 * D_FF), lambda w, t, e, v, s, n: (e[w], 0, 0)),
            pl.BlockSpec((1, D_FF, D_MODEL), lambda w, t, e, v, s, n: (e[w], 0, 0)),
            pl.BlockSpec((1, 1, D_MODEL), lambda w, t, e, v, s, n: (e[w], 0, 0)),
        ],
        out_specs=pl.BlockSpec((EXPERT_TILE, half), lambda w, t, e, v, s, n: (t[w], 0)),
    )
    return pl.pallas_call(
        _expert_kernel,
        grid_spec=grid_spec,
        out_shape=jax.ShapeDtypeStruct((rows, half), jnp.uint32),
        compiler_params=pltpu.CompilerParams(dimension_semantics=("arbitrary",),
                                             vmem_limit_bytes=VMEM_LIMIT),
        name="experts",
    )(tile_of, expert_of, valid, starts, ends, xs, wgu, bgu, wd, bd)


def _unpack_pair(packed):
    lo = lax.bitcast_convert_type(packed << 16, F32)
    hi = lax.bitcast_convert_type(packed & jnp.uint32(0xFFFF0000), F32)
    return lo, hi


def _pack_pair(lo, hi):
    lo_bits = lax.bitcast_convert_type(lo.astype(BF16).astype(F32), jnp.uint32)
    hi_bits = lax.bitcast_convert_type(hi.astype(BF16).astype(F32), jnp.uint32)
    return (lo_bits >> 16) | (hi_bits & jnp.uint32(0xFFFF0000))


def _combine_kernel(yg_ref, h2_ref, route_ref, o_ref):
    half = D_MODEL // 2
    route = route_ref[...]
    lo_sum = h2_ref[:, :half]
    hi_sum = h2_ref[:, half:]
    for k in range(TOP_K):
        gate = route[:, TOP_K + k:TOP_K + k + 1]
        lo, hi = _unpack_pair(yg_ref[k])
        lo_sum = lo_sum + gate * lo
        hi_sum = hi_sum + gate * hi
    o_ref[:, :half] = lo_sum
    o_ref[:, half:] = hi_sum


def _combine(yg, h2, route):
    n = h2.shape[0]
    half = D_MODEL // 2
    row = lambda i: (i, 0)
    return pl.pallas_call(
        _combine_kernel,
        grid=(n // ROW_TILE,),
        in_specs=[
            pl.BlockSpec((TOP_K, ROW_TILE, half), lambda i: (0, i, 0)),
            pl.BlockSpec((ROW_TILE, D_MODEL), row),
            pl.BlockSpec((ROW_TILE, LANES), row),
        ],
        out_specs=pl.BlockSpec((ROW_TILE, D_MODEL), row),
        out_shape=jax.ShapeDtypeStruct((n, D_MODEL), F32),
        compiler_params=pltpu.CompilerParams(dimension_semantics=("arbitrary",),
                                             vmem_limit_bytes=VMEM_LIMIT),
        name="combine",
    )(yg, h2, route)


def _work_schedule(counts, n_rows):
    n_tiles = n_rows // EXPERT_TILE
    n_work = n_tiles + N_EXPERTS - 1
    ends = jnp.cumsum(counts)
    starts = ends - counts
    tile_lo = jnp.arange(n_tiles, dtype=jnp.int32) * EXPERT_TILE
    first_e = jnp.searchsorted(ends, tile_lo, side="right").astype(jnp.int32)
    last_e = (jnp.searchsorted(starts, tile_lo + (EXPERT_TILE - 1), side="right") - 1).astype(jnp.int32)
    per_tile = last_e - first_e + 1
    w_end = jnp.cumsum(per_tile)
    w_start = w_end - per_tile
    total = w_end[-1]
    w = jnp.arange(n_work, dtype=jnp.int32)
    wc = jnp.minimum(w, total - 1)
    tile_of = (jnp.searchsorted(w_end, wc, side="right")).astype(jnp.int32)
    expert_of = first_e[tile_of] + (wc - w_start[tile_of])
    valid = (w < total).astype(jnp.int32)
    return tile_of, expert_of.astype(jnp.int32), valid, starts.astype(jnp.int32), ends.astype(jnp.int32)


def kernel(x, meta_tokens, norm1_g, w_in, q_norm_g, k_norm_g, attn_sink, dw_w, dw_b, conv_ln_g,
           conv_ln_b, w_out, norm2_g, router_w, router_b, w_gate_up, b_gate_up, w_down, b_down):
    assert norm1_g.shape[0] == 1, "single-layer trunk: meta-token query rows are not materialised"
    b, s, d = x.shape
    n = b * s
    x2d = x.reshape(n, d)

    scale = HEAD_DIM ** -0.5
    qg = (jnp.tile(q_norm_g[0], N_Q_HEADS) * scale).reshape(1, ATTN_W)
    kg = jnp.tile(k_norm_g[0], N_KV_HEADS).reshape(1, KV_W)
    head_of = np.arange(ATTN_W) // HEAD_DIM
    pq = jnp.asarray((head_of[:, None] == head_of[None, :]) / HEAD_DIM, BF16)
    pk = pq[:KV_W, :KV_W]
    g1 = norm1_g[0].reshape(1, d)
    w_in_b = w_in[0].astype(BF16)

    q, kd, vd, hc = _inproj(x2d, g1, w_in_b, qg, kg, pq, pk, ROW_TILE)
    _, kmeta, vmeta, hc_meta = _inproj(meta_tokens, g1, w_in_b, qg, kg, pq, pk, N_META)

    attn = _attention(attn_sink[0], q.reshape(b, s, ATTN_W), kd.reshape(b, s, 2 * KV_W),
                      vd.reshape(b, s, 2 * KV_W), kmeta, vmeta, jnp.asarray(_alibi_bias()))
    conv = _conv(hc.reshape(b, s, CONV_W), hc_meta, dw_w[0].reshape(CONV_K, CONV_W),
                 dw_b[0].reshape(1, CONV_W), conv_ln_g[0].reshape(1, CONV_W),
                 conv_ln_b[0].reshape(1, CONV_W))

    w_out_b = w_out[0].astype(BF16)
    rw = jnp.pad(router_w[0], ((0, 0), (0, LANES - N_EXPERTS)))
    rwh = rw.astype(BF16)
    rwl = (rw - rwh.astype(F32)).astype(BF16)
    rb = jnp.pad(router_b[0], (0, LANES - N_EXPERTS), constant_values=NEG_INF).reshape(1, LANES)
    tri = jnp.asarray(np.tril(np.ones((ROW_TILE, ROW_TILE), np.float32), -1), BF16)
    h2, xp, route, counts = _outproj(attn.reshape(n, ATTN_W), conv.reshape(n, CONV_W), x2d,
                                     w_out_b[:ATTN_W], w_out_b[ATTN_W:], norm2_g[0].reshape(1, d),
                                     rwh, rwl, rb, tri)

    counts_i = counts[0, :N_EXPERTS].astype(jnp.int32)
    idx = route[:, :TOP_K].astype(jnp.int32)
    rank = route[:, 2 * TOP_K:3 * TOP_K].astype(jnp.int32)
    tile_of, expert_of, valid, starts, ends = _work_schedule(counts_i, n * TOP_K)
    pos = starts[idx] + rank
    pos3 = pos.reshape(n // SC_CHUNK, SC_CHUNK, TOP_K).transpose(0, 2, 1)

    xs = _sc_dispatch(pos3, xp)
    y = _experts(tile_of, expert_of, valid, starts, ends, xs, w_gate_up[0],
                 b_gate_up[0].reshape(N_EXPERTS, 1, 2 * D_FF), w_down[0],
                 b_down[0].reshape(N_EXPERTS, 1, D_MODEL))
    out = _combine(_sc_collect(pos3, y), h2, route)
    return out.reshape(b, s, d)
```

```python
import functools

import numpy as np
import jax
import jax.numpy as jnp
from jax import lax
from jax.experimental import pallas as pl
from jax.experimental.pallas import tpu as pltpu
from jax.experimental.pallas import tpu_sc as plsc

F32 = jnp.float32
BF16 = jnp.bfloat16

D_MODEL = 1024
N_META = 16
HEAD_DIM = 64
N_Q_HEADS = 8
N_KV_HEADS = 2
GROUP = N_Q_HEADS // N_KV_HEADS
ATTN_W = N_Q_HEADS * HEAD_DIM
KV_W = N_KV_HEADS * HEAD_DIM
CONV_W = D_MODEL - ATTN_W
IN_W = ATTN_W + 2 * KV_W + 2 * CONV_W
WINDOW = 128
BLOCK = 128
CONV_K = 31
CONV_PAD = CONV_K // 2
N_EXPERTS = 32
TOP_K = 4
D_FF = D_MODEL
SWIGLU_LIMIT = 7.0
SWIGLU_ALPHA = 1.702
RMS_EPS = 1e-6
LN_EPS = 1e-5
NEG_INF = -1e30

LANES = 128
ROW_TILE = 512
EXPERT_TILE = 512
SC_CORES = 2
SC_SUBCORES = 16
SC_WORKERS = SC_CORES * SC_SUBCORES
SC_CHUNK = 64
CONV_ROWS = 128
FF_CHUNK = 256
VMEM_LIMIT = 56 * 1024 * 1024


def _rms(x, eps):
    return x * lax.rsqrt(jnp.mean(x * x, axis=-1, keepdims=True) + eps)


def _inproj_kernel(x_ref, g1_ref, w_ref, qg_ref, kg_ref, pq_ref, pk_ref,
                   q_ref, k_ref, v_ref, hc_ref):
    x = x_ref[...]
    n = (_rms(x, RMS_EPS) * g1_ref[...]).astype(BF16)
    proj = jnp.dot(n, w_ref[...], preferred_element_type=F32)
    q = proj[:, :ATTN_W]
    k = proj[:, ATTN_W:ATTN_W + KV_W]
    v = proj[:, ATTN_W + KV_W:ATTN_W + 2 * KV_W]
    a = proj[:, ATTN_W + 2 * KV_W:ATTN_W + 2 * KV_W + CONV_W]
    g = proj[:, ATTN_W + 2 * KV_W + CONV_W:]
    qms = jnp.dot((q * q).astype(BF16), pq_ref[...], preferred_element_type=F32)
    kms = jnp.dot((k * k).astype(BF16), pk_ref[...], preferred_element_type=F32)
    q_ref[...] = (q * lax.rsqrt(qms + RMS_EPS) * qg_ref[...]).astype(BF16)
    kn = k * lax.rsqrt(kms + RMS_EPS) * kg_ref[...]
    lo = lax.broadcasted_iota(jnp.int32, kn.shape, 1) < HEAD_DIM
    ksw = pltpu.roll(kn, HEAD_DIM, 1)
    k_ref[...] = jnp.concatenate([jnp.where(lo, kn, ksw), jnp.where(lo, ksw, kn)], axis=1).astype(BF16)
    vsw = pltpu.roll(v, HEAD_DIM, 1)
    v_ref[...] = jnp.concatenate([jnp.where(lo, v, vsw), jnp.where(lo, vsw, v)], axis=1).astype(BF16)
    hc_ref[...] = a * jax.nn.sigmoid(g)


def _inproj(x2d, g1, w_in_b, qg, kg, pq, pk, tile):
    n = x2d.shape[0]
    const = lambda i: (0, 0)
    row = lambda i: (i, 0)
    return pl.pallas_call(
        _inproj_kernel,
        grid=(n // tile,),
        in_specs=[
            pl.BlockSpec((tile, D_MODEL), row),
            pl.BlockSpec((1, D_MODEL), const),
            pl.BlockSpec((D_MODEL, IN_W), const),
            pl.BlockSpec((1, ATTN_W), const),
            pl.BlockSpec((1, KV_W), const),
            pl.BlockSpec((ATTN_W, ATTN_W), const),
            pl.BlockSpec((KV_W, KV_W), const),
        ],
        out_specs=[
            pl.BlockSpec((tile, ATTN_W), row),
            pl.BlockSpec((tile, 2 * KV_W), row),
            pl.BlockSpec((tile, 2 * KV_W), row),
            pl.BlockSpec((tile, CONV_W), row),
        ],
        out_shape=[
            jax.ShapeDtypeStruct((n, ATTN_W), BF16),
            jax.ShapeDtypeStruct((n, 2 * KV_W), BF16),
            jax.ShapeDtypeStruct((n, 2 * KV_W), BF16),
            jax.ShapeDtypeStruct((n, CONV_W), F32),
        ],
        compiler_params=pltpu.CompilerParams(dimension_semantics=("arbitrary",),
                                             vmem_limit_bytes=VMEM_LIMIT),
        name="inproj",
    )(x2d, g1, w_in_b, qg, kg, pq, pk)


def _alibi_bias():
    qi = np.arange(BLOCK)[:, None]
    kj = np.arange(BLOCK)[None, :]
    dists = [qi + BLOCK - kj, np.abs(qi - kj), kj + BLOCK - qi]
    out = np.zeros((N_KV_HEADS, 3, GROUP * BLOCK, BLOCK), np.float32)
    for h in range(N_KV_HEADS):
        for g in range(GROUP):
            slope = 2.0 ** (-8.0 * (h * GROUP + g + 1) / N_Q_HEADS)
            for p, d in enumerate(dists):
                out[h, p, g * BLOCK:(g + 1) * BLOCK] = np.where(d <= WINDOW, -slope * d, NEG_INF)
    return out


def _attn_kernel(sink_ref, q_ref, kp_ref, kc_ref, kn_ref, vp_ref, vc_ref, vn_ref,
                 km_ref, vm_ref, bias_ref, o_ref, *, n_blocks):
    i = pl.program_id(1)
    lo = lax.broadcasted_iota(jnp.int32, (BLOCK, LANES), 1) < HEAD_DIM
    edge_p = jnp.where(i == 0, NEG_INF, 0.0).astype(F32)
    edge_n = jnp.where(i == n_blocks - 1, NEG_INF, 0.0).astype(F32)
    nt = (((1,), (1,)), ((), ()))
    zero = jnp.zeros((BLOCK, LANES), BF16)
    for h in range(N_KV_HEADS):
        ks = slice(h * LANES, (h + 1) * LANES)
        rows = []
        for j in range(2):
            pair = q_ref[0, :, (2 * h + j) * LANES:(2 * h + j + 1) * LANES]
            rows.append(jnp.where(lo, pair, zero))
            rows.append(jnp.where(lo, zero, pair))
        qs = jnp.concatenate(rows, axis=0)
        s_p = lax.dot_general(qs, kp_ref[0, :, ks], nt, preferred_element_type=F32) + bias_ref[h, 0] + edge_p
        s_c = lax.dot_general(qs, kc_ref[0, :, ks], nt, preferred_element_type=F32) + bias_ref[h, 1]
        s_n = lax.dot_general(qs, kn_ref[0, :, ks], nt, preferred_element_type=F32) + bias_ref[h, 2] + edge_n
        s_m = lax.dot_general(qs, km_ref[:, ks], nt, preferred_element_type=F32)
        sink = jnp.concatenate(
            [jnp.full((BLOCK, 1), sink_ref[h * GROUP + g], F32) for g in range(GROUP)], axis=0)
        m = jnp.maximum(
            jnp.maximum(jnp.max(s_p, axis=-1, keepdims=True), jnp.max(s_c, axis=-1, keepdims=True)),
            jnp.maximum(jnp.max(s_n, axis=-1, keepdims=True), jnp.max(s_m, axis=-1, keepdims=True)))
        m = jnp.maximum(m, sink)
        p_p = jnp.exp(s_p - m)
        p_c = jnp.exp(s_c - m)
        p_n = jnp.exp(s_n - m)
        p_m = jnp.exp(s_m - m)
        denom = (jnp.sum(p_p, axis=-1, keepdims=True) + jnp.sum(p_c, axis=-1, keepdims=True)
                 + jnp.sum(p_n, axis=-1, keepdims=True) + jnp.sum(p_m, axis=-1, keepdims=True)
                 + jnp.exp(sink - m))
        o = (jnp.dot(p_p.astype(BF16), vp_ref[0, :, ks], preferred_element_type=F32)
             + jnp.dot(p_c.astype(BF16), vc_ref[0, :, ks], preferred_element_type=F32)
             + jnp.dot(p_n.astype(BF16), vn_ref[0, :, ks], preferred_element_type=F32)
             + jnp.dot(p_m.astype(BF16), vm_ref[:, ks], preferred_element_type=F32))
        o = o / denom
        for j in range(2):
            even = o[(2 * j) * BLOCK:(2 * j + 1) * BLOCK]
            odd = o[(2 * j + 1) * BLOCK:(2 * j + 2) * BLOCK]
            o_ref[0, :, (2 * h + j) * LANES:(2 * h + j + 1) * LANES] = jnp.where(lo, even, odd).astype(BF16)


def _attention(sink, q, kd, vd, kmeta, vmeta, bias):
    b, s, _ = q.shape
    nb = s // BLOCK
    cur = lambda bi, i: (bi, i, 0)
    prev = lambda bi, i: (bi, jnp.maximum(i - 1, 0), 0)
    nxt = lambda bi, i: (bi, jnp.minimum(i + 1, nb - 1), 0)
    const2 = lambda bi, i: (0, 0)
    kv_blk = (1, BLOCK, 2 * KV_W)
    return pl.pallas_call(
        functools.partial(_attn_kernel, n_blocks=nb),
        grid=(b, nb),
        in_specs=[
            pl.BlockSpec(memory_space=pltpu.SMEM),
            pl.BlockSpec((1, BLOCK, ATTN_W), cur),
            pl.BlockSpec(kv_blk, prev), pl.BlockSpec(kv_blk, cur), pl.BlockSpec(kv_blk, nxt),
            pl.BlockSpec(kv_blk, prev), pl.BlockSpec(kv_blk, cur), pl.BlockSpec(kv_blk, nxt),
            pl.BlockSpec((N_META, 2 * KV_W), const2),
            pl.BlockSpec((N_META, 2 * KV_W), const2),
            pl.BlockSpec(bias.shape, lambda bi, i: (0, 0, 0, 0)),
        ],
        out_specs=pl.BlockSpec((1, BLOCK, ATTN_W), cur),
        out_shape=jax.ShapeDtypeStruct((b, s, ATTN_W), BF16),
        compiler_params=pltpu.CompilerParams(dimension_semantics=("arbitrary", "arbitrary"),
                                             vmem_limit_bytes=VMEM_LIMIT),
        name="attention",
    )(sink, q, kd, kd, kd, vd, vd, vd, kmeta, vmeta, bias)


def _conv_kernel(hc_ref, hm_ref, w_ref, b_ref, lg_ref, lb_ref, o_ref, pad_ref, y_ref, *, seq):
    tail = pad_ref.shape[0] - (N_META + seq)
    pad_ref[0:N_META, :] = hm_ref[...]
    pad_ref[N_META:N_META + seq, :] = hc_ref[0]
    pad_ref[N_META + seq:, :] = jnp.zeros((tail, CONV_W), F32)
    first = N_META - CONV_PAD
    span = CONV_ROWS + 8

    def chunk(c, carry):
        base = pl.multiple_of(c * CONV_ROWS, CONV_ROWS)
        for lt in range(CONV_W // LANES):
            ls = slice(lt * LANES, (lt + 1) * LANES)
            win = pad_ref[pl.ds(base, CONV_ROWS + 32), ls]
            acc = None
            for sub in range(8):
                part = None
                for al in range(4):
                    k = 8 * al + sub - first
                    if 0 <= k < CONV_K:
                        term = win[8 * al:8 * al + span] * w_ref[k:k + 1, ls]
                        part = term if part is None else part + term
                if sub:
                    part = pltpu.roll(part, span - sub, 0)
                acc = part[:CONV_ROWS] if acc is None else acc + part[:CONV_ROWS]
            y_ref[:, ls] = acc
        y = y_ref[...] + b_ref[...]
        mu = jnp.mean(y, axis=-1, keepdims=True)
        yc = y - mu
        var = jnp.mean(yc * yc, axis=-1, keepdims=True)
        z = yc * lax.rsqrt(var + LN_EPS) * lg_ref[...] + lb_ref[...]
        o_ref[0, pl.ds(base, CONV_ROWS), :] = (z * jax.nn.sigmoid(z)).astype(BF16)
        return carry

    lax.fori_loop(0, seq // CONV_ROWS, chunk, 0)


def _conv(hc, hc_meta, dw_w, dw_b, ln_g, ln_b):
    b, s, _ = hc.shape
    const = lambda bi: (0, 0)
    return pl.pallas_call(
        functools.partial(_conv_kernel, seq=s),
        grid=(b,),
        in_specs=[
            pl.BlockSpec((1, s, CONV_W), lambda bi: (bi, 0, 0)),
            pl.BlockSpec((N_META, CONV_W), const),
            pl.BlockSpec((CONV_K, CONV_W), const),
            pl.BlockSpec((1, CONV_W), const),
            pl.BlockSpec((1, CONV_W), const),
            pl.BlockSpec((1, CONV_W), const),
        ],
        out_specs=pl.BlockSpec((1, s, CONV_W), lambda bi: (bi, 0, 0)),
        out_shape=jax.ShapeDtypeStruct((b, s, CONV_W), BF16),
        scratch_shapes=[pltpu.VMEM((N_META + s + 32, CONV_W), F32),
                        pltpu.VMEM((CONV_ROWS, CONV_W), F32)],
        compiler_params=pltpu.CompilerParams(dimension_semantics=("arbitrary",),
                                             vmem_limit_bytes=VMEM_LIMIT),
        name="conv",
    )(hc, hc_meta, dw_w, dw_b, ln_g, ln_b)


def _outproj_kernel(attn_ref, conv_ref, x_ref, wa_ref, wc_ref, g2_ref, rwh_ref, rwl_ref, rb_ref,
                    tri_ref, h2_ref, xp_ref, route_ref, cnt_ref, run_ref):
    step = pl.program_id(0)

    @pl.when(step == 0)
    def _():
        run_ref[...] = jnp.zeros_like(run_ref)

    mix = (jnp.dot(attn_ref[...], wa_ref[...], preferred_element_type=F32)
           + jnp.dot(conv_ref[...], wc_ref[...], preferred_element_type=F32))
    h2 = x_ref[...] + mix
    h2_ref[...] = h2
    hn = _rms(h2, RMS_EPS) * g2_ref[...]
    hn_hi = hn.astype(BF16)
    half = D_MODEL // 2
    xp_ref[...] = _pack_pair(hn[:, :half], hn[:, half:])
    hn_lo = (hn - hn_hi.astype(F32)).astype(BF16)
    logits = (jnp.dot(hn_hi, rwh_ref[...], preferred_element_type=F32)
              + jnp.dot(hn_lo, rwh_ref[...], preferred_element_type=F32)
              + jnp.dot(hn_hi, rwl_ref[...], preferred_element_type=F32)) + rb_ref[...]
    lane = lax.broadcasted_iota(jnp.int32, logits.shape, 1).astype(F32)
    work = logits
    vals, idxs = [], []
    onehot = jnp.zeros(logits.shape, F32)
    for _ in range(TOP_K):
        m = jnp.max(work, axis=-1, keepdims=True)
        idx = jnp.min(jnp.where(work == m, lane, float(LANES)), axis=-1, keepdims=True)
        sel = lane == idx
        onehot = jnp.where(sel, 1.0, onehot)
        work = jnp.where(sel, -jnp.inf, work)
        vals.append(m)
        idxs.append(idx)
    exps = [jnp.exp(v - vals[0]) for v in vals]
    tot = exps[0] + exps[1] + exps[2] + exps[3]
    before = jnp.dot(tri_ref[...], onehot.astype(BF16), preferred_element_type=F32) + run_ref[...]
    route = jnp.zeros(logits.shape, F32)
    for r in range(TOP_K):
        rank = jnp.sum(jnp.where(lane == idxs[r], before, 0.0), axis=-1, keepdims=True)
        route = jnp.where(lane == r, idxs[r], route)
        route = jnp.where(lane == TOP_K + r, exps[r] / tot, route)
        route = jnp.where(lane == 2 * TOP_K + r, rank, route)
    route_ref[...] = route
    run_ref[...] = run_ref[...] + jnp.sum(onehot, axis=0, keepdims=True)
    cnt_ref[...] = run_ref[...]


def _outproj(attn2d, conv2d, x2d, wa, wc, g2, rwh, rwl, rb, tri):
    n = x2d.shape[0]
    tile = ROW_TILE
    const = lambda i: (0, 0)
    row = lambda i: (i, 0)
    half = D_MODEL // 2
    return pl.pallas_call(
        _outproj_kernel,
        grid=(n // tile,),
        in_specs=[
            pl.BlockSpec((tile, ATTN_W), row),
            pl.BlockSpec((tile, CONV_W), row),
            pl.BlockSpec((tile, D_MODEL), row),
            pl.BlockSpec((ATTN_W, D_MODEL), const),
            pl.BlockSpec((CONV_W, D_MODEL), const),
            pl.BlockSpec((1, D_MODEL), const),
            pl.BlockSpec((D_MODEL, LANES), const),
            pl.BlockSpec((D_MODEL, LANES), const),
            pl.BlockSpec((1, LANES), const),
            pl.BlockSpec((tile, tile), const),
        ],
        out_specs=[
            pl.BlockSpec((tile, D_MODEL), row),
            pl.BlockSpec((tile, half), row),
            pl.BlockSpec((tile, LANES), row),
            pl.BlockSpec((1, LANES), const),
        ],
        out_shape=[
            jax.ShapeDtypeStruct((n, D_MODEL), F32),
            jax.ShapeDtypeStruct((n, half), jnp.uint32),
            jax.ShapeDtypeStruct((n, LANES), F32),
            jax.ShapeDtypeStruct((1, LANES), F32),
        ],
        scratch_shapes=[pltpu.VMEM((1, LANES), F32)],
        compiler_params=pltpu.CompilerParams(dimension_semantics=("arbitrary",),
                                             vmem_limit_bytes=VMEM_LIMIT),
        name="outproj",
    )(attn2d, conv2d, x2d, wa, wc, g2, rwh, rwl, rb, tri)


def _sc_mesh():
    return plsc.VectorSubcoreMesh(core_axis_name="c", subcore_axis_name="s",
                                  num_cores=SC_CORES, num_subcores=SC_SUBCORES)


def _sc_worker():
    return lax.axis_index("s") * SC_CORES + lax.axis_index("c")


def _sc_dispatch(pos3, xp):
    n, half = xp.shape
    per_worker = n // SC_CHUNK // SC_WORKERS

    @functools.partial(
        pl.kernel, mesh=_sc_mesh(),
        out_type=jax.ShapeDtypeStruct((n * TOP_K, half), jnp.uint32),
        scratch_types=[pltpu.VMEM((TOP_K, SC_CHUNK), jnp.int32),
                       pltpu.VMEM((SC_CHUNK, half), jnp.uint32),
                       pltpu.SemaphoreType.DMA],
        name="sc_dispatch")
    def run(pos_hbm, xp_hbm, xs_hbm, idx_v, rows_v, sem):
        first = _sc_worker() * per_worker

        @pl.loop(0, per_worker)
        def _(j):
            c = first + j
            pltpu.sync_copy(pos_hbm.at[c], idx_v)
            pltpu.sync_copy(xp_hbm.at[pl.ds(c * SC_CHUNK, SC_CHUNK)], rows_v)
            copies = [pltpu.async_copy(rows_v, xs_hbm.at[idx_v.at[k]], sem) for k in range(TOP_K)]
            for cp in copies:
                cp.wait()

    return run(pos3, xp)


def _sc_collect(pos3, y):
    rows, half = y.shape
    n = rows // TOP_K
    per_worker = n // SC_CHUNK // SC_WORKERS

    @functools.partial(
        pl.kernel, mesh=_sc_mesh(),
        out_type=jax.ShapeDtypeStruct((TOP_K, n, half), jnp.uint32),
        scratch_types=[pltpu.VMEM((TOP_K, SC_CHUNK), jnp.int32),
                       pltpu.VMEM((SC_CHUNK, half), jnp.uint32),
                       pltpu.SemaphoreType.DMA],
        name="sc_collect")
    def run(pos_hbm, y_hbm, yg_hbm, idx_v, rows_v, sem):
        first = _sc_worker() * per_worker

        @pl.loop(0, per_worker)
        def _(j):
            c = first + j
            pltpu.sync_copy(pos_hbm.at[c], idx_v)
            for k in range(TOP_K):
                pltpu.async_copy(y_hbm.at[idx_v.at[k]], rows_v, sem).wait()
                pltpu.sync_copy(rows_v, yg_hbm.at[k, pl.ds(c * SC_CHUNK, SC_CHUNK)])

    return run(pos3, y)


def _expert_kernel(tile_ref, exp_ref, valid_ref, start_ref, end_ref,
                   xs_ref, wgu_ref, bgu_ref, wd_ref, bd_ref, y_ref):
    w = pl.program_id(0)
    tile = tile_ref[w]
    e = exp_ref[w]
    first = jnp.logical_or(w == 0, tile_ref[jnp.maximum(w - 1, 0)] != tile)

    @pl.when(valid_ref[w] == 1)
    def _():
        half = D_MODEL // 2
        x_lo, x_hi = _unpack_pair(xs_ref[...])
        x_lo = x_lo.astype(BF16)
        x_hi = x_hi.astype(BF16)
        acc = jnp.zeros((EXPERT_TILE, D_MODEL), F32)
        for c in range(D_FF // FF_CHUNK):
            gc = slice(c * FF_CHUNK, (c + 1) * FF_CHUNK)
            uc = slice(D_FF + c * FF_CHUNK, D_FF + (c + 1) * FF_CHUNK)
            gate = (jnp.dot(x_lo, wgu_ref[0, :half, gc].astype(BF16), preferred_element_type=F32)
                    + jnp.dot(x_hi, wgu_ref[0, half:, gc].astype(BF16), preferred_element_type=F32)
                    + bgu_ref[0, :, gc])
            up = (jnp.dot(x_lo, wgu_ref[0, :half, uc].astype(BF16), preferred_element_type=F32)
                  + jnp.dot(x_hi, wgu_ref[0, half:, uc].astype(BF16), preferred_element_type=F32)
                  + bgu_ref[0, :, uc])
            gate = jnp.minimum(gate, SWIGLU_LIMIT)
            up = jnp.clip(up, -SWIGLU_LIMIT, SWIGLU_LIMIT)
            act = gate * jax.nn.sigmoid(SWIGLU_ALPHA * gate) * (up + 1.0)
            acc = acc + jnp.dot(act.astype(BF16), wd_ref[0, gc, :].astype(BF16),
                                preferred_element_type=F32)
        y = acc + bd_ref[0]
        y = _pack_pair(y[:, :half], y[:, half:])
        rows = tile * EXPERT_TILE + lax.broadcasted_iota(jnp.int32, (EXPERT_TILE, 1), 0)
        mine = jnp.logical_and(rows >= start_ref[e], rows < end_ref[e])

        @pl.when(first)
        def _():
            y_ref[...] = jnp.where(mine, y, jnp.uint32(0))

        @pl.when(jnp.logical_not(first))
        def _():
            y_ref[...] = jnp.where(mine, y, y_ref[...])


def _experts(tile_of, expert_of, valid, starts, ends, xs, wgu, bgu, wd, bd):
    rows, half = xs.shape
    n_work = tile_of.shape[0]
    grid_spec = pltpu.PrefetchScalarGridSpec(
        num_scalar_prefetch=5,
        grid=(n_work,),
        in_specs=[
            pl.BlockSpec((EXPERT_TILE, half), lambda w, t, e, v, s, n: (t[w], 0)),
            pl.BlockSpec((1, D_MODEL, 2 * D_FF), lambda w, t, e, v, s, n: (e[w], 0, 0)),
            pl.BlockSpec((1, 1, 2 * D_FF), lambda w, t, e, v, s, n: (e[w], 0, 0)),
            pl.BlockSpec((1, D_FF, D_MODEL), lambda w, t, e, v, s, n: (e[w], 0, 0)),
            pl.BlockSpec((1, 1, D_MODEL), lambda w, t, e, v, s, n: (e[w], 0, 0)),
        ],
        out_specs=pl.BlockSpec((EXPERT_TILE, half), lambda w, t, e, v, s, n: (t[w], 0)),
    )
    return pl.pallas_call(
        _expert_kernel,
        grid_spec=grid_spec,
        out_shape=jax.ShapeDtypeStruct((rows, half), jnp.uint32),
        compiler_params=pltpu.CompilerParams(dimension_semantics=("arbitrary",),
                                             vmem_limit_bytes=VMEM_LIMIT),
        name="experts",
    )(tile_of, expert_of, valid, starts, ends, xs, wgu, bgu, wd, bd)


def _unpack_pair(packed):
    lo = lax.bitcast_convert_type(packed << 16, F32)
    hi = lax.bitcast_convert_type(packed & jnp.uint32(0xFFFF0000), F32)
    return lo, hi


def _pack_pair(lo, hi):
    lo_bits = lax.bitcast_convert_type(lo.astype(BF16).astype(F32), jnp.uint32)
    hi_bits = lax.bitcast_convert_type(hi.astype(BF16).astype(F32), jnp.uint32)
    return (lo_bits >> 16) | (hi_bits & jnp.uint32(0xFFFF0000))


def _combine_kernel(yg_ref, h2_ref, route_ref, o_ref):
    half = D_MODEL // 2
    route = route_ref[...]
    lo_sum = h2_ref[:, :half]
    hi_sum = h2_ref[:, half:]
    for k in range(TOP_K):
        gate = route[:, TOP_K + k:TOP_K + k + 1]
        lo, hi = _unpack_pair(yg_ref[k])
        lo_sum = lo_sum + gate * lo
        hi_sum = hi_sum + gate * hi
    o_ref[:, :half] = lo_sum
    o_ref[:, half:] = hi_sum


def _combine(yg, h2, route):
    n = h2.shape[0]
    half = D_MODEL // 2
    row = lambda i: (i, 0)
    return pl.pallas_call(
        _combine_kernel,
        grid=(n // ROW_TILE,),
        in_specs=[
            pl.BlockSpec((TOP_K, ROW_TILE, half), lambda i: (0, i, 0)),
            pl.BlockSpec((ROW_TILE, D_MODEL), row),
            pl.BlockSpec((ROW_TILE, LANES), row),
        ],
        out_specs=pl.BlockSpec((ROW_TILE, D_MODEL), row),
        out_shape=jax.ShapeDtypeStruct((n, D_MODEL), F32),
        compiler_params=pltpu.CompilerParams(dimension_semantics=("arbitrary",),
                                             vmem_limit_bytes=VMEM_LIMIT),
        name="combine",
    )(yg, h2, route)


def _work_schedule(counts, n_rows):
    n_tiles = n_rows // EXPERT_TILE
    n_work = n_tiles + N_EXPERTS - 1
    ends = jnp.cumsum(counts)
    starts = ends - counts
    count_le = lambda table, q: jnp.sum((table[None, :] <= q[:, None]).astype(jnp.int32), axis=1)
    tile_lo = jnp.arange(n_tiles, dtype=jnp.int32) * EXPERT_TILE
    first_e = count_le(ends, tile_lo)
    last_e = count_le(starts, tile_lo + (EXPERT_TILE - 1)) - 1
    per_tile = last_e - first_e + 1
    w_end = jnp.cumsum(per_tile)
    total = w_end[-1]
    w = jnp.arange(n_work, dtype=jnp.int32)
    wc = jnp.minimum(w, total - 1)
    tile_of = count_le(w_end, wc)
    in_tile = tile_of[:, None] == jnp.arange(n_tiles, dtype=jnp.int32)[None, :]
    expert_of = wc + jnp.sum(jnp.where(in_tile, (first_e - (w_end - per_tile))[None, :], 0), axis=1)
    valid = (w < total).astype(jnp.int32)
    return tile_of, expert_of.astype(jnp.int32), valid, starts.astype(jnp.int32), ends.astype(jnp.int32)


def kernel(x, meta_tokens, norm1_g, w_in, q_norm_g, k_norm_g, attn_sink, dw_w, dw_b, conv_ln_g,
           conv_ln_b, w_out, norm2_g, router_w, router_b, w_gate_up, b_gate_up, w_down, b_down):
    assert norm1_g.shape[0] == 1, "single-layer trunk: meta-token query rows are not materialised"
    b, s, d = x.shape
    n = b * s
    x2d = x.reshape(n, d)

    scale = HEAD_DIM ** -0.5
    qg = (jnp.tile(q_norm_g[0], N_Q_HEADS) * scale).reshape(1, ATTN_W)
    kg = jnp.tile(k_norm_g[0], N_KV_HEADS).reshape(1, KV_W)
    head_of = np.arange(ATTN_W) // HEAD_DIM
    pq = jnp.asarray((head_of[:, None] == head_of[None, :]) / HEAD_DIM, BF16)
    pk = pq[:KV_W, :KV_W]
    g1 = norm1_g[0].reshape(1, d)
    w_in_b = w_in[0].astype(BF16)

    q, kd, vd, hc = _inproj(x2d, g1, w_in_b, qg, kg, pq, pk, ROW_TILE)
    _, kmeta, vmeta, hc_meta = _inproj(meta_tokens, g1, w_in_b, qg, kg, pq, pk, N_META)

    attn = _attention(attn_sink[0], q.reshape(b, s, ATTN_W), kd.reshape(b, s, 2 * KV_W),
                      vd.reshape(b, s, 2 * KV_W), kmeta, vmeta, jnp.asarray(_alibi_bias()))
    conv = _conv(hc.reshape(b, s, CONV_W), hc_meta, dw_w[0].reshape(CONV_K, CONV_W),
                 dw_b[0].reshape(1, CONV_W), conv_ln_g[0].reshape(1, CONV_W),
                 conv_ln_b[0].reshape(1, CONV_W))

    w_out_b = w_out[0].astype(BF16)
    rw = jnp.pad(router_w[0], ((0, 0), (0, LANES - N_EXPERTS)))
    rwh = rw.astype(BF16)
    rwl = (rw - rwh.astype(F32)).astype(BF16)
    rb = jnp.pad(router_b[0], (0, LANES - N_EXPERTS), constant_values=NEG_INF).reshape(1, LANES)
    tri = jnp.asarray(np.tril(np.ones((ROW_TILE, ROW_TILE), np.float32), -1), BF16)
    h2, xp, route, counts = _outproj(attn.reshape(n, ATTN_W), conv.reshape(n, CONV_W), x2d,
                                     w_out_b[:ATTN_W], w_out_b[ATTN_W:], norm2_g[0].reshape(1, d),
                                     rwh, rwl, rb, tri)

    counts_i = counts[0, :N_EXPERTS].astype(jnp.int32)
    idx = route[:, :TOP_K].astype(jnp.int32)
    rank = route[:, 2 * TOP_K:3 * TOP_K].astype(jnp.int32)
    tile_of, expert_of, valid, starts, ends = _work_schedule(counts_i, n * TOP_K)
    chosen = idx[:, :, None] == jnp.arange(N_EXPERTS, dtype=jnp.int32)
    pos = rank + jnp.sum(jnp.where(chosen, starts, 0), axis=-1)
    pos3 = pos.reshape(n // SC_CHUNK, SC_CHUNK, TOP_K).transpose(0, 2, 1)

    xs = _sc_dispatch(pos3, xp)
    y = _experts(tile_of, expert_of, valid, starts, ends, xs, w_gate_up[0],
                 b_gate_up[0].reshape(N_EXPERTS, 1, 2 * D_FF), w_down[0],
                 b_down[0].reshape(N_EXPERTS, 1, D_MODEL))
    out = _combine(_sc_collect(pos3, y), h2, route)
    return out.reshape(b, s, d)
```

```python
import functools

import numpy as np
import jax
import jax.numpy as jnp
from jax import lax
from jax.experimental import pallas as pl
from jax.experimental.pallas import tpu as pltpu
from jax.experimental.pallas import tpu_sc as plsc

F32 = jnp.float32
BF16 = jnp.bfloat16

D_MODEL = 1024
N_META = 16
HEAD_DIM = 64
N_Q_HEADS = 8
N_KV_HEADS = 2
GROUP = N_Q_HEADS // N_KV_HEADS
ATTN_W = N_Q_HEADS * HEAD_DIM
KV_W = N_KV_HEADS * HEAD_DIM
CONV_W = D_MODEL - ATTN_W
IN_W = ATTN_W + 2 * KV_W + 2 * CONV_W
WINDOW = 128
BLOCK = 128
CONV_K = 31
CONV_PAD = CONV_K // 2
N_EXPERTS = 32
TOP_K = 4
D_FF = D_MODEL
SWIGLU_LIMIT = 7.0
SWIGLU_ALPHA = 1.702
RMS_EPS = 1e-6
LN_EPS = 1e-5
NEG_INF = -1e30

LANES = 128
ROW_TILE = 512
EXPERT_TILE = 512
SC_CORES = 2
SC_SUBCORES = 16
SC_WORKERS = SC_CORES * SC_SUBCORES
SC_CHUNK = 64
CONV_ROWS = 128
FF_CHUNK = 256
ROUTE_ROWS = 16
VMEM_LIMIT = 56 * 1024 * 1024


def _rms(x, eps):
    return x * lax.rsqrt(jnp.mean(x * x, axis=-1, keepdims=True) + eps)


def _inproj_kernel(x_ref, g1_ref, w_ref, qg_ref, kg_ref, pq_ref, pk_ref,
                   q_ref, k_ref, v_ref, hc_ref):
    x = x_ref[...]
    n = (_rms(x, RMS_EPS) * g1_ref[...]).astype(BF16)
    proj = jnp.dot(n, w_ref[...], preferred_element_type=F32)
    q = proj[:, :ATTN_W]
    k = proj[:, ATTN_W:ATTN_W + KV_W]
    v = proj[:, ATTN_W + KV_W:ATTN_W + 2 * KV_W]
    a = proj[:, ATTN_W + 2 * KV_W:ATTN_W + 2 * KV_W + CONV_W]
    g = proj[:, ATTN_W + 2 * KV_W + CONV_W:]
    qms = jnp.dot((q * q).astype(BF16), pq_ref[...], preferred_element_type=F32)
    kms = jnp.dot((k * k).astype(BF16), pk_ref[...], preferred_element_type=F32)
    q_ref[...] = (q * lax.rsqrt(qms + RMS_EPS) * qg_ref[...]).astype(BF16)
    kn = k * lax.rsqrt(kms + RMS_EPS) * kg_ref[...]
    lo = lax.broadcasted_iota(jnp.int32, kn.shape, 1) < HEAD_DIM
    ksw = pltpu.roll(kn, HEAD_DIM, 1)
    k_ref[...] = jnp.concatenate([jnp.where(lo, kn, ksw), jnp.where(lo, ksw, kn)], axis=1).astype(BF16)
    vsw = pltpu.roll(v, HEAD_DIM, 1)
    v_ref[...] = jnp.concatenate([jnp.where(lo, v, vsw), jnp.where(lo, vsw, v)], axis=1).astype(BF16)
    hc_ref[...] = a * jax.nn.sigmoid(g)


def _inproj(x2d, g1, w_in_b, qg, kg, pq, pk, tile):
    n = x2d.shape[0]
    const = lambda i: (0, 0)
    row = lambda i: (i, 0)
    return pl.pallas_call(
        _inproj_kernel,
        grid=(n // tile,),
        in_specs=[
            pl.BlockSpec((tile, D_MODEL), row),
            pl.BlockSpec((1, D_MODEL), const),
            pl.BlockSpec((D_MODEL, IN_W), const),
            pl.BlockSpec((1, ATTN_W), const),
            pl.BlockSpec((1, KV_W), const),
            pl.BlockSpec((ATTN_W, ATTN_W), const),
            pl.BlockSpec((KV_W, KV_W), const),
        ],
        out_specs=[
            pl.BlockSpec((tile, ATTN_W), row),
            pl.BlockSpec((tile, 2 * KV_W), row),
            pl.BlockSpec((tile, 2 * KV_W), row),
            pl.BlockSpec((tile, CONV_W), row),
        ],
        out_shape=[
            jax.ShapeDtypeStruct((n, ATTN_W), BF16),
            jax.ShapeDtypeStruct((n, 2 * KV_W), BF16),
            jax.ShapeDtypeStruct((n, 2 * KV_W), BF16),
            jax.ShapeDtypeStruct((n, CONV_W), F32),
        ],
        compiler_params=pltpu.CompilerParams(dimension_semantics=("arbitrary",),
                                             vmem_limit_bytes=VMEM_LIMIT),
        name="inproj",
    )(x2d, g1, w_in_b, qg, kg, pq, pk)


def _alibi_bias():
    qi = np.arange(BLOCK)[:, None]
    kj = np.arange(BLOCK)[None, :]
    dists = [qi + BLOCK - kj, np.abs(qi - kj), kj + BLOCK - qi]
    out = np.zeros((N_KV_HEADS, 3, GROUP * BLOCK, BLOCK), np.float32)
    for h in range(N_KV_HEADS):
        for g in range(GROUP):
            slope = 2.0 ** (-8.0 * (h * GROUP + g + 1) / N_Q_HEADS)
            for p, d in enumerate(dists):
                out[h, p, g * BLOCK:(g + 1) * BLOCK] = np.where(d <= WINDOW, -slope * d, NEG_INF)
    return out


def _attn_kernel(sink_ref, q_ref, kp_ref, kc_ref, kn_ref, vp_ref, vc_ref, vn_ref,
                 km_ref, vm_ref, bias_ref, o_ref, *, n_blocks):
    i = pl.program_id(1)
    lo = lax.broadcasted_iota(jnp.int32, (BLOCK, LANES), 1) < HEAD_DIM
    edge_p = jnp.where(i == 0, NEG_INF, 0.0).astype(F32)
    edge_n = jnp.where(i == n_blocks - 1, NEG_INF, 0.0).astype(F32)
    nt = (((1,), (1,)), ((), ()))
    zero = jnp.zeros((BLOCK, LANES), BF16)
    for h in range(N_KV_HEADS):
        ks = slice(h * LANES, (h + 1) * LANES)
        rows = []
        for j in range(2):
            pair = q_ref[0, :, (2 * h + j) * LANES:(2 * h + j + 1) * LANES]
            rows.append(jnp.where(lo, pair, zero))
            rows.append(jnp.where(lo, zero, pair))
        qs = jnp.concatenate(rows, axis=0)
        s_p = lax.dot_general(qs, kp_ref[0, :, ks], nt, preferred_element_type=F32) + bias_ref[h, 0] + edge_p
        s_c = lax.dot_general(qs, kc_ref[0, :, ks], nt, preferred_element_type=F32) + bias_ref[h, 1]
        s_n = lax.dot_general(qs, kn_ref[0, :, ks], nt, preferred_element_type=F32) + bias_ref[h, 2] + edge_n
        s_m = lax.dot_general(qs, km_ref[:, ks], nt, preferred_element_type=F32)
        sink = jnp.concatenate(
            [jnp.full((BLOCK, 1), sink_ref[h * GROUP + g], F32) for g in range(GROUP)], axis=0)
        m = jnp.maximum(
            jnp.maximum(jnp.max(s_p, axis=-1, keepdims=True), jnp.max(s_c, axis=-1, keepdims=True)),
            jnp.maximum(jnp.max(s_n, axis=-1, keepdims=True), jnp.max(s_m, axis=-1, keepdims=True)))
        m = jnp.maximum(m, sink)
        p_p = jnp.exp(s_p - m)
        p_c = jnp.exp(s_c - m)
        p_n = jnp.exp(s_n - m)
        p_m = jnp.exp(s_m - m)
        denom = (jnp.sum(p_p, axis=-1, keepdims=True) + jnp.sum(p_c, axis=-1, keepdims=True)
                 + jnp.sum(p_n, axis=-1, keepdims=True) + jnp.sum(p_m, axis=-1, keepdims=True)
                 + jnp.exp(sink - m))
        o = (jnp.dot(p_p.astype(BF16), vp_ref[0, :, ks], preferred_element_type=F32)
             + jnp.dot(p_c.astype(BF16), vc_ref[0, :, ks], preferred_element_type=F32)
             + jnp.dot(p_n.astype(BF16), vn_ref[0, :, ks], preferred_element_type=F32)
             + jnp.dot(p_m.astype(BF16), vm_ref[:, ks], preferred_element_type=F32))
        o = o / denom
        for j in range(2):
            even = o[(2 * j) * BLOCK:(2 * j + 1) * BLOCK]
            odd = o[(2 * j + 1) * BLOCK:(2 * j + 2) * BLOCK]
            o_ref[0, :, (2 * h + j) * LANES:(2 * h + j + 1) * LANES] = jnp.where(lo, even, odd).astype(BF16)


def _attention(sink, q, kd, vd, kmeta, vmeta, bias):
    b, s, _ = q.shape
    nb = s // BLOCK
    cur = lambda bi, i: (bi, i, 0)
    prev = lambda bi, i: (bi, jnp.maximum(i - 1, 0), 0)
    nxt = lambda bi, i: (bi, jnp.minimum(i + 1, nb - 1), 0)
    const2 = lambda bi, i: (0, 0)
    kv_blk = (1, BLOCK, 2 * KV_W)
    return pl.pallas_call(
        functools.partial(_attn_kernel, n_blocks=nb),
        grid=(b, nb),
        in_specs=[
            pl.BlockSpec(memory_space=pltpu.SMEM),
            pl.BlockSpec((1, BLOCK, ATTN_W), cur),
            pl.BlockSpec(kv_blk, prev), pl.BlockSpec(kv_blk, cur), pl.BlockSpec(kv_blk, nxt),
            pl.BlockSpec(kv_blk, prev), pl.BlockSpec(kv_blk, cur), pl.BlockSpec(kv_blk, nxt),
            pl.BlockSpec((N_META, 2 * KV_W), const2),
            pl.BlockSpec((N_META, 2 * KV_W), const2),
            pl.BlockSpec(bias.shape, lambda bi, i: (0, 0, 0, 0)),
        ],
        out_specs=pl.BlockSpec((1, BLOCK, ATTN_W), cur),
        out_shape=jax.ShapeDtypeStruct((b, s, ATTN_W), BF16),
        compiler_params=pltpu.CompilerParams(dimension_semantics=("arbitrary", "arbitrary"),
                                             vmem_limit_bytes=VMEM_LIMIT),
        name="attention",
    )(sink, q, kd, kd, kd, vd, vd, vd, kmeta, vmeta, bias)


def _conv_kernel(hc_ref, hm_ref, w_ref, b_ref, lg_ref, lb_ref, o_ref, pad_ref, y_ref, *, seq):
    tail = pad_ref.shape[0] - (N_META + seq)
    pad_ref[0:N_META, :] = hm_ref[...]
    pad_ref[N_META:N_META + seq, :] = hc_ref[0]
    pad_ref[N_META + seq:, :] = jnp.zeros((tail, CONV_W), F32)
    first = N_META - CONV_PAD
    span = CONV_ROWS + 8

    def chunk(c, carry):
        base = pl.multiple_of(c * CONV_ROWS, CONV_ROWS)
        for lt in range(CONV_W // LANES):
            ls = slice(lt * LANES, (lt + 1) * LANES)
            win = pad_ref[pl.ds(base, CONV_ROWS + 32), ls]
            acc = None
            for sub in range(8):
                part = None
                for al in range(4):
                    k = 8 * al + sub - first
                    if 0 <= k < CONV_K:
                        term = win[8 * al:8 * al + span] * w_ref[k:k + 1, ls]
                        part = term if part is None else part + term
                if sub:
                    part = pltpu.roll(part, span - sub, 0)
                acc = part[:CONV_ROWS] if acc is None else acc + part[:CONV_ROWS]
            y_ref[:, ls] = acc
        y = y_ref[...] + b_ref[...]
        mu = jnp.mean(y, axis=-1, keepdims=True)
        yc = y - mu
        var = jnp.mean(yc * yc, axis=-1, keepdims=True)
        z = yc * lax.rsqrt(var + LN_EPS) * lg_ref[...] + lb_ref[...]
        o_ref[0, pl.ds(base, CONV_ROWS), :] = (z * jax.nn.sigmoid(z)).astype(BF16)
        return carry

    lax.fori_loop(0, seq // CONV_ROWS, chunk, 0)


def _conv(hc, hc_meta, dw_w, dw_b, ln_g, ln_b):
    b, s, _ = hc.shape
    const = lambda bi: (0, 0)
    return pl.pallas_call(
        functools.partial(_conv_kernel, seq=s),
        grid=(b,),
        in_specs=[
            pl.BlockSpec((1, s, CONV_W), lambda bi: (bi, 0, 0)),
            pl.BlockSpec((N_META, CONV_W), const),
            pl.BlockSpec((CONV_K, CONV_W), const),
            pl.BlockSpec((1, CONV_W), const),
            pl.BlockSpec((1, CONV_W), const),
            pl.BlockSpec((1, CONV_W), const),
        ],
        out_specs=pl.BlockSpec((1, s, CONV_W), lambda bi: (bi, 0, 0)),
        out_shape=jax.ShapeDtypeStruct((b, s, CONV_W), BF16),
        scratch_shapes=[pltpu.VMEM((N_META + s + 32, CONV_W), F32),
                        pltpu.VMEM((CONV_ROWS, CONV_W), F32)],
        compiler_params=pltpu.CompilerParams(dimension_semantics=("arbitrary",),
                                             vmem_limit_bytes=VMEM_LIMIT),
        name="conv",
    )(hc, hc_meta, dw_w, dw_b, ln_g, ln_b)


def _outproj_kernel(attn_ref, conv_ref, x_ref, wa_ref, wc_ref, g2_ref, rwh_ref, rwl_ref, rb_ref,
                    tri_ref, h2_ref, xp_ref, route_ref, cnt_ref, run_ref):
    step = pl.program_id(0)

    @pl.when(step == 0)
    def _():
        run_ref[...] = jnp.zeros_like(run_ref)

    mix = (jnp.dot(attn_ref[...], wa_ref[...], preferred_element_type=F32)
           + jnp.dot(conv_ref[...], wc_ref[...], preferred_element_type=F32))
    h2 = x_ref[...] + mix
    h2_ref[...] = h2
    hn = _rms(h2, RMS_EPS) * g2_ref[...]
    hn_hi = hn.astype(BF16)
    half = D_MODEL // 2
    xp_ref[...] = _pack_pair(hn[:, :half], hn[:, half:])
    hn_lo = (hn - hn_hi.astype(F32)).astype(BF16)
    nt = (((1,), (1,)), ((), ()))
    logits = (lax.dot_general(rwh_ref[...], hn_hi, nt, preferred_element_type=F32)
              + lax.dot_general(rwh_ref[...], hn_lo, nt, preferred_element_type=F32)
              + lax.dot_general(rwl_ref[...], hn_hi, nt, preferred_element_type=F32)) + rb_ref[...]
    expert = lax.broadcasted_iota(jnp.int32, logits.shape, 0).astype(F32)
    work = logits
    vals, sels = [], []
    for r in range(TOP_K):
        m = jnp.max(work, axis=0, keepdims=True)
        idx = jnp.min(jnp.where(work == m, expert, float(N_EXPERTS)), axis=0, keepdims=True)
        sel = expert == idx
        work = jnp.where(sel, -jnp.inf, work)
        route_ref[r:r + 1, :] = idx
        vals.append(m)
        sels.append(sel)
    onehot = jnp.where(jnp.logical_or(jnp.logical_or(sels[0], sels[1]), jnp.logical_or(sels[2], sels[3])),
                       1.0, 0.0)
    exps = [jnp.exp(v - vals[0]) for v in vals]
    tot = exps[0] + exps[1] + exps[2] + exps[3]
    before = jnp.dot(onehot.astype(BF16), tri_ref[...], preferred_element_type=F32) + run_ref[...]
    for r in range(TOP_K):
        route_ref[TOP_K + r:TOP_K + r + 1, :] = exps[r] / tot
        route_ref[2 * TOP_K + r:2 * TOP_K + r + 1, :] = jnp.sum(jnp.where(sels[r], before, 0.0), axis=0,
                                                               keepdims=True)
    route_ref[3 * TOP_K:, :] = jnp.zeros((ROUTE_ROWS - 3 * TOP_K, logits.shape[1]), F32)
    run_ref[...] = run_ref[...] + jnp.sum(onehot, axis=1, keepdims=True)
    cnt_ref[...] = run_ref[...]


def _outproj(attn2d, conv2d, x2d, wa, wc, g2, rwh, rwl, rb, tri):
    n = x2d.shape[0]
    tile = ROW_TILE
    const = lambda i: (0, 0)
    row = lambda i: (i, 0)
    half = D_MODEL // 2
    return pl.pallas_call(
        _outproj_kernel,
        grid=(n // tile,),
        in_specs=[
            pl.BlockSpec((tile, ATTN_W), row),
            pl.BlockSpec((tile, CONV_W), row),
            pl.BlockSpec((tile, D_MODEL), row),
            pl.BlockSpec((ATTN_W, D_MODEL), const),
            pl.BlockSpec((CONV_W, D_MODEL), const),
            pl.BlockSpec((1, D_MODEL), const),
            pl.BlockSpec((N_EXPERTS, D_MODEL), const),
            pl.BlockSpec((N_EXPERTS, D_MODEL), const),
            pl.BlockSpec((N_EXPERTS, 1), const),
            pl.BlockSpec((tile, tile), const),
        ],
        out_specs=[
            pl.BlockSpec((tile, D_MODEL), row),
            pl.BlockSpec((tile, half), row),
            pl.BlockSpec((ROUTE_ROWS, tile), lambda i: (0, i)),
            pl.BlockSpec((N_EXPERTS, 1), const),
        ],
        out_shape=[
            jax.ShapeDtypeStruct((n, D_MODEL), F32),
            jax.ShapeDtypeStruct((n, half), jnp.uint32),
            jax.ShapeDtypeStruct((ROUTE_ROWS, n), F32),
            jax.ShapeDtypeStruct((N_EXPERTS, 1), F32),
        ],
        scratch_shapes=[pltpu.VMEM((N_EXPERTS, 1), F32)],
        compiler_params=pltpu.CompilerParams(dimension_semantics=("arbitrary",),
                                             vmem_limit_bytes=VMEM_LIMIT),
        name="outproj",
    )(attn2d, conv2d, x2d, wa, wc, g2, rwh, rwl, rb, tri)


def _sc_mesh():
    return plsc.VectorSubcoreMesh(core_axis_name="c", subcore_axis_name="s",
                                  num_cores=SC_CORES, num_subcores=SC_SUBCORES)


def _sc_worker():
    return lax.axis_index("s") * SC_CORES + lax.axis_index("c")


def _sc_dispatch(pos3, xp):
    n, half = xp.shape
    per_worker = n // SC_CHUNK // SC_WORKERS

    @functools.partial(
        pl.kernel, mesh=_sc_mesh(),
        out_type=jax.ShapeDtypeStruct((n * TOP_K, half), jnp.uint32),
        scratch_types=[pltpu.VMEM((TOP_K, SC_CHUNK), jnp.int32),
                       pltpu.VMEM((SC_CHUNK, half), jnp.uint32),
                       pltpu.SemaphoreType.DMA],
        name="sc_dispatch")
    def run(pos_hbm, xp_hbm, xs_hbm, idx_v, rows_v, sem):
        first = _sc_worker() * per_worker

        @pl.loop(0, per_worker)
        def _(j):
            c = first + j
            pltpu.sync_copy(pos_hbm.at[c], idx_v)
            pltpu.sync_copy(xp_hbm.at[pl.ds(c * SC_CHUNK, SC_CHUNK)], rows_v)
            copies = [pltpu.async_copy(rows_v, xs_hbm.at[idx_v.at[k]], sem) for k in range(TOP_K)]
            for cp in copies:
                cp.wait()

    return run(pos3, xp)


def _sc_collect(pos3, y):
    rows, half = y.shape
    n = rows // TOP_K
    per_worker = n // SC_CHUNK // SC_WORKERS

    @functools.partial(
        pl.kernel, mesh=_sc_mesh(),
        out_type=jax.ShapeDtypeStruct((TOP_K, n, half), jnp.uint32),
        scratch_types=[pltpu.VMEM((TOP_K, SC_CHUNK), jnp.int32),
                       pltpu.VMEM((SC_CHUNK, half), jnp.uint32),
                       pltpu.SemaphoreType.DMA],
        name="sc_collect")
    def run(pos_hbm, y_hbm, yg_hbm, idx_v, rows_v, sem):
        first = _sc_worker() * per_worker

        @pl.loop(0, per_worker)
        def _(j):
            c = first + j
            pltpu.sync_copy(pos_hbm.at[c], idx_v)
            for k in range(TOP_K):
                pltpu.async_copy(y_hbm.at[idx_v.at[k]], rows_v, sem).wait()
                pltpu.sync_copy(rows_v, yg_hbm.at[k, pl.ds(c * SC_CHUNK, SC_CHUNK)])

    return run(pos3, y)


def _expert_kernel(tile_ref, exp_ref, valid_ref, start_ref, end_ref,
                   xs_ref, wgu_ref, bgu_ref, wd_ref, bd_ref, y_ref, act_ref):
    w = pl.program_id(0)
    tile = tile_ref[w]
    e = exp_ref[w]
    first = jnp.logical_or(w == 0, tile_ref[jnp.maximum(w - 1, 0)] != tile)

    @pl.when(first)
    def _():
        y_ref[...] = jnp.zeros_like(y_ref)

    @pl.when(valid_ref[w] == 1)
    def _():
        half = D_MODEL // 2
        x_lo, x_hi = _unpack_pair(xs_ref[...])
        x = jnp.concatenate([x_lo.astype(BF16), x_hi.astype(BF16)], axis=1)
        for c in range(D_FF // FF_CHUNK):
            gc = slice(c * FF_CHUNK, (c + 1) * FF_CHUNK)
            uc = slice(D_FF + c * FF_CHUNK, D_FF + (c + 1) * FF_CHUNK)
            gate = jnp.dot(x, wgu_ref[0, :, gc].astype(BF16), preferred_element_type=F32) + bgu_ref[0, :, gc]
            up = jnp.dot(x, wgu_ref[0, :, uc].astype(BF16), preferred_element_type=F32) + bgu_ref[0, :, uc]
            gate = jnp.minimum(gate, SWIGLU_LIMIT)
            up = jnp.clip(up, -SWIGLU_LIMIT, SWIGLU_LIMIT)
            act_ref[:, gc] = (gate * jax.nn.sigmoid(SWIGLU_ALPHA * gate) * (up + 1.0)).astype(BF16)
        rows = tile * EXPERT_TILE + lax.broadcasted_iota(jnp.int32, (EXPERT_TILE, 1), 0)
        mine = jnp.logical_and(rows >= start_ref[e], rows < end_ref[e])
        for c in range(half // FF_CHUNK):
            lo_c = slice(c * FF_CHUNK, (c + 1) * FF_CHUNK)
            hi_c = slice(half + c * FF_CHUNK, half + (c + 1) * FF_CHUNK)
            y_lo = (jnp.dot(act_ref[...], wd_ref[0, :, lo_c].astype(BF16), preferred_element_type=F32)
                    + bd_ref[0, :, lo_c])
            y_hi = (jnp.dot(act_ref[...], wd_ref[0, :, hi_c].astype(BF16), preferred_element_type=F32)
                    + bd_ref[0, :, hi_c])
            y_ref[:, lo_c] = jnp.where(mine, _pack_pair(y_lo, y_hi), y_ref[:, lo_c])


def _experts(tile_of, expert_of, valid, starts, ends, xs, wgu, bgu, wd, bd):
    rows, half = xs.shape
    n_work = tile_of.shape[0]
    grid_spec = pltpu.PrefetchScalarGridSpec(
        num_scalar_prefetch=5,
        grid=(n_work,),
        in_specs=[
            pl.BlockSpec((EXPERT_TILE, half), lambda w, t, e, v, s, n: (t[w], 0)),
            pl.BlockSpec((1, D_MODEL, 2 * D_FF), lambda w, t, e, v, s, n: (e[w], 0, 0)),
            pl.BlockSpec((1, 1, 2 * D_FF), lambda w, t, e, v, s, n: (e[w], 0, 0)),
            pl.BlockSpec((1, D_FF, D_MODEL), lambda w, t, e, v, s, n: (e[w], 0, 0)),
            pl.BlockSpec((1, 1, D_MODEL), lambda w, t, e, v, s, n: (e[w], 0, 0)),
        ],
        out_specs=pl.BlockSpec((EXPERT_TILE, half), lambda w, t, e, v, s, n: (t[w], 0)),
        scratch_shapes=[pltpu.VMEM((EXPERT_TILE, D_FF), BF16)],
    )
    return pl.pallas_call(
        _expert_kernel,
        grid_spec=grid_spec,
        out_shape=jax.ShapeDtypeStruct((rows, half), jnp.uint32),
        compiler_params=pltpu.CompilerParams(dimension_semantics=("arbitrary",),
                                             vmem_limit_bytes=VMEM_LIMIT),
        name="experts",
    )(tile_of, expert_of, valid, starts, ends, xs, wgu, bgu, wd, bd)


def _unpack_pair(packed):
    lo = lax.bitcast_convert_type(packed << 16, F32)
    hi = lax.bitcast_convert_type(packed & jnp.uint32(0xFFFF0000), F32)
    return lo, hi


def _pack_pair(lo, hi):
    lo_bits = lax.bitcast_convert_type(lo.astype(BF16).astype(F32), jnp.uint32)
    hi_bits = lax.bitcast_convert_type(hi.astype(BF16).astype(F32), jnp.uint32)
    return (lo_bits >> 16) | (hi_bits & jnp.uint32(0xFFFF0000))


def _combine_kernel(yg_ref, h2_ref, gate_ref, o_ref):
    half = D_MODEL // 2
    gates = gate_ref[...]
    lo_sum = h2_ref[:, :half]
    hi_sum = h2_ref[:, half:]
    for k in range(TOP_K):
        gate = gates[:, k:k + 1]
        lo, hi = _unpack_pair(yg_ref[k])
        lo_sum = lo_sum + gate * lo
        hi_sum = hi_sum + gate * hi
    o_ref[:, :half] = lo_sum
    o_ref[:, half:] = hi_sum


def _combine(yg, h2, gates):
    n = h2.shape[0]
    half = D_MODEL // 2
    row = lambda i: (i, 0)
    return pl.pallas_call(
        _combine_kernel,
        grid=(n // ROW_TILE,),
        in_specs=[
            pl.BlockSpec((TOP_K, ROW_TILE, half), lambda i: (0, i, 0)),
            pl.BlockSpec((ROW_TILE, D_MODEL), row),
            pl.BlockSpec((ROW_TILE, TOP_K), row),
        ],
        out_specs=pl.BlockSpec((ROW_TILE, D_MODEL), row),
        out_shape=jax.ShapeDtypeStruct((n, D_MODEL), F32),
        compiler_params=pltpu.CompilerParams(dimension_semantics=("arbitrary",),
                                             vmem_limit_bytes=VMEM_LIMIT),
        name="combine",
    )(yg, h2, gates)


def _work_schedule(counts, n_rows):
    n_tiles = n_rows // EXPERT_TILE
    n_work = n_tiles + N_EXPERTS - 1
    ends = jnp.cumsum(counts)
    starts = ends - counts
    count_le = lambda table, q: jnp.sum((table[None, :] <= q[:, None]).astype(jnp.int32), axis=1)
    tile_lo = jnp.arange(n_tiles, dtype=jnp.int32) * EXPERT_TILE
    first_e = count_le(ends, tile_lo)
    last_e = count_le(starts, tile_lo + (EXPERT_TILE - 1)) - 1
    per_tile = last_e - first_e + 1
    w_end = jnp.cumsum(per_tile)
    total = w_end[-1]
    w = jnp.arange(n_work, dtype=jnp.int32)
    wc = jnp.minimum(w, total - 1)
    tile_of = count_le(w_end, wc)
    in_tile = tile_of[:, None] == jnp.arange(n_tiles, dtype=jnp.int32)[None, :]
    expert_of = wc + jnp.sum(jnp.where(in_tile, (first_e - (w_end - per_tile))[None, :], 0), axis=1)
    valid = (w < total).astype(jnp.int32)
    return tile_of, expert_of.astype(jnp.int32), valid, starts.astype(jnp.int32), ends.astype(jnp.int32)


def kernel(x, meta_tokens, norm1_g, w_in, q_norm_g, k_norm_g, attn_sink, dw_w, dw_b, conv_ln_g,
           conv_ln_b, w_out, norm2_g, router_w, router_b, w_gate_up, b_gate_up, w_down, b_down):
    assert norm1_g.shape[0] == 1, "single-layer trunk: meta-token query rows are not materialised"
    b, s, d = x.shape
    n = b * s
    x2d = x.reshape(n, d)

    scale = HEAD_DIM ** -0.5
    qg = (jnp.tile(q_norm_g[0], N_Q_HEADS) * scale).reshape(1, ATTN_W)
    kg = jnp.tile(k_norm_g[0], N_KV_HEADS).reshape(1, KV_W)
    head_of = np.arange(ATTN_W) // HEAD_DIM
    pq = jnp.asarray((head_of[:, None] == head_of[None, :]) / HEAD_DIM, BF16)
    pk = pq[:KV_W, :KV_W]
    g1 = norm1_g[0].reshape(1, d)
    w_in_b = w_in[0].astype(BF16)

    q, kd, vd, hc = _inproj(x2d, g1, w_in_b, qg, kg, pq, pk, ROW_TILE)
    _, kmeta, vmeta, hc_meta = _inproj(meta_tokens, g1, w_in_b, qg, kg, pq, pk, N_META)

    attn = _attention(attn_sink[0], q.reshape(b, s, ATTN_W), kd.reshape(b, s, 2 * KV_W),
                      vd.reshape(b, s, 2 * KV_W), kmeta, vmeta, jnp.asarray(_alibi_bias()))
    conv = _conv(hc.reshape(b, s, CONV_W), hc_meta, dw_w[0].reshape(CONV_K, CONV_W),
                 dw_b[0].reshape(1, CONV_W), conv_ln_g[0].reshape(1, CONV_W),
                 conv_ln_b[0].reshape(1, CONV_W))

    w_out_b = w_out[0].astype(BF16)
    rw = router_w[0].T
    rwh = rw.astype(BF16)
    rwl = (rw - rwh.astype(F32)).astype(BF16)
    rb = router_b[0].reshape(N_EXPERTS, 1)
    tri = jnp.asarray(np.triu(np.ones((ROW_TILE, ROW_TILE), np.float32), 1), BF16)
    h2, xp, route, counts = _outproj(attn.reshape(n, ATTN_W), conv.reshape(n, CONV_W), x2d,
                                     w_out_b[:ATTN_W], w_out_b[ATTN_W:], norm2_g[0].reshape(1, d),
                                     rwh, rwl, rb, tri)

    counts_i = counts[:, 0].astype(jnp.int32)
    idx = route[:TOP_K].astype(jnp.int32)
    rank = route[2 * TOP_K:3 * TOP_K].astype(jnp.int32)
    gates = route[TOP_K:2 * TOP_K].T
    tile_of, expert_of, valid, starts, ends = _work_schedule(counts_i, n * TOP_K)
    chosen = idx[:, :, None] == jnp.arange(N_EXPERTS, dtype=jnp.int32)
    pos = rank + jnp.sum(jnp.where(chosen, starts, 0), axis=-1)
    pos3 = pos.reshape(TOP_K, n // SC_CHUNK, SC_CHUNK).transpose(1, 0, 2)

    xs = _sc_dispatch(pos3, xp)
    y = _experts(tile_of, expert_of, valid, starts, ends, xs, w_gate_up[0],
                 b_gate_up[0].reshape(N_EXPERTS, 1, 2 * D_FF), w_down[0],
                 b_down[0].reshape(N_EXPERTS, 1, D_MODEL))
    out = _combine(_sc_collect(pos3, y), h2, gates)
    return out.reshape(b, s, d)
```

```python
import functools

import numpy as np
import jax
import jax.numpy as jnp
from jax import lax
from jax.experimental import pallas as pl
from jax.experimental.pallas import tpu as pltpu
from jax.experimental.pallas import tpu_sc as plsc

F32 = jnp.float32
BF16 = jnp.bfloat16

D_MODEL = 1024
N_META = 16
HEAD_DIM = 64
N_Q_HEADS = 8
N_KV_HEADS = 2
GROUP = N_Q_HEADS // N_KV_HEADS
ATTN_W = N_Q_HEADS * HEAD_DIM
KV_W = N_KV_HEADS * HEAD_DIM
CONV_W = D_MODEL - ATTN_W
IN_W = ATTN_W + 2 * KV_W + 2 * CONV_W
WINDOW = 128
BLOCK = 128
CONV_K = 31
CONV_PAD = CONV_K // 2
N_EXPERTS = 32
TOP_K = 4
D_FF = D_MODEL
SWIGLU_LIMIT = 7.0
SWIGLU_ALPHA = 1.702
RMS_EPS = 1e-6
LN_EPS = 1e-5
NEG_INF = -1e30

LANES = 128
ROW_TILE = 512
EXPERT_TILE = 512
SC_CORES = 2
SC_SUBCORES = 16
SC_WORKERS = SC_CORES * SC_SUBCORES
SC_CHUNK = 64
CONV_ROWS = 128
FF_CHUNK = 256
ROUTE_ROWS = 16
VMEM_LIMIT = 56 * 1024 * 1024


def _rms(x, eps):
    return x * lax.rsqrt(jnp.mean(x * x, axis=-1, keepdims=True) + eps)


def _inproj_kernel(x_ref, g1_ref, w_ref, qg_ref, kg_ref, pq_ref, pk_ref,
                   q_ref, k_ref, v_ref, hc_ref):
    x = x_ref[...]
    n = (_rms(x, RMS_EPS) * g1_ref[...]).astype(BF16)
    proj = jnp.dot(n, w_ref[...], preferred_element_type=F32)
    q = proj[:, :ATTN_W]
    k = proj[:, ATTN_W:ATTN_W + KV_W]
    v = proj[:, ATTN_W + KV_W:ATTN_W + 2 * KV_W]
    a = proj[:, ATTN_W + 2 * KV_W:ATTN_W + 2 * KV_W + CONV_W]
    g = proj[:, ATTN_W + 2 * KV_W + CONV_W:]
    qms = jnp.dot((q * q).astype(BF16), pq_ref[...], preferred_element_type=F32)
    kms = jnp.dot((k * k).astype(BF16), pk_ref[...], preferred_element_type=F32)
    q_ref[...] = (q * lax.rsqrt(qms + RMS_EPS) * qg_ref[...]).astype(BF16)
    kn = k * lax.rsqrt(kms + RMS_EPS) * kg_ref[...]
    lo = lax.broadcasted_iota(jnp.int32, kn.shape, 1) < HEAD_DIM
    ksw = pltpu.roll(kn, HEAD_DIM, 1)
    k_ref[...] = jnp.concatenate([jnp.where(lo, kn, ksw), jnp.where(lo, ksw, kn)], axis=1).astype(BF16)
    vsw = pltpu.roll(v, HEAD_DIM, 1)
    v_ref[...] = jnp.concatenate([jnp.where(lo, v, vsw), jnp.where(lo, vsw, v)], axis=1).astype(BF16)
    hc_ref[...] = a * jax.nn.sigmoid(g)


def _inproj(x2d, g1, w_in_b, qg, kg, pq, pk, tile):
    n = x2d.shape[0]
    const = lambda i: (0, 0)
    row = lambda i: (i, 0)
    return pl.pallas_call(
        _inproj_kernel,
        grid=(n // tile,),
        in_specs=[
            pl.BlockSpec((tile, D_MODEL), row),
            pl.BlockSpec((1, D_MODEL), const),
            pl.BlockSpec((D_MODEL, IN_W), const),
            pl.BlockSpec((1, ATTN_W), const),
            pl.BlockSpec((1, KV_W), const),
            pl.BlockSpec((ATTN_W, ATTN_W), const),
            pl.BlockSpec((KV_W, KV_W), const),
        ],
        out_specs=[
            pl.BlockSpec((tile, ATTN_W), row),
            pl.BlockSpec((tile, 2 * KV_W), row),
            pl.BlockSpec((tile, 2 * KV_W), row),
            pl.BlockSpec((tile, CONV_W), row),
        ],
        out_shape=[
            jax.ShapeDtypeStruct((n, ATTN_W), BF16),
            jax.ShapeDtypeStruct((n, 2 * KV_W), BF16),
            jax.ShapeDtypeStruct((n, 2 * KV_W), BF16),
            jax.ShapeDtypeStruct((n, CONV_W), F32),
        ],
        compiler_params=pltpu.CompilerParams(dimension_semantics=("arbitrary",),
                                             vmem_limit_bytes=VMEM_LIMIT),
        name="inproj",
    )(x2d, g1, w_in_b, qg, kg, pq, pk)


def _alibi_bias():
    qi = np.arange(BLOCK)[:, None]
    kj = np.arange(BLOCK)[None, :]
    dists = [qi + BLOCK - kj, np.abs(qi - kj), kj + BLOCK - qi]
    out = np.zeros((N_KV_HEADS, 3, GROUP * BLOCK, BLOCK), np.float32)
    for h in range(N_KV_HEADS):
        for g in range(GROUP):
            slope = 2.0 ** (-8.0 * (h * GROUP + g + 1) / N_Q_HEADS)
            for p, d in enumerate(dists):
                out[h, p, g * BLOCK:(g + 1) * BLOCK] = np.where(d <= WINDOW, -slope * d, NEG_INF)
    return out


def _attn_kernel(sink_ref, q_ref, kp_ref, kc_ref, kn_ref, vp_ref, vc_ref, vn_ref,
                 km_ref, vm_ref, bias_ref, o_ref, *, n_blocks):
    i = pl.program_id(1)
    lo = lax.broadcasted_iota(jnp.int32, (BLOCK, LANES), 1) < HEAD_DIM
    edge_p = jnp.where(i == 0, NEG_INF, 0.0).astype(F32)
    edge_n = jnp.where(i == n_blocks - 1, NEG_INF, 0.0).astype(F32)
    nt = (((1,), (1,)), ((), ()))
    zero = jnp.zeros((BLOCK, LANES), BF16)
    for h in range(N_KV_HEADS):
        ks = slice(h * LANES, (h + 1) * LANES)
        rows = []
        for j in range(2):
            pair = q_ref[0, :, (2 * h + j) * LANES:(2 * h + j + 1) * LANES]
            rows.append(jnp.where(lo, pair, zero))
            rows.append(jnp.where(lo, zero, pair))
        qs = jnp.concatenate(rows, axis=0)
        s_p = lax.dot_general(qs, kp_ref[0, :, ks], nt, preferred_element_type=F32) + bias_ref[h, 0] + edge_p
        s_c = lax.dot_general(qs, kc_ref[0, :, ks], nt, preferred_element_type=F32) + bias_ref[h, 1]
        s_n = lax.dot_general(qs, kn_ref[0, :, ks], nt, preferred_element_type=F32) + bias_ref[h, 2] + edge_n
        s_m = lax.dot_general(qs, km_ref[:, ks], nt, preferred_element_type=F32)
        sink = jnp.concatenate(
            [jnp.full((BLOCK, 1), sink_ref[h * GROUP + g], F32) for g in range(GROUP)], axis=0)
        m = jnp.maximum(
            jnp.maximum(jnp.max(s_p, axis=-1, keepdims=True), jnp.max(s_c, axis=-1, keepdims=True)),
            jnp.maximum(jnp.max(s_n, axis=-1, keepdims=True), jnp.max(s_m, axis=-1, keepdims=True)))
        m = jnp.maximum(m, sink)
        p_p = jnp.exp(s_p - m)
        p_c = jnp.exp(s_c - m)
        p_n = jnp.exp(s_n - m)
        p_m = jnp.exp(s_m - m)
        denom = (jnp.sum(p_p, axis=-1, keepdims=True) + jnp.sum(p_c, axis=-1, keepdims=True)
                 + jnp.sum(p_n, axis=-1, keepdims=True) + jnp.sum(p_m, axis=-1, keepdims=True)
                 + jnp.exp(sink - m))
        o = (jnp.dot(p_p.astype(BF16), vp_ref[0, :, ks], preferred_element_type=F32)
             + jnp.dot(p_c.astype(BF16), vc_ref[0, :, ks], preferred_element_type=F32)
             + jnp.dot(p_n.astype(BF16), vn_ref[0, :, ks], preferred_element_type=F32)
             + jnp.dot(p_m.astype(BF16), vm_ref[:, ks], preferred_element_type=F32))
        o = o / denom
        for j in range(2):
            even = o[(2 * j) * BLOCK:(2 * j + 1) * BLOCK]
            odd = o[(2 * j + 1) * BLOCK:(2 * j + 2) * BLOCK]
            o_ref[0, :, (2 * h + j) * LANES:(2 * h + j + 1) * LANES] = jnp.where(lo, even, odd).astype(BF16)


def _attention(sink, q, kd, vd, kmeta, vmeta, bias):
    b, s, _ = q.shape
    nb = s // BLOCK
    cur = lambda bi, i: (bi, i, 0)
    prev = lambda bi, i: (bi, jnp.maximum(i - 1, 0), 0)
    nxt = lambda bi, i: (bi, jnp.minimum(i + 1, nb - 1), 0)
    const2 = lambda bi, i: (0, 0)
    kv_blk = (1, BLOCK, 2 * KV_W)
    return pl.pallas_call(
        functools.partial(_attn_kernel, n_blocks=nb),
        grid=(b, nb),
        in_specs=[
            pl.BlockSpec(memory_space=pltpu.SMEM),
            pl.BlockSpec((1, BLOCK, ATTN_W), cur),
            pl.BlockSpec(kv_blk, prev), pl.BlockSpec(kv_blk, cur), pl.BlockSpec(kv_blk, nxt),
            pl.BlockSpec(kv_blk, prev), pl.BlockSpec(kv_blk, cur), pl.BlockSpec(kv_blk, nxt),
            pl.BlockSpec((N_META, 2 * KV_W), const2),
            pl.BlockSpec((N_META, 2 * KV_W), const2),
            pl.BlockSpec(bias.shape, lambda bi, i: (0, 0, 0, 0)),
        ],
        out_specs=pl.BlockSpec((1, BLOCK, ATTN_W), cur),
        out_shape=jax.ShapeDtypeStruct((b, s, ATTN_W), BF16),
        compiler_params=pltpu.CompilerParams(dimension_semantics=("arbitrary", "arbitrary"),
                                             vmem_limit_bytes=VMEM_LIMIT),
        name="attention",
    )(sink, q, kd, kd, kd, vd, vd, vd, kmeta, vmeta, bias)


def _conv_kernel(hc_ref, hm_ref, w_ref, b_ref, lg_ref, lb_ref, o_ref, pad_ref, y_ref, *, seq):
    tail = pad_ref.shape[0] - (N_META + seq)
    pad_ref[0:N_META, :] = hm_ref[...]
    pad_ref[N_META:N_META + seq, :] = hc_ref[0]
    pad_ref[N_META + seq:, :] = jnp.zeros((tail, CONV_W), F32)
    first = N_META - CONV_PAD
    span = CONV_ROWS + 8

    def chunk(c, carry):
        base = pl.multiple_of(c * CONV_ROWS, CONV_ROWS)
        for lt in range(CONV_W // LANES):
            ls = slice(lt * LANES, (lt + 1) * LANES)
            win = pad_ref[pl.ds(base, CONV_ROWS + 32), ls]
            acc = None
            for sub in range(8):
                part = None
                for al in range(4):
                    k = 8 * al + sub - first
                    if 0 <= k < CONV_K:
                        term = win[8 * al:8 * al + span] * w_ref[k:k + 1, ls]
                        part = term if part is None else part + term
                if sub:
                    part = pltpu.roll(part, span - sub, 0)
                acc = part[:CONV_ROWS] if acc is None else acc + part[:CONV_ROWS]
            y_ref[:, ls] = acc
        y = y_ref[...] + b_ref[...]
        mu = jnp.mean(y, axis=-1, keepdims=True)
        yc = y - mu
        var = jnp.mean(yc * yc, axis=-1, keepdims=True)
        z = yc * lax.rsqrt(var + LN_EPS) * lg_ref[...] + lb_ref[...]
        o_ref[0, pl.ds(base, CONV_ROWS), :] = (z * jax.nn.sigmoid(z)).astype(BF16)
        return carry

    lax.fori_loop(0, seq // CONV_ROWS, chunk, 0)


def _conv(hc, hc_meta, dw_w, dw_b, ln_g, ln_b):
    b, s, _ = hc.shape
    const = lambda bi: (0, 0)
    return pl.pallas_call(
        functools.partial(_conv_kernel, seq=s),
        grid=(b,),
        in_specs=[
            pl.BlockSpec((1, s, CONV_W), lambda bi: (bi, 0, 0)),
            pl.BlockSpec((N_META, CONV_W), const),
            pl.BlockSpec((CONV_K, CONV_W), const),
            pl.BlockSpec((1, CONV_W), const),
            pl.BlockSpec((1, CONV_W), const),
            pl.BlockSpec((1, CONV_W), const),
        ],
        out_specs=pl.BlockSpec((1, s, CONV_W), lambda bi: (bi, 0, 0)),
        out_shape=jax.ShapeDtypeStruct((b, s, CONV_W), BF16),
        scratch_shapes=[pltpu.VMEM((N_META + s + 32, CONV_W), F32),
                        pltpu.VMEM((CONV_ROWS, CONV_W), F32)],
        compiler_params=pltpu.CompilerParams(dimension_semantics=("arbitrary",),
                                             vmem_limit_bytes=VMEM_LIMIT),
        name="conv",
    )(hc, hc_meta, dw_w, dw_b, ln_g, ln_b)


def _outproj_kernel(attn_ref, conv_ref, x_ref, wa_ref, wc_ref, g2_ref, rwh_ref, rwl_ref, rb_ref,
                    tri_ref, h2_ref, xp_ref, route_ref, cnt_ref, run_ref):
    step = pl.program_id(0)

    @pl.when(step == 0)
    def _():
        run_ref[...] = jnp.zeros_like(run_ref)

    mix = (jnp.dot(attn_ref[...], wa_ref[...], preferred_element_type=F32)
           + jnp.dot(conv_ref[...], wc_ref[...], preferred_element_type=F32))
    h2 = x_ref[...] + mix
    h2_ref[...] = h2
    hn = _rms(h2, RMS_EPS) * g2_ref[...]
    hn_hi = hn.astype(BF16)
    half = D_MODEL // 2
    xp_ref[...] = _pack_pair(hn[:, :half], hn[:, half:])
    hn_lo = (hn - hn_hi.astype(F32)).astype(BF16)
    nt = (((1,), (1,)), ((), ()))
    logits = (lax.dot_general(rwh_ref[...], hn_hi, nt, preferred_element_type=F32)
              + lax.dot_general(rwh_ref[...], hn_lo, nt, preferred_element_type=F32)
              + lax.dot_general(rwl_ref[...], hn_hi, nt, preferred_element_type=F32)) + rb_ref[...]
    expert = lax.broadcasted_iota(jnp.int32, logits.shape, 0).astype(F32)
    work = logits
    vals, sels = [], []
    for r in range(TOP_K):
        m = jnp.max(work, axis=0, keepdims=True)
        idx = jnp.min(jnp.where(work == m, expert, float(N_EXPERTS)), axis=0, keepdims=True)
        sel = expert == idx
        work = jnp.where(sel, -jnp.inf, work)
        route_ref[r:r + 1, :] = idx
        vals.append(m)
        sels.append(sel)
    onehot = jnp.where(jnp.logical_or(jnp.logical_or(sels[0], sels[1]), jnp.logical_or(sels[2], sels[3])),
                       1.0, 0.0)
    exps = [jnp.exp(v - vals[0]) for v in vals]
    tot = exps[0] + exps[1] + exps[2] + exps[3]
    before = jnp.dot(onehot.astype(BF16), tri_ref[...], preferred_element_type=F32) + run_ref[...]
    for r in range(TOP_K):
        route_ref[TOP_K + r:TOP_K + r + 1, :] = exps[r] / tot
        route_ref[2 * TOP_K + r:2 * TOP_K + r + 1, :] = jnp.sum(jnp.where(sels[r], before, 0.0), axis=0,
                                                               keepdims=True)
    route_ref[3 * TOP_K:, :] = jnp.zeros((ROUTE_ROWS - 3 * TOP_K, logits.shape[1]), F32)
    run_ref[...] = run_ref[...] + jnp.sum(onehot, axis=1, keepdims=True)
    cnt_ref[...] = run_ref[...]


def _outproj(attn2d, conv2d, x2d, wa, wc, g2, rwh, rwl, rb, tri):
    n = x2d.shape[0]
    tile = ROW_TILE
    const = lambda i: (0, 0)
    row = lambda i: (i, 0)
    half = D_MODEL // 2
    return pl.pallas_call(
        _outproj_kernel,
        grid=(n // tile,),
        in_specs=[
            pl.BlockSpec((tile, ATTN_W), row),
            pl.BlockSpec((tile, CONV_W), row),
            pl.BlockSpec((tile, D_MODEL), row),
            pl.BlockSpec((ATTN_W, D_MODEL), const),
            pl.BlockSpec((CONV_W, D_MODEL), const),
            pl.BlockSpec((1, D_MODEL), const),
            pl.BlockSpec((N_EXPERTS, D_MODEL), const),
            pl.BlockSpec((N_EXPERTS, D_MODEL), const),
            pl.BlockSpec((N_EXPERTS, 1), const),
            pl.BlockSpec((tile, tile), const),
        ],
        out_specs=[
            pl.BlockSpec((tile, D_MODEL), row),
            pl.BlockSpec((tile, half), row),
            pl.BlockSpec((ROUTE_ROWS, tile), lambda i: (0, i)),
            pl.BlockSpec((N_EXPERTS, 1), const),
        ],
        out_shape=[
            jax.ShapeDtypeStruct((n, D_MODEL), F32),
            jax.ShapeDtypeStruct((n, half), jnp.uint32),
            jax.ShapeDtypeStruct((ROUTE_ROWS, n), F32),
            jax.ShapeDtypeStruct((N_EXPERTS, 1), F32),
        ],
        scratch_shapes=[pltpu.VMEM((N_EXPERTS, 1), F32)],
        compiler_params=pltpu.CompilerParams(dimension_semantics=("arbitrary",),
                                             vmem_limit_bytes=VMEM_LIMIT),
        name="outproj",
    )(attn2d, conv2d, x2d, wa, wc, g2, rwh, rwl, rb, tri)


def _sc_mesh():
    return plsc.VectorSubcoreMesh(core_axis_name="c", subcore_axis_name="s",
                                  num_cores=SC_CORES, num_subcores=SC_SUBCORES)


def _sc_worker():
    return lax.axis_index("s") * SC_CORES + lax.axis_index("c")


def _sc_dispatch(pos3, xp):
    n, half = xp.shape
    per_worker = n // SC_CHUNK // SC_WORKERS

    @functools.partial(
        pl.kernel, mesh=_sc_mesh(),
        out_type=jax.ShapeDtypeStruct((n * TOP_K, half), jnp.uint32),
        scratch_types=[pltpu.VMEM((TOP_K, SC_CHUNK), jnp.int32),
                       pltpu.VMEM((SC_CHUNK, half), jnp.uint32),
                       pltpu.SemaphoreType.DMA],
        name="sc_dispatch")
    def run(pos_hbm, xp_hbm, xs_hbm, idx_v, rows_v, sem):
        first = _sc_worker() * per_worker

        @pl.loop(0, per_worker)
        def _(j):
            c = first + j
            pltpu.sync_copy(pos_hbm.at[c], idx_v)
            pltpu.sync_copy(xp_hbm.at[pl.ds(c * SC_CHUNK, SC_CHUNK)], rows_v)
            copies = [pltpu.async_copy(rows_v, xs_hbm.at[idx_v.at[k]], sem) for k in range(TOP_K)]
            for cp in copies:
                cp.wait()

    return run(pos3, xp)


def _sc_collect(pos3, y):
    rows, half = y.shape
    n = rows // TOP_K
    per_worker = n // SC_CHUNK // SC_WORKERS

    @functools.partial(
        pl.kernel, mesh=_sc_mesh(),
        out_type=jax.ShapeDtypeStruct((TOP_K, n, half), jnp.uint32),
        scratch_types=[pltpu.VMEM((TOP_K, SC_CHUNK), jnp.int32),
                       pltpu.VMEM((SC_CHUNK, half), jnp.uint32),
                       pltpu.SemaphoreType.DMA],
        name="sc_collect")
    def run(pos_hbm, y_hbm, yg_hbm, idx_v, rows_v, sem):
        first = _sc_worker() * per_worker

        @pl.loop(0, per_worker)
        def _(j):
            c = first + j
            pltpu.sync_copy(pos_hbm.at[c], idx_v)
            for k in range(TOP_K):
                pltpu.async_copy(y_hbm.at[idx_v.at[k]], rows_v, sem).wait()
                pltpu.sync_copy(rows_v, yg_hbm.at[k, pl.ds(c * SC_CHUNK, SC_CHUNK)])

    return run(pos3, y)


S_TILE, S_EXPERT, S_VALID, S_NEW, S_SLOT, S_NEXT = range(6)


def _expert_kernel(sched_ref, bound_ref, xs_ref, wgu_hbm, bgu_ref, wd_hbm, bd_ref, y_ref,
                   act_ref, wgu_buf, wd_buf, sem):
    w = pl.program_id(0)
    tile = sched_ref[S_TILE, w]
    e = sched_ref[S_EXPERT, w]
    slot = sched_ref[S_SLOT, w]
    first = jnp.logical_or(w == 0, sched_ref[S_TILE, jnp.maximum(w - 1, 0)] != tile)

    def weight_copies(expert, s):
        return (pltpu.make_async_copy(wgu_hbm.at[expert], wgu_buf.at[s], sem.at[0, s]),
                pltpu.make_async_copy(wd_hbm.at[expert], wd_buf.at[s], sem.at[1, s]))

    @pl.when(sched_ref[S_NEW, w] == 1)
    def _():
        @pl.when(w == 0)
        def _():
            for cp in weight_copies(e, slot):
                cp.start()

        nxt = sched_ref[S_NEXT, w]

        @pl.when(nxt >= 0)
        def _():
            for cp in weight_copies(nxt, 1 - slot):
                cp.start()

        for cp in weight_copies(e, slot):
            cp.wait()

    @pl.when(first)
    def _():
        y_ref[...] = jnp.zeros_like(y_ref)

    @pl.when(sched_ref[S_VALID, w] == 1)
    def _():
        half = D_MODEL // 2
        x_lo, x_hi = _unpack_pair(xs_ref[...])
        x = jnp.concatenate([x_lo.astype(BF16), x_hi.astype(BF16)], axis=1)
        for c in range(D_FF // FF_CHUNK):
            gc = slice(c * FF_CHUNK, (c + 1) * FF_CHUNK)
            uc = slice(D_FF + c * FF_CHUNK, D_FF + (c + 1) * FF_CHUNK)
            gate = jnp.dot(x, wgu_buf[slot, :, gc].astype(BF16), preferred_element_type=F32) + bgu_ref[0, :, gc]
            up = jnp.dot(x, wgu_buf[slot, :, uc].astype(BF16), preferred_element_type=F32) + bgu_ref[0, :, uc]
            gate = jnp.minimum(gate, SWIGLU_LIMIT)
            up = jnp.clip(up, -SWIGLU_LIMIT, SWIGLU_LIMIT)
            act_ref[:, gc] = (gate * jax.nn.sigmoid(SWIGLU_ALPHA * gate) * (up + 1.0)).astype(BF16)
        rows = tile * EXPERT_TILE + lax.broadcasted_iota(jnp.int32, (EXPERT_TILE, 1), 0)
        mine = jnp.logical_and(rows >= bound_ref[0, e], rows < bound_ref[1, e])
        for c in range(half // FF_CHUNK):
            lo_c = slice(c * FF_CHUNK, (c + 1) * FF_CHUNK)
            hi_c = slice(half + c * FF_CHUNK, half + (c + 1) * FF_CHUNK)
            y_lo = (jnp.dot(act_ref[...], wd_buf[slot, :, lo_c].astype(BF16), preferred_element_type=F32)
                    + bd_ref[0, :, lo_c])
            y_hi = (jnp.dot(act_ref[...], wd_buf[slot, :, hi_c].astype(BF16), preferred_element_type=F32)
                    + bd_ref[0, :, hi_c])
            y_ref[:, lo_c] = jnp.where(mine, _pack_pair(y_lo, y_hi), y_ref[:, lo_c])


def _experts(sched, bounds, xs, wgu, bgu, wd, bd):
    rows, half = xs.shape
    n_work = sched.shape[1]
    grid_spec = pltpu.PrefetchScalarGridSpec(
        num_scalar_prefetch=2,
        grid=(n_work,),
        in_specs=[
            pl.BlockSpec((EXPERT_TILE, half), lambda w, sc, bo: (sc[S_TILE, w], 0)),
            pl.BlockSpec(memory_space=pl.ANY),
            pl.BlockSpec((1, 1, 2 * D_FF), lambda w, sc, bo: (sc[S_EXPERT, w], 0, 0)),
            pl.BlockSpec(memory_space=pl.ANY),
            pl.BlockSpec((1, 1, D_MODEL), lambda w, sc, bo: (sc[S_EXPERT, w], 0, 0)),
        ],
        out_specs=pl.BlockSpec((EXPERT_TILE, half), lambda w, sc, bo: (sc[S_TILE, w], 0)),
        scratch_shapes=[pltpu.VMEM((EXPERT_TILE, D_FF), BF16),
                        pltpu.VMEM((2, D_MODEL, 2 * D_FF), F32),
                        pltpu.VMEM((2, D_FF, D_MODEL), F32),
                        pltpu.SemaphoreType.DMA((2, 2))],
    )
    return pl.pallas_call(
        _expert_kernel,
        grid_spec=grid_spec,
        out_shape=jax.ShapeDtypeStruct((rows, half), jnp.uint32),
        compiler_params=pltpu.CompilerParams(dimension_semantics=("arbitrary",),
                                             vmem_limit_bytes=VMEM_LIMIT),
        name="experts",
    )(sched, bounds, xs, wgu, bgu, wd, bd)


def _unpack_pair(packed):
    lo = lax.bitcast_convert_type(packed << 16, F32)
    hi = lax.bitcast_convert_type(packed & jnp.uint32(0xFFFF0000), F32)
    return lo, hi


def _pack_pair(lo, hi):
    lo_bits = lax.bitcast_convert_type(lo.astype(BF16).astype(F32), jnp.uint32)
    hi_bits = lax.bitcast_convert_type(hi.astype(BF16).astype(F32), jnp.uint32)
    return (lo_bits >> 16) | (hi_bits & jnp.uint32(0xFFFF0000))


def _combine_kernel(yg_ref, h2_ref, gate_ref, o_ref):
    half = D_MODEL // 2
    gates = gate_ref[...]
    lo_sum = h2_ref[:, :half]
    hi_sum = h2_ref[:, half:]
    for k in range(TOP_K):
        gate = gates[:, k:k + 1]
        lo, hi = _unpack_pair(yg_ref[k])
        lo_sum = lo_sum + gate * lo
        hi_sum = hi_sum + gate * hi
    o_ref[:, :half] = lo_sum
    o_ref[:, half:] = hi_sum


def _combine(yg, h2, gates):
    n = h2.shape[0]
    half = D_MODEL // 2
    row = lambda i: (i, 0)
    return pl.pallas_call(
        _combine_kernel,
        grid=(n // ROW_TILE,),
        in_specs=[
            pl.BlockSpec((TOP_K, ROW_TILE, half), lambda i: (0, i, 0)),
            pl.BlockSpec((ROW_TILE, D_MODEL), row),
            pl.BlockSpec((ROW_TILE, TOP_K), row),
        ],
        out_specs=pl.BlockSpec((ROW_TILE, D_MODEL), row),
        out_shape=jax.ShapeDtypeStruct((n, D_MODEL), F32),
        compiler_params=pltpu.CompilerParams(dimension_semantics=("arbitrary",),
                                             vmem_limit_bytes=VMEM_LIMIT),
        name="combine",
    )(yg, h2, gates)


def _work_schedule(counts, n_rows):
    n_tiles = n_rows // EXPERT_TILE
    n_work = n_tiles + N_EXPERTS - 1
    ends = jnp.cumsum(counts)
    starts = ends - counts
    count_le = lambda table, q: jnp.sum((table[None, :] <= q[:, None]).astype(jnp.int32), axis=1)
    tile_lo = jnp.arange(n_tiles, dtype=jnp.int32) * EXPERT_TILE
    first_e = count_le(ends, tile_lo)
    last_e = count_le(starts, tile_lo + (EXPERT_TILE - 1)) - 1
    per_tile = last_e - first_e + 1
    w_end = jnp.cumsum(per_tile)
    total = w_end[-1]
    w = jnp.arange(n_work, dtype=jnp.int32)
    wc = jnp.minimum(w, total - 1)
    tile_of = count_le(w_end, wc)
    in_tile = tile_of[:, None] == jnp.arange(n_tiles, dtype=jnp.int32)[None, :]
    expert_of = (wc + jnp.sum(jnp.where(in_tile, (first_e - (w_end - per_tile))[None, :], 0), axis=1)
                 ).astype(jnp.int32)
    valid = jnp.logical_and(w < total, jnp.take(counts, expert_of) > 0)
    prev_e = jnp.concatenate([jnp.full((1,), -1, jnp.int32), expert_of[:-1]])
    new_e = jnp.logical_and(valid, expert_of != prev_e)
    slot = (jnp.cumsum(new_e.astype(jnp.int32)) - 1) % 2
    later = jnp.logical_and(new_e[None, :], w[None, :] > w[:, None])
    next_w = jnp.min(jnp.where(later, w[None, :], n_work), axis=1)
    next_e = jnp.sum(jnp.where(next_w[:, None] == w[None, :], expert_of[None, :], 0), axis=1)
    next_e = jnp.where(next_w < n_work, next_e, -1)
    sched = jnp.stack([tile_of, expert_of, valid.astype(jnp.int32), new_e.astype(jnp.int32), slot,
                       next_e]).astype(jnp.int32)
    return sched, jnp.stack([starts, ends]).astype(jnp.int32)


def kernel(x, meta_tokens, norm1_g, w_in, q_norm_g, k_norm_g, attn_sink, dw_w, dw_b, conv_ln_g,
           conv_ln_b, w_out, norm2_g, router_w, router_b, w_gate_up, b_gate_up, w_down, b_down):
    assert norm1_g.shape[0] == 1, "single-layer trunk: meta-token query rows are not materialised"
    b, s, d = x.shape
    n = b * s
    x2d = x.reshape(n, d)

    scale = HEAD_DIM ** -0.5
    qg = (jnp.tile(q_norm_g[0], N_Q_HEADS) * scale).reshape(1, ATTN_W)
    kg = jnp.tile(k_norm_g[0], N_KV_HEADS).reshape(1, KV_W)
    head_of = np.arange(ATTN_W) // HEAD_DIM
    pq = jnp.asarray((head_of[:, None] == head_of[None, :]) / HEAD_DIM, BF16)
    pk = pq[:KV_W, :KV_W]
    g1 = norm1_g[0].reshape(1, d)
    w_in_b = w_in[0].astype(BF16)

    q, kd, vd, hc = _inproj(x2d, g1, w_in_b, qg, kg, pq, pk, ROW_TILE)
    _, kmeta, vmeta, hc_meta = _inproj(meta_tokens, g1, w_in_b, qg, kg, pq, pk, N_META)

    attn = _attention(attn_sink[0], q.reshape(b, s, ATTN_W), kd.reshape(b, s, 2 * KV_W),
                      vd.reshape(b, s, 2 * KV_W), kmeta, vmeta, jnp.asarray(_alibi_bias()))
    conv = _conv(hc.reshape(b, s, CONV_W), hc_meta, dw_w[0].reshape(CONV_K, CONV_W),
                 dw_b[0].reshape(1, CONV_W), conv_ln_g[0].reshape(1, CONV_W),
                 conv_ln_b[0].reshape(1, CONV_W))

    w_out_b = w_out[0].astype(BF16)
    rw = router_w[0].T
    rwh = rw.astype(BF16)
    rwl = (rw - rwh.astype(F32)).astype(BF16)
    rb = router_b[0].reshape(N_EXPERTS, 1)
    tri = jnp.asarray(np.triu(np.ones((ROW_TILE, ROW_TILE), np.float32), 1), BF16)
    h2, xp, route, counts = _outproj(attn.reshape(n, ATTN_W), conv.reshape(n, CONV_W), x2d,
                                     w_out_b[:ATTN_W], w_out_b[ATTN_W:], norm2_g[0].reshape(1, d),
                                     rwh, rwl, rb, tri)

    counts_i = counts[:, 0].astype(jnp.int32)
    idx = route[:TOP_K].astype(jnp.int32)
    rank = route[2 * TOP_K:3 * TOP_K].astype(jnp.int32)
    gates = route[TOP_K:2 * TOP_K].T
    sched, bounds = _work_schedule(counts_i, n * TOP_K)
    chosen = idx[:, :, None] == jnp.arange(N_EXPERTS, dtype=jnp.int32)
    pos = rank + jnp.sum(jnp.where(chosen, bounds[0], 0), axis=-1)
    pos3 = pos.reshape(TOP_K, n // SC_CHUNK, SC_CHUNK).transpose(1, 0, 2)

    xs = _sc_dispatch(pos3, xp)
    y = _experts(sched, bounds, xs, w_gate_up[0],
                 b_gate_up[0].reshape(N_EXPERTS, 1, 2 * D_FF), w_down[0],
                 b_down[0].reshape(N_EXPERTS, 1, D_MODEL))
    out = _combine(_sc_collect(pos3, y), h2, gates)
    return out.reshape(b, s, d)
```

```python
import functools

import numpy as np
import jax
import jax.numpy as jnp
from jax import lax
from jax.experimental import pallas as pl
from jax.experimental.pallas import tpu as pltpu
from jax.experimental.pallas import tpu_sc as plsc

F32 = jnp.float32
BF16 = jnp.bfloat16

D_MODEL = 1024
N_META = 16
HEAD_DIM = 64
N_Q_HEADS = 8
N_KV_HEADS = 2
GROUP = N_Q_HEADS // N_KV_HEADS
ATTN_W = N_Q_HEADS * HEAD_DIM
KV_W = N_KV_HEADS * HEAD_DIM
CONV_W = D_MODEL - ATTN_W
IN_W = ATTN_W + 2 * KV_W + 2 * CONV_W
WINDOW = 128
BLOCK = 128
CONV_K = 31
CONV_PAD = CONV_K // 2
N_EXPERTS = 32
TOP_K = 4
D_FF = D_MODEL
SWIGLU_LIMIT = 7.0
SWIGLU_ALPHA = 1.702
RMS_EPS = 1e-6
LN_EPS = 1e-5
NEG_INF = -1e30

LANES = 128
ROW_TILE = 512
EXPERT_TILE = 512
SC_CORES = 2
SC_SUBCORES = 16
SC_WORKERS = SC_CORES * SC_SUBCORES
SC_CHUNK = 64
CONV_ROWS = 128
FF_CHUNK = 256
ROUTE_ROWS = 16
VMEM_LIMIT = 56 * 1024 * 1024


def _rms(x, eps):
    return x * lax.rsqrt(jnp.mean(x * x, axis=-1, keepdims=True) + eps)


def _inproj_kernel(x_ref, g1_ref, w_ref, qg_ref, kg_ref, pq_ref, pk_ref,
                   q_ref, k_ref, v_ref, hc_ref):
    x = x_ref[...]
    n = (_rms(x, RMS_EPS) * g1_ref[...]).astype(BF16)
    proj = jnp.dot(n, w_ref[...], preferred_element_type=F32)
    q = proj[:, :ATTN_W]
    k = proj[:, ATTN_W:ATTN_W + KV_W]
    v = proj[:, ATTN_W + KV_W:ATTN_W + 2 * KV_W]
    a = proj[:, ATTN_W + 2 * KV_W:ATTN_W + 2 * KV_W + CONV_W]
    g = proj[:, ATTN_W + 2 * KV_W + CONV_W:]
    qms = jnp.dot((q * q).astype(BF16), pq_ref[...], preferred_element_type=F32)
    kms = jnp.dot((k * k).astype(BF16), pk_ref[...], preferred_element_type=F32)
    q_ref[...] = (q * lax.rsqrt(qms + RMS_EPS) * qg_ref[...]).astype(BF16)
    kn = k * lax.rsqrt(kms + RMS_EPS) * kg_ref[...]
    lo = lax.broadcasted_iota(jnp.int32, kn.shape, 1) < HEAD_DIM
    ksw = pltpu.roll(kn, HEAD_DIM, 1)
    k_ref[...] = jnp.concatenate([jnp.where(lo, kn, ksw), jnp.where(lo, ksw, kn)], axis=1).astype(BF16)
    vsw = pltpu.roll(v, HEAD_DIM, 1)
    v_ref[...] = jnp.concatenate([jnp.where(lo, v, vsw), jnp.where(lo, vsw, v)], axis=1).astype(BF16)
    hc_ref[...] = a * jax.nn.sigmoid(g)


def _inproj(x2d, g1, w_in_b, qg, kg, pq, pk, tile):
    n = x2d.shape[0]
    const = lambda i: (0, 0)
    row = lambda i: (i, 0)
    return pl.pallas_call(
        _inproj_kernel,
        grid=(n // tile,),
        in_specs=[
            pl.BlockSpec((tile, D_MODEL), row),
            pl.BlockSpec((1, D_MODEL), const),
            pl.BlockSpec((D_MODEL, IN_W), const),
            pl.BlockSpec((1, ATTN_W), const),
            pl.BlockSpec((1, KV_W), const),
            pl.BlockSpec((ATTN_W, ATTN_W), const),
            pl.BlockSpec((KV_W, KV_W), const),
        ],
        out_specs=[
            pl.BlockSpec((tile, ATTN_W), row),
            pl.BlockSpec((tile, 2 * KV_W), row),
            pl.BlockSpec((tile, 2 * KV_W), row),
            pl.BlockSpec((tile, CONV_W), row),
        ],
        out_shape=[
            jax.ShapeDtypeStruct((n, ATTN_W), BF16),
            jax.ShapeDtypeStruct((n, 2 * KV_W), BF16),
            jax.ShapeDtypeStruct((n, 2 * KV_W), BF16),
            jax.ShapeDtypeStruct((n, CONV_W), F32),
        ],
        compiler_params=pltpu.CompilerParams(dimension_semantics=("arbitrary",),
                                             vmem_limit_bytes=VMEM_LIMIT),
        name="inproj",
    )(x2d, g1, w_in_b, qg, kg, pq, pk)


def _conv_fill(pad_ref, hm_ref, hc_ref, seq):
    tail = pad_ref.shape[0] - (N_META + seq)
    pad_ref[0:N_META, :] = hm_ref[...]
    pad_ref[N_META:N_META + seq, :] = hc_ref[0]
    pad_ref[N_META + seq:, :] = jnp.zeros((tail, CONV_W), F32)


def _conv_chunk(pad_ref, y_ref, w_ref, b_ref, lg_ref, lb_ref, base):
    first = N_META - CONV_PAD
    span = CONV_ROWS + 8
    for lt in range(CONV_W // LANES):
        ls = slice(lt * LANES, (lt + 1) * LANES)
        win = pad_ref[pl.ds(base, CONV_ROWS + 32), ls]
        acc = None
        for sub in range(8):
            part = None
            for al in range(4):
                k = 8 * al + sub - first
                if 0 <= k < CONV_K:
                    term = win[8 * al:8 * al + span] * w_ref[k:k + 1, ls]
                    part = term if part is None else part + term
            if sub:
                part = pltpu.roll(part, span - sub, 0)
            acc = part[:CONV_ROWS] if acc is None else acc + part[:CONV_ROWS]
        y_ref[:, ls] = acc
    y = y_ref[...] + b_ref[...]
    mu = jnp.mean(y, axis=-1, keepdims=True)
    yc = y - mu
    var = jnp.mean(yc * yc, axis=-1, keepdims=True)
    z = yc * lax.rsqrt(var + LN_EPS) * lg_ref[...] + lb_ref[...]
    return (z * jax.nn.sigmoid(z)).astype(BF16)


def _alibi_bias():
    qi = np.arange(BLOCK)[:, None]
    kj = np.arange(BLOCK)[None, :]
    dists = [qi + BLOCK - kj, np.abs(qi - kj), kj + BLOCK - qi]
    out = np.zeros((N_KV_HEADS, 4, GROUP * BLOCK, BLOCK), np.float32)
    for h in range(N_KV_HEADS):
        for g in range(GROUP):
            slope = 2.0 ** (-8.0 * (h * GROUP + g + 1) / N_Q_HEADS)
            for p, d in enumerate(dists):
                out[h, p, g * BLOCK:(g + 1) * BLOCK] = np.where(d <= WINDOW, -slope * d, NEG_INF)
    out[:, 3, :, N_META:] = NEG_INF
    return out


def _attn_kernel(sink_ref, q_ref, kp_ref, kc_ref, kn_ref, vp_ref, vc_ref, vn_ref,
                 km_ref, vm_ref, bias_ref, o_ref, *, n_blocks):
    i = pl.program_id(1)
    lo = lax.broadcasted_iota(jnp.int32, (BLOCK, LANES), 1) < HEAD_DIM
    edge_p = jnp.where(i == 0, NEG_INF, 0.0).astype(F32)
    edge_n = jnp.where(i == n_blocks - 1, NEG_INF, 0.0).astype(F32)
    nt = (((1,), (1,)), ((), ()))
    zero = jnp.zeros((BLOCK, LANES), BF16)
    for h in range(N_KV_HEADS):
        ks = slice(h * LANES, (h + 1) * LANES)
        rows = []
        for j in range(2):
            pair = q_ref[0, :, (2 * h + j) * LANES:(2 * h + j + 1) * LANES]
            rows.append(jnp.where(lo, pair, zero))
            rows.append(jnp.where(lo, zero, pair))
        qs = jnp.concatenate(rows, axis=0)
        s_p = lax.dot_general(qs, kp_ref[0, :, ks], nt, preferred_element_type=F32) + bias_ref[h, 0] + edge_p
        s_c = lax.dot_general(qs, kc_ref[0, :, ks], nt, preferred_element_type=F32) + bias_ref[h, 1]
        s_n = lax.dot_general(qs, kn_ref[0, :, ks], nt, preferred_element_type=F32) + bias_ref[h, 2] + edge_n
        s_m = lax.dot_general(qs, km_ref[:, ks], nt, preferred_element_type=F32) + bias_ref[h, 3]
        sink = jnp.concatenate(
            [jnp.full((BLOCK, 1), sink_ref[h * GROUP + g], F32) for g in range(GROUP)], axis=0)
        m = jnp.max(jnp.maximum(jnp.maximum(s_p, s_c), jnp.maximum(s_n, s_m)), axis=-1, keepdims=True)
        m = jnp.maximum(m, sink)
        p_p = jnp.exp(s_p - m)
        p_c = jnp.exp(s_c - m)
        p_n = jnp.exp(s_n - m)
        p_m = jnp.exp(s_m - m)
        denom = jnp.sum((p_p + p_c) + (p_n + p_m), axis=-1, keepdims=True) + jnp.exp(sink - m)
        o = (jnp.dot(p_p.astype(BF16), vp_ref[0, :, ks], preferred_element_type=F32)
             + jnp.dot(p_c.astype(BF16), vc_ref[0, :, ks], preferred_element_type=F32)
             + jnp.dot(p_n.astype(BF16), vn_ref[0, :, ks], preferred_element_type=F32)
             + jnp.dot(p_m.astype(BF16), vm_ref[:, ks], preferred_element_type=F32))
        o = o / denom
        for j in range(2):
            even = o[(2 * j) * BLOCK:(2 * j + 1) * BLOCK]
            odd = o[(2 * j + 1) * BLOCK:(2 * j + 2) * BLOCK]
            o_ref[0, :, (2 * h + j) * LANES:(2 * h + j + 1) * LANES] = jnp.where(lo, even, odd).astype(BF16)


def _attention(sink, q, kd, vd, kmeta, vmeta, bias):
    b, s, _ = q.shape
    nb = s // BLOCK
    cur = lambda bi, i: (bi, i, 0)
    prev = lambda bi, i: (bi, jnp.maximum(i - 1, 0), 0)
    nxt = lambda bi, i: (bi, jnp.minimum(i + 1, nb - 1), 0)
    const2 = lambda bi, i: (0, 0)
    kv_blk = (1, BLOCK, 2 * KV_W)
    return pl.pallas_call(
        functools.partial(_attn_kernel, n_blocks=nb),
        grid=(b, nb),
        in_specs=[
            pl.BlockSpec(memory_space=pltpu.SMEM),
            pl.BlockSpec((1, BLOCK, ATTN_W), cur),
            pl.BlockSpec(kv_blk, prev), pl.BlockSpec(kv_blk, cur), pl.BlockSpec(kv_blk, nxt),
            pl.BlockSpec(kv_blk, prev), pl.BlockSpec(kv_blk, cur), pl.BlockSpec(kv_blk, nxt),
            pl.BlockSpec((BLOCK, 2 * KV_W), const2),
            pl.BlockSpec((BLOCK, 2 * KV_W), const2),
            pl.BlockSpec(bias.shape, lambda bi, i: (0, 0, 0, 0)),
        ],
        out_specs=pl.BlockSpec((1, BLOCK, ATTN_W), cur),
        out_shape=jax.ShapeDtypeStruct((b, s, ATTN_W), BF16),
        compiler_params=pltpu.CompilerParams(dimension_semantics=("arbitrary", "arbitrary"),
                                             vmem_limit_bytes=VMEM_LIMIT),
        name="attention",
    )(sink, q, kd, kd, kd, vd, vd, vd, kmeta, vmeta, bias)


def _outproj_kernel(attn_ref, hc_ref, hm_ref, cw_ref, cb_ref, lg_ref, lb_ref, x_ref, wa_ref, wc_ref, g2_ref,
                    rwh_ref, rwl_ref, rb_ref, tri_ref, h2_ref, xp_ref, route_ref, cnt_ref,
                    run_ref, pad_ref, y_ref, conv_ref, *, seq):
    step = pl.program_id(0)
    tiles_per_seq = seq // ROW_TILE
    part = step % tiles_per_seq

    @pl.when(step == 0)
    def _():
        run_ref[...] = jnp.zeros_like(run_ref)

    @pl.when(part == 0)
    def _():
        _conv_fill(pad_ref, hm_ref, hc_ref, seq)

    for c in range(ROW_TILE // CONV_ROWS):
        base = pl.multiple_of(part * ROW_TILE + c * CONV_ROWS, CONV_ROWS)
        conv_ref[c * CONV_ROWS:(c + 1) * CONV_ROWS, :] = _conv_chunk(pad_ref, y_ref, cw_ref, cb_ref, lg_ref,
                                                                     lb_ref, base)
    mix = (jnp.dot(attn_ref[...], wa_ref[...], preferred_element_type=F32)
           + jnp.dot(conv_ref[...], wc_ref[...], preferred_element_type=F32))
    h2 = x_ref[...] + mix
    h2_ref[...] = h2
    hn = _rms(h2, RMS_EPS) * g2_ref[...]
    hn_hi = hn.astype(BF16)
    half = D_MODEL // 2
    xp_ref[...] = _pack_pair(hn[:, :half], hn[:, half:])
    hn_lo = (hn - hn_hi.astype(F32)).astype(BF16)
    nt = (((1,), (1,)), ((), ()))
    logits = (lax.dot_general(rwh_ref[...], hn_hi, nt, preferred_element_type=F32)
              + lax.dot_general(rwh_ref[...], hn_lo, nt, preferred_element_type=F32)
              + lax.dot_general(rwl_ref[...], hn_hi, nt, preferred_element_type=F32)) + rb_ref[...]
    expert = lax.broadcasted_iota(jnp.int32, logits.shape, 0).astype(F32)
    work = logits
    vals, sels = [], []
    for r in range(TOP_K):
        m = jnp.max(work, axis=0, keepdims=True)
        idx = jnp.min(jnp.where(work == m, expert, float(N_EXPERTS)), axis=0, keepdims=True)
        sel = expert == idx
        work = jnp.where(sel, -jnp.inf, work)
        route_ref[r:r + 1, :] = idx
        vals.append(m)
        sels.append(sel)
    onehot = jnp.where(jnp.logical_or(jnp.logical_or(sels[0], sels[1]), jnp.logical_or(sels[2], sels[3])),
                       1.0, 0.0)
    exps = [jnp.exp(v - vals[0]) for v in vals]
    tot = exps[0] + exps[1] + exps[2] + exps[3]
    before = jnp.dot(onehot.astype(BF16), tri_ref[...], preferred_element_type=F32) + run_ref[...]
    for r in range(TOP_K):
        route_ref[TOP_K + r:TOP_K + r + 1, :] = exps[r] / tot
        route_ref[2 * TOP_K + r:2 * TOP_K + r + 1, :] = jnp.sum(jnp.where(sels[r], before, 0.0), axis=0,
                                                               keepdims=True)
    route_ref[3 * TOP_K:, :] = jnp.zeros((ROUTE_ROWS - 3 * TOP_K, logits.shape[1]), F32)
    run_ref[...] = run_ref[...] + jnp.sum(onehot, axis=1, keepdims=True)
    cnt_ref[...] = run_ref[...]


def _outproj(attn2d, hc, hc_meta, dw_w, dw_b, ln_g, ln_b, x2d, wa, wc, g2, rwh, rwl, rb, tri):
    n = x2d.shape[0]
    seq = hc.shape[1]
    tile = ROW_TILE
    const = lambda i: (0, 0)
    row = lambda i: (i, 0)
    half = D_MODEL // 2
    return pl.pallas_call(
        functools.partial(_outproj_kernel, seq=seq),
        grid=(n // tile,),
        in_specs=[
            pl.BlockSpec((tile, ATTN_W), row),
            pl.BlockSpec((1, seq, CONV_W), lambda i: (i // (seq // tile), 0, 0)),
            pl.BlockSpec((N_META, CONV_W), const),
            pl.BlockSpec((CONV_K, CONV_W), const),
            pl.BlockSpec((1, CONV_W), const),
            pl.BlockSpec((1, CONV_W), const),
            pl.BlockSpec((1, CONV_W), const),
            pl.BlockSpec((tile, D_MODEL), row),
            pl.BlockSpec((ATTN_W, D_MODEL), const),
            pl.BlockSpec((CONV_W, D_MODEL), const),
            pl.BlockSpec((1, D_MODEL), const),
            pl.BlockSpec((N_EXPERTS, D_MODEL), const),
            pl.BlockSpec((N_EXPERTS, D_MODEL), const),
            pl.BlockSpec((N_EXPERTS, 1), const),
            pl.BlockSpec((tile, tile), const),
        ],
        out_specs=[
            pl.BlockSpec((tile, D_MODEL), row),
            pl.BlockSpec((tile, half), row),
            pl.BlockSpec((ROUTE_ROWS, tile), lambda i: (0, i)),
            pl.BlockSpec((N_EXPERTS, 1), const),
        ],
        out_shape=[
            jax.ShapeDtypeStruct((n, D_MODEL), F32),
            jax.ShapeDtypeStruct((n, half), jnp.uint32),
            jax.ShapeDtypeStruct((ROUTE_ROWS, n), F32),
            jax.ShapeDtypeStruct((N_EXPERTS, 1), F32),
        ],
        scratch_shapes=[pltpu.VMEM((N_EXPERTS, 1), F32),
                        pltpu.VMEM((N_META + seq + 32, CONV_W), F32),
                        pltpu.VMEM((CONV_ROWS, CONV_W), F32),
                        pltpu.VMEM((tile, CONV_W), BF16)],
        compiler_params=pltpu.CompilerParams(dimension_semantics=("arbitrary",),
                                             vmem_limit_bytes=VMEM_LIMIT),
        name="outproj",
    )(attn2d, hc, hc_meta, dw_w, dw_b, ln_g, ln_b, x2d, wa, wc, g2, rwh, rwl, rb, tri)


def _sc_mesh():
    return plsc.VectorSubcoreMesh(core_axis_name="c", subcore_axis_name="s",
                                  num_cores=SC_CORES, num_subcores=SC_SUBCORES)


def _sc_worker():
    return lax.axis_index("s") * SC_CORES + lax.axis_index("c")


def _sc_dispatch(pos3, xp):
    n, half = xp.shape
    per_worker = n // SC_CHUNK // SC_WORKERS

    @functools.partial(
        pl.kernel, mesh=_sc_mesh(),
        out_type=jax.ShapeDtypeStruct((n * TOP_K, half), jnp.uint32),
        scratch_types=[pltpu.VMEM((TOP_K, SC_CHUNK), jnp.int32),
                       pltpu.VMEM((SC_CHUNK, half), jnp.uint32),
                       pltpu.SemaphoreType.DMA],
        name="sc_dispatch")
    def run(pos_hbm, xp_hbm, xs_hbm, idx_v, rows_v, sem):
        first = _sc_worker() * per_worker

        @pl.loop(0, per_worker)
        def _(j):
            c = first + j
            pltpu.sync_copy(pos_hbm.at[c], idx_v)
            pltpu.sync_copy(xp_hbm.at[pl.ds(c * SC_CHUNK, SC_CHUNK)], rows_v)
            copies = [pltpu.async_copy(rows_v, xs_hbm.at[idx_v.at[k]], sem) for k in range(TOP_K)]
            for cp in copies:
                cp.wait()

    return run(pos3, xp)


def _sc_collect(pos3, y):
    rows, half = y.shape
    n = rows // TOP_K
    per_worker = n // SC_CHUNK // SC_WORKERS

    @functools.partial(
        pl.kernel, mesh=_sc_mesh(),
        out_type=jax.ShapeDtypeStruct((TOP_K, n, half), jnp.uint32),
        scratch_types=[pltpu.VMEM((TOP_K, SC_CHUNK), jnp.int32),
                       pltpu.VMEM((SC_CHUNK, half), jnp.uint32),
                       pltpu.SemaphoreType.DMA],
        name="sc_collect")
    def run(pos_hbm, y_hbm, yg_hbm, idx_v, rows_v, sem):
        first = _sc_worker() * per_worker

        @pl.loop(0, per_worker)
        def _(j):
            c = first + j
            pltpu.sync_copy(pos_hbm.at[c], idx_v)
            for k in range(TOP_K):
                pltpu.async_copy(y_hbm.at[idx_v.at[k]], rows_v, sem).wait()
                pltpu.sync_copy(rows_v, yg_hbm.at[k, pl.ds(c * SC_CHUNK, SC_CHUNK)])

    return run(pos3, y)


S_TILE, S_EXPERT, S_VALID, S_NEW, S_SLOT, S_NEXT = range(6)


def _expert_kernel(sched_ref, bound_ref, xs_ref, wgu_hbm, bgu_ref, wd_hbm, bd_ref, y_ref,
                   act_ref, wgu_buf, wd_buf, sem):
    w = pl.program_id(0)
    tile = sched_ref[S_TILE, w]
    e = sched_ref[S_EXPERT, w]
    slot = sched_ref[S_SLOT, w]
    first = jnp.logical_or(w == 0, sched_ref[S_TILE, jnp.maximum(w - 1, 0)] != tile)

    def weight_copies(expert, s):
        return (pltpu.make_async_copy(wgu_hbm.at[expert], wgu_buf.at[s], sem.at[0, s]),
                pltpu.make_async_copy(wd_hbm.at[expert], wd_buf.at[s], sem.at[1, s]))

    @pl.when(sched_ref[S_NEW, w] == 1)
    def _():
        @pl.when(w == 0)
        def _():
            for cp in weight_copies(e, slot):
                cp.start()

        nxt = sched_ref[S_NEXT, w]

        @pl.when(nxt >= 0)
        def _():
            for cp in weight_copies(nxt, 1 - slot):
                cp.start()

        for cp in weight_copies(e, slot):
            cp.wait()

    @pl.when(first)
    def _():
        y_ref[...] = jnp.zeros_like(y_ref)

    @pl.when(sched_ref[S_VALID, w] == 1)
    def _():
        half = D_MODEL // 2
        x_lo, x_hi = _unpack_pair(xs_ref[...])
        x = jnp.concatenate([x_lo.astype(BF16), x_hi.astype(BF16)], axis=1)
        for c in range(D_FF // FF_CHUNK):
            gc = slice(c * FF_CHUNK, (c + 1) * FF_CHUNK)
            uc = slice(D_FF + c * FF_CHUNK, D_FF + (c + 1) * FF_CHUNK)
            gate = jnp.dot(x, wgu_buf[slot, :, gc].astype(BF16), preferred_element_type=F32) + bgu_ref[0, :, gc]
            up = jnp.dot(x, wgu_buf[slot, :, uc].astype(BF16), preferred_element_type=F32) + bgu_ref[0, :, uc]
            gate = jnp.minimum(gate, SWIGLU_LIMIT)
            up = jnp.clip(up, -SWIGLU_LIMIT, SWIGLU_LIMIT)
            act_ref[:, gc] = (gate * jax.nn.sigmoid(SWIGLU_ALPHA * gate) * (up + 1.0)).astype(BF16)
        rows = tile * EXPERT_TILE + lax.broadcasted_iota(jnp.int32, (EXPERT_TILE, 1), 0)
        mine = jnp.logical_and(rows >= bound_ref[0, e], rows < bound_ref[1, e])
        for c in range(half // FF_CHUNK):
            lo_c = slice(c * FF_CHUNK, (c + 1) * FF_CHUNK)
            hi_c = slice(half + c * FF_CHUNK, half + (c + 1) * FF_CHUNK)
            y_lo = (jnp.dot(act_ref[...], wd_buf[slot, :, lo_c].astype(BF16), preferred_element_type=F32)
                    + bd_ref[0, :, lo_c])
            y_hi = (jnp.dot(act_ref[...], wd_buf[slot, :, hi_c].astype(BF16), preferred_element_type=F32)
                    + bd_ref[0, :, hi_c])
            y_ref[:, lo_c] = jnp.where(mine, _pack_pair(y_lo, y_hi), y_ref[:, lo_c])


def _experts(sched, bounds, xs, wgu, bgu, wd, bd):
    rows, half = xs.shape
    n_work = sched.shape[1]
    grid_spec = pltpu.PrefetchScalarGridSpec(
        num_scalar_prefetch=2,
        grid=(n_work,),
        in_specs=[
            pl.BlockSpec((EXPERT_TILE, half), lambda w, sc, bo: (sc[S_TILE, w], 0)),
            pl.BlockSpec(memory_space=pl.ANY),
            pl.BlockSpec((1, 1, 2 * D_FF), lambda w, sc, bo: (sc[S_EXPERT, w], 0, 0)),
            pl.BlockSpec(memory_space=pl.ANY),
            pl.BlockSpec((1, 1, D_MODEL), lambda w, sc, bo: (sc[S_EXPERT, w], 0, 0)),
        ],
        out_specs=pl.BlockSpec((EXPERT_TILE, half), lambda w, sc, bo: (sc[S_TILE, w], 0)),
        scratch_shapes=[pltpu.VMEM((EXPERT_TILE, D_FF), BF16),
                        pltpu.VMEM((2, D_MODEL, 2 * D_FF), F32),
                        pltpu.VMEM((2, D_FF, D_MODEL), F32),
                        pltpu.SemaphoreType.DMA((2, 2))],
    )
    return pl.pallas_call(
        _expert_kernel,
        grid_spec=grid_spec,
        out_shape=jax.ShapeDtypeStruct((rows, half), jnp.uint32),
        compiler_params=pltpu.CompilerParams(dimension_semantics=("arbitrary",),
                                             vmem_limit_bytes=VMEM_LIMIT),
        name="experts",
    )(sched, bounds, xs, wgu, bgu, wd, bd)


def _unpack_pair(packed):
    lo = lax.bitcast_convert_type(packed << 16, F32)
    hi = lax.bitcast_convert_type(packed & jnp.uint32(0xFFFF0000), F32)
    return lo, hi


def _pack_pair(lo, hi):
    lo_bits = lax.bitcast_convert_type(lo.astype(BF16).astype(F32), jnp.uint32)
    hi_bits = lax.bitcast_convert_type(hi.astype(BF16).astype(F32), jnp.uint32)
    return (lo_bits >> 16) | (hi_bits & jnp.uint32(0xFFFF0000))


def _combine_kernel(yg_ref, h2_ref, gate_ref, o_ref):
    half = D_MODEL // 2
    gates = gate_ref[...]
    lo_sum = h2_ref[:, :half]
    hi_sum = h2_ref[:, half:]
    for k in range(TOP_K):
        gate = gates[:, k:k + 1]
        lo, hi = _unpack_pair(yg_ref[k])
        lo_sum = lo_sum + gate * lo
        hi_sum = hi_sum + gate * hi
    o_ref[:, :half] = lo_sum
    o_ref[:, half:] = hi_sum


def _combine(yg, h2, gates):
    n = h2.shape[0]
    half = D_MODEL // 2
    row = lambda i: (i, 0)
    return pl.pallas_call(
        _combine_kernel,
        grid=(n // ROW_TILE,),
        in_specs=[
            pl.BlockSpec((TOP_K, ROW_TILE, half), lambda i: (0, i, 0)),
            pl.BlockSpec((ROW_TILE, D_MODEL), row),
            pl.BlockSpec((ROW_TILE, TOP_K), row),
        ],
        out_specs=pl.BlockSpec((ROW_TILE, D_MODEL), row),
        out_shape=jax.ShapeDtypeStruct((n, D_MODEL), F32),
        compiler_params=pltpu.CompilerParams(dimension_semantics=("arbitrary",),
                                             vmem_limit_bytes=VMEM_LIMIT),
        name="combine",
    )(yg, h2, gates)


def _work_schedule(counts, n_rows):
    n_tiles = n_rows // EXPERT_TILE
    n_work = n_tiles + N_EXPERTS - 1
    ends = jnp.cumsum(counts)
    starts = ends - counts
    count_le = lambda table, q: jnp.sum((table[None, :] <= q[:, None]).astype(jnp.int32), axis=1)
    tile_lo = jnp.arange(n_tiles, dtype=jnp.int32) * EXPERT_TILE
    first_e = count_le(ends, tile_lo)
    last_e = count_le(starts, tile_lo + (EXPERT_TILE - 1)) - 1
    per_tile = last_e - first_e + 1
    w_end = jnp.cumsum(per_tile)
    total = w_end[-1]
    w = jnp.arange(n_work, dtype=jnp.int32)
    wc = jnp.minimum(w, total - 1)
    tile_of = count_le(w_end, wc)
    in_tile = tile_of[:, None] == jnp.arange(n_tiles, dtype=jnp.int32)[None, :]
    expert_of = (wc + jnp.sum(jnp.where(in_tile, (first_e - (w_end - per_tile))[None, :], 0), axis=1)
                 ).astype(jnp.int32)
    valid = jnp.logical_and(w < total, jnp.take(counts, expert_of) > 0)
    prev_e = jnp.concatenate([jnp.full((1,), -1, jnp.int32), expert_of[:-1]])
    new_e = jnp.logical_and(valid, expert_of != prev_e)
    slot = (jnp.cumsum(new_e.astype(jnp.int32)) - 1) % 2
    later = jnp.logical_and(new_e[None, :], w[None, :] > w[:, None])
    next_w = jnp.min(jnp.where(later, w[None, :], n_work), axis=1)
    next_e = jnp.sum(jnp.where(next_w[:, None] == w[None, :], expert_of[None, :], 0), axis=1)
    next_e = jnp.where(next_w < n_work, next_e, -1)
    sched = jnp.stack([tile_of, expert_of, valid.astype(jnp.int32), new_e.astype(jnp.int32), slot,
                       next_e]).astype(jnp.int32)
    return sched, jnp.stack([starts, ends]).astype(jnp.int32)


def kernel(x, meta_tokens, norm1_g, w_in, q_norm_g, k_norm_g, attn_sink, dw_w, dw_b, conv_ln_g,
           conv_ln_b, w_out, norm2_g, router_w, router_b, w_gate_up, b_gate_up, w_down, b_down):
    assert norm1_g.shape[0] == 1, "single-layer trunk: meta-token query rows are not materialised"
    b, s, d = x.shape
    n = b * s
    x2d = x.reshape(n, d)

    scale = HEAD_DIM ** -0.5
    qg = (jnp.tile(q_norm_g[0], N_Q_HEADS) * scale).reshape(1, ATTN_W)
    kg = jnp.tile(k_norm_g[0], N_KV_HEADS).reshape(1, KV_W)
    head_of = np.arange(ATTN_W) // HEAD_DIM
    pq = jnp.asarray((head_of[:, None] == head_of[None, :]) / HEAD_DIM, BF16)
    pk = pq[:KV_W, :KV_W]
    g1 = norm1_g[0].reshape(1, d)
    w_in_b = w_in[0].astype(BF16)

    q, kd, vd, hc = _inproj(x2d, g1, w_in_b, qg, kg, pq, pk, ROW_TILE)
    _, kmeta, vmeta, hc_meta = _inproj(meta_tokens, g1, w_in_b, qg, kg, pq, pk, N_META)

    meta_pad = ((0, BLOCK - N_META), (0, 0))
    attn = _attention(attn_sink[0], q.reshape(b, s, ATTN_W), kd.reshape(b, s, 2 * KV_W),
                      vd.reshape(b, s, 2 * KV_W), jnp.pad(kmeta, meta_pad), jnp.pad(vmeta, meta_pad),
                      jnp.asarray(_alibi_bias()))

    w_out_b = w_out[0].astype(BF16)
    rw = router_w[0].T
    rwh = rw.astype(BF16)
    rwl = (rw - rwh.astype(F32)).astype(BF16)
    rb = router_b[0].reshape(N_EXPERTS, 1)
    tri = jnp.asarray(np.triu(np.ones((ROW_TILE, ROW_TILE), np.float32), 1), BF16)
    h2, xp, route, counts = _outproj(attn.reshape(n, ATTN_W), hc.reshape(b, s, CONV_W), hc_meta,
                                     dw_w[0].reshape(CONV_K, CONV_W), dw_b[0].reshape(1, CONV_W),
                                     conv_ln_g[0].reshape(1, CONV_W), conv_ln_b[0].reshape(1, CONV_W), x2d,
                                     w_out_b[:ATTN_W], w_out_b[ATTN_W:], norm2_g[0].reshape(1, d),
                                     rwh, rwl, rb, tri)

    counts_i = counts[:, 0].astype(jnp.int32)
    idx = route[:TOP_K].astype(jnp.int32)
    rank = route[2 * TOP_K:3 * TOP_K].astype(jnp.int32)
    gates = route[TOP_K:2 * TOP_K].T
    sched, bounds = _work_schedule(counts_i, n * TOP_K)
    chosen = idx[:, :, None] == jnp.arange(N_EXPERTS, dtype=jnp.int32)
    pos = rank + jnp.sum(jnp.where(chosen, bounds[0], 0), axis=-1)
    pos3 = pos.reshape(TOP_K, n // SC_CHUNK, SC_CHUNK).transpose(1, 0, 2)

    xs = _sc_dispatch(pos3, xp)
    y = _experts(sched, bounds, xs, w_gate_up[0],
                 b_gate_up[0].reshape(N_EXPERTS, 1, 2 * D_FF), w_down[0],
                 b_down[0].reshape(N_EXPERTS, 1, D_MODEL))
    out = _combine(_sc_collect(pos3, y), h2, gates)
    return out.reshape(b, s, d)
```

```python
import functools

import numpy as np
import jax
import jax.numpy as jnp
from jax import lax
from jax.experimental import pallas as pl
from jax.experimental.pallas import tpu as pltpu
from jax.experimental.pallas import tpu_sc as plsc

F32 = jnp.float32
BF16 = jnp.bfloat16

D_MODEL = 1024
N_META = 16
HEAD_DIM = 64
N_Q_HEADS = 8
N_KV_HEADS = 2
GROUP = N_Q_HEADS // N_KV_HEADS
ATTN_W = N_Q_HEADS * HEAD_DIM
KV_W = N_KV_HEADS * HEAD_DIM
CONV_W = D_MODEL - ATTN_W
IN_W = ATTN_W + 2 * KV_W + 2 * CONV_W
WINDOW = 128
BLOCK = 128
CONV_K = 31
CONV_PAD = CONV_K // 2
N_EXPERTS = 32
TOP_K = 4
D_FF = D_MODEL
SWIGLU_LIMIT = 7.0
SWIGLU_ALPHA = 1.702
RMS_EPS = 1e-6
LN_EPS = 1e-5
NEG_INF = -1e30

LANES = 128
ROW_TILE = 512
EXPERT_TILE = 512
SC_CORES = 2
SC_SUBCORES = 16
SC_WORKERS = SC_CORES * SC_SUBCORES
SC_CHUNK = 64
CONV_ROWS = 128
FF_CHUNK = 256
COMBINE_PARTS = 2
ROUTE_ROWS = 16
VMEM_LIMIT = 56 * 1024 * 1024


def _rms(x, eps):
    return x * lax.rsqrt(jnp.mean(x * x, axis=-1, keepdims=True) + eps)


def _inproj_kernel(x_ref, g1_ref, w_ref, qg_ref, kg_ref, pq_ref, pk_ref,
                   q_ref, k_ref, v_ref, hc_ref):
    x = x_ref[...]
    n = (_rms(x, RMS_EPS) * g1_ref[...]).astype(BF16)
    proj = jnp.dot(n, w_ref[...], preferred_element_type=F32)
    q = proj[:, :ATTN_W]
    k = proj[:, ATTN_W:ATTN_W + KV_W]
    v = proj[:, ATTN_W + KV_W:ATTN_W + 2 * KV_W]
    a = proj[:, ATTN_W + 2 * KV_W:ATTN_W + 2 * KV_W + CONV_W]
    g = proj[:, ATTN_W + 2 * KV_W + CONV_W:]
    qms = jnp.dot((q * q).astype(BF16), pq_ref[...], preferred_element_type=F32)
    kms = jnp.dot((k * k).astype(BF16), pk_ref[...], preferred_element_type=F32)
    q_ref[...] = (q * lax.rsqrt(qms + RMS_EPS) * qg_ref[...]).astype(BF16)
    kn = k * lax.rsqrt(kms + RMS_EPS) * kg_ref[...]
    lo = lax.broadcasted_iota(jnp.int32, kn.shape, 1) < HEAD_DIM
    ksw = pltpu.roll(kn, HEAD_DIM, 1)
    k_ref[...] = jnp.concatenate([jnp.where(lo, kn, ksw), jnp.where(lo, ksw, kn)], axis=1).astype(BF16)
    vsw = pltpu.roll(v, HEAD_DIM, 1)
    v_ref[...] = jnp.concatenate([jnp.where(lo, v, vsw), jnp.where(lo, vsw, v)], axis=1).astype(BF16)
    hc_ref[...] = a * jax.nn.sigmoid(g)


def _inproj(x2d, g1, w_in_b, qg, kg, pq, pk, tile):
    n = x2d.shape[0]
    const = lambda i: (0, 0)
    row = lambda i: (i, 0)
    return pl.pallas_call(
        _inproj_kernel,
        grid=(n // tile,),
        in_specs=[
            pl.BlockSpec((tile, D_MODEL), row),
            pl.BlockSpec((1, D_MODEL), const),
            pl.BlockSpec((D_MODEL, IN_W), const),
            pl.BlockSpec((1, ATTN_W), const),
            pl.BlockSpec((1, KV_W), const),
            pl.BlockSpec((ATTN_W, ATTN_W), const),
            pl.BlockSpec((KV_W, KV_W), const),
        ],
        out_specs=[
            pl.BlockSpec((tile, ATTN_W), row),
            pl.BlockSpec((tile, 2 * KV_W), row),
            pl.BlockSpec((tile, 2 * KV_W), row),
            pl.BlockSpec((tile, CONV_W), row),
        ],
        out_shape=[
            jax.ShapeDtypeStruct((n, ATTN_W), BF16),
            jax.ShapeDtypeStruct((n, 2 * KV_W), BF16),
            jax.ShapeDtypeStruct((n, 2 * KV_W), BF16),
            jax.ShapeDtypeStruct((n, CONV_W), F32),
        ],
        compiler_params=pltpu.CompilerParams(dimension_semantics=("arbitrary",),
                                             vmem_limit_bytes=VMEM_LIMIT),
        name="inproj",
    )(x2d, g1, w_in_b, qg, kg, pq, pk)


def _conv_fill(pad_ref, hm_ref, hc_ref, seq):
    tail = pad_ref.shape[0] - (N_META + seq)
    pad_ref[0:N_META, :] = hm_ref[...]
    pad_ref[N_META:N_META + seq, :] = hc_ref[0]
    pad_ref[N_META + seq:, :] = jnp.zeros((tail, CONV_W), F32)


def _conv_chunk(pad_ref, y_ref, w_ref, b_ref, lg_ref, lb_ref, base):
    first = N_META - CONV_PAD
    span = CONV_ROWS + 8
    for lt in range(CONV_W // LANES):
        ls = slice(lt * LANES, (lt + 1) * LANES)
        win = pad_ref[pl.ds(base, CONV_ROWS + 32), ls]
        acc = None
        for sub in range(8):
            part = None
            for al in range(4):
                k = 8 * al + sub - first
                if 0 <= k < CONV_K:
                    term = win[8 * al:8 * al + span] * w_ref[k:k + 1, ls]
                    part = term if part is None else part + term
            if sub:
                part = pltpu.roll(part, span - sub, 0)
            acc = part[:CONV_ROWS] if acc is None else acc + part[:CONV_ROWS]
        y_ref[:, ls] = acc
    y = y_ref[...] + b_ref[...]
    mu = jnp.mean(y, axis=-1, keepdims=True)
    yc = y - mu
    var = jnp.mean(yc * yc, axis=-1, keepdims=True)
    z = yc * lax.rsqrt(var + LN_EPS) * lg_ref[...] + lb_ref[...]
    return (z * jax.nn.sigmoid(z)).astype(BF16)


def _alibi_bias():
    qi = np.arange(BLOCK)[:, None]
    kj = np.arange(BLOCK)[None, :]
    dists = [qi + BLOCK - kj, np.abs(qi - kj), kj + BLOCK - qi]
    out = np.zeros((N_KV_HEADS, 4, GROUP * BLOCK, BLOCK), np.float32)
    for h in range(N_KV_HEADS):
        for g in range(GROUP):
            slope = 2.0 ** (-8.0 * (h * GROUP + g + 1) / N_Q_HEADS)
            for p, d in enumerate(dists):
                out[h, p, g * BLOCK:(g + 1) * BLOCK] = np.where(d <= WINDOW, -slope * d, NEG_INF)
    out[:, 3, :, N_META:] = NEG_INF
    return out


def _attn_kernel(sink_ref, q_ref, kp_ref, kc_ref, kn_ref, vp_ref, vc_ref, vn_ref,
                 km_ref, vm_ref, bias_ref, o_ref, *, n_blocks):
    i = pl.program_id(1)
    lo = lax.broadcasted_iota(jnp.int32, (BLOCK, LANES), 1) < HEAD_DIM
    edge_p = jnp.where(i == 0, NEG_INF, 0.0).astype(F32)
    edge_n = jnp.where(i == n_blocks - 1, NEG_INF, 0.0).astype(F32)
    nt = (((1,), (1,)), ((), ()))
    zero = jnp.zeros((BLOCK, LANES), BF16)
    for h in range(N_KV_HEADS):
        ks = slice(h * LANES, (h + 1) * LANES)
        rows = []
        for j in range(2):
            pair = q_ref[0, :, (2 * h + j) * LANES:(2 * h + j + 1) * LANES]
            rows.append(jnp.where(lo, pair, zero))
            rows.append(jnp.where(lo, zero, pair))
        qs = jnp.concatenate(rows, axis=0)
        s_p = lax.dot_general(qs, kp_ref[0, :, ks], nt, preferred_element_type=F32) + bias_ref[h, 0] + edge_p
        s_c = lax.dot_general(qs, kc_ref[0, :, ks], nt, preferred_element_type=F32) + bias_ref[h, 1]
        s_n = lax.dot_general(qs, kn_ref[0, :, ks], nt, preferred_element_type=F32) + bias_ref[h, 2] + edge_n
        s_m = lax.dot_general(qs, km_ref[:, ks], nt, preferred_element_type=F32) + bias_ref[h, 3]
        sink = jnp.concatenate(
            [jnp.full((BLOCK, 1), sink_ref[h * GROUP + g], F32) for g in range(GROUP)], axis=0)
        m = jnp.max(jnp.maximum(jnp.maximum(s_p, s_c), jnp.maximum(s_n, s_m)), axis=-1, keepdims=True)
        m = jnp.maximum(m, sink)
        p_p = jnp.exp(s_p - m)
        p_c = jnp.exp(s_c - m)
        p_n = jnp.exp(s_n - m)
        p_m = jnp.exp(s_m - m)
        denom = jnp.sum((p_p + p_c) + (p_n + p_m), axis=-1, keepdims=True) + jnp.exp(sink - m)
        o = (jnp.dot(p_p.astype(BF16), vp_ref[0, :, ks], preferred_element_type=F32)
             + jnp.dot(p_c.astype(BF16), vc_ref[0, :, ks], preferred_element_type=F32)
             + jnp.dot(p_n.astype(BF16), vn_ref[0, :, ks], preferred_element_type=F32)
             + jnp.dot(p_m.astype(BF16), vm_ref[:, ks], preferred_element_type=F32))
        o = o / denom
        for j in range(2):
            even = o[(2 * j) * BLOCK:(2 * j + 1) * BLOCK]
            odd = o[(2 * j + 1) * BLOCK:(2 * j + 2) * BLOCK]
            o_ref[0, :, (2 * h + j) * LANES:(2 * h + j + 1) * LANES] = jnp.where(lo, even, odd).astype(BF16)


def _attention(sink, q, kd, vd, kmeta, vmeta, bias):
    b, s, _ = q.shape
    nb = s // BLOCK
    cur = lambda bi, i: (bi, i, 0)
    prev = lambda bi, i: (bi, jnp.maximum(i - 1, 0), 0)
    nxt = lambda bi, i: (bi, jnp.minimum(i + 1, nb - 1), 0)
    const2 = lambda bi, i: (0, 0)
    kv_blk = (1, BLOCK, 2 * KV_W)
    return pl.pallas_call(
        functools.partial(_attn_kernel, n_blocks=nb),
        grid=(b, nb),
        in_specs=[
            pl.BlockSpec(memory_space=pltpu.SMEM),
            pl.BlockSpec((1, BLOCK, ATTN_W), cur),
            pl.BlockSpec(kv_blk, prev), pl.BlockSpec(kv_blk, cur), pl.BlockSpec(kv_blk, nxt),
            pl.BlockSpec(kv_blk, prev), pl.BlockSpec(kv_blk, cur), pl.BlockSpec(kv_blk, nxt),
            pl.BlockSpec((BLOCK, 2 * KV_W), const2),
            pl.BlockSpec((BLOCK, 2 * KV_W), const2),
            pl.BlockSpec(bias.shape, lambda bi, i: (0, 0, 0, 0)),
        ],
        out_specs=pl.BlockSpec((1, BLOCK, ATTN_W), cur),
        out_shape=jax.ShapeDtypeStruct((b, s, ATTN_W), BF16),
        compiler_params=pltpu.CompilerParams(dimension_semantics=("arbitrary", "arbitrary"),
                                             vmem_limit_bytes=VMEM_LIMIT),
        name="attention",
    )(sink, q, kd, kd, kd, vd, vd, vd, kmeta, vmeta, bias)


def _outproj_kernel(attn_ref, hc_ref, hm_ref, cw_ref, cb_ref, lg_ref, lb_ref, x_ref, wa_ref, wc_ref, g2_ref,
                    rwh_ref, rwl_ref, rb_ref, tri_ref, h2_ref, xp_ref, route_ref, cnt_ref,
                    run_ref, pad_ref, y_ref, conv_ref, *, seq):
    step = pl.program_id(0)
    tiles_per_seq = seq // ROW_TILE
    part = step % tiles_per_seq

    @pl.when(step == 0)
    def _():
        run_ref[...] = jnp.zeros_like(run_ref)

    @pl.when(part == 0)
    def _():
        _conv_fill(pad_ref, hm_ref, hc_ref, seq)

    for c in range(ROW_TILE // CONV_ROWS):
        base = pl.multiple_of(part * ROW_TILE + c * CONV_ROWS, CONV_ROWS)
        conv_ref[c * CONV_ROWS:(c + 1) * CONV_ROWS, :] = _conv_chunk(pad_ref, y_ref, cw_ref, cb_ref, lg_ref,
                                                                     lb_ref, base)
    mix = (jnp.dot(attn_ref[...], wa_ref[...], preferred_element_type=F32)
           + jnp.dot(conv_ref[...], wc_ref[...], preferred_element_type=F32))
    h2 = x_ref[...] + mix
    h2_ref[...] = h2
    hn = _rms(h2, RMS_EPS) * g2_ref[...]
    hn_hi = hn.astype(BF16)
    half = D_MODEL // 2
    xp_ref[...] = _pack_pair(hn[:, :half], hn[:, half:])
    hn_lo = (hn - hn_hi.astype(F32)).astype(BF16)
    nt = (((1,), (1,)), ((), ()))
    logits = (lax.dot_general(rwh_ref[...], hn_hi, nt, preferred_element_type=F32)
              + lax.dot_general(rwh_ref[...], hn_lo, nt, preferred_element_type=F32)
              + lax.dot_general(rwl_ref[...], hn_hi, nt, preferred_element_type=F32)) + rb_ref[...]
    expert = lax.broadcasted_iota(jnp.int32, logits.shape, 0).astype(F32)
    work = logits
    vals, sels = [], []
    for r in range(TOP_K):
        m = jnp.max(work, axis=0, keepdims=True)
        idx = jnp.min(jnp.where(work == m, expert, float(N_EXPERTS)), axis=0, keepdims=True)
        sel = expert == idx
        work = jnp.where(sel, -jnp.inf, work)
        route_ref[r:r + 1, :] = idx
        vals.append(m)
        sels.append(sel)
    onehot = jnp.where(jnp.logical_or(jnp.logical_or(sels[0], sels[1]), jnp.logical_or(sels[2], sels[3])),
                       1.0, 0.0)
    exps = [jnp.exp(v - vals[0]) for v in vals]
    tot = exps[0] + exps[1] + exps[2] + exps[3]
    before = jnp.dot(onehot.astype(BF16), tri_ref[...], preferred_element_type=F32) + run_ref[...]
    for r in range(TOP_K):
        route_ref[TOP_K + r:TOP_K + r + 1, :] = exps[r] / tot
        route_ref[2 * TOP_K + r:2 * TOP_K + r + 1, :] = jnp.sum(jnp.where(sels[r], before, 0.0), axis=0,
                                                               keepdims=True)
    route_ref[3 * TOP_K:, :] = jnp.zeros((ROUTE_ROWS - 3 * TOP_K, logits.shape[1]), F32)
    run_ref[...] = run_ref[...] + jnp.sum(onehot, axis=1, keepdims=True)
    cnt_ref[...] = run_ref[...]


def _outproj(attn2d, hc, hc_meta, dw_w, dw_b, ln_g, ln_b, x2d, wa, wc, g2, rwh, rwl, rb, tri):
    n = x2d.shape[0]
    seq = hc.shape[1]
    tile = ROW_TILE
    const = lambda i: (0, 0)
    row = lambda i: (i, 0)
    half = D_MODEL // 2
    return pl.pallas_call(
        functools.partial(_outproj_kernel, seq=seq),
        grid=(n // tile,),
        in_specs=[
            pl.BlockSpec((tile, ATTN_W), row),
            pl.BlockSpec((1, seq, CONV_W), lambda i: (i // (seq // tile), 0, 0)),
            pl.BlockSpec((N_META, CONV_W), const),
            pl.BlockSpec((CONV_K, CONV_W), const),
            pl.BlockSpec((1, CONV_W), const),
            pl.BlockSpec((1, CONV_W), const),
            pl.BlockSpec((1, CONV_W), const),
            pl.BlockSpec((tile, D_MODEL), row),
            pl.BlockSpec((ATTN_W, D_MODEL), const),
            pl.BlockSpec((CONV_W, D_MODEL), const),
            pl.BlockSpec((1, D_MODEL), const),
            pl.BlockSpec((N_EXPERTS, D_MODEL), const),
            pl.BlockSpec((N_EXPERTS, D_MODEL), const),
            pl.BlockSpec((N_EXPERTS, 1), const),
            pl.BlockSpec((tile, tile), const),
        ],
        out_specs=[
            pl.BlockSpec((tile, D_MODEL), row),
            pl.BlockSpec((tile, half), row),
            pl.BlockSpec((ROUTE_ROWS, tile), lambda i: (0, i)),
            pl.BlockSpec((N_EXPERTS, 1), const),
        ],
        out_shape=[
            jax.ShapeDtypeStruct((n, D_MODEL), F32),
            jax.ShapeDtypeStruct((n, half), jnp.uint32),
            jax.ShapeDtypeStruct((ROUTE_ROWS, n), F32),
            jax.ShapeDtypeStruct((N_EXPERTS, 1), F32),
        ],
        scratch_shapes=[pltpu.VMEM((N_EXPERTS, 1), F32),
                        pltpu.VMEM((N_META + seq + 32, CONV_W), F32),
                        pltpu.VMEM((CONV_ROWS, CONV_W), F32),
                        pltpu.VMEM((tile, CONV_W), BF16)],
        compiler_params=pltpu.CompilerParams(dimension_semantics=("arbitrary",),
                                             vmem_limit_bytes=VMEM_LIMIT),
        name="outproj",
    )(attn2d, hc, hc_meta, dw_w, dw_b, ln_g, ln_b, x2d, wa, wc, g2, rwh, rwl, rb, tri)


def _sc_mesh():
    return plsc.VectorSubcoreMesh(core_axis_name="c", subcore_axis_name="s",
                                  num_cores=SC_CORES, num_subcores=SC_SUBCORES)


def _sc_worker():
    return lax.axis_index("s") * SC_CORES + lax.axis_index("c")


def _sc_dispatch(pos3, xp):
    n, half = xp.shape
    per_worker = n // SC_CHUNK // SC_WORKERS

    @functools.partial(
        pl.kernel, mesh=_sc_mesh(),
        out_type=jax.ShapeDtypeStruct((n * TOP_K, half), jnp.uint32),
        scratch_types=[pltpu.VMEM((2, TOP_K, SC_CHUNK), jnp.int32),
                       pltpu.VMEM((2, SC_CHUNK, half), jnp.uint32),
                       pltpu.SemaphoreType.DMA((2,)),
                       pltpu.SemaphoreType.DMA],
        name="sc_dispatch")
    def run(pos_hbm, xp_hbm, xs_hbm, idx_v, rows_v, load_sem, scatter_sem):
        first = _sc_worker() * per_worker

        def load(j, buf):
            c = first + j
            return (pltpu.async_copy(pos_hbm.at[c], idx_v.at[buf], load_sem.at[buf]),
                    pltpu.async_copy(xp_hbm.at[pl.ds(c * SC_CHUNK, SC_CHUNK)], rows_v.at[buf], load_sem.at[buf]))

        loads = load(0, 0)
        for j in range(per_worker):
            buf = j % 2
            for cp in loads:
                cp.wait()
            if j + 1 < per_worker:
                loads = load(j + 1, 1 - buf)
            copies = [pltpu.async_copy(rows_v.at[buf], xs_hbm.at[idx_v.at[buf].at[k]], scatter_sem)
                      for k in range(TOP_K)]
            for cp in copies:
                cp.wait()

    return run(pos3, xp)


def _sc_collect(pos3, y):
    half = y.shape[1]
    n = pos3.shape[0] * SC_CHUNK
    per_worker = n // SC_CHUNK // SC_WORKERS

    @functools.partial(
        pl.kernel, mesh=_sc_mesh(),
        out_type=jax.ShapeDtypeStruct((TOP_K, n, half), jnp.uint32),
        scratch_types=[pltpu.VMEM((TOP_K, SC_CHUNK), jnp.int32),
                       pltpu.VMEM((2, SC_CHUNK, half), jnp.uint32),
                       pltpu.SemaphoreType.DMA((2,)),
                       pltpu.SemaphoreType.DMA((2,))],
        name="sc_collect")
    def run(pos_hbm, y_hbm, yg_hbm, idx_v, rows_v, gather_sem, store_sem):
        first = _sc_worker() * per_worker

        @pl.loop(0, per_worker)
        def _(j):
            c = first + j
            pltpu.sync_copy(pos_hbm.at[c], idx_v)

            def gather(k):
                return pltpu.async_copy(y_hbm.at[idx_v.at[k]], rows_v.at[k % 2], gather_sem.at[k % 2])

            gathers = [gather(0)]
            stores = []
            for k in range(TOP_K):
                if k + 1 < TOP_K:
                    if k >= 1:
                        stores[k - 1].wait()
                    gathers.append(gather(k + 1))
                gathers[k].wait()
                stores.append(pltpu.async_copy(rows_v.at[k % 2], yg_hbm.at[k, pl.ds(c * SC_CHUNK, SC_CHUNK)],
                                               store_sem.at[k % 2]))
            stores[TOP_K - 2].wait()
            stores[TOP_K - 1].wait()

    return run(pos3, y)


S_TILE, S_EXPERT, S_VALID, S_NEW, S_SLOT, S_NEXT = range(6)


def _expert_kernel(sched_ref, bound_ref, xs_ref, wgu_hbm, bgu_ref, wd_hbm, bd_ref, y_ref,
                   act_ref, wgu_buf, wd_buf, sem):
    w = pl.program_id(0)
    tile = sched_ref[S_TILE, w]
    e = sched_ref[S_EXPERT, w]
    slot = sched_ref[S_SLOT, w]
    first = jnp.logical_or(w == 0, sched_ref[S_TILE, jnp.maximum(w - 1, 0)] != tile)

    def weight_copies(expert, s):
        return (pltpu.make_async_copy(wgu_hbm.at[expert], wgu_buf.at[s], sem.at[0, s]),
                pltpu.make_async_copy(wd_hbm.at[expert], wd_buf.at[s], sem.at[1, s]))

    @pl.when(sched_ref[S_NEW, w] == 1)
    def _():
        @pl.when(w == 0)
        def _():
            for cp in weight_copies(e, slot):
                cp.start()

        nxt = sched_ref[S_NEXT, w]

        @pl.when(nxt >= 0)
        def _():
            for cp in weight_copies(nxt, 1 - slot):
                cp.start()

        for cp in weight_copies(e, slot):
            cp.wait()

    @pl.when(first)
    def _():
        y_ref[...] = jnp.zeros_like(y_ref)

    @pl.when(sched_ref[S_VALID, w] == 1)
    def _():
        half = D_MODEL // 2
        x_lo, x_hi = _unpack_pair(xs_ref[...])
        x = jnp.concatenate([x_lo.astype(BF16), x_hi.astype(BF16)], axis=1)
        for c in range(D_FF // FF_CHUNK):
            gc = slice(c * FF_CHUNK, (c + 1) * FF_CHUNK)
            uc = slice(D_FF + c * FF_CHUNK, D_FF + (c + 1) * FF_CHUNK)
            gate = jnp.dot(x, wgu_buf[slot, :, gc].astype(BF16), preferred_element_type=F32) + bgu_ref[0, :, gc]
            up = jnp.dot(x, wgu_buf[slot, :, uc].astype(BF16), preferred_element_type=F32) + bgu_ref[0, :, uc]
            gate = jnp.minimum(gate, SWIGLU_LIMIT)
            up = jnp.clip(up, -SWIGLU_LIMIT, SWIGLU_LIMIT)
            act_ref[:, gc] = (gate * jax.nn.sigmoid(SWIGLU_ALPHA * gate) * (up + 1.0)).astype(BF16)
        rows = tile * EXPERT_TILE + lax.broadcasted_iota(jnp.int32, (EXPERT_TILE, 1), 0)
        mine = jnp.logical_and(rows >= bound_ref[0, e], rows < bound_ref[1, e])
        for c in range(half // FF_CHUNK):
            lo_c = slice(c * FF_CHUNK, (c + 1) * FF_CHUNK)
            hi_c = slice(half + c * FF_CHUNK, half + (c + 1) * FF_CHUNK)
            y_lo = (jnp.dot(act_ref[...], wd_buf[slot, :, lo_c].astype(BF16), preferred_element_type=F32)
                    + bd_ref[0, :, lo_c])
            y_hi = (jnp.dot(act_ref[...], wd_buf[slot, :, hi_c].astype(BF16), preferred_element_type=F32)
                    + bd_ref[0, :, hi_c])
            y_ref[:, lo_c] = jnp.where(mine, _pack_pair(y_lo, y_hi), y_ref[:, lo_c])


def _experts(sched, bounds, xs, wgu, bgu, wd, bd):
    rows, half = xs.shape
    n_work = sched.shape[1]
    grid_spec = pltpu.PrefetchScalarGridSpec(
        num_scalar_prefetch=2,
        grid=(n_work,),
        in_specs=[
            pl.BlockSpec((EXPERT_TILE, half), lambda w, sc, bo: (sc[S_TILE, w], 0)),
            pl.BlockSpec(memory_space=pl.ANY),
            pl.BlockSpec((1, 1, 2 * D_FF), lambda w, sc, bo: (sc[S_EXPERT, w], 0, 0)),
            pl.BlockSpec(memory_space=pl.ANY),
            pl.BlockSpec((1, 1, D_MODEL), lambda w, sc, bo: (sc[S_EXPERT, w], 0, 0)),
        ],
        out_specs=pl.BlockSpec((EXPERT_TILE, half), lambda w, sc, bo: (sc[S_TILE, w], 0)),
        scratch_shapes=[pltpu.VMEM((EXPERT_TILE, D_FF), BF16),
                        pltpu.VMEM((2, D_MODEL, 2 * D_FF), F32),
                        pltpu.VMEM((2, D_FF, D_MODEL), F32),
                        pltpu.SemaphoreType.DMA((2, 2))],
    )
    return pl.pallas_call(
        _expert_kernel,
        grid_spec=grid_spec,
        out_shape=jax.ShapeDtypeStruct((rows, half), jnp.uint32),
        compiler_params=pltpu.CompilerParams(dimension_semantics=("arbitrary",),
                                             vmem_limit_bytes=VMEM_LIMIT),
        name="experts",
    )(sched, bounds, xs, wgu, bgu, wd, bd)


def _unpack_pair(packed):
    lo = lax.bitcast_convert_type(packed << 16, F32)
    hi = lax.bitcast_convert_type(packed & jnp.uint32(0xFFFF0000), F32)
    return lo, hi


def _pack_pair(lo, hi):
    lo_bits = lax.bitcast_convert_type(lo.astype(BF16).astype(F32), jnp.uint32)
    hi_bits = lax.bitcast_convert_type(hi.astype(BF16).astype(F32), jnp.uint32)
    return (lo_bits >> 16) | (hi_bits & jnp.uint32(0xFFFF0000))


def _combine_kernel(yg_ref, h2_ref, gate_ref, o_ref):
    half = D_MODEL // 2
    gates = gate_ref[...]
    lo_sum = h2_ref[:, :half]
    hi_sum = h2_ref[:, half:]
    for k in range(TOP_K):
        gate = gates[:, k:k + 1]
        lo, hi = _unpack_pair(yg_ref[k])
        lo_sum = lo_sum + gate * lo
        hi_sum = hi_sum + gate * hi
    o_ref[:, :half] = lo_sum
    o_ref[:, half:] = hi_sum


def _combine(yg, h2, gates, part):
    n = h2.shape[0]
    half = D_MODEL // 2
    steps = yg.shape[1] // ROW_TILE
    row = lambda i: (i + part * steps, 0)
    return pl.pallas_call(
        _combine_kernel,
        grid=(steps,),
        in_specs=[
            pl.BlockSpec((TOP_K, ROW_TILE, half), lambda i: (0, i, 0)),
            pl.BlockSpec((ROW_TILE, D_MODEL), row),
            pl.BlockSpec((ROW_TILE, TOP_K), row),
        ],
        out_specs=pl.BlockSpec((ROW_TILE, D_MODEL), row),
        out_shape=jax.ShapeDtypeStruct((n, D_MODEL), F32),
        input_output_aliases={1: 0},
        compiler_params=pltpu.CompilerParams(dimension_semantics=("arbitrary",),
                                             vmem_limit_bytes=VMEM_LIMIT),
        name="combine",
    )(yg, h2, gates)


def _work_schedule(counts, n_rows):
    n_tiles = n_rows // EXPERT_TILE
    n_work = n_tiles + N_EXPERTS - 1
    ends = jnp.cumsum(counts)
    starts = ends - counts
    count_le = lambda table, q: jnp.sum((table[None, :] <= q[:, None]).astype(jnp.int32), axis=1)
    tile_lo = jnp.arange(n_tiles, dtype=jnp.int32) * EXPERT_TILE
    first_e = count_le(ends, tile_lo)
    last_e = count_le(starts, tile_lo + (EXPERT_TILE - 1)) - 1
    per_tile = last_e - first_e + 1
    w_end = jnp.cumsum(per_tile)
    total = w_end[-1]
    w = jnp.arange(n_work, dtype=jnp.int32)
    wc = jnp.minimum(w, total - 1)
    tile_of = count_le(w_end, wc)
    in_tile = tile_of[:, None] == jnp.arange(n_tiles, dtype=jnp.int32)[None, :]
    expert_of = (wc + jnp.sum(jnp.where(in_tile, (first_e - (w_end - per_tile))[None, :], 0), axis=1)
                 ).astype(jnp.int32)
    valid = jnp.logical_and(w < total, jnp.take(counts, expert_of) > 0)
    prev_e = jnp.concatenate([jnp.full((1,), -1, jnp.int32), expert_of[:-1]])
    new_e = jnp.logical_and(valid, expert_of != prev_e)
    slot = (jnp.cumsum(new_e.astype(jnp.int32)) - 1) % 2
    later = jnp.logical_and(new_e[None, :], w[None, :] > w[:, None])
    next_w = jnp.min(jnp.where(later, w[None, :], n_work), axis=1)
    next_e = jnp.sum(jnp.where(next_w[:, None] == w[None, :], expert_of[None, :], 0), axis=1)
    next_e = jnp.where(next_w < n_work, next_e, -1)
    sched = jnp.stack([tile_of, expert_of, valid.astype(jnp.int32), new_e.astype(jnp.int32), slot,
                       next_e]).astype(jnp.int32)
    return sched, jnp.stack([starts, ends]).astype(jnp.int32)


def kernel(x, meta_tokens, norm1_g, w_in, q_norm_g, k_norm_g, attn_sink, dw_w, dw_b, conv_ln_g,
           conv_ln_b, w_out, norm2_g, router_w, router_b, w_gate_up, b_gate_up, w_down, b_down):
    assert norm1_g.shape[0] == 1, "single-layer trunk: meta-token query rows are not materialised"
    b, s, d = x.shape
    n = b * s
    x2d = x.reshape(n, d)

    scale = HEAD_DIM ** -0.5
    qg = (jnp.tile(q_norm_g[0], N_Q_HEADS) * scale).reshape(1, ATTN_W)
    kg = jnp.tile(k_norm_g[0], N_KV_HEADS).reshape(1, KV_W)
    head_of = np.arange(ATTN_W) // HEAD_DIM
    pq = jnp.asarray((head_of[:, None] == head_of[None, :]) / HEAD_DIM, BF16)
    pk = pq[:KV_W, :KV_W]
    g1 = norm1_g[0].reshape(1, d)
    w_in_b = w_in[0].astype(BF16)

    q, kd, vd, hc = _inproj(x2d, g1, w_in_b, qg, kg, pq, pk, ROW_TILE)
    _, kmeta, vmeta, hc_meta = _inproj(meta_tokens, g1, w_in_b, qg, kg, pq, pk, N_META)

    meta_pad = ((0, BLOCK - N_META), (0, 0))
    attn = _attention(attn_sink[0], q.reshape(b, s, ATTN_W), kd.reshape(b, s, 2 * KV_W),
                      vd.reshape(b, s, 2 * KV_W), jnp.pad(kmeta, meta_pad), jnp.pad(vmeta, meta_pad),
                      jnp.asarray(_alibi_bias()))

    w_out_b = w_out[0].astype(BF16)
    rw = router_w[0].T
    rwh = rw.astype(BF16)
    rwl = (rw - rwh.astype(F32)).astype(BF16)
    rb = router_b[0].reshape(N_EXPERTS, 1)
    tri = jnp.asarray(np.triu(np.ones((ROW_TILE, ROW_TILE), np.float32), 1), BF16)
    h2, xp, route, counts = _outproj(attn.reshape(n, ATTN_W), hc.reshape(b, s, CONV_W), hc_meta,
                                     dw_w[0].reshape(CONV_K, CONV_W), dw_b[0].reshape(1, CONV_W),
                                     conv_ln_g[0].reshape(1, CONV_W), conv_ln_b[0].reshape(1, CONV_W), x2d,
                                     w_out_b[:ATTN_W], w_out_b[ATTN_W:], norm2_g[0].reshape(1, d),
                                     rwh, rwl, rb, tri)

    counts_i = counts[:, 0].astype(jnp.int32)
    idx = route[:TOP_K].astype(jnp.int32)
    rank = route[2 * TOP_K:3 * TOP_K].astype(jnp.int32)
    gates = route[TOP_K:2 * TOP_K].T
    sched, bounds = _work_schedule(counts_i, n * TOP_K)
    chosen = idx[:, :, None] == jnp.arange(N_EXPERTS, dtype=jnp.int32)
    pos = rank + jnp.sum(jnp.where(chosen, bounds[0], 0), axis=-1)
    pos3 = pos.reshape(TOP_K, n // SC_CHUNK, SC_CHUNK).transpose(1, 0, 2)

    xs = _sc_dispatch(pos3, xp)
    y = _experts(sched, bounds, xs, w_gate_up[0],
                 b_gate_up[0].reshape(N_EXPERTS, 1, 2 * D_FF), w_down[0],
                 b_down[0].reshape(N_EXPERTS, 1, D_MODEL))
    out = h2
    chunks = pos3.shape[0] // COMBINE_PARTS
    for part in range(COMBINE_PARTS):
        out = _combine(_sc_collect(pos3[part * chunks:(part + 1) * chunks], y), out, gates, part)
    return out.reshape(b, s, d)
```

```python
import functools

import numpy as np
import jax
import jax.numpy as jnp
from jax import lax
from jax.experimental import pallas as pl
from jax.experimental.pallas import tpu as pltpu
from jax.experimental.pallas import tpu_sc as plsc

F32 = jnp.float32
BF16 = jnp.bfloat16

D_MODEL = 1024
N_META = 16
HEAD_DIM = 64
N_Q_HEADS = 8
N_KV_HEADS = 2
GROUP = N_Q_HEADS // N_KV_HEADS
ATTN_W = N_Q_HEADS * HEAD_DIM
KV_W = N_KV_HEADS * HEAD_DIM
CONV_W = D_MODEL - ATTN_W
IN_W = ATTN_W + 2 * KV_W + 2 * CONV_W
WINDOW = 128
BLOCK = 128
CONV_K = 31
CONV_PAD = CONV_K // 2
N_EXPERTS = 32
TOP_K = 4
D_FF = D_MODEL
SWIGLU_LIMIT = 7.0
SWIGLU_ALPHA = 1.702
RMS_EPS = 1e-6
LN_EPS = 1e-5
NEG_INF = -1e30

LANES = 128
ROW_TILE = 512
EXPERT_TILE = 512
SC_CORES = 2
SC_SUBCORES = 16
SC_WORKERS = SC_CORES * SC_SUBCORES
SC_CHUNK = 64
CONV_ROWS = 128
FF_CHUNK = 512
Q_BLOCKS = 4
COMBINE_PARTS = 2
ROUTE_ROWS = 16
VMEM_LIMIT = 56 * 1024 * 1024


def _rms(x, eps):
    return x * lax.rsqrt(jnp.mean(x * x, axis=-1, keepdims=True) + eps)


def _inproj_kernel(x_ref, g1_ref, w_ref, qg_ref, kg_ref, pq_ref, pk_ref,
                   q_ref, k_ref, v_ref, hc_ref):
    x = x_ref[...]
    n = (_rms(x, RMS_EPS) * g1_ref[...]).astype(BF16)
    proj = jnp.dot(n, w_ref[...], preferred_element_type=F32)
    q = proj[:, :ATTN_W]
    k = proj[:, ATTN_W:ATTN_W + KV_W]
    v = proj[:, ATTN_W + KV_W:ATTN_W + 2 * KV_W]
    a = proj[:, ATTN_W + 2 * KV_W:ATTN_W + 2 * KV_W + CONV_W]
    g = proj[:, ATTN_W + 2 * KV_W + CONV_W:]
    qms = jnp.dot((q * q).astype(BF16), pq_ref[...], preferred_element_type=F32)
    kms = jnp.dot((k * k).astype(BF16), pk_ref[...], preferred_element_type=F32)
    q_ref[...] = (q * lax.rsqrt(qms + RMS_EPS) * qg_ref[...]).astype(BF16)
    kn = k * lax.rsqrt(kms + RMS_EPS) * kg_ref[...]
    lo = lax.broadcasted_iota(jnp.int32, kn.shape, 1) < HEAD_DIM
    ksw = pltpu.roll(kn, HEAD_DIM, 1)
    k_ref[...] = jnp.concatenate([jnp.where(lo, kn, ksw), jnp.where(lo, ksw, kn)], axis=1).astype(BF16)
    vsw = pltpu.roll(v, HEAD_DIM, 1)
    v_ref[...] = jnp.concatenate([jnp.where(lo, v, vsw), jnp.where(lo, vsw, v)], axis=1).astype(BF16)
    hc_ref[...] = a * jax.nn.sigmoid(g)


def _inproj(x2d, g1, w_in_b, qg, kg, pq, pk, tile):
    n = x2d.shape[0]
    const = lambda i: (0, 0)
    row = lambda i: (i, 0)
    return pl.pallas_call(
        _inproj_kernel,
        grid=(n // tile,),
        in_specs=[
            pl.BlockSpec((tile, D_MODEL), row),
            pl.BlockSpec((1, D_MODEL), const),
            pl.BlockSpec((D_MODEL, IN_W), const),
            pl.BlockSpec((1, ATTN_W), const),
            pl.BlockSpec((1, KV_W), const),
            pl.BlockSpec((ATTN_W, ATTN_W), const),
            pl.BlockSpec((KV_W, KV_W), const),
        ],
        out_specs=[
            pl.BlockSpec((tile, ATTN_W), row),
            pl.BlockSpec((tile, 2 * KV_W), row),
            pl.BlockSpec((tile, 2 * KV_W), row),
            pl.BlockSpec((tile, CONV_W), row),
        ],
        out_shape=[
            jax.ShapeDtypeStruct((n, ATTN_W), BF16),
            jax.ShapeDtypeStruct((n, 2 * KV_W), BF16),
            jax.ShapeDtypeStruct((n, 2 * KV_W), BF16),
            jax.ShapeDtypeStruct((n, CONV_W), F32),
        ],
        compiler_params=pltpu.CompilerParams(dimension_semantics=("arbitrary",),
                                             vmem_limit_bytes=VMEM_LIMIT),
        name="inproj",
    )(x2d, g1, w_in_b, qg, kg, pq, pk)


def _conv_fill(pad_ref, hm_ref, hc_ref, seq):
    tail = pad_ref.shape[0] - (N_META + seq)
    pad_ref[0:N_META, :] = hm_ref[...]
    pad_ref[N_META:N_META + seq, :] = hc_ref[0]
    pad_ref[N_META + seq:, :] = jnp.zeros((tail, CONV_W), F32)


def _conv_chunk(pad_ref, y_ref, w_ref, b_ref, lg_ref, lb_ref, base):
    first = N_META - CONV_PAD
    span = CONV_ROWS + 8
    for lt in range(CONV_W // LANES):
        ls = slice(lt * LANES, (lt + 1) * LANES)
        win = pad_ref[pl.ds(base, CONV_ROWS + 32), ls]
        acc = None
        for sub in range(8):
            part = None
            for al in range(4):
                k = 8 * al + sub - first
                if 0 <= k < CONV_K:
                    term = win[8 * al:8 * al + span] * w_ref[k:k + 1, ls]
                    part = term if part is None else part + term
            if sub:
                part = pltpu.roll(part, span - sub, 0)
            acc = part[:CONV_ROWS] if acc is None else acc + part[:CONV_ROWS]
        y_ref[:, ls] = acc
    y = y_ref[...] + b_ref[...]
    mu = jnp.mean(y, axis=-1, keepdims=True)
    yc = y - mu
    var = jnp.mean(yc * yc, axis=-1, keepdims=True)
    z = yc * lax.rsqrt(var + LN_EPS) * lg_ref[...] + lb_ref[...]
    return (z * jax.nn.sigmoid(z)).astype(BF16)


def _alibi_bias():
    qi = np.arange(BLOCK)[:, None]
    kj = np.arange(BLOCK)[None, :]
    dists = [qi + BLOCK - kj, np.abs(qi - kj), kj + BLOCK - qi]
    out = np.zeros((N_KV_HEADS, 4, GROUP * BLOCK, BLOCK), np.float32)
    for h in range(N_KV_HEADS):
        for g in range(GROUP):
            slope = 2.0 ** (-8.0 * (h * GROUP + g + 1) / N_Q_HEADS)
            for p, d in enumerate(dists):
                out[h, p, g * BLOCK:(g + 1) * BLOCK] = np.where(d <= WINDOW, -slope * d, NEG_INF)
    out[:, 3, :, N_META:] = NEG_INF
    return out


def _attn_kernel(sink_ref, q_ref, kp_ref, kc_ref, kn_ref, vp_ref, vc_ref, vn_ref,
                 km_ref, vm_ref, bias_ref, o_ref, *, n_steps):
    i = pl.program_id(1)
    lo = lax.broadcasted_iota(jnp.int32, (BLOCK, LANES), 1) < HEAD_DIM
    edge_first = jnp.where(i == 0, NEG_INF, 0.0).astype(F32)
    edge_last = jnp.where(i == n_steps - 1, NEG_INF, 0.0).astype(F32)
    nt = (((1,), (1,)), ((), ()))
    zero = jnp.zeros((BLOCK, LANES), BF16)

    def key_block(before_ref, here_ref, after_ref, idx, ks):
        if idx < 0:
            return before_ref[0, :, ks]
        if idx >= Q_BLOCKS:
            return after_ref[0, :, ks]
        return here_ref[0, idx * BLOCK:(idx + 1) * BLOCK, ks]

    for qb in range(Q_BLOCKS):
        qr = slice(qb * BLOCK, (qb + 1) * BLOCK)
        for h in range(N_KV_HEADS):
            ks = slice(h * LANES, (h + 1) * LANES)
            rows = []
            for j in range(2):
                pair = q_ref[0, qr, (2 * h + j) * LANES:(2 * h + j + 1) * LANES]
                rows.append(jnp.where(lo, pair, zero))
                rows.append(jnp.where(lo, zero, pair))
            qs = jnp.concatenate(rows, axis=0)
            k_p, k_c, k_n = (key_block(kp_ref, kc_ref, kn_ref, qb + rel, ks) for rel in (-1, 0, 1))
            v_p, v_c, v_n = (key_block(vp_ref, vc_ref, vn_ref, qb + rel, ks) for rel in (-1, 0, 1))
            s_p = lax.dot_general(qs, k_p, nt, preferred_element_type=F32) + bias_ref[h, 0]
            s_c = lax.dot_general(qs, k_c, nt, preferred_element_type=F32) + bias_ref[h, 1]
            s_n = lax.dot_general(qs, k_n, nt, preferred_element_type=F32) + bias_ref[h, 2]
            s_m = lax.dot_general(qs, km_ref[:, ks], nt, preferred_element_type=F32) + bias_ref[h, 3]
            if qb == 0:
                s_p = s_p + edge_first
            if qb == Q_BLOCKS - 1:
                s_n = s_n + edge_last
            sink = jnp.concatenate(
                [jnp.full((BLOCK, 1), sink_ref[h * GROUP + g], F32) for g in range(GROUP)], axis=0)
            m = jnp.max(jnp.maximum(jnp.maximum(s_p, s_c), jnp.maximum(s_n, s_m)), axis=-1, keepdims=True)
            m = jnp.maximum(m, sink)
            p_p = jnp.exp(s_p - m)
            p_c = jnp.exp(s_c - m)
            p_n = jnp.exp(s_n - m)
            p_m = jnp.exp(s_m - m)
            denom = jnp.sum((p_p + p_c) + (p_n + p_m), axis=-1, keepdims=True) + jnp.exp(sink - m)
            o = (jnp.dot(p_p.astype(BF16), v_p, preferred_element_type=F32)
                 + jnp.dot(p_c.astype(BF16), v_c, preferred_element_type=F32)
                 + jnp.dot(p_n.astype(BF16), v_n, preferred_element_type=F32)
                 + jnp.dot(p_m.astype(BF16), vm_ref[:, ks], preferred_element_type=F32))
            o = o / denom
            for j in range(2):
                even = o[(2 * j) * BLOCK:(2 * j + 1) * BLOCK]
                odd = o[(2 * j + 1) * BLOCK:(2 * j + 2) * BLOCK]
                o_ref[0, qr, (2 * h + j) * LANES:(2 * h + j + 1) * LANES] = jnp.where(lo, even, odd).astype(BF16)


def _attention(sink, q, kd, vd, kmeta, vmeta, bias):
    b, s, _ = q.shape
    nb = s // BLOCK
    steps = nb // Q_BLOCKS
    here = lambda bi, i: (bi, i, 0)
    before = lambda bi, i: (bi, jnp.maximum(Q_BLOCKS * i - 1, 0), 0)
    after = lambda bi, i: (bi, jnp.minimum(Q_BLOCKS * i + Q_BLOCKS, nb - 1), 0)
    const2 = lambda bi, i: (0, 0)
    edge_blk = (1, BLOCK, 2 * KV_W)
    here_blk = (1, Q_BLOCKS * BLOCK, 2 * KV_W)
    return pl.pallas_call(
        functools.partial(_attn_kernel, n_steps=steps),
        grid=(b, steps),
        in_specs=[
            pl.BlockSpec(memory_space=pltpu.SMEM),
            pl.BlockSpec((1, Q_BLOCKS * BLOCK, ATTN_W), here),
            pl.BlockSpec(edge_blk, before), pl.BlockSpec(here_blk, here), pl.BlockSpec(edge_blk, after),
            pl.BlockSpec(edge_blk, before), pl.BlockSpec(here_blk, here), pl.BlockSpec(edge_blk, after),
            pl.BlockSpec((BLOCK, 2 * KV_W), const2),
            pl.BlockSpec((BLOCK, 2 * KV_W), const2),
            pl.BlockSpec(bias.shape, lambda bi, i: (0, 0, 0, 0)),
        ],
        out_specs=pl.BlockSpec((1, Q_BLOCKS * BLOCK, ATTN_W), here),
        out_shape=jax.ShapeDtypeStruct((b, s, ATTN_W), BF16),
        compiler_params=pltpu.CompilerParams(dimension_semantics=("arbitrary", "arbitrary"),
                                             vmem_limit_bytes=VMEM_LIMIT),
        name="attention",
    )(sink, q, kd, kd, kd, vd, vd, vd, kmeta, vmeta, bias)


def _outproj_kernel(attn_ref, hc_ref, hm_ref, cw_ref, cb_ref, lg_ref, lb_ref, x_ref, wa_ref, wc_ref, g2_ref,
                    rwh_ref, rwl_ref, rb_ref, tri_ref, h2_ref, xp_ref, route_ref, cnt_ref,
                    run_ref, pad_ref, y_ref, conv_ref, *, seq):
    step = pl.program_id(0)
    tiles_per_seq = seq // ROW_TILE
    part = step % tiles_per_seq

    @pl.when(step == 0)
    def _():
        run_ref[...] = jnp.zeros_like(run_ref)

    @pl.when(part == 0)
    def _():
        _conv_fill(pad_ref, hm_ref, hc_ref, seq)

    for c in range(ROW_TILE // CONV_ROWS):
        base = pl.multiple_of(part * ROW_TILE + c * CONV_ROWS, CONV_ROWS)
        conv_ref[c * CONV_ROWS:(c + 1) * CONV_ROWS, :] = _conv_chunk(pad_ref, y_ref, cw_ref, cb_ref, lg_ref,
                                                                     lb_ref, base)
    mix = (jnp.dot(attn_ref[...], wa_ref[...], preferred_element_type=F32)
           + jnp.dot(conv_ref[...], wc_ref[...], preferred_element_type=F32))
    h2 = x_ref[...] + mix
    h2_ref[...] = h2
    hn = _rms(h2, RMS_EPS) * g2_ref[...]
    hn_hi = hn.astype(BF16)
    half = D_MODEL // 2
    xp_ref[...] = _pack_pair(hn[:, :half], hn[:, half:])
    hn_lo = (hn - hn_hi.astype(F32)).astype(BF16)
    nt = (((1,), (1,)), ((), ()))
    logits = (lax.dot_general(rwh_ref[...], hn_hi, nt, preferred_element_type=F32)
              + lax.dot_general(rwh_ref[...], hn_lo, nt, preferred_element_type=F32)
              + lax.dot_general(rwl_ref[...], hn_hi, nt, preferred_element_type=F32)) + rb_ref[...]
    expert = lax.broadcasted_iota(jnp.int32, logits.shape, 0).astype(F32)
    work = logits
    vals, sels = [], []
    for r in range(TOP_K):
        m = jnp.max(work, axis=0, keepdims=True)
        idx = jnp.min(jnp.where(work == m, expert, float(N_EXPERTS)), axis=0, keepdims=True)
        sel = expert == idx
        work = jnp.where(sel, -jnp.inf, work)
        route_ref[r:r + 1, :] = idx
        vals.append(m)
        sels.append(sel)
    onehot = jnp.where(jnp.logical_or(jnp.logical_or(sels[0], sels[1]), jnp.logical_or(sels[2], sels[3])),
                       1.0, 0.0)
    exps = [jnp.exp(v - vals[0]) for v in vals]
    tot = exps[0] + exps[1] + exps[2] + exps[3]
    before = jnp.dot(onehot.astype(BF16), tri_ref[...], preferred_element_type=F32) + run_ref[...]
    for r in range(TOP_K):
        route_ref[TOP_K + r:TOP_K + r + 1, :] = exps[r] / tot
        route_ref[2 * TOP_K + r:2 * TOP_K + r + 1, :] = jnp.sum(jnp.where(sels[r], before, 0.0), axis=0,
                                                               keepdims=True)
    route_ref[3 * TOP_K:, :] = jnp.zeros((ROUTE_ROWS - 3 * TOP_K, logits.shape[1]), F32)
    run_ref[...] = run_ref[...] + jnp.sum(onehot, axis=1, keepdims=True)
    cnt_ref[...] = run_ref[...]


def _outproj(attn2d, hc, hc_meta, dw_w, dw_b, ln_g, ln_b, x2d, wa, wc, g2, rwh, rwl, rb, tri):
    n = x2d.shape[0]
    seq = hc.shape[1]
    tile = ROW_TILE
    const = lambda i: (0, 0)
    row = lambda i: (i, 0)
    half = D_MODEL // 2
    return pl.pallas_call(
        functools.partial(_outproj_kernel, seq=seq),
        grid=(n // tile,),
        in_specs=[
            pl.BlockSpec((tile, ATTN_W), row),
            pl.BlockSpec((1, seq, CONV_W), lambda i: (i // (seq // tile), 0, 0)),
            pl.BlockSpec((N_META, CONV_W), const),
            pl.BlockSpec((CONV_K, CONV_W), const),
            pl.BlockSpec((1, CONV_W), const),
            pl.BlockSpec((1, CONV_W), const),
            pl.BlockSpec((1, CONV_W), const),
            pl.BlockSpec((tile, D_MODEL), row),
            pl.BlockSpec((ATTN_W, D_MODEL), const),
            pl.BlockSpec((CONV_W, D_MODEL), const),
            pl.BlockSpec((1, D_MODEL), const),
            pl.BlockSpec((N_EXPERTS, D_MODEL), const),
            pl.BlockSpec((N_EXPERTS, D_MODEL), const),
            pl.BlockSpec((N_EXPERTS, 1), const),
            pl.BlockSpec((tile, tile), const),
        ],
        out_specs=[
            pl.BlockSpec((tile, D_MODEL), row),
            pl.BlockSpec((tile, half), row),
            pl.BlockSpec((ROUTE_ROWS, tile), lambda i: (0, i)),
            pl.BlockSpec((N_EXPERTS, 1), const),
        ],
        out_shape=[
            jax.ShapeDtypeStruct((n, D_MODEL), F32),
            jax.ShapeDtypeStruct((n, half), jnp.uint32),
            jax.ShapeDtypeStruct((ROUTE_ROWS, n), F32),
            jax.ShapeDtypeStruct((N_EXPERTS, 1), F32),
        ],
        scratch_shapes=[pltpu.VMEM((N_EXPERTS, 1), F32),
                        pltpu.VMEM((N_META + seq + 32, CONV_W), F32),
                        pltpu.VMEM((CONV_ROWS, CONV_W), F32),
                        pltpu.VMEM((tile, CONV_W), BF16)],
        compiler_params=pltpu.CompilerParams(dimension_semantics=("arbitrary",),
                                             vmem_limit_bytes=VMEM_LIMIT),
        name="outproj",
    )(attn2d, hc, hc_meta, dw_w, dw_b, ln_g, ln_b, x2d, wa, wc, g2, rwh, rwl, rb, tri)


def _sc_mesh():
    return plsc.VectorSubcoreMesh(core_axis_name="c", subcore_axis_name="s",
                                  num_cores=SC_CORES, num_subcores=SC_SUBCORES)


def _sc_worker():
    return lax.axis_index("s") * SC_CORES + lax.axis_index("c")


def _sc_dispatch(pos3, xp):
    n, half = xp.shape
    per_worker = n // SC_CHUNK // SC_WORKERS

    @functools.partial(
        pl.kernel, mesh=_sc_mesh(),
        out_type=jax.ShapeDtypeStruct((n * TOP_K, half), jnp.uint32),
        scratch_types=[pltpu.VMEM((2, TOP_K, SC_CHUNK), jnp.int32),
                       pltpu.VMEM((2, SC_CHUNK, half), jnp.uint32),
                       pltpu.SemaphoreType.DMA((2,)),
                       pltpu.SemaphoreType.DMA],
        name="sc_dispatch")
    def run(pos_hbm, xp_hbm, xs_hbm, idx_v, rows_v, load_sem, scatter_sem):
        first = _sc_worker() * per_worker

        def load(j, buf):
            c = first + j
            return (pltpu.async_copy(pos_hbm.at[c], idx_v.at[buf], load_sem.at[buf]),
                    pltpu.async_copy(xp_hbm.at[pl.ds(c * SC_CHUNK, SC_CHUNK)], rows_v.at[buf], load_sem.at[buf]))

        loads = load(0, 0)
        for j in range(per_worker):
            buf = j % 2
            for cp in loads:
                cp.wait()
            if j + 1 < per_worker:
                loads = load(j + 1, 1 - buf)
            copies = [pltpu.async_copy(rows_v.at[buf], xs_hbm.at[idx_v.at[buf].at[k]], scatter_sem)
                      for k in range(TOP_K)]
            for cp in copies:
                cp.wait()

    return run(pos3, xp)


def _sc_collect(pos3, y):
    half = y.shape[1]
    n = pos3.shape[0] * SC_CHUNK
    per_worker = n // SC_CHUNK // SC_WORKERS

    @functools.partial(
        pl.kernel, mesh=_sc_mesh(),
        out_type=jax.ShapeDtypeStruct((TOP_K, n, half), jnp.uint32),
        scratch_types=[pltpu.VMEM((TOP_K, SC_CHUNK), jnp.int32),
                       pltpu.VMEM((2, SC_CHUNK, half), jnp.uint32),
                       pltpu.SemaphoreType.DMA((2,)),
                       pltpu.SemaphoreType.DMA((2,))],
        name="sc_collect")
    def run(pos_hbm, y_hbm, yg_hbm, idx_v, rows_v, gather_sem, store_sem):
        first = _sc_worker() * per_worker

        @pl.loop(0, per_worker)
        def _(j):
            c = first + j
            pltpu.sync_copy(pos_hbm.at[c], idx_v)

            def gather(k):
                return pltpu.async_copy(y_hbm.at[idx_v.at[k]], rows_v.at[k % 2], gather_sem.at[k % 2])

            gathers = [gather(0)]
            stores = []
            for k in range(TOP_K):
                if k + 1 < TOP_K:
                    if k >= 1:
                        stores[k - 1].wait()
                    gathers.append(gather(k + 1))
                gathers[k].wait()
                stores.append(pltpu.async_copy(rows_v.at[k % 2], yg_hbm.at[k, pl.ds(c * SC_CHUNK, SC_CHUNK)],
                                               store_sem.at[k % 2]))
            stores[TOP_K - 2].wait()
            stores[TOP_K - 1].wait()

    return run(pos3, y)


S_TILE, S_EXPERT, S_VALID, S_NEW, S_SLOT, S_NEXT = range(6)


def _expert_kernel(sched_ref, bound_ref, xs_ref, wgu_hbm, bgu_ref, wd_hbm, bd_ref, y_ref,
                   act_ref, wgu_buf, wd_buf, sem):
    w = pl.program_id(0)
    tile = sched_ref[S_TILE, w]
    e = sched_ref[S_EXPERT, w]
    slot = sched_ref[S_SLOT, w]
    first = jnp.logical_or(w == 0, sched_ref[S_TILE, jnp.maximum(w - 1, 0)] != tile)

    def weight_copies(expert, s):
        return (pltpu.make_async_copy(wgu_hbm.at[expert], wgu_buf.at[s], sem.at[0, s]),
                pltpu.make_async_copy(wd_hbm.at[expert], wd_buf.at[s], sem.at[1, s]))

    @pl.when(sched_ref[S_NEW, w] == 1)
    def _():
        @pl.when(w == 0)
        def _():
            for cp in weight_copies(e, slot):
                cp.start()

        nxt = sched_ref[S_NEXT, w]

        @pl.when(nxt >= 0)
        def _():
            for cp in weight_copies(nxt, 1 - slot):
                cp.start()

        for cp in weight_copies(e, slot):
            cp.wait()

    @pl.when(first)
    def _():
        y_ref[...] = jnp.zeros_like(y_ref)

    @pl.when(sched_ref[S_VALID, w] == 1)
    def _():
        half = D_MODEL // 2
        x_lo, x_hi = _unpack_pair(xs_ref[...])
        x = jnp.concatenate([x_lo.astype(BF16), x_hi.astype(BF16)], axis=1)
        for c in range(D_FF // FF_CHUNK):
            gc = slice(c * FF_CHUNK, (c + 1) * FF_CHUNK)
            uc = slice(D_FF + c * FF_CHUNK, D_FF + (c + 1) * FF_CHUNK)
            gate = jnp.dot(x, wgu_buf[slot, :, gc].astype(BF16), preferred_element_type=F32) + bgu_ref[0, :, gc]
            up = jnp.dot(x, wgu_buf[slot, :, uc].astype(BF16), preferred_element_type=F32) + bgu_ref[0, :, uc]
            gate = jnp.minimum(gate, SWIGLU_LIMIT)
            up = jnp.clip(up, -SWIGLU_LIMIT, SWIGLU_LIMIT)
            act_ref[:, gc] = (gate * jax.nn.sigmoid(SWIGLU_ALPHA * gate) * (up + 1.0)).astype(BF16)
        rows = tile * EXPERT_TILE + lax.broadcasted_iota(jnp.int32, (EXPERT_TILE, 1), 0)
        mine = jnp.logical_and(rows >= bound_ref[0, e], rows < bound_ref[1, e])
        for c in range(half // FF_CHUNK):
            lo_c = slice(c * FF_CHUNK, (c + 1) * FF_CHUNK)
            hi_c = slice(half + c * FF_CHUNK, half + (c + 1) * FF_CHUNK)
            y_lo = (jnp.dot(act_ref[...], wd_buf[slot, :, lo_c].astype(BF16), preferred_element_type=F32)
                    + bd_ref[0, :, lo_c])
            y_hi = (jnp.dot(act_ref[...], wd_buf[slot, :, hi_c].astype(BF16), preferred_element_type=F32)
                    + bd_ref[0, :, hi_c])
            y_ref[:, lo_c] = jnp.where(mine, _pack_pair(y_lo, y_hi), y_ref[:, lo_c])


def _experts(sched, bounds, xs, wgu, bgu, wd, bd):
    rows, half = xs.shape
    n_work = sched.shape[1]
    grid_spec = pltpu.PrefetchScalarGridSpec(
        num_scalar_prefetch=2,
        grid=(n_work,),
        in_specs=[
            pl.BlockSpec((EXPERT_TILE, half), lambda w, sc, bo: (sc[S_TILE, w], 0)),
            pl.BlockSpec(memory_space=pl.ANY),
            pl.BlockSpec((1, 1, 2 * D_FF), lambda w, sc, bo: (sc[S_EXPERT, w], 0, 0)),
            pl.BlockSpec(memory_space=pl.ANY),
            pl.BlockSpec((1, 1, D_MODEL), lambda w, sc, bo: (sc[S_EXPERT, w], 0, 0)),
        ],
        out_specs=pl.BlockSpec((EXPERT_TILE, half), lambda w, sc, bo: (sc[S_TILE, w], 0)),
        scratch_shapes=[pltpu.VMEM((EXPERT_TILE, D_FF), BF16),
                        pltpu.VMEM((2, D_MODEL, 2 * D_FF), F32),
                        pltpu.VMEM((2, D_FF, D_MODEL), F32),
                        pltpu.SemaphoreType.DMA((2, 2))],
    )
    return pl.pallas_call(
        _expert_kernel,
        grid_spec=grid_spec,
        out_shape=jax.ShapeDtypeStruct((rows, half), jnp.uint32),
        compiler_params=pltpu.CompilerParams(dimension_semantics=("arbitrary",),
                                             vmem_limit_bytes=VMEM_LIMIT),
        name="experts",
    )(sched, bounds, xs, wgu, bgu, wd, bd)


def _unpack_pair(packed):
    lo = lax.bitcast_convert_type(packed << 16, F32)
    hi = lax.bitcast_convert_type(packed & jnp.uint32(0xFFFF0000), F32)
    return lo, hi


def _pack_pair(lo, hi):
    lo_bits = lax.bitcast_convert_type(lo.astype(BF16).astype(F32), jnp.uint32)
    hi_bits = lax.bitcast_convert_type(hi.astype(BF16).astype(F32), jnp.uint32)
    return (lo_bits >> 16) | (hi_bits & jnp.uint32(0xFFFF0000))


def _combine_kernel(yg_ref, h2_ref, gate_ref, o_ref):
    half = D_MODEL // 2
    gates = gate_ref[...]
    lo_sum = h2_ref[:, :half]
    hi_sum = h2_ref[:, half:]
    for k in range(TOP_K):
        gate = gates[:, k:k + 1]
        lo, hi = _unpack_pair(yg_ref[k])
        lo_sum = lo_sum + gate * lo
        hi_sum = hi_sum + gate * hi
    o_ref[:, :half] = lo_sum
    o_ref[:, half:] = hi_sum


def _combine(yg, h2, gates, part):
    n = h2.shape[0]
    half = D_MODEL // 2
    steps = yg.shape[1] // ROW_TILE
    row = lambda i: (i + part * steps, 0)
    return pl.pallas_call(
        _combine_kernel,
        grid=(steps,),
        in_specs=[
            pl.BlockSpec((TOP_K, ROW_TILE, half), lambda i: (0, i, 0)),
            pl.BlockSpec((ROW_TILE, D_MODEL), row),
            pl.BlockSpec((ROW_TILE, TOP_K), row),
        ],
        out_specs=pl.BlockSpec((ROW_TILE, D_MODEL), row),
        out_shape=jax.ShapeDtypeStruct((n, D_MODEL), F32),
        input_output_aliases={1: 0},
        compiler_params=pltpu.CompilerParams(dimension_semantics=("arbitrary",),
                                             vmem_limit_bytes=VMEM_LIMIT),
        name="combine",
    )(yg, h2, gates)


def _work_schedule(counts, n_rows):
    n_tiles = n_rows // EXPERT_TILE
    n_work = n_tiles + N_EXPERTS - 1
    ends = jnp.cumsum(counts)
    starts = ends - counts
    count_le = lambda table, q: jnp.sum((table[None, :] <= q[:, None]).astype(jnp.int32), axis=1)
    tile_lo = jnp.arange(n_tiles, dtype=jnp.int32) * EXPERT_TILE
    first_e = count_le(ends, tile_lo)
    last_e = count_le(starts, tile_lo + (EXPERT_TILE - 1)) - 1
    per_tile = last_e - first_e + 1
    w_end = jnp.cumsum(per_tile)
    total = w_end[-1]
    w = jnp.arange(n_work, dtype=jnp.int32)
    wc = jnp.minimum(w, total - 1)
    tile_of = count_le(w_end, wc)
    in_tile = tile_of[:, None] == jnp.arange(n_tiles, dtype=jnp.int32)[None, :]
    expert_of = (wc + jnp.sum(jnp.where(in_tile, (first_e - (w_end - per_tile))[None, :], 0), axis=1)
                 ).astype(jnp.int32)
    valid = jnp.logical_and(w < total, jnp.take(counts, expert_of) > 0)
    prev_e = jnp.concatenate([jnp.full((1,), -1, jnp.int32), expert_of[:-1]])
    new_e = jnp.logical_and(valid, expert_of != prev_e)
    slot = (jnp.cumsum(new_e.astype(jnp.int32)) - 1) % 2
    later = jnp.logical_and(new_e[None, :], w[None, :] > w[:, None])
    next_w = jnp.min(jnp.where(later, w[None, :], n_work), axis=1)
    next_e = jnp.sum(jnp.where(next_w[:, None] == w[None, :], expert_of[None, :], 0), axis=1)
    next_e = jnp.where(next_w < n_work, next_e, -1)
    sched = jnp.stack([tile_of, expert_of, valid.astype(jnp.int32), new_e.astype(jnp.int32), slot,
                       next_e]).astype(jnp.int32)
    return sched, jnp.stack([starts, ends]).astype(jnp.int32)


def kernel(x, meta_tokens, norm1_g, w_in, q_norm_g, k_norm_g, attn_sink, dw_w, dw_b, conv_ln_g,
           conv_ln_b, w_out, norm2_g, router_w, router_b, w_gate_up, b_gate_up, w_down, b_down):
    assert norm1_g.shape[0] == 1, "single-layer trunk: meta-token query rows are not materialised"
    b, s, d = x.shape
    n = b * s
    x2d = x.reshape(n, d)

    scale = HEAD_DIM ** -0.5
    qg = (jnp.tile(q_norm_g[0], N_Q_HEADS) * scale).reshape(1, ATTN_W)
    kg = jnp.tile(k_norm_g[0], N_KV_HEADS).reshape(1, KV_W)
    head_of = np.arange(ATTN_W) // HEAD_DIM
    pq = jnp.asarray((head_of[:, None] == head_of[None, :]) / HEAD_DIM, BF16)
    pk = pq[:KV_W, :KV_W]
    g1 = norm1_g[0].reshape(1, d)
    w_in_b = w_in[0].astype(BF16)

    q, kd, vd, hc = _inproj(x2d, g1, w_in_b, qg, kg, pq, pk, ROW_TILE)
    _, kmeta, vmeta, hc_meta = _inproj(meta_tokens, g1, w_in_b, qg, kg, pq, pk, N_META)

    meta_pad = ((0, BLOCK - N_META), (0, 0))
    attn = _attention(attn_sink[0], q.reshape(b, s, ATTN_W), kd.reshape(b, s, 2 * KV_W),
                      vd.reshape(b, s, 2 * KV_W), jnp.pad(kmeta, meta_pad), jnp.pad(vmeta, meta_pad),
                      jnp.asarray(_alibi_bias()))

    w_out_b = w_out[0].astype(BF16)
    rw = router_w[0].T
    rwh = rw.astype(BF16)
    rwl = (rw - rwh.astype(F32)).astype(BF16)
    rb = router_b[0].reshape(N_EXPERTS, 1)
    tri = jnp.asarray(np.triu(np.ones((ROW_TILE, ROW_TILE), np.float32), 1), BF16)
    h2, xp, route, counts = _outproj(attn.reshape(n, ATTN_W), hc.reshape(b, s, CONV_W), hc_meta,
                                     dw_w[0].reshape(CONV_K, CONV_W), dw_b[0].reshape(1, CONV_W),
                                     conv_ln_g[0].reshape(1, CONV_W), conv_ln_b[0].reshape(1, CONV_W), x2d,
                                     w_out_b[:ATTN_W], w_out_b[ATTN_W:], norm2_g[0].reshape(1, d),
                                     rwh, rwl, rb, tri)

    counts_i = counts[:, 0].astype(jnp.int32)
    idx = route[:TOP_K].astype(jnp.int32)
    rank = route[2 * TOP_K:3 * TOP_K].astype(jnp.int32)
    gates = route[TOP_K:2 * TOP_K].T
    sched, bounds = _work_schedule(counts_i, n * TOP_K)
    chosen = idx[:, :, None] == jnp.arange(N_EXPERTS, dtype=jnp.int32)
    pos = rank + jnp.sum(jnp.where(chosen, bounds[0], 0), axis=-1)
    pos3 = pos.reshape(TOP_K, n // SC_CHUNK, SC_CHUNK).transpose(1, 0, 2)

    xs = _sc_dispatch(pos3, xp)
    y = _experts(sched, bounds, xs, w_gate_up[0],
                 b_gate_up[0].reshape(N_EXPERTS, 1, 2 * D_FF), w_down[0],
                 b_down[0].reshape(N_EXPERTS, 1, D_MODEL))
    out = h2
    chunks = pos3.shape[0] // COMBINE_PARTS
    for part in range(COMBINE_PARTS):
        out = _combine(_sc_collect(pos3[part * chunks:(part + 1) * chunks], y), out, gates, part)
    return out.reshape(b, s, d)
```

```python
import functools

import numpy as np
import jax
import jax.numpy as jnp
from jax import lax
from jax.experimental import pallas as pl
from jax.experimental.pallas import tpu as pltpu
from jax.experimental.pallas import tpu_sc as plsc

F32 = jnp.float32
BF16 = jnp.bfloat16

D_MODEL = 1024
N_META = 16
HEAD_DIM = 64
N_Q_HEADS = 8
N_KV_HEADS = 2
GROUP = N_Q_HEADS // N_KV_HEADS
ATTN_W = N_Q_HEADS * HEAD_DIM
KV_W = N_KV_HEADS * HEAD_DIM
CONV_W = D_MODEL - ATTN_W
IN_W = ATTN_W + 2 * KV_W + 2 * CONV_W
WINDOW = 128
BLOCK = 128
CONV_K = 31
CONV_PAD = CONV_K // 2
N_EXPERTS = 32
TOP_K = 4
D_FF = D_MODEL
SWIGLU_LIMIT = 7.0
SWIGLU_ALPHA = 1.702
RMS_EPS = 1e-6
LN_EPS = 1e-5
NEG_INF = -1e30

LANES = 128
ROW_TILE = 512
EXPERT_TILE = 512
SC_CORES = 2
SC_SUBCORES = 16
SC_WORKERS = SC_CORES * SC_SUBCORES
SC_CHUNK = 64
CONV_ROWS = 128
FF_CHUNK = 512
Q_BLOCKS = 4
COMBINE_PARTS = 2
ROUTE_ROWS = 16
VMEM_LIMIT = 56 * 1024 * 1024


def _rms(x, eps):
    return x * lax.rsqrt(jnp.mean(x * x, axis=-1, keepdims=True) + eps)


def _inproj_kernel(x_ref, g1_ref, w_ref, qg_ref, kg_ref, pq_ref, pk_ref,
                   q_ref, k_ref, v_ref, hc_ref):
    x = x_ref[...]
    n = (_rms(x, RMS_EPS) * g1_ref[...]).astype(BF16)
    proj = jnp.dot(n, w_ref[...], preferred_element_type=F32)
    q = proj[:, :ATTN_W]
    k = proj[:, ATTN_W:ATTN_W + KV_W]
    v = proj[:, ATTN_W + KV_W:ATTN_W + 2 * KV_W]
    a = proj[:, ATTN_W + 2 * KV_W:ATTN_W + 2 * KV_W + CONV_W]
    g = proj[:, ATTN_W + 2 * KV_W + CONV_W:]
    qms = jnp.dot((q * q).astype(BF16), pq_ref[...], preferred_element_type=F32)
    kms = jnp.dot((k * k).astype(BF16), pk_ref[...], preferred_element_type=F32)
    q_ref[...] = (q * lax.rsqrt(qms + RMS_EPS) * qg_ref[...]).astype(BF16)
    kn = k * lax.rsqrt(kms + RMS_EPS) * kg_ref[...]
    lo = lax.broadcasted_iota(jnp.int32, kn.shape, 1) < HEAD_DIM
    ksw = pltpu.roll(kn, HEAD_DIM, 1)
    k_ref[...] = jnp.concatenate([jnp.where(lo, kn, ksw), jnp.where(lo, ksw, kn)], axis=1).astype(BF16)
    vsw = pltpu.roll(v, HEAD_DIM, 1)
    v_ref[...] = jnp.concatenate([jnp.where(lo, v, vsw), jnp.where(lo, vsw, v)], axis=1).astype(BF16)
    hc_ref[...] = a * jax.nn.sigmoid(g)


def _inproj(x2d, g1, w_in_b, qg, kg, pq, pk, tile):
    n = x2d.shape[0]
    const = lambda i: (0, 0)
    row = lambda i: (i, 0)
    return pl.pallas_call(
        _inproj_kernel,
        grid=(n // tile,),
        in_specs=[
            pl.BlockSpec((tile, D_MODEL), row),
            pl.BlockSpec((1, D_MODEL), const),
            pl.BlockSpec((D_MODEL, IN_W), const),
            pl.BlockSpec((1, ATTN_W), const),
            pl.BlockSpec((1, KV_W), const),
            pl.BlockSpec((ATTN_W, ATTN_W), const),
            pl.BlockSpec((KV_W, KV_W), const),
        ],
        out_specs=[
            pl.BlockSpec((tile, ATTN_W), row),
            pl.BlockSpec((tile, 2 * KV_W), row),
            pl.BlockSpec((tile, 2 * KV_W), row),
            pl.BlockSpec((tile, CONV_W), row),
        ],
        out_shape=[
            jax.ShapeDtypeStruct((n, ATTN_W), BF16),
            jax.ShapeDtypeStruct((n, 2 * KV_W), BF16),
            jax.ShapeDtypeStruct((n, 2 * KV_W), BF16),
            jax.ShapeDtypeStruct((n, CONV_W), F32),
        ],
        compiler_params=pltpu.CompilerParams(dimension_semantics=("arbitrary",),
                                             vmem_limit_bytes=VMEM_LIMIT),
        name="inproj",
    )(x2d, g1, w_in_b, qg, kg, pq, pk)


def _conv_fill(pad_ref, hm_ref, hc_ref, seq):
    tail = pad_ref.shape[0] - (N_META + seq)
    pad_ref[0:N_META, :] = hm_ref[...]
    pad_ref[N_META:N_META + seq, :] = hc_ref[0]
    pad_ref[N_META + seq:, :] = jnp.zeros((tail, CONV_W), F32)


def _conv_chunk(pad_ref, y_ref, w_ref, b_ref, lg_ref, lb_ref, base):
    first = N_META - CONV_PAD
    span = CONV_ROWS + 8
    for lt in range(CONV_W // LANES):
        ls = slice(lt * LANES, (lt + 1) * LANES)
        win = pad_ref[pl.ds(base, CONV_ROWS + 32), ls]
        acc = None
        for sub in range(8):
            part = None
            for al in range(4):
                k = 8 * al + sub - first
                if 0 <= k < CONV_K:
                    term = win[8 * al:8 * al + span] * w_ref[k:k + 1, ls]
                    part = term if part is None else part + term
            if sub:
                part = pltpu.roll(part, span - sub, 0)
            acc = part[:CONV_ROWS] if acc is None else acc + part[:CONV_ROWS]
        y_ref[:, ls] = acc
    y = y_ref[...] + b_ref[...]
    mu = jnp.mean(y, axis=-1, keepdims=True)
    yc = y - mu
    var = jnp.mean(yc * yc, axis=-1, keepdims=True)
    z = yc * lax.rsqrt(var + LN_EPS) * lg_ref[...] + lb_ref[...]
    return (z * jax.nn.sigmoid(z)).astype(BF16)


def _alibi_bias():
    qi = np.arange(BLOCK)[:, None]
    kj = np.arange(BLOCK)[None, :]
    dists = [qi + BLOCK - kj, np.abs(qi - kj), kj + BLOCK - qi]
    out = np.zeros((N_KV_HEADS, 4, GROUP * BLOCK, BLOCK), np.float32)
    for h in range(N_KV_HEADS):
        for g in range(GROUP):
            slope = 2.0 ** (-8.0 * (h * GROUP + g + 1) / N_Q_HEADS)
            for p, d in enumerate(dists):
                out[h, p, g * BLOCK:(g + 1) * BLOCK] = np.where(d <= WINDOW, -slope * d, NEG_INF)
    out[:, 3, :, N_META:] = NEG_INF
    return out


def _attn_kernel(sink_ref, q_ref, kp_ref, kc_ref, kn_ref, vp_ref, vc_ref, vn_ref,
                 km_ref, vm_ref, bias_ref, o_ref, *, n_steps):
    i = pl.program_id(1)
    lo = lax.broadcasted_iota(jnp.int32, (BLOCK, LANES), 1) < HEAD_DIM
    edge_first = jnp.where(i == 0, NEG_INF, 0.0).astype(F32)
    edge_last = jnp.where(i == n_steps - 1, NEG_INF, 0.0).astype(F32)
    nt = (((1,), (1,)), ((), ()))
    zero = jnp.zeros((BLOCK, LANES), BF16)

    def key_block(before_ref, here_ref, after_ref, idx, ks):
        if idx < 0:
            return before_ref[0, :, ks]
        if idx >= Q_BLOCKS:
            return after_ref[0, :, ks]
        return here_ref[0, idx * BLOCK:(idx + 1) * BLOCK, ks]

    for qb in range(Q_BLOCKS):
        qr = slice(qb * BLOCK, (qb + 1) * BLOCK)
        for h in range(N_KV_HEADS):
            ks = slice(h * LANES, (h + 1) * LANES)
            rows = []
            for j in range(2):
                pair = q_ref[0, qr, (2 * h + j) * LANES:(2 * h + j + 1) * LANES]
                rows.append(jnp.where(lo, pair, zero))
                rows.append(jnp.where(lo, zero, pair))
            qs = jnp.concatenate(rows, axis=0)
            k_p, k_c, k_n = (key_block(kp_ref, kc_ref, kn_ref, qb + rel, ks) for rel in (-1, 0, 1))
            v_p, v_c, v_n = (key_block(vp_ref, vc_ref, vn_ref, qb + rel, ks) for rel in (-1, 0, 1))
            s_p = lax.dot_general(qs, k_p, nt, preferred_element_type=F32) + bias_ref[h, 0]
            s_c = lax.dot_general(qs, k_c, nt, preferred_element_type=F32) + bias_ref[h, 1]
            s_n = lax.dot_general(qs, k_n, nt, preferred_element_type=F32) + bias_ref[h, 2]
            s_m = lax.dot_general(qs, km_ref[:, ks], nt, preferred_element_type=F32) + bias_ref[h, 3]
            if qb == 0:
                s_p = s_p + edge_first
            if qb == Q_BLOCKS - 1:
                s_n = s_n + edge_last
            sink = jnp.concatenate(
                [jnp.full((BLOCK, 1), sink_ref[h * GROUP + g], F32) for g in range(GROUP)], axis=0)
            m = jnp.max(jnp.maximum(jnp.maximum(s_p, s_c), jnp.maximum(s_n, s_m)), axis=-1, keepdims=True)
            m = jnp.maximum(m, sink)
            p_p = jnp.exp(s_p - m)
            p_c = jnp.exp(s_c - m)
            p_n = jnp.exp(s_n - m)
            p_m = jnp.exp(s_m - m)
            denom = jnp.sum((p_p + p_c) + (p_n + p_m), axis=-1, keepdims=True) + jnp.exp(sink - m)
            o = (jnp.dot(p_p.astype(BF16), v_p, preferred_element_type=F32)
                 + jnp.dot(p_c.astype(BF16), v_c, preferred_element_type=F32)
                 + jnp.dot(p_n.astype(BF16), v_n, preferred_element_type=F32)
                 + jnp.dot(p_m.astype(BF16), vm_ref[:, ks], preferred_element_type=F32))
            o = o / denom
            for j in range(2):
                even = o[(2 * j) * BLOCK:(2 * j + 1) * BLOCK]
                odd = o[(2 * j + 1) * BLOCK:(2 * j + 2) * BLOCK]
                o_ref[0, qr, (2 * h + j) * LANES:(2 * h + j + 1) * LANES] = jnp.where(lo, even, odd).astype(BF16)


def _attention(sink, q, kd, vd, kmeta, vmeta, bias):
    b, s, _ = q.shape
    nb = s // BLOCK
    steps = nb // Q_BLOCKS
    here = lambda bi, i: (bi, i, 0)
    before = lambda bi, i: (bi, jnp.maximum(Q_BLOCKS * i - 1, 0), 0)
    after = lambda bi, i: (bi, jnp.minimum(Q_BLOCKS * i + Q_BLOCKS, nb - 1), 0)
    const2 = lambda bi, i: (0, 0)
    edge_blk = (1, BLOCK, 2 * KV_W)
    here_blk = (1, Q_BLOCKS * BLOCK, 2 * KV_W)
    return pl.pallas_call(
        functools.partial(_attn_kernel, n_steps=steps),
        grid=(b, steps),
        in_specs=[
            pl.BlockSpec(memory_space=pltpu.SMEM),
            pl.BlockSpec((1, Q_BLOCKS * BLOCK, ATTN_W), here),
            pl.BlockSpec(edge_blk, before), pl.BlockSpec(here_blk, here), pl.BlockSpec(edge_blk, after),
            pl.BlockSpec(edge_blk, before), pl.BlockSpec(here_blk, here), pl.BlockSpec(edge_blk, after),
            pl.BlockSpec((BLOCK, 2 * KV_W), const2),
            pl.BlockSpec((BLOCK, 2 * KV_W), const2),
            pl.BlockSpec(bias.shape, lambda bi, i: (0, 0, 0, 0)),
        ],
        out_specs=pl.BlockSpec((1, Q_BLOCKS * BLOCK, ATTN_W), here),
        out_shape=jax.ShapeDtypeStruct((b, s, ATTN_W), BF16),
        compiler_params=pltpu.CompilerParams(dimension_semantics=("arbitrary", "arbitrary"),
                                             vmem_limit_bytes=VMEM_LIMIT),
        name="attention",
    )(sink, q, kd, kd, kd, vd, vd, vd, kmeta, vmeta, bias)


def _outproj_kernel(attn_ref, hc_ref, hm_ref, cw_ref, cb_ref, lg_ref, lb_ref, x_ref, wa_ref, wc_ref, g2_ref,
                    rwh_ref, rwl_ref, rb_ref, tri_ref, h2_ref, xp_ref, route_ref, cnt_ref,
                    run_ref, pad_ref, y_ref, conv_ref, *, seq):
    step = pl.program_id(0)
    tiles_per_seq = seq // ROW_TILE
    part = step % tiles_per_seq

    @pl.when(step == 0)
    def _():
        run_ref[...] = jnp.zeros_like(run_ref)

    @pl.when(part == 0)
    def _():
        _conv_fill(pad_ref, hm_ref, hc_ref, seq)

    for c in range(ROW_TILE // CONV_ROWS):
        base = pl.multiple_of(part * ROW_TILE + c * CONV_ROWS, CONV_ROWS)
        conv_ref[c * CONV_ROWS:(c + 1) * CONV_ROWS, :] = _conv_chunk(pad_ref, y_ref, cw_ref, cb_ref, lg_ref,
                                                                     lb_ref, base)
    mix = (jnp.dot(attn_ref[...], wa_ref[...], preferred_element_type=F32)
           + jnp.dot(conv_ref[...], wc_ref[...], preferred_element_type=F32))
    h2 = x_ref[...] + mix
    h2_ref[...] = h2
    hn = _rms(h2, RMS_EPS) * g2_ref[...]
    hn_hi = hn.astype(BF16)
    half = D_MODEL // 2
    xp_ref[...] = _pack_pair(hn[:, :half], hn[:, half:])
    hn_lo = (hn - hn_hi.astype(F32)).astype(BF16)
    nt = (((1,), (1,)), ((), ()))
    logits = (lax.dot_general(rwh_ref[...], hn_hi, nt, preferred_element_type=F32)
              + lax.dot_general(rwh_ref[...], hn_lo, nt, preferred_element_type=F32)
              + lax.dot_general(rwl_ref[...], hn_hi, nt, preferred_element_type=F32)) + rb_ref[...]
    expert = lax.broadcasted_iota(jnp.int32, logits.shape, 0).astype(F32)
    work = logits
    vals, sels = [], []
    for r in range(TOP_K):
        m = jnp.max(work, axis=0, keepdims=True)
        idx = jnp.min(jnp.where(work == m, expert, float(N_EXPERTS)), axis=0, keepdims=True)
        sel = expert == idx
        work = jnp.where(sel, -jnp.inf, work)
        route_ref[r:r + 1, :] = idx
        vals.append(m)
        sels.append(sel)
    onehot = jnp.where(jnp.logical_or(jnp.logical_or(sels[0], sels[1]), jnp.logical_or(sels[2], sels[3])),
                       1.0, 0.0)
    exps = [jnp.exp(v - vals[0]) for v in vals]
    tot = exps[0] + exps[1] + exps[2] + exps[3]
    before = jnp.dot(onehot.astype(BF16), tri_ref[...], preferred_element_type=F32) + run_ref[...]
    for r in range(TOP_K):
        route_ref[TOP_K + r:TOP_K + r + 1, :] = exps[r] / tot
        route_ref[2 * TOP_K + r:2 * TOP_K + r + 1, :] = jnp.sum(jnp.where(sels[r], before, 0.0), axis=0,
                                                               keepdims=True)
    route_ref[3 * TOP_K:, :] = jnp.zeros((ROUTE_ROWS - 3 * TOP_K, logits.shape[1]), F32)
    run_ref[...] = run_ref[...] + jnp.sum(onehot, axis=1, keepdims=True)
    cnt_ref[...] = run_ref[...]


def _outproj(attn2d, hc, hc_meta, dw_w, dw_b, ln_g, ln_b, x2d, wa, wc, g2, rwh, rwl, rb, tri):
    n = x2d.shape[0]
    seq = hc.shape[1]
    tile = ROW_TILE
    const = lambda i: (0, 0)
    row = lambda i: (i, 0)
    half = D_MODEL // 2
    return pl.pallas_call(
        functools.partial(_outproj_kernel, seq=seq),
        grid=(n // tile,),
        in_specs=[
            pl.BlockSpec((tile, ATTN_W), row),
            pl.BlockSpec((1, seq, CONV_W), lambda i: (i // (seq // tile), 0, 0)),
            pl.BlockSpec((N_META, CONV_W), const),
            pl.BlockSpec((CONV_K, CONV_W), const),
            pl.BlockSpec((1, CONV_W), const),
            pl.BlockSpec((1, CONV_W), const),
            pl.BlockSpec((1, CONV_W), const),
            pl.BlockSpec((tile, D_MODEL), row),
            pl.BlockSpec((ATTN_W, D_MODEL), const),
            pl.BlockSpec((CONV_W, D_MODEL), const),
            pl.BlockSpec((1, D_MODEL), const),
            pl.BlockSpec((N_EXPERTS, D_MODEL), const),
            pl.BlockSpec((N_EXPERTS, D_MODEL), const),
            pl.BlockSpec((N_EXPERTS, 1), const),
            pl.BlockSpec((tile, tile), const),
        ],
        out_specs=[
            pl.BlockSpec((tile, D_MODEL), row),
            pl.BlockSpec((tile, half), row),
            pl.BlockSpec((ROUTE_ROWS, tile), lambda i: (0, i)),
            pl.BlockSpec((N_EXPERTS, 1), const),
        ],
        out_shape=[
            jax.ShapeDtypeStruct((n, D_MODEL), F32),
            jax.ShapeDtypeStruct((n, half), jnp.uint32),
            jax.ShapeDtypeStruct((ROUTE_ROWS, n), F32),
            jax.ShapeDtypeStruct((N_EXPERTS, 1), F32),
        ],
        scratch_shapes=[pltpu.VMEM((N_EXPERTS, 1), F32),
                        pltpu.VMEM((N_META + seq + 32, CONV_W), F32),
                        pltpu.VMEM((CONV_ROWS, CONV_W), F32),
                        pltpu.VMEM((tile, CONV_W), BF16)],
        compiler_params=pltpu.CompilerParams(dimension_semantics=("arbitrary",),
                                             vmem_limit_bytes=VMEM_LIMIT),
        name="outproj",
    )(attn2d, hc, hc_meta, dw_w, dw_b, ln_g, ln_b, x2d, wa, wc, g2, rwh, rwl, rb, tri)


def _sc_mesh():
    return plsc.VectorSubcoreMesh(core_axis_name="c", subcore_axis_name="s",
                                  num_cores=SC_CORES, num_subcores=SC_SUBCORES)


def _sc_worker():
    return lax.axis_index("s") * SC_CORES + lax.axis_index("c")


def _sc_dispatch(pos3, xp, out_rows):
    n, half = xp.shape
    per_worker = n // SC_CHUNK // SC_WORKERS

    @functools.partial(
        pl.kernel, mesh=_sc_mesh(),
        out_type=jax.ShapeDtypeStruct((out_rows, half), jnp.uint32),
        scratch_types=[pltpu.VMEM((2, TOP_K, SC_CHUNK), jnp.int32),
                       pltpu.VMEM((2, SC_CHUNK, half), jnp.uint32),
                       pltpu.SemaphoreType.DMA((2,)),
                       pltpu.SemaphoreType.DMA],
        name="sc_dispatch")
    def run(pos_hbm, xp_hbm, xs_hbm, idx_v, rows_v, load_sem, scatter_sem):
        first = _sc_worker() * per_worker

        def load(j, buf):
            c = first + j
            return (pltpu.async_copy(pos_hbm.at[c], idx_v.at[buf], load_sem.at[buf]),
                    pltpu.async_copy(xp_hbm.at[pl.ds(c * SC_CHUNK, SC_CHUNK)], rows_v.at[buf], load_sem.at[buf]))

        loads = load(0, 0)
        for j in range(per_worker):
            buf = j % 2
            for cp in loads:
                cp.wait()
            if j + 1 < per_worker:
                loads = load(j + 1, 1 - buf)
            copies = [pltpu.async_copy(rows_v.at[buf], xs_hbm.at[idx_v.at[buf].at[k]], scatter_sem)
                      for k in range(TOP_K)]
            for cp in copies:
                cp.wait()

    return run(pos3, xp)


def _sc_collect(pos3, y):
    half = y.shape[1]
    n = pos3.shape[0] * SC_CHUNK
    per_worker = n // SC_CHUNK // SC_WORKERS

    @functools.partial(
        pl.kernel, mesh=_sc_mesh(),
        out_type=jax.ShapeDtypeStruct((TOP_K, n, half), jnp.uint32),
        scratch_types=[pltpu.VMEM((TOP_K, SC_CHUNK), jnp.int32),
                       pltpu.VMEM((2, SC_CHUNK, half), jnp.uint32),
                       pltpu.SemaphoreType.DMA((2,)),
                       pltpu.SemaphoreType.DMA((2,))],
        name="sc_collect")
    def run(pos_hbm, y_hbm, yg_hbm, idx_v, rows_v, gather_sem, store_sem):
        first = _sc_worker() * per_worker

        @pl.loop(0, per_worker)
        def _(j):
            c = first + j
            pltpu.sync_copy(pos_hbm.at[c], idx_v)

            def gather(k):
                return pltpu.async_copy(y_hbm.at[idx_v.at[k]], rows_v.at[k % 2], gather_sem.at[k % 2])

            gathers = [gather(0)]
            stores = []
            for k in range(TOP_K):
                if k + 1 < TOP_K:
                    if k >= 1:
                        stores[k - 1].wait()
                    gathers.append(gather(k + 1))
                gathers[k].wait()
                stores.append(pltpu.async_copy(rows_v.at[k % 2], yg_hbm.at[k, pl.ds(c * SC_CHUNK, SC_CHUNK)],
                                               store_sem.at[k % 2]))
            stores[TOP_K - 2].wait()
            stores[TOP_K - 1].wait()

    return run(pos3, y)


S_TILE, S_EXPERT, S_ROWS, S_NEW, S_SLOT, S_NEXT = range(6)


def _expert_kernel(sched_ref, xs_ref, wgu_hbm, bgu_ref, wd_hbm, bd_ref, y_ref,
                   act_ref, wgu_buf, wd_buf, sem):
    w = pl.program_id(0)
    e = sched_ref[S_EXPERT, w]
    slot = sched_ref[S_SLOT, w]
    n_rows = sched_ref[S_ROWS, w]

    def weight_copies(expert, s):
        return (pltpu.make_async_copy(wgu_hbm.at[expert], wgu_buf.at[s], sem.at[0, s]),
                pltpu.make_async_copy(wd_hbm.at[expert], wd_buf.at[s], sem.at[1, s]))

    @pl.when(sched_ref[S_NEW, w] == 1)
    def _():
        @pl.when(w == 0)
        def _():
            for cp in weight_copies(e, slot):
                cp.start()

        nxt = sched_ref[S_NEXT, w]

        @pl.when(nxt >= 0)
        def _():
            for cp in weight_copies(nxt, 1 - slot):
                cp.start()

        for cp in weight_copies(e, slot):
            cp.wait()

    @pl.when(n_rows > 0)
    def _():
        half = D_MODEL // 2
        live = lax.broadcasted_iota(jnp.int32, (EXPERT_TILE, 1), 0) < n_rows
        x_lo, x_hi = _unpack_pair(jnp.where(live, xs_ref[...], jnp.uint32(0)))
        x = jnp.concatenate([x_lo.astype(BF16), x_hi.astype(BF16)], axis=1)
        for c in range(D_FF // FF_CHUNK):
            gc = slice(c * FF_CHUNK, (c + 1) * FF_CHUNK)
            uc = slice(D_FF + c * FF_CHUNK, D_FF + (c + 1) * FF_CHUNK)
            gate = jnp.dot(x, wgu_buf[slot, :, gc].astype(BF16), preferred_element_type=F32) + bgu_ref[0, :, gc]
            up = jnp.dot(x, wgu_buf[slot, :, uc].astype(BF16), preferred_element_type=F32) + bgu_ref[0, :, uc]
            gate = jnp.minimum(gate, SWIGLU_LIMIT)
            up = jnp.clip(up, -SWIGLU_LIMIT, SWIGLU_LIMIT)
            act_ref[:, gc] = (gate * jax.nn.sigmoid(SWIGLU_ALPHA * gate) * (up + 1.0)).astype(BF16)
        for c in range(half // FF_CHUNK):
            lo_c = slice(c * FF_CHUNK, (c + 1) * FF_CHUNK)
            hi_c = slice(half + c * FF_CHUNK, half + (c + 1) * FF_CHUNK)
            y_lo = (jnp.dot(act_ref[...], wd_buf[slot, :, lo_c].astype(BF16), preferred_element_type=F32)
                    + bd_ref[0, :, lo_c])
            y_hi = (jnp.dot(act_ref[...], wd_buf[slot, :, hi_c].astype(BF16), preferred_element_type=F32)
                    + bd_ref[0, :, hi_c])
            y_ref[:, lo_c] = _pack_pair(y_lo, y_hi)


def _experts(sched, xs, wgu, bgu, wd, bd):
    rows, half = xs.shape
    n_work = sched.shape[1]
    grid_spec = pltpu.PrefetchScalarGridSpec(
        num_scalar_prefetch=1,
        grid=(n_work,),
        in_specs=[
            pl.BlockSpec((EXPERT_TILE, half), lambda w, sc: (sc[S_TILE, w], 0)),
            pl.BlockSpec(memory_space=pl.ANY),
            pl.BlockSpec((1, 1, 2 * D_FF), lambda w, sc: (sc[S_EXPERT, w], 0, 0)),
            pl.BlockSpec(memory_space=pl.ANY),
            pl.BlockSpec((1, 1, D_MODEL), lambda w, sc: (sc[S_EXPERT, w], 0, 0)),
        ],
        out_specs=pl.BlockSpec((EXPERT_TILE, half), lambda w, sc: (sc[S_TILE, w], 0)),
        scratch_shapes=[pltpu.VMEM((EXPERT_TILE, D_FF), BF16),
                        pltpu.VMEM((2, D_MODEL, 2 * D_FF), F32),
                        pltpu.VMEM((2, D_FF, D_MODEL), F32),
                        pltpu.SemaphoreType.DMA((2, 2))],
    )
    return pl.pallas_call(
        _expert_kernel,
        grid_spec=grid_spec,
        out_shape=jax.ShapeDtypeStruct((rows, half), jnp.uint32),
        compiler_params=pltpu.CompilerParams(dimension_semantics=("arbitrary",),
                                             vmem_limit_bytes=VMEM_LIMIT),
        name="experts",
    )(sched, xs, wgu, bgu, wd, bd)


def _unpack_pair(packed):
    lo = lax.bitcast_convert_type(packed << 16, F32)
    hi = lax.bitcast_convert_type(packed & jnp.uint32(0xFFFF0000), F32)
    return lo, hi


def _pack_pair(lo, hi):
    lo_bits = lax.bitcast_convert_type(lo.astype(BF16).astype(F32), jnp.uint32)
    hi_bits = lax.bitcast_convert_type(hi.astype(BF16).astype(F32), jnp.uint32)
    return (lo_bits >> 16) | (hi_bits & jnp.uint32(0xFFFF0000))


def _combine_kernel(yg_ref, h2_ref, gate_ref, o_ref):
    half = D_MODEL // 2
    gates = gate_ref[...]
    lo_sum = h2_ref[:, :half]
    hi_sum = h2_ref[:, half:]
    for k in range(TOP_K):
        gate = gates[:, k:k + 1]
        lo, hi = _unpack_pair(yg_ref[k])
        lo_sum = lo_sum + gate * lo
        hi_sum = hi_sum + gate * hi
    o_ref[:, :half] = lo_sum
    o_ref[:, half:] = hi_sum


def _combine(yg, h2, gates, part):
    n = h2.shape[0]
    half = D_MODEL // 2
    steps = yg.shape[1] // ROW_TILE
    row = lambda i: (i + part * steps, 0)
    return pl.pallas_call(
        _combine_kernel,
        grid=(steps,),
        in_specs=[
            pl.BlockSpec((TOP_K, ROW_TILE, half), lambda i: (0, i, 0)),
            pl.BlockSpec((ROW_TILE, D_MODEL), row),
            pl.BlockSpec((ROW_TILE, TOP_K), row),
        ],
        out_specs=pl.BlockSpec((ROW_TILE, D_MODEL), row),
        out_shape=jax.ShapeDtypeStruct((n, D_MODEL), F32),
        input_output_aliases={1: 0},
        compiler_params=pltpu.CompilerParams(dimension_semantics=("arbitrary",),
                                             vmem_limit_bytes=VMEM_LIMIT),
        name="combine",
    )(yg, h2, gates)


def _work_schedule(counts, n_rows):
    n_work = n_rows // EXPERT_TILE + N_EXPERTS
    tiles = (counts + (EXPERT_TILE - 1)) // EXPERT_TILE
    tile_end = jnp.cumsum(tiles)
    tile_start = tile_end - tiles
    total = tile_end[-1]
    w = jnp.arange(n_work, dtype=jnp.int32)
    wc = jnp.minimum(w, total - 1)
    expert_of = jnp.sum((tile_end[None, :] <= wc[:, None]).astype(jnp.int32), axis=1)
    mine = expert_of[:, None] == jnp.arange(N_EXPERTS, dtype=jnp.int32)[None, :]
    pick = lambda table: jnp.sum(jnp.where(mine, table[None, :], 0), axis=1)
    live = jnp.clip(pick(counts) - (wc - pick(tile_start)) * EXPERT_TILE, 0, EXPERT_TILE)
    live = jnp.where(w < total, live, 0)
    prev_e = jnp.concatenate([jnp.full((1,), -1, jnp.int32), expert_of[:-1]])
    new_e = jnp.logical_and(w < total, expert_of != prev_e)
    slot = (jnp.cumsum(new_e.astype(jnp.int32)) - 1) % 2
    later = jnp.logical_and(new_e[None, :], w[None, :] > w[:, None])
    next_w = jnp.min(jnp.where(later, w[None, :], n_work), axis=1)
    next_e = jnp.sum(jnp.where(next_w[:, None] == w[None, :], expert_of[None, :], 0), axis=1)
    next_e = jnp.where(next_w < n_work, next_e, -1)
    sched = jnp.stack([wc, expert_of, live, new_e.astype(jnp.int32), slot, next_e]).astype(jnp.int32)
    return sched, (tile_start * EXPERT_TILE).astype(jnp.int32)


def kernel(x, meta_tokens, norm1_g, w_in, q_norm_g, k_norm_g, attn_sink, dw_w, dw_b, conv_ln_g,
           conv_ln_b, w_out, norm2_g, router_w, router_b, w_gate_up, b_gate_up, w_down, b_down):
    assert norm1_g.shape[0] == 1, "single-layer trunk: meta-token query rows are not materialised"
    b, s, d = x.shape
    n = b * s
    x2d = x.reshape(n, d)

    scale = HEAD_DIM ** -0.5
    qg = (jnp.tile(q_norm_g[0], N_Q_HEADS) * scale).reshape(1, ATTN_W)
    kg = jnp.tile(k_norm_g[0], N_KV_HEADS).reshape(1, KV_W)
    head_of = np.arange(ATTN_W) // HEAD_DIM
    pq = jnp.asarray((head_of[:, None] == head_of[None, :]) / HEAD_DIM, BF16)
    pk = pq[:KV_W, :KV_W]
    g1 = norm1_g[0].reshape(1, d)
    w_in_b = w_in[0].astype(BF16)

    q, kd, vd, hc = _inproj(x2d, g1, w_in_b, qg, kg, pq, pk, ROW_TILE)
    _, kmeta, vmeta, hc_meta = _inproj(meta_tokens, g1, w_in_b, qg, kg, pq, pk, N_META)

    meta_pad = ((0, BLOCK - N_META), (0, 0))
    attn = _attention(attn_sink[0], q.reshape(b, s, ATTN_W), kd.reshape(b, s, 2 * KV_W),
                      vd.reshape(b, s, 2 * KV_W), jnp.pad(kmeta, meta_pad), jnp.pad(vmeta, meta_pad),
                      jnp.asarray(_alibi_bias()))

    w_out_b = w_out[0].astype(BF16)
    rw = router_w[0].T
    rwh = rw.astype(BF16)
    rwl = (rw - rwh.astype(F32)).astype(BF16)
    rb = router_b[0].reshape(N_EXPERTS, 1)
    tri = jnp.asarray(np.triu(np.ones((ROW_TILE, ROW_TILE), np.float32), 1), BF16)
    h2, xp, route, counts = _outproj(attn.reshape(n, ATTN_W), hc.reshape(b, s, CONV_W), hc_meta,
                                     dw_w[0].reshape(CONV_K, CONV_W), dw_b[0].reshape(1, CONV_W),
                                     conv_ln_g[0].reshape(1, CONV_W), conv_ln_b[0].reshape(1, CONV_W), x2d,
                                     w_out_b[:ATTN_W], w_out_b[ATTN_W:], norm2_g[0].reshape(1, d),
                                     rwh, rwl, rb, tri)

    counts_i = counts[:, 0].astype(jnp.int32)
    idx = route[:TOP_K].astype(jnp.int32)
    rank = route[2 * TOP_K:3 * TOP_K].astype(jnp.int32)
    gates = route[TOP_K:2 * TOP_K].T
    sched, first_row = _work_schedule(counts_i, n * TOP_K)
    chosen = idx[:, :, None] == jnp.arange(N_EXPERTS, dtype=jnp.int32)
    pos = rank + jnp.sum(jnp.where(chosen, first_row, 0), axis=-1)
    pos3 = pos.reshape(TOP_K, n // SC_CHUNK, SC_CHUNK).transpose(1, 0, 2)

    xs = _sc_dispatch(pos3, xp, sched.shape[1] * EXPERT_TILE)
    y = _experts(sched, xs, w_gate_up[0],
                 b_gate_up[0].reshape(N_EXPERTS, 1, 2 * D_FF), w_down[0],
                 b_down[0].reshape(N_EXPERTS, 1, D_MODEL))
    out = h2
    chunks = pos3.shape[0] // COMBINE_PARTS
    for part in range(COMBINE_PARTS):
        out = _combine(_sc_collect(pos3[part * chunks:(part + 1) * chunks], y), out, gates, part)
    return out.reshape(b, s, d)
```

```python
import functools

import numpy as np
import jax
import jax.numpy as jnp
from jax import lax
from jax.experimental import pallas as pl
from jax.experimental.pallas import tpu as pltpu
from jax.experimental.pallas import tpu_sc as plsc

F32 = jnp.float32
BF16 = jnp.bfloat16

D_MODEL = 1024
N_META = 16
HEAD_DIM = 64
N_Q_HEADS = 8
N_KV_HEADS = 2
GROUP = N_Q_HEADS // N_KV_HEADS
ATTN_W = N_Q_HEADS * HEAD_DIM
KV_W = N_KV_HEADS * HEAD_DIM
CONV_W = D_MODEL - ATTN_W
IN_W = ATTN_W + 2 * KV_W + 2 * CONV_W
WINDOW = 128
BLOCK = 128
CONV_K = 31
CONV_PAD = CONV_K // 2
N_EXPERTS = 32
TOP_K = 4
D_FF = D_MODEL
SWIGLU_LIMIT = 7.0
SWIGLU_ALPHA = 1.702
RMS_EPS = 1e-6
LN_EPS = 1e-5
NEG_INF = -1e30

LANES = 128
ROW_TILE = 512
EXPERT_TILE = 512
SC_CORES = 2
SC_SUBCORES = 16
SC_WORKERS = SC_CORES * SC_SUBCORES
SC_CHUNK = 64
CONV_ROWS = 128
FF_CHUNK = 512
Q_BLOCKS = 4
COMBINE_PARTS = 2
ROUTE_ROWS = 16
VMEM_LIMIT = 56 * 1024 * 1024


def _rms(x, eps):
    return x * lax.rsqrt(jnp.mean(x * x, axis=-1, keepdims=True) + eps)


def _inproj_kernel(x_ref, g1_ref, w_ref, qg_ref, kg_ref, pq_ref, pk_ref,
                   q_ref, k_ref, v_ref, hc_ref):
    x = x_ref[...]
    n = (_rms(x, RMS_EPS) * g1_ref[...]).astype(BF16)
    proj = jnp.dot(n, w_ref[...], preferred_element_type=F32)
    q = proj[:, :ATTN_W]
    k = proj[:, ATTN_W:ATTN_W + KV_W]
    v = proj[:, ATTN_W + KV_W:ATTN_W + 2 * KV_W]
    a = proj[:, ATTN_W + 2 * KV_W:ATTN_W + 2 * KV_W + CONV_W]
    g = proj[:, ATTN_W + 2 * KV_W + CONV_W:]
    qms = jnp.dot((q * q).astype(BF16), pq_ref[...], preferred_element_type=F32)
    kms = jnp.dot((k * k).astype(BF16), pk_ref[...], preferred_element_type=F32)
    q_ref[...] = (q * lax.rsqrt(qms + RMS_EPS) * qg_ref[...]).astype(BF16)
    kn = k * lax.rsqrt(kms + RMS_EPS) * kg_ref[...]
    lo = lax.broadcasted_iota(jnp.int32, kn.shape, 1) < HEAD_DIM
    ksw = pltpu.roll(kn, HEAD_DIM, 1)
    k_ref[...] = jnp.concatenate([jnp.where(lo, kn, ksw), jnp.where(lo, ksw, kn)], axis=1).astype(BF16)
    vsw = pltpu.roll(v, HEAD_DIM, 1)
    v_ref[...] = jnp.concatenate([jnp.where(lo, v, vsw), jnp.where(lo, vsw, v)], axis=1).astype(BF16)
    hc_ref[...] = a * jax.nn.sigmoid(g)


def _inproj(x2d, g1, w_in_b, qg, kg, pq, pk, tile):
    n = x2d.shape[0]
    const = lambda i: (0, 0)
    row = lambda i: (i, 0)
    return pl.pallas_call(
        _inproj_kernel,
        grid=(n // tile,),
        in_specs=[
            pl.BlockSpec((tile, D_MODEL), row),
            pl.BlockSpec((1, D_MODEL), const),
            pl.BlockSpec((D_MODEL, IN_W), const),
            pl.BlockSpec((1, ATTN_W), const),
            pl.BlockSpec((1, KV_W), const),
            pl.BlockSpec((ATTN_W, ATTN_W), const),
            pl.BlockSpec((KV_W, KV_W), const),
        ],
        out_specs=[
            pl.BlockSpec((tile, ATTN_W), row),
            pl.BlockSpec((tile, 2 * KV_W), row),
            pl.BlockSpec((tile, 2 * KV_W), row),
            pl.BlockSpec((tile, CONV_W), row),
        ],
        out_shape=[
            jax.ShapeDtypeStruct((n, ATTN_W), BF16),
            jax.ShapeDtypeStruct((n, 2 * KV_W), BF16),
            jax.ShapeDtypeStruct((n, 2 * KV_W), BF16),
            jax.ShapeDtypeStruct((n, CONV_W), F32),
        ],
        compiler_params=pltpu.CompilerParams(dimension_semantics=("arbitrary",),
                                             vmem_limit_bytes=VMEM_LIMIT),
        name="inproj",
    )(x2d, g1, w_in_b, qg, kg, pq, pk)


def _conv_fill(pad_ref, hm_ref, hc_ref, seq):
    tail = pad_ref.shape[0] - (N_META + seq)
    pad_ref[0:N_META, :] = hm_ref[...]
    pad_ref[N_META:N_META + seq, :] = hc_ref[0]
    pad_ref[N_META + seq:, :] = jnp.zeros((tail, CONV_W), F32)


def _conv_chunk(pad_ref, y_ref, w_ref, b_ref, lg_ref, lb_ref, base):
    first = N_META - CONV_PAD
    span = CONV_ROWS + 8
    for lt in range(CONV_W // LANES):
        ls = slice(lt * LANES, (lt + 1) * LANES)
        win = pad_ref[pl.ds(base, CONV_ROWS + 32), ls]
        acc = None
        for sub in range(8):
            part = None
            for al in range(4):
                k = 8 * al + sub - first
                if 0 <= k < CONV_K:
                    term = win[8 * al:8 * al + span] * w_ref[k:k + 1, ls]
                    part = term if part is None else part + term
            if sub:
                part = pltpu.roll(part, span - sub, 0)
            acc = part[:CONV_ROWS] if acc is None else acc + part[:CONV_ROWS]
        y_ref[:, ls] = acc
    y = y_ref[...] + b_ref[...]
    mu = jnp.mean(y, axis=-1, keepdims=True)
    yc = y - mu
    var = jnp.mean(yc * yc, axis=-1, keepdims=True)
    z = yc * lax.rsqrt(var + LN_EPS) * lg_ref[...] + lb_ref[...]
    return (z * jax.nn.sigmoid(z)).astype(BF16)


def _alibi_bias():
    qi = np.arange(BLOCK)[:, None]
    kj = np.arange(BLOCK)[None, :]
    dists = [qi + BLOCK - kj, np.abs(qi - kj), kj + BLOCK - qi]
    out = np.zeros((N_KV_HEADS, 4, GROUP * BLOCK, BLOCK), np.float32)
    for h in range(N_KV_HEADS):
        for g in range(GROUP):
            slope = 2.0 ** (-8.0 * (h * GROUP + g + 1) / N_Q_HEADS)
            for p, d in enumerate(dists):
                out[h, p, g * BLOCK:(g + 1) * BLOCK] = np.where(d <= WINDOW, -slope * d, NEG_INF)
    out[:, 3, :, N_META:] = NEG_INF
    return out


def _attn_kernel(sink_ref, q_ref, kp_ref, kc_ref, kn_ref, vp_ref, vc_ref, vn_ref,
                 km_ref, vm_ref, bias_ref, o_ref, *, n_steps):
    i = pl.program_id(1)
    lo = lax.broadcasted_iota(jnp.int32, (BLOCK, LANES), 1) < HEAD_DIM
    edge_first = jnp.where(i == 0, NEG_INF, 0.0).astype(F32)
    edge_last = jnp.where(i == n_steps - 1, NEG_INF, 0.0).astype(F32)
    nt = (((1,), (1,)), ((), ()))
    zero = jnp.zeros((BLOCK, LANES), BF16)

    def key_block(before_ref, here_ref, after_ref, idx, ks):
        if idx < 0:
            return before_ref[0, :, ks]
        if idx >= Q_BLOCKS:
            return after_ref[0, :, ks]
        return here_ref[0, idx * BLOCK:(idx + 1) * BLOCK, ks]

    for qb in range(Q_BLOCKS):
        qr = slice(qb * BLOCK, (qb + 1) * BLOCK)
        for h in range(N_KV_HEADS):
            ks = slice(h * LANES, (h + 1) * LANES)
            rows = []
            for j in range(2):
                pair = q_ref[0, qr, (2 * h + j) * LANES:(2 * h + j + 1) * LANES]
                rows.append(jnp.where(lo, pair, zero))
                rows.append(jnp.where(lo, zero, pair))
            qs = jnp.concatenate(rows, axis=0)
            k_p, k_c, k_n = (key_block(kp_ref, kc_ref, kn_ref, qb + rel, ks) for rel in (-1, 0, 1))
            v_p, v_c, v_n = (key_block(vp_ref, vc_ref, vn_ref, qb + rel, ks) for rel in (-1, 0, 1))
            s_p = lax.dot_general(qs, k_p, nt, preferred_element_type=F32) + bias_ref[h, 0]
            s_c = lax.dot_general(qs, k_c, nt, preferred_element_type=F32) + bias_ref[h, 1]
            s_n = lax.dot_general(qs, k_n, nt, preferred_element_type=F32) + bias_ref[h, 2]
            s_m = lax.dot_general(qs, km_ref[:, ks], nt, preferred_element_type=F32) + bias_ref[h, 3]
            if qb == 0:
                s_p = s_p + edge_first
            if qb == Q_BLOCKS - 1:
                s_n = s_n + edge_last
            sink = jnp.concatenate(
                [jnp.full((BLOCK, 1), sink_ref[h * GROUP + g], F32) for g in range(GROUP)], axis=0)
            m = jnp.max(jnp.maximum(jnp.maximum(s_p, s_c), jnp.maximum(s_n, s_m)), axis=-1, keepdims=True)
            m = jnp.maximum(m, sink)
            p_p = jnp.exp(s_p - m)
            p_c = jnp.exp(s_c - m)
            p_n = jnp.exp(s_n - m)
            p_m = jnp.exp(s_m - m)
            denom = jnp.sum((p_p + p_c) + (p_n + p_m), axis=-1, keepdims=True) + jnp.exp(sink - m)
            o = (jnp.dot(p_p.astype(BF16), v_p, preferred_element_type=F32)
                 + jnp.dot(p_c.astype(BF16), v_c, preferred_element_type=F32)
                 + jnp.dot(p_n.astype(BF16), v_n, preferred_element_type=F32)
                 + jnp.dot(p_m.astype(BF16), vm_ref[:, ks], preferred_element_type=F32))
            o = o / denom
            for j in range(2):
                even = o[(2 * j) * BLOCK:(2 * j + 1) * BLOCK]
                odd = o[(2 * j + 1) * BLOCK:(2 * j + 2) * BLOCK]
                o_ref[0, qr, (2 * h + j) * LANES:(2 * h + j + 1) * LANES] = jnp.where(lo, even, odd).astype(BF16)


def _attention(sink, q, kd, vd, kmeta, vmeta, bias):
    b, s, _ = q.shape
    nb = s // BLOCK
    steps = nb // Q_BLOCKS
    here = lambda bi, i: (bi, i, 0)
    before = lambda bi, i: (bi, jnp.maximum(Q_BLOCKS * i - 1, 0), 0)
    after = lambda bi, i: (bi, jnp.minimum(Q_BLOCKS * i + Q_BLOCKS, nb - 1), 0)
    const2 = lambda bi, i: (0, 0)
    edge_blk = (1, BLOCK, 2 * KV_W)
    here_blk = (1, Q_BLOCKS * BLOCK, 2 * KV_W)
    return pl.pallas_call(
        functools.partial(_attn_kernel, n_steps=steps),
        grid=(b, steps),
        in_specs=[
            pl.BlockSpec(memory_space=pltpu.SMEM),
            pl.BlockSpec((1, Q_BLOCKS * BLOCK, ATTN_W), here),
            pl.BlockSpec(edge_blk, before), pl.BlockSpec(here_blk, here), pl.BlockSpec(edge_blk, after),
            pl.BlockSpec(edge_blk, before), pl.BlockSpec(here_blk, here), pl.BlockSpec(edge_blk, after),
            pl.BlockSpec((BLOCK, 2 * KV_W), const2),
            pl.BlockSpec((BLOCK, 2 * KV_W), const2),
            pl.BlockSpec(bias.shape, lambda bi, i: (0, 0, 0, 0)),
        ],
        out_specs=pl.BlockSpec((1, Q_BLOCKS * BLOCK, ATTN_W), here),
        out_shape=jax.ShapeDtypeStruct((b, s, ATTN_W), BF16),
        compiler_params=pltpu.CompilerParams(dimension_semantics=("arbitrary", "arbitrary"),
                                             vmem_limit_bytes=VMEM_LIMIT),
        name="attention",
    )(sink, q, kd, kd, kd, vd, vd, vd, kmeta, vmeta, bias)


def _outproj_kernel(attn_ref, hc_ref, hm_ref, cw_ref, cb_ref, lg_ref, lb_ref, x_ref, wa_ref, wc_ref, g2_ref,
                    rwh_ref, rwl_ref, rb_ref, tri_ref, h2_ref, xp_ref, route_ref, cnt_ref,
                    run_ref, pad_ref, y_ref, conv_ref, *, seq):
    step = pl.program_id(0)
    tiles_per_seq = seq // ROW_TILE
    part = step % tiles_per_seq

    @pl.when(step == 0)
    def _():
        run_ref[...] = jnp.zeros_like(run_ref)

    @pl.when(part == 0)
    def _():
        _conv_fill(pad_ref, hm_ref, hc_ref, seq)

    for c in range(ROW_TILE // CONV_ROWS):
        base = pl.multiple_of(part * ROW_TILE + c * CONV_ROWS, CONV_ROWS)
        conv_ref[c * CONV_ROWS:(c + 1) * CONV_ROWS, :] = _conv_chunk(pad_ref, y_ref, cw_ref, cb_ref, lg_ref,
                                                                     lb_ref, base)
    mix = (jnp.dot(attn_ref[...], wa_ref[...], preferred_element_type=F32)
           + jnp.dot(conv_ref[...], wc_ref[...], preferred_element_type=F32))
    h2 = x_ref[...] + mix
    h2_ref[...] = h2
    hn = _rms(h2, RMS_EPS) * g2_ref[...]
    hn_hi = hn.astype(BF16)
    half = D_MODEL // 2
    xp_ref[...] = _pack_pair(hn[:, :half], hn[:, half:])
    hn_lo = (hn - hn_hi.astype(F32)).astype(BF16)
    nt = (((1,), (1,)), ((), ()))
    logits = (lax.dot_general(rwh_ref[...], hn_hi, nt, preferred_element_type=F32)
              + lax.dot_general(rwh_ref[...], hn_lo, nt, preferred_element_type=F32)
              + lax.dot_general(rwl_ref[...], hn_hi, nt, preferred_element_type=F32)) + rb_ref[...]
    expert = lax.broadcasted_iota(jnp.int32, logits.shape, 0).astype(F32)
    work = logits
    vals, sels = [], []
    for r in range(TOP_K):
        m = jnp.max(work, axis=0, keepdims=True)
        idx = jnp.min(jnp.where(work == m, expert, float(N_EXPERTS)), axis=0, keepdims=True)
        sel = expert == idx
        work = jnp.where(sel, -jnp.inf, work)
        route_ref[r:r + 1, :] = idx
        vals.append(m)
        sels.append(sel)
    onehot = jnp.where(jnp.logical_or(jnp.logical_or(sels[0], sels[1]), jnp.logical_or(sels[2], sels[3])),
                       1.0, 0.0)
    exps = [jnp.exp(v - vals[0]) for v in vals]
    tot = exps[0] + exps[1] + exps[2] + exps[3]
    before = jnp.dot(onehot.astype(BF16), tri_ref[...], preferred_element_type=F32) + run_ref[...]
    for r in range(TOP_K):
        route_ref[TOP_K + r:TOP_K + r + 1, :] = exps[r] / tot
        route_ref[2 * TOP_K + r:2 * TOP_K + r + 1, :] = jnp.sum(jnp.where(sels[r], before, 0.0), axis=0,
                                                               keepdims=True)
    route_ref[3 * TOP_K:, :] = jnp.zeros((ROUTE_ROWS - 3 * TOP_K, logits.shape[1]), F32)
    run_ref[...] = run_ref[...] + jnp.sum(onehot, axis=1, keepdims=True)
    cnt_ref[...] = run_ref[...]


def _outproj(attn2d, hc, hc_meta, dw_w, dw_b, ln_g, ln_b, x2d, wa, wc, g2, rwh, rwl, rb, tri):
    n = x2d.shape[0]
    seq = hc.shape[1]
    tile = ROW_TILE
    const = lambda i: (0, 0)
    row = lambda i: (i, 0)
    half = D_MODEL // 2
    return pl.pallas_call(
        functools.partial(_outproj_kernel, seq=seq),
        grid=(n // tile,),
        in_specs=[
            pl.BlockSpec((tile, ATTN_W), row),
            pl.BlockSpec((1, seq, CONV_W), lambda i: (i // (seq // tile), 0, 0)),
            pl.BlockSpec((N_META, CONV_W), const),
            pl.BlockSpec((CONV_K, CONV_W), const),
            pl.BlockSpec((1, CONV_W), const),
            pl.BlockSpec((1, CONV_W), const),
            pl.BlockSpec((1, CONV_W), const),
            pl.BlockSpec((tile, D_MODEL), row),
            pl.BlockSpec((ATTN_W, D_MODEL), const),
            pl.BlockSpec((CONV_W, D_MODEL), const),
            pl.BlockSpec((1, D_MODEL), const),
            pl.BlockSpec((N_EXPERTS, D_MODEL), const),
            pl.BlockSpec((N_EXPERTS, D_MODEL), const),
            pl.BlockSpec((N_EXPERTS, 1), const),
            pl.BlockSpec((tile, tile), const),
        ],
        out_specs=[
            pl.BlockSpec((tile, D_MODEL), row),
            pl.BlockSpec((tile, half), row),
            pl.BlockSpec((ROUTE_ROWS, tile), lambda i: (0, i)),
            pl.BlockSpec((N_EXPERTS, 1), const),
        ],
        out_shape=[
            jax.ShapeDtypeStruct((n, D_MODEL), F32),
            jax.ShapeDtypeStruct((n, half), jnp.uint32),
            jax.ShapeDtypeStruct((ROUTE_ROWS, n), F32),
            jax.ShapeDtypeStruct((N_EXPERTS, 1), F32),
        ],
        scratch_shapes=[pltpu.VMEM((N_EXPERTS, 1), F32),
                        pltpu.VMEM((N_META + seq + 32, CONV_W), F32),
                        pltpu.VMEM((CONV_ROWS, CONV_W), F32),
                        pltpu.VMEM((tile, CONV_W), BF16)],
        compiler_params=pltpu.CompilerParams(dimension_semantics=("arbitrary",),
                                             vmem_limit_bytes=VMEM_LIMIT),
        name="outproj",
    )(attn2d, hc, hc_meta, dw_w, dw_b, ln_g, ln_b, x2d, wa, wc, g2, rwh, rwl, rb, tri)


def _sc_mesh():
    return plsc.VectorSubcoreMesh(core_axis_name="c", subcore_axis_name="s",
                                  num_cores=SC_CORES, num_subcores=SC_SUBCORES)


def _sc_worker():
    return lax.axis_index("s") * SC_CORES + lax.axis_index("c")


def _sc_dispatch(pos3, xp, out_rows):
    n, half = xp.shape
    per_worker = n // SC_CHUNK // SC_WORKERS

    @functools.partial(
        pl.kernel, mesh=_sc_mesh(),
        out_type=jax.ShapeDtypeStruct((out_rows, half), jnp.uint32),
        scratch_types=[pltpu.VMEM((2, TOP_K, SC_CHUNK), jnp.int32),
                       pltpu.VMEM((2, SC_CHUNK, half), jnp.uint32),
                       pltpu.SemaphoreType.DMA((2,)),
                       pltpu.SemaphoreType.DMA],
        name="sc_dispatch")
    def run(pos_hbm, xp_hbm, xs_hbm, idx_v, rows_v, load_sem, scatter_sem):
        first = _sc_worker() * per_worker

        def load(j, buf):
            c = first + j
            return (pltpu.async_copy(pos_hbm.at[c], idx_v.at[buf], load_sem.at[buf]),
                    pltpu.async_copy(xp_hbm.at[pl.ds(c * SC_CHUNK, SC_CHUNK)], rows_v.at[buf], load_sem.at[buf]))

        loads = load(0, 0)
        for j in range(per_worker):
            buf = j % 2
            for cp in loads:
                cp.wait()
            if j + 1 < per_worker:
                loads = load(j + 1, 1 - buf)
            copies = [pltpu.async_copy(rows_v.at[buf], xs_hbm.at[idx_v.at[buf].at[k]], scatter_sem)
                      for k in range(TOP_K)]
            for cp in copies:
                cp.wait()

    return run(pos3, xp)


def _sc_collect(pos3, y):
    half = y.shape[1]
    n = pos3.shape[0] * SC_CHUNK
    per_worker = n // SC_CHUNK // SC_WORKERS

    @functools.partial(
        pl.kernel, mesh=_sc_mesh(),
        out_type=jax.ShapeDtypeStruct((TOP_K, n, half), jnp.uint32),
        scratch_types=[pltpu.VMEM((TOP_K, SC_CHUNK), jnp.int32),
                       pltpu.VMEM((2, SC_CHUNK, half), jnp.uint32),
                       pltpu.SemaphoreType.DMA((2,)),
                       pltpu.SemaphoreType.DMA((2,))],
        name="sc_collect")
    def run(pos_hbm, y_hbm, yg_hbm, idx_v, rows_v, gather_sem, store_sem):
        first = _sc_worker() * per_worker

        @pl.loop(0, per_worker)
        def _(j):
            c = first + j
            pltpu.sync_copy(pos_hbm.at[c], idx_v)

            def gather(k):
                return pltpu.async_copy(y_hbm.at[idx_v.at[k]], rows_v.at[k % 2], gather_sem.at[k % 2])

            gathers = [gather(0)]
            stores = []
            for k in range(TOP_K):
                if k + 1 < TOP_K:
                    if k >= 1:
                        stores[k - 1].wait()
                    gathers.append(gather(k + 1))
                gathers[k].wait()
                stores.append(pltpu.async_copy(rows_v.at[k % 2], yg_hbm.at[k, pl.ds(c * SC_CHUNK, SC_CHUNK)],
                                               store_sem.at[k % 2]))
            stores[TOP_K - 2].wait()
            stores[TOP_K - 1].wait()

    return run(pos3, y)


S_TILE, S_EXPERT, S_ROWS, S_NEW, S_SLOT, S_NEXT = range(6)


def _expert_kernel(sched_ref, xs_ref, wgu_hbm, bgu_ref, wd_hbm, bd_ref, y_ref,
                   act_ref, wgu_buf, wd_buf, sem):
    w = pl.program_id(0)
    e = sched_ref[S_EXPERT, w]
    slot = sched_ref[S_SLOT, w]
    n_rows = sched_ref[S_ROWS, w]

    def weight_copies(expert, s):
        return (pltpu.make_async_copy(wgu_hbm.at[expert], wgu_buf.at[s], sem.at[0, s]),
                pltpu.make_async_copy(wd_hbm.at[expert], wd_buf.at[s], sem.at[1, s]))

    @pl.when(sched_ref[S_NEW, w] == 1)
    def _():
        @pl.when(w == 0)
        def _():
            for cp in weight_copies(e, slot):
                cp.start()

        nxt = sched_ref[S_NEXT, w]

        @pl.when(nxt >= 0)
        def _():
            for cp in weight_copies(nxt, 1 - slot):
                cp.start(priority=1)

        for cp in weight_copies(e, slot):
            cp.wait()

    @pl.when(n_rows > 0)
    def _():
        half = D_MODEL // 2
        live = lax.broadcasted_iota(jnp.int32, (EXPERT_TILE, 1), 0) < n_rows
        x_lo, x_hi = _unpack_pair(jnp.where(live, xs_ref[...], jnp.uint32(0)))
        x = jnp.concatenate([x_lo.astype(BF16), x_hi.astype(BF16)], axis=1)
        for c in range(D_FF // FF_CHUNK):
            gc = slice(c * FF_CHUNK, (c + 1) * FF_CHUNK)
            uc = slice(D_FF + c * FF_CHUNK, D_FF + (c + 1) * FF_CHUNK)
            gate = jnp.dot(x, wgu_buf[slot, :, gc].astype(BF16), preferred_element_type=F32) + bgu_ref[0, :, gc]
            up = jnp.dot(x, wgu_buf[slot, :, uc].astype(BF16), preferred_element_type=F32) + bgu_ref[0, :, uc]
            gate = jnp.minimum(gate, SWIGLU_LIMIT)
            up = jnp.clip(up, -SWIGLU_LIMIT, SWIGLU_LIMIT)
            act_ref[:, gc] = (gate * jax.nn.sigmoid(SWIGLU_ALPHA * gate) * (up + 1.0)).astype(BF16)
        for c in range(half // FF_CHUNK):
            lo_c = slice(c * FF_CHUNK, (c + 1) * FF_CHUNK)
            hi_c = slice(half + c * FF_CHUNK, half + (c + 1) * FF_CHUNK)
            y_lo = (jnp.dot(act_ref[...], wd_buf[slot, :, lo_c].astype(BF16), preferred_element_type=F32)
                    + bd_ref[0, :, lo_c])
            y_hi = (jnp.dot(act_ref[...], wd_buf[slot, :, hi_c].astype(BF16), preferred_element_type=F32)
                    + bd_ref[0, :, hi_c])
            y_ref[:, lo_c] = _pack_pair(y_lo, y_hi)


def _experts(sched, xs, wgu, bgu, wd, bd):
    rows, half = xs.shape
    n_work = sched.shape[1]
    grid_spec = pltpu.PrefetchScalarGridSpec(
        num_scalar_prefetch=1,
        grid=(n_work,),
        in_specs=[
            pl.BlockSpec((EXPERT_TILE, half), lambda w, sc: (sc[S_TILE, w], 0)),
            pl.BlockSpec(memory_space=pl.ANY),
            pl.BlockSpec((1, 1, 2 * D_FF), lambda w, sc: (sc[S_EXPERT, w], 0, 0)),
            pl.BlockSpec(memory_space=pl.ANY),
            pl.BlockSpec((1, 1, D_MODEL), lambda w, sc: (sc[S_EXPERT, w], 0, 0)),
        ],
        out_specs=pl.BlockSpec((EXPERT_TILE, half), lambda w, sc: (sc[S_TILE, w], 0)),
        scratch_shapes=[pltpu.VMEM((EXPERT_TILE, D_FF), BF16),
                        pltpu.VMEM((2, D_MODEL, 2 * D_FF), F32),
                        pltpu.VMEM((2, D_FF, D_MODEL), F32),
                        pltpu.SemaphoreType.DMA((2, 2))],
    )
    return pl.pallas_call(
        _expert_kernel,
        grid_spec=grid_spec,
        out_shape=jax.ShapeDtypeStruct((rows, half), jnp.uint32),
        compiler_params=pltpu.CompilerParams(dimension_semantics=("arbitrary",),
                                             vmem_limit_bytes=VMEM_LIMIT),
        name="experts",
    )(sched, xs, wgu, bgu, wd, bd)


def _unpack_pair(packed):
    lo = lax.bitcast_convert_type(packed << 16, F32)
    hi = lax.bitcast_convert_type(packed & jnp.uint32(0xFFFF0000), F32)
    return lo, hi


def _pack_pair(lo, hi):
    lo_bits = lax.bitcast_convert_type(lo.astype(BF16).astype(F32), jnp.uint32)
    hi_bits = lax.bitcast_convert_type(hi.astype(BF16).astype(F32), jnp.uint32)
    return (lo_bits >> 16) | (hi_bits & jnp.uint32(0xFFFF0000))


def _combine_kernel(yg_ref, h2_ref, gate_ref, o_ref):
    half = D_MODEL // 2
    gates = gate_ref[...]
    lo_sum = h2_ref[:, :half]
    hi_sum = h2_ref[:, half:]
    for k in range(TOP_K):
        gate = gates[:, k:k + 1]
        lo, hi = _unpack_pair(yg_ref[k])
        lo_sum = lo_sum + gate * lo
        hi_sum = hi_sum + gate * hi
    o_ref[:, :half] = lo_sum
    o_ref[:, half:] = hi_sum


def _combine(yg, h2, gates, part):
    n = h2.shape[0]
    half = D_MODEL // 2
    steps = yg.shape[1] // ROW_TILE
    row = lambda i: (i + part * steps, 0)
    return pl.pallas_call(
        _combine_kernel,
        grid=(steps,),
        in_specs=[
            pl.BlockSpec((TOP_K, ROW_TILE, half), lambda i: (0, i, 0)),
            pl.BlockSpec((ROW_TILE, D_MODEL), row),
            pl.BlockSpec((ROW_TILE, TOP_K), row),
        ],
        out_specs=pl.BlockSpec((ROW_TILE, D_MODEL), row),
        out_shape=jax.ShapeDtypeStruct((n, D_MODEL), F32),
        input_output_aliases={1: 0},
        compiler_params=pltpu.CompilerParams(dimension_semantics=("arbitrary",),
                                             vmem_limit_bytes=VMEM_LIMIT),
        name="combine",
    )(yg, h2, gates)


def _work_schedule(counts, n_rows):
    n_work = n_rows // EXPERT_TILE + N_EXPERTS
    tiles = (counts + (EXPERT_TILE - 1)) // EXPERT_TILE
    tile_end = jnp.cumsum(tiles)
    tile_start = tile_end - tiles
    total = tile_end[-1]
    w = jnp.arange(n_work, dtype=jnp.int32)
    wc = jnp.minimum(w, total - 1)
    expert_of = jnp.sum((tile_end[None, :] <= wc[:, None]).astype(jnp.int32), axis=1)
    mine = expert_of[:, None] == jnp.arange(N_EXPERTS, dtype=jnp.int32)[None, :]
    pick = lambda table: jnp.sum(jnp.where(mine, table[None, :], 0), axis=1)
    live = jnp.clip(pick(counts) - (wc - pick(tile_start)) * EXPERT_TILE, 0, EXPERT_TILE)
    live = jnp.where(w < total, live, 0)
    prev_e = jnp.concatenate([jnp.full((1,), -1, jnp.int32), expert_of[:-1]])
    new_e = jnp.logical_and(w < total, expert_of != prev_e)
    slot = (jnp.cumsum(new_e.astype(jnp.int32)) - 1) % 2
    later = jnp.logical_and(new_e[None, :], w[None, :] > w[:, None])
    next_w = jnp.min(jnp.where(later, w[None, :], n_work), axis=1)
    next_e = jnp.sum(jnp.where(next_w[:, None] == w[None, :], expert_of[None, :], 0), axis=1)
    next_e = jnp.where(next_w < n_work, next_e, -1)
    sched = jnp.stack([wc, expert_of, live, new_e.astype(jnp.int32), slot, next_e]).astype(jnp.int32)
    return sched, (tile_start * EXPERT_TILE).astype(jnp.int32)


def kernel(x, meta_tokens, norm1_g, w_in, q_norm_g, k_norm_g, attn_sink, dw_w, dw_b, conv_ln_g,
           conv_ln_b, w_out, norm2_g, router_w, router_b, w_gate_up, b_gate_up, w_down, b_down):
    assert norm1_g.shape[0] == 1, "single-layer trunk: meta-token query rows are not materialised"
    b, s, d = x.shape
    n = b * s
    x2d = x.reshape(n, d)

    scale = HEAD_DIM ** -0.5
    qg = (jnp.tile(q_norm_g[0], N_Q_HEADS) * scale).reshape(1, ATTN_W)
    kg = jnp.tile(k_norm_g[0], N_KV_HEADS).reshape(1, KV_W)
    head_of = np.arange(ATTN_W) // HEAD_DIM
    pq = jnp.asarray((head_of[:, None] == head_of[None, :]) / HEAD_DIM, BF16)
    pk = pq[:KV_W, :KV_W]
    g1 = norm1_g[0].reshape(1, d)
    w_in_b = w_in[0].astype(BF16)

    q, kd, vd, hc = _inproj(x2d, g1, w_in_b, qg, kg, pq, pk, ROW_TILE)
    _, kmeta, vmeta, hc_meta = _inproj(meta_tokens, g1, w_in_b, qg, kg, pq, pk, N_META)

    meta_pad = ((0, BLOCK - N_META), (0, 0))
    attn = _attention(attn_sink[0], q.reshape(b, s, ATTN_W), kd.reshape(b, s, 2 * KV_W),
                      vd.reshape(b, s, 2 * KV_W), jnp.pad(kmeta, meta_pad), jnp.pad(vmeta, meta_pad),
                      jnp.asarray(_alibi_bias()))

    w_out_b = w_out[0].astype(BF16)
    rw = router_w[0].T
    rwh = rw.astype(BF16)
    rwl = (rw - rwh.astype(F32)).astype(BF16)
    rb = router_b[0].reshape(N_EXPERTS, 1)
    tri = jnp.asarray(np.triu(np.ones((ROW_TILE, ROW_TILE), np.float32), 1), BF16)
    h2, xp, route, counts = _outproj(attn.reshape(n, ATTN_W), hc.reshape(b, s, CONV_W), hc_meta,
                                     dw_w[0].reshape(CONV_K, CONV_W), dw_b[0].reshape(1, CONV_W),
                                     conv_ln_g[0].reshape(1, CONV_W), conv_ln_b[0].reshape(1, CONV_W), x2d,
                                     w_out_b[:ATTN_W], w_out_b[ATTN_W:], norm2_g[0].reshape(1, d),
                                     rwh, rwl, rb, tri)

    counts_i = counts[:, 0].astype(jnp.int32)
    idx = route[:TOP_K].astype(jnp.int32)
    rank = route[2 * TOP_K:3 * TOP_K].astype(jnp.int32)
    gates = route[TOP_K:2 * TOP_K].T
    sched, first_row = _work_schedule(counts_i, n * TOP_K)
    chosen = idx[:, :, None] == jnp.arange(N_EXPERTS, dtype=jnp.int32)
    pos = rank + jnp.sum(jnp.where(chosen, first_row, 0), axis=-1)
    pos3 = pos.reshape(TOP_K, n // SC_CHUNK, SC_CHUNK).transpose(1, 0, 2)

    xs = _sc_dispatch(pos3, xp, sched.shape[1] * EXPERT_TILE)
    y = _experts(sched, xs, w_gate_up[0],
                 b_gate_up[0].reshape(N_EXPERTS, 1, 2 * D_FF), w_down[0],
                 b_down[0].reshape(N_EXPERTS, 1, D_MODEL))
    out = h2
    chunks = pos3.shape[0] // COMBINE_PARTS
    for part in range(COMBINE_PARTS):
        out = _combine(_sc_collect(pos3[part * chunks:(part + 1) * chunks], y), out, gates, part)
    return out.reshape(b, s, d)
```

```python
import functools

import numpy as np
import jax
import jax.numpy as jnp
from jax import lax
from jax.experimental import pallas as pl
from jax.experimental.pallas import tpu as pltpu
from jax.experimental.pallas import tpu_sc as plsc

F32 = jnp.float32
BF16 = jnp.bfloat16

D_MODEL = 1024
N_META = 16
HEAD_DIM = 64
N_Q_HEADS = 8
N_KV_HEADS = 2
GROUP = N_Q_HEADS // N_KV_HEADS
ATTN_W = N_Q_HEADS * HEAD_DIM
KV_W = N_KV_HEADS * HEAD_DIM
CONV_W = D_MODEL - ATTN_W
IN_W = ATTN_W + 2 * KV_W + 2 * CONV_W
WINDOW = 128
BLOCK = 128
CONV_K = 31
CONV_PAD = CONV_K // 2
N_EXPERTS = 32
TOP_K = 4
D_FF = D_MODEL
SWIGLU_LIMIT = 7.0
SWIGLU_ALPHA = 1.702
RMS_EPS = 1e-6
LN_EPS = 1e-5
NEG_INF = -1e30

LANES = 128
ROW_TILE = 512
EXPERT_TILE = 512
SC_CORES = 2
SC_SUBCORES = 16
SC_WORKERS = SC_CORES * SC_SUBCORES
SC_CHUNK = 64
CONV_ROWS = 128
FF_CHUNK = 512
TILE_SPANS = (128, 256, 384, 512)
Q_BLOCKS = 4
COMBINE_PARTS = 2
ROUTE_ROWS = 16
VMEM_LIMIT = 56 * 1024 * 1024


def _rms(x, eps):
    return x * lax.rsqrt(jnp.mean(x * x, axis=-1, keepdims=True) + eps)


def _inproj_kernel(x_ref, g1_ref, w_ref, qg_ref, kg_ref, pq_ref, pk_ref,
                   q_ref, k_ref, v_ref, hc_ref):
    x = x_ref[...]
    n = (_rms(x, RMS_EPS) * g1_ref[...]).astype(BF16)
    proj = jnp.dot(n, w_ref[...], preferred_element_type=F32)
    q = proj[:, :ATTN_W]
    k = proj[:, ATTN_W:ATTN_W + KV_W]
    v = proj[:, ATTN_W + KV_W:ATTN_W + 2 * KV_W]
    a = proj[:, ATTN_W + 2 * KV_W:ATTN_W + 2 * KV_W + CONV_W]
    g = proj[:, ATTN_W + 2 * KV_W + CONV_W:]
    qms = jnp.dot((q * q).astype(BF16), pq_ref[...], preferred_element_type=F32)
    kms = jnp.dot((k * k).astype(BF16), pk_ref[...], preferred_element_type=F32)
    q_ref[...] = (q * lax.rsqrt(qms + RMS_EPS) * qg_ref[...]).astype(BF16)
    kn = k * lax.rsqrt(kms + RMS_EPS) * kg_ref[...]
    lo = lax.broadcasted_iota(jnp.int32, kn.shape, 1) < HEAD_DIM
    ksw = pltpu.roll(kn, HEAD_DIM, 1)
    k_ref[...] = jnp.concatenate([jnp.where(lo, kn, ksw), jnp.where(lo, ksw, kn)], axis=1).astype(BF16)
    vsw = pltpu.roll(v, HEAD_DIM, 1)
    v_ref[...] = jnp.concatenate([jnp.where(lo, v, vsw), jnp.where(lo, vsw, v)], axis=1).astype(BF16)
    hc_ref[...] = a * jax.nn.sigmoid(g)


def _inproj(x2d, g1, w_in_b, qg, kg, pq, pk, tile):
    n = x2d.shape[0]
    const = lambda i: (0, 0)
    row = lambda i: (i, 0)
    return pl.pallas_call(
        _inproj_kernel,
        grid=(n // tile,),
        in_specs=[
            pl.BlockSpec((tile, D_MODEL), row),
            pl.BlockSpec((1, D_MODEL), const),
            pl.BlockSpec((D_MODEL, IN_W), const),
            pl.BlockSpec((1, ATTN_W), const),
            pl.BlockSpec((1, KV_W), const),
            pl.BlockSpec((ATTN_W, ATTN_W), const),
            pl.BlockSpec((KV_W, KV_W), const),
        ],
        out_specs=[
            pl.BlockSpec((tile, ATTN_W), row),
            pl.BlockSpec((tile, 2 * KV_W), row),
            pl.BlockSpec((tile, 2 * KV_W), row),
            pl.BlockSpec((tile, CONV_W), row),
        ],
        out_shape=[
            jax.ShapeDtypeStruct((n, ATTN_W), BF16),
            jax.ShapeDtypeStruct((n, 2 * KV_W), BF16),
            jax.ShapeDtypeStruct((n, 2 * KV_W), BF16),
            jax.ShapeDtypeStruct((n, CONV_W), F32),
        ],
        compiler_params=pltpu.CompilerParams(dimension_semantics=("arbitrary",),
                                             vmem_limit_bytes=VMEM_LIMIT),
        name="inproj",
    )(x2d, g1, w_in_b, qg, kg, pq, pk)


def _conv_fill(pad_ref, hm_ref, hc_ref, seq):
    tail = pad_ref.shape[0] - (N_META + seq)
    pad_ref[0:N_META, :] = hm_ref[...]
    pad_ref[N_META:N_META + seq, :] = hc_ref[0]
    pad_ref[N_META + seq:, :] = jnp.zeros((tail, CONV_W), F32)


def _conv_chunk(pad_ref, y_ref, w_ref, b_ref, lg_ref, lb_ref, base):
    first = N_META - CONV_PAD
    span = CONV_ROWS + 8
    for lt in range(CONV_W // LANES):
        ls = slice(lt * LANES, (lt + 1) * LANES)
        win = pad_ref[pl.ds(base, CONV_ROWS + 32), ls]
        acc = None
        for sub in range(8):
            part = None
            for al in range(4):
                k = 8 * al + sub - first
                if 0 <= k < CONV_K:
                    term = win[8 * al:8 * al + span] * w_ref[k:k + 1, ls]
                    part = term if part is None else part + term
            if sub:
                part = pltpu.roll(part, span - sub, 0)
            acc = part[:CONV_ROWS] if acc is None else acc + part[:CONV_ROWS]
        y_ref[:, ls] = acc
    y = y_ref[...] + b_ref[...]
    mu = jnp.mean(y, axis=-1, keepdims=True)
    yc = y - mu
    var = jnp.mean(yc * yc, axis=-1, keepdims=True)
    z = yc * lax.rsqrt(var + LN_EPS) * lg_ref[...] + lb_ref[...]
    return (z * jax.nn.sigmoid(z)).astype(BF16)


def _alibi_bias():
    qi = np.arange(BLOCK)[:, None]
    kj = np.arange(BLOCK)[None, :]
    dists = [qi + BLOCK - kj, np.abs(qi - kj), kj + BLOCK - qi]
    out = np.zeros((N_KV_HEADS, 4, GROUP * BLOCK, BLOCK), np.float32)
    for h in range(N_KV_HEADS):
        for g in range(GROUP):
            slope = 2.0 ** (-8.0 * (h * GROUP + g + 1) / N_Q_HEADS)
            for p, d in enumerate(dists):
                out[h, p, g * BLOCK:(g + 1) * BLOCK] = np.where(d <= WINDOW, -slope * d, NEG_INF)
    out[:, 3, :, N_META:] = NEG_INF
    return out


def _attn_kernel(sink_ref, q_ref, kp_ref, kc_ref, kn_ref, vp_ref, vc_ref, vn_ref,
                 km_ref, vm_ref, bias_ref, o_ref, *, n_steps):
    i = pl.program_id(1)
    lo = lax.broadcasted_iota(jnp.int32, (BLOCK, LANES), 1) < HEAD_DIM
    edge_first = jnp.where(i == 0, NEG_INF, 0.0).astype(F32)
    edge_last = jnp.where(i == n_steps - 1, NEG_INF, 0.0).astype(F32)
    nt = (((1,), (1,)), ((), ()))
    zero = jnp.zeros((BLOCK, LANES), BF16)

    def key_block(before_ref, here_ref, after_ref, idx, ks):
        if idx < 0:
            return before_ref[0, :, ks]
        if idx >= Q_BLOCKS:
            return after_ref[0, :, ks]
        return here_ref[0, idx * BLOCK:(idx + 1) * BLOCK, ks]

    for qb in range(Q_BLOCKS):
        qr = slice(qb * BLOCK, (qb + 1) * BLOCK)
        for h in range(N_KV_HEADS):
            ks = slice(h * LANES, (h + 1) * LANES)
            rows = []
            for j in range(2):
                pair = q_ref[0, qr, (2 * h + j) * LANES:(2 * h + j + 1) * LANES]
                rows.append(jnp.where(lo, pair, zero))
                rows.append(jnp.where(lo, zero, pair))
            qs = jnp.concatenate(rows, axis=0)
            k_p, k_c, k_n = (key_block(kp_ref, kc_ref, kn_ref, qb + rel, ks) for rel in (-1, 0, 1))
            v_p, v_c, v_n = (key_block(vp_ref, vc_ref, vn_ref, qb + rel, ks) for rel in (-1, 0, 1))
            s_p = lax.dot_general(qs, k_p, nt, preferred_element_type=F32) + bias_ref[h, 0]
            s_c = lax.dot_general(qs, k_c, nt, preferred_element_type=F32) + bias_ref[h, 1]
            s_n = lax.dot_general(qs, k_n, nt, preferred_element_type=F32) + bias_ref[h, 2]
            s_m = lax.dot_general(qs, km_ref[:, ks], nt, preferred_element_type=F32) + bias_ref[h, 3]
            if qb == 0:
                s_p = s_p + edge_first
            if qb == Q_BLOCKS - 1:
                s_n = s_n + edge_last
            sink = jnp.concatenate(
                [jnp.full((BLOCK, 1), sink_ref[h * GROUP + g], F32) for g in range(GROUP)], axis=0)
            m = jnp.max(jnp.maximum(jnp.maximum(s_p, s_c), jnp.maximum(s_n, s_m)), axis=-1, keepdims=True)
            m = jnp.maximum(m, sink)
            p_p = jnp.exp(s_p - m)
            p_c = jnp.exp(s_c - m)
            p_n = jnp.exp(s_n - m)
            p_m = jnp.exp(s_m - m)
            denom = jnp.sum((p_p + p_c) + (p_n + p_m), axis=-1, keepdims=True) + jnp.exp(sink - m)
            o = (jnp.dot(p_p.astype(BF16), v_p, preferred_element_type=F32)
                 + jnp.dot(p_c.astype(BF16), v_c, preferred_element_type=F32)
                 + jnp.dot(p_n.astype(BF16), v_n, preferred_element_type=F32)
                 + jnp.dot(p_m.astype(BF16), vm_ref[:, ks], preferred_element_type=F32))
            o = o / denom
            for j in range(2):
                even = o[(2 * j) * BLOCK:(2 * j + 1) * BLOCK]
                odd = o[(2 * j + 1) * BLOCK:(2 * j + 2) * BLOCK]
                o_ref[0, qr, (2 * h + j) * LANES:(2 * h + j + 1) * LANES] = jnp.where(lo, even, odd).astype(BF16)


def _attention(sink, q, kd, vd, kmeta, vmeta, bias):
    b, s, _ = q.shape
    nb = s // BLOCK
    steps = nb // Q_BLOCKS
    here = lambda bi, i: (bi, i, 0)
    before = lambda bi, i: (bi, jnp.maximum(Q_BLOCKS * i - 1, 0), 0)
    after = lambda bi, i: (bi, jnp.minimum(Q_BLOCKS * i + Q_BLOCKS, nb - 1), 0)
    const2 = lambda bi, i: (0, 0)
    edge_blk = (1, BLOCK, 2 * KV_W)
    here_blk = (1, Q_BLOCKS * BLOCK, 2 * KV_W)
    return pl.pallas_call(
        functools.partial(_attn_kernel, n_steps=steps),
        grid=(b, steps),
        in_specs=[
            pl.BlockSpec(memory_space=pltpu.SMEM),
            pl.BlockSpec((1, Q_BLOCKS * BLOCK, ATTN_W), here),
            pl.BlockSpec(edge_blk, before), pl.BlockSpec(here_blk, here), pl.BlockSpec(edge_blk, after),
            pl.BlockSpec(edge_blk, before), pl.BlockSpec(here_blk, here), pl.BlockSpec(edge_blk, after),
            pl.BlockSpec((BLOCK, 2 * KV_W), const2),
            pl.BlockSpec((BLOCK, 2 * KV_W), const2),
            pl.BlockSpec(bias.shape, lambda bi, i: (0, 0, 0, 0)),
        ],
        out_specs=pl.BlockSpec((1, Q_BLOCKS * BLOCK, ATTN_W), here),
        out_shape=jax.ShapeDtypeStruct((b, s, ATTN_W), BF16),
        compiler_params=pltpu.CompilerParams(dimension_semantics=("arbitrary", "arbitrary"),
                                             vmem_limit_bytes=VMEM_LIMIT),
        name="attention",
    )(sink, q, kd, kd, kd, vd, vd, vd, kmeta, vmeta, bias)


def _outproj_kernel(attn_ref, hc_ref, hm_ref, cw_ref, cb_ref, lg_ref, lb_ref, x_ref, wa_ref, wc_ref, g2_ref,
                    rwh_ref, rwl_ref, rb_ref, tri_ref, h2_ref, xp_ref, route_ref, cnt_ref,
                    run_ref, pad_ref, y_ref, conv_ref, *, seq):
    step = pl.program_id(0)
    tiles_per_seq = seq // ROW_TILE
    part = step % tiles_per_seq

    @pl.when(step == 0)
    def _():
        run_ref[...] = jnp.zeros_like(run_ref)

    @pl.when(part == 0)
    def _():
        _conv_fill(pad_ref, hm_ref, hc_ref, seq)

    for c in range(ROW_TILE // CONV_ROWS):
        base = pl.multiple_of(part * ROW_TILE + c * CONV_ROWS, CONV_ROWS)
        conv_ref[c * CONV_ROWS:(c + 1) * CONV_ROWS, :] = _conv_chunk(pad_ref, y_ref, cw_ref, cb_ref, lg_ref,
                                                                     lb_ref, base)
    mix = (jnp.dot(attn_ref[...], wa_ref[...], preferred_element_type=F32)
           + jnp.dot(conv_ref[...], wc_ref[...], preferred_element_type=F32))
    h2 = x_ref[...] + mix
    h2_ref[...] = h2
    hn = _rms(h2, RMS_EPS) * g2_ref[...]
    hn_hi = hn.astype(BF16)
    half = D_MODEL // 2
    xp_ref[...] = _pack_pair(hn[:, :half], hn[:, half:])
    hn_lo = (hn - hn_hi.astype(F32)).astype(BF16)
    nt = (((1,), (1,)), ((), ()))
    logits = (lax.dot_general(rwh_ref[...], hn_hi, nt, preferred_element_type=F32)
              + lax.dot_general(rwh_ref[...], hn_lo, nt, preferred_element_type=F32)
              + lax.dot_general(rwl_ref[...], hn_hi, nt, preferred_element_type=F32)) + rb_ref[...]
    expert = lax.broadcasted_iota(jnp.int32, logits.shape, 0).astype(F32)
    work = logits
    vals, sels = [], []
    for r in range(TOP_K):
        m = jnp.max(work, axis=0, keepdims=True)
        idx = jnp.min(jnp.where(work == m, expert, float(N_EXPERTS)), axis=0, keepdims=True)
        sel = expert == idx
        work = jnp.where(sel, -jnp.inf, work)
        route_ref[r:r + 1, :] = idx
        vals.append(m)
        sels.append(sel)
    onehot = jnp.where(jnp.logical_or(jnp.logical_or(sels[0], sels[1]), jnp.logical_or(sels[2], sels[3])),
                       1.0, 0.0)
    exps = [jnp.exp(v - vals[0]) for v in vals]
    tot = exps[0] + exps[1] + exps[2] + exps[3]
    before = jnp.dot(onehot.astype(BF16), tri_ref[...], preferred_element_type=F32) + run_ref[...]
    for r in range(TOP_K):
        route_ref[TOP_K + r:TOP_K + r + 1, :] = exps[r] / tot
        route_ref[2 * TOP_K + r:2 * TOP_K + r + 1, :] = jnp.sum(jnp.where(sels[r], before, 0.0), axis=0,
                                                               keepdims=True)
    route_ref[3 * TOP_K:, :] = jnp.zeros((ROUTE_ROWS - 3 * TOP_K, logits.shape[1]), F32)
    run_ref[...] = run_ref[...] + jnp.sum(onehot, axis=1, keepdims=True)
    cnt_ref[...] = run_ref[...]


def _outproj(attn2d, hc, hc_meta, dw_w, dw_b, ln_g, ln_b, x2d, wa, wc, g2, rwh, rwl, rb, tri):
    n = x2d.shape[0]
    seq = hc.shape[1]
    tile = ROW_TILE
    const = lambda i: (0, 0)
    row = lambda i: (i, 0)
    half = D_MODEL // 2
    return pl.pallas_call(
        functools.partial(_outproj_kernel, seq=seq),
        grid=(n // tile,),
        in_specs=[
            pl.BlockSpec((tile, ATTN_W), row),
            pl.BlockSpec((1, seq, CONV_W), lambda i: (i // (seq // tile), 0, 0)),
            pl.BlockSpec((N_META, CONV_W), const),
            pl.BlockSpec((CONV_K, CONV_W), const),
            pl.BlockSpec((1, CONV_W), const),
            pl.BlockSpec((1, CONV_W), const),
            pl.BlockSpec((1, CONV_W), const),
            pl.BlockSpec((tile, D_MODEL), row),
            pl.BlockSpec((ATTN_W, D_MODEL), const),
            pl.BlockSpec((CONV_W, D_MODEL), const),
            pl.BlockSpec((1, D_MODEL), const),
            pl.BlockSpec((N_EXPERTS, D_MODEL), const),
            pl.BlockSpec((N_EXPERTS, D_MODEL), const),
            pl.BlockSpec((N_EXPERTS, 1), const),
            pl.BlockSpec((tile, tile), const),
        ],
        out_specs=[
            pl.BlockSpec((tile, D_MODEL), row),
            pl.BlockSpec((tile, half), row),
            pl.BlockSpec((ROUTE_ROWS, tile), lambda i: (0, i)),
            pl.BlockSpec((N_EXPERTS, 1), const),
        ],
        out_shape=[
            jax.ShapeDtypeStruct((n, D_MODEL), F32),
            jax.ShapeDtypeStruct((n, half), jnp.uint32),
            jax.ShapeDtypeStruct((ROUTE_ROWS, n), F32),
            jax.ShapeDtypeStruct((N_EXPERTS, 1), F32),
        ],
        scratch_shapes=[pltpu.VMEM((N_EXPERTS, 1), F32),
                        pltpu.VMEM((N_META + seq + 32, CONV_W), F32),
                        pltpu.VMEM((CONV_ROWS, CONV_W), F32),
                        pltpu.VMEM((tile, CONV_W), BF16)],
        compiler_params=pltpu.CompilerParams(dimension_semantics=("arbitrary",),
                                             vmem_limit_bytes=VMEM_LIMIT),
        name="outproj",
    )(attn2d, hc, hc_meta, dw_w, dw_b, ln_g, ln_b, x2d, wa, wc, g2, rwh, rwl, rb, tri)


def _sc_mesh():
    return plsc.VectorSubcoreMesh(core_axis_name="c", subcore_axis_name="s",
                                  num_cores=SC_CORES, num_subcores=SC_SUBCORES)


def _sc_worker():
    return lax.axis_index("s") * SC_CORES + lax.axis_index("c")


def _sc_dispatch(pos3, xp, out_rows):
    n, half = xp.shape
    per_worker = n // SC_CHUNK // SC_WORKERS

    @functools.partial(
        pl.kernel, mesh=_sc_mesh(),
        out_type=jax.ShapeDtypeStruct((out_rows, half), jnp.uint32),
        scratch_types=[pltpu.VMEM((2, TOP_K, SC_CHUNK), jnp.int32),
                       pltpu.VMEM((2, SC_CHUNK, half), jnp.uint32),
                       pltpu.SemaphoreType.DMA((2,)),
                       pltpu.SemaphoreType.DMA],
        name="sc_dispatch")
    def run(pos_hbm, xp_hbm, xs_hbm, idx_v, rows_v, load_sem, scatter_sem):
        first = _sc_worker() * per_worker

        def load(j, buf):
            c = first + j
            return (pltpu.async_copy(pos_hbm.at[c], idx_v.at[buf], load_sem.at[buf]),
                    pltpu.async_copy(xp_hbm.at[pl.ds(c * SC_CHUNK, SC_CHUNK)], rows_v.at[buf], load_sem.at[buf]))

        loads = load(0, 0)
        for j in range(per_worker):
            buf = j % 2
            for cp in loads:
                cp.wait()
            if j + 1 < per_worker:
                loads = load(j + 1, 1 - buf)
            copies = [pltpu.async_copy(rows_v.at[buf], xs_hbm.at[idx_v.at[buf].at[k]], scatter_sem)
                      for k in range(TOP_K)]
            for cp in copies:
                cp.wait()

    return run(pos3, xp)


def _sc_collect(pos3, y):
    half = y.shape[1]
    n = pos3.shape[0] * SC_CHUNK
    per_worker = n // SC_CHUNK // SC_WORKERS

    @functools.partial(
        pl.kernel, mesh=_sc_mesh(),
        out_type=jax.ShapeDtypeStruct((TOP_K, n, half), jnp.uint32),
        scratch_types=[pltpu.VMEM((TOP_K, SC_CHUNK), jnp.int32),
                       pltpu.VMEM((2, SC_CHUNK, half), jnp.uint32),
                       pltpu.SemaphoreType.DMA((2,)),
                       pltpu.SemaphoreType.DMA((2,))],
        name="sc_collect")
    def run(pos_hbm, y_hbm, yg_hbm, idx_v, rows_v, gather_sem, store_sem):
        first = _sc_worker() * per_worker

        @pl.loop(0, per_worker)
        def _(j):
            c = first + j
            pltpu.sync_copy(pos_hbm.at[c], idx_v)

            def gather(k):
                return pltpu.async_copy(y_hbm.at[idx_v.at[k]], rows_v.at[k % 2], gather_sem.at[k % 2])

            gathers = [gather(0)]
            stores = []
            for k in range(TOP_K):
                if k + 1 < TOP_K:
                    if k >= 1:
                        stores[k - 1].wait()
                    gathers.append(gather(k + 1))
                gathers[k].wait()
                stores.append(pltpu.async_copy(rows_v.at[k % 2], yg_hbm.at[k, pl.ds(c * SC_CHUNK, SC_CHUNK)],
                                               store_sem.at[k % 2]))
            stores[TOP_K - 2].wait()
            stores[TOP_K - 1].wait()

    return run(pos3, y)


S_TILE, S_EXPERT, S_ROWS, S_NEW, S_SLOT, S_NEXT = range(6)


def _expert_kernel(sched_ref, xs_ref, wgu_hbm, bgu_ref, wd_hbm, bd_ref, y_ref,
                   act_ref, wgu_buf, wd_buf, sem):
    w = pl.program_id(0)
    e = sched_ref[S_EXPERT, w]
    slot = sched_ref[S_SLOT, w]
    n_rows = sched_ref[S_ROWS, w]

    def weight_copies(expert, s):
        return (pltpu.make_async_copy(wgu_hbm.at[expert], wgu_buf.at[s], sem.at[0, s]),
                pltpu.make_async_copy(wd_hbm.at[expert], wd_buf.at[s], sem.at[1, s]))

    @pl.when(sched_ref[S_NEW, w] == 1)
    def _():
        @pl.when(w == 0)
        def _():
            for cp in weight_copies(e, slot):
                cp.start()

        nxt = sched_ref[S_NEXT, w]

        @pl.when(nxt >= 0)
        def _():
            for cp in weight_copies(nxt, 1 - slot):
                cp.start()

        for cp in weight_copies(e, slot):
            cp.wait()

    def tile_path(m):
        half = D_MODEL // 2
        live = lax.broadcasted_iota(jnp.int32, (m, 1), 0) < n_rows
        x_lo, x_hi = _unpack_pair(jnp.where(live, xs_ref[0:m, :], jnp.uint32(0)))
        x = jnp.concatenate([x_lo.astype(BF16), x_hi.astype(BF16)], axis=1)
        for c in range(D_FF // FF_CHUNK):
            gc = slice(c * FF_CHUNK, (c + 1) * FF_CHUNK)
            uc = slice(D_FF + c * FF_CHUNK, D_FF + (c + 1) * FF_CHUNK)
            gate = jnp.dot(x, wgu_buf[slot, :, gc].astype(BF16), preferred_element_type=F32) + bgu_ref[0, :, gc]
            up = jnp.dot(x, wgu_buf[slot, :, uc].astype(BF16), preferred_element_type=F32) + bgu_ref[0, :, uc]
            gate = jnp.minimum(gate, SWIGLU_LIMIT)
            up = jnp.clip(up, -SWIGLU_LIMIT, SWIGLU_LIMIT)
            act_ref[0:m, gc] = (gate * jax.nn.sigmoid(SWIGLU_ALPHA * gate) * (up + 1.0)).astype(BF16)
        for c in range(half // FF_CHUNK):
            lo_c = slice(c * FF_CHUNK, (c + 1) * FF_CHUNK)
            hi_c = slice(half + c * FF_CHUNK, half + (c + 1) * FF_CHUNK)
            y_lo = (jnp.dot(act_ref[0:m, :], wd_buf[slot, :, lo_c].astype(BF16), preferred_element_type=F32)
                    + bd_ref[0, :, lo_c])
            y_hi = (jnp.dot(act_ref[0:m, :], wd_buf[slot, :, hi_c].astype(BF16), preferred_element_type=F32)
                    + bd_ref[0, :, hi_c])
            y_ref[0:m, lo_c] = _pack_pair(y_lo, y_hi)

    for below, m in zip((0,) + TILE_SPANS[:-1], TILE_SPANS):
        pl.when(jnp.logical_and(n_rows > below, n_rows <= m))(functools.partial(tile_path, m))


def _experts(sched, xs, wgu, bgu, wd, bd):
    rows, half = xs.shape
    n_work = sched.shape[1]
    grid_spec = pltpu.PrefetchScalarGridSpec(
        num_scalar_prefetch=1,
        grid=(n_work,),
        in_specs=[
            pl.BlockSpec((EXPERT_TILE, half), lambda w, sc: (sc[S_TILE, w], 0)),
            pl.BlockSpec(memory_space=pl.ANY),
            pl.BlockSpec((1, 1, 2 * D_FF), lambda w, sc: (sc[S_EXPERT, w], 0, 0)),
            pl.BlockSpec(memory_space=pl.ANY),
            pl.BlockSpec((1, 1, D_MODEL), lambda w, sc: (sc[S_EXPERT, w], 0, 0)),
        ],
        out_specs=pl.BlockSpec((EXPERT_TILE, half), lambda w, sc: (sc[S_TILE, w], 0)),
        scratch_shapes=[pltpu.VMEM((EXPERT_TILE, D_FF), BF16),
                        pltpu.VMEM((2, D_MODEL, 2 * D_FF), F32),
                        pltpu.VMEM((2, D_FF, D_MODEL), F32),
                        pltpu.SemaphoreType.DMA((2, 2))],
    )
    return pl.pallas_call(
        _expert_kernel,
        grid_spec=grid_spec,
        out_shape=jax.ShapeDtypeStruct((rows, half), jnp.uint32),
        compiler_params=pltpu.CompilerParams(dimension_semantics=("arbitrary",),
                                             vmem_limit_bytes=VMEM_LIMIT),
        name="experts",
    )(sched, xs, wgu, bgu, wd, bd)


def _unpack_pair(packed):
    lo = lax.bitcast_convert_type(packed << 16, F32)
    hi = lax.bitcast_convert_type(packed & jnp.uint32(0xFFFF0000), F32)
    return lo, hi


def _pack_pair(lo, hi):
    lo_bits = lax.bitcast_convert_type(lo.astype(BF16).astype(F32), jnp.uint32)
    hi_bits = lax.bitcast_convert_type(hi.astype(BF16).astype(F32), jnp.uint32)
    return (lo_bits >> 16) | (hi_bits & jnp.uint32(0xFFFF0000))


def _combine_kernel(yg_ref, h2_ref, gate_ref, o_ref):
    half = D_MODEL // 2
    gates = gate_ref[...]
    lo_sum = h2_ref[:, :half]
    hi_sum = h2_ref[:, half:]
    for k in range(TOP_K):
        gate = gates[:, k:k + 1]
        lo, hi = _unpack_pair(yg_ref[k])
        lo_sum = lo_sum + gate * lo
        hi_sum = hi_sum + gate * hi
    o_ref[:, :half] = lo_sum
    o_ref[:, half:] = hi_sum


def _combine(yg, h2, gates, part):
    n = h2.shape[0]
    half = D_MODEL // 2
    steps = yg.shape[1] // ROW_TILE
    row = lambda i: (i + part * steps, 0)
    return pl.pallas_call(
        _combine_kernel,
        grid=(steps,),
        in_specs=[
            pl.BlockSpec((TOP_K, ROW_TILE, half), lambda i: (0, i, 0)),
            pl.BlockSpec((ROW_TILE, D_MODEL), row),
            pl.BlockSpec((ROW_TILE, TOP_K), row),
        ],
        out_specs=pl.BlockSpec((ROW_TILE, D_MODEL), row),
        out_shape=jax.ShapeDtypeStruct((n, D_MODEL), F32),
        input_output_aliases={1: 0},
        compiler_params=pltpu.CompilerParams(dimension_semantics=("arbitrary",),
                                             vmem_limit_bytes=VMEM_LIMIT),
        name="combine",
    )(yg, h2, gates)


def _work_schedule(counts, n_rows):
    n_work = n_rows // EXPERT_TILE + N_EXPERTS
    tiles = (counts + (EXPERT_TILE - 1)) // EXPERT_TILE
    tile_end = jnp.cumsum(tiles)
    tile_start = tile_end - tiles
    total = tile_end[-1]
    w = jnp.arange(n_work, dtype=jnp.int32)
    wc = jnp.minimum(w, total - 1)
    expert_of = jnp.sum((tile_end[None, :] <= wc[:, None]).astype(jnp.int32), axis=1)
    mine = expert_of[:, None] == jnp.arange(N_EXPERTS, dtype=jnp.int32)[None, :]
    pick = lambda table: jnp.sum(jnp.where(mine, table[None, :], 0), axis=1)
    live = jnp.clip(pick(counts) - (wc - pick(tile_start)) * EXPERT_TILE, 0, EXPERT_TILE)
    live = jnp.where(w < total, live, 0)
    prev_e = jnp.concatenate([jnp.full((1,), -1, jnp.int32), expert_of[:-1]])
    new_e = jnp.logical_and(w < total, expert_of != prev_e)
    slot = (jnp.cumsum(new_e.astype(jnp.int32)) - 1) % 2
    later = jnp.logical_and(new_e[None, :], w[None, :] > w[:, None])
    next_w = jnp.min(jnp.where(later, w[None, :], n_work), axis=1)
    next_e = jnp.sum(jnp.where(next_w[:, None] == w[None, :], expert_of[None, :], 0), axis=1)
    next_e = jnp.where(next_w < n_work, next_e, -1)
    sched = jnp.stack([wc, expert_of, live, new_e.astype(jnp.int32), slot, next_e]).astype(jnp.int32)
    return sched, (tile_start * EXPERT_TILE).astype(jnp.int32)


def kernel(x, meta_tokens, norm1_g, w_in, q_norm_g, k_norm_g, attn_sink, dw_w, dw_b, conv_ln_g,
           conv_ln_b, w_out, norm2_g, router_w, router_b, w_gate_up, b_gate_up, w_down, b_down):
    assert norm1_g.shape[0] == 1, "single-layer trunk: meta-token query rows are not materialised"
    b, s, d = x.shape
    n = b * s
    x2d = x.reshape(n, d)

    scale = HEAD_DIM ** -0.5
    qg = (jnp.tile(q_norm_g[0], N_Q_HEADS) * scale).reshape(1, ATTN_W)
    kg = jnp.tile(k_norm_g[0], N_KV_HEADS).reshape(1, KV_W)
    head_of = np.arange(ATTN_W) // HEAD_DIM
    pq = jnp.asarray((head_of[:, None] == head_of[None, :]) / HEAD_DIM, BF16)
    pk = pq[:KV_W, :KV_W]
    g1 = norm1_g[0].reshape(1, d)
    w_in_b = w_in[0].astype(BF16)

    q, kd, vd, hc = _inproj(x2d, g1, w_in_b, qg, kg, pq, pk, ROW_TILE)
    _, kmeta, vmeta, hc_meta = _inproj(meta_tokens, g1, w_in_b, qg, kg, pq, pk, N_META)

    meta_pad = ((0, BLOCK - N_META), (0, 0))
    attn = _attention(attn_sink[0], q.reshape(b, s, ATTN_W), kd.reshape(b, s, 2 * KV_W),
                      vd.reshape(b, s, 2 * KV_W), jnp.pad(kmeta, meta_pad), jnp.pad(vmeta, meta_pad),
                      jnp.asarray(_alibi_bias()))

    w_out_b = w_out[0].astype(BF16)
    rw = router_w[0].T
    rwh = rw.astype(BF16)
    rwl = (rw - rwh.astype(F32)).astype(BF16)
    rb = router_b[0].reshape(N_EXPERTS, 1)
    tri = jnp.asarray(np.triu(np.ones((ROW_TILE, ROW_TILE), np.float32), 1), BF16)
    h2, xp, route, counts = _outproj(attn.reshape(n, ATTN_W), hc.reshape(b, s, CONV_W), hc_meta,
                                     dw_w[0].reshape(CONV_K, CONV_W), dw_b[0].reshape(1, CONV_W),
                                     conv_ln_g[0].reshape(1, CONV_W), conv_ln_b[0].reshape(1, CONV_W), x2d,
                                     w_out_b[:ATTN_W], w_out_b[ATTN_W:], norm2_g[0].reshape(1, d),
                                     rwh, rwl, rb, tri)

    counts_i = counts[:, 0].astype(jnp.int32)
    idx = route[:TOP_K].astype(jnp.int32)
    rank = route[2 * TOP_K:3 * TOP_K].astype(jnp.int32)
    gates = route[TOP_K:2 * TOP_K].T
    sched, first_row = _work_schedule(counts_i, n * TOP_K)
    chosen = idx[:, :, None] == jnp.arange(N_EXPERTS, dtype=jnp.int32)
    pos = rank + jnp.sum(jnp.where(chosen, first_row, 0), axis=-1)
    pos3 = pos.reshape(TOP_K, n // SC_CHUNK, SC_CHUNK).transpose(1, 0, 2)

    xs = _sc_dispatch(pos3, xp, sched.shape[1] * EXPERT_TILE)
    y = _experts(sched, xs, w_gate_up[0],
                 b_gate_up[0].reshape(N_EXPERTS, 1, 2 * D_FF), w_down[0],
                 b_down[0].reshape(N_EXPERTS, 1, D_MODEL))
    out = h2
    chunks = pos3.shape[0] // COMBINE_PARTS
    for part in range(COMBINE_PARTS):
        out = _combine(_sc_collect(pos3[part * chunks:(part + 1) * chunks], y), out, gates, part)
    return out.reshape(b, s, d)
```

```python
import functools

import numpy as np
import jax
import jax.numpy as jnp
from jax import lax
from jax.experimental import pallas as pl
from jax.experimental.pallas import tpu as pltpu
from jax.experimental.pallas import tpu_sc as plsc

F32 = jnp.float32
BF16 = jnp.bfloat16

D_MODEL = 1024
N_META = 16
HEAD_DIM = 64
N_Q_HEADS = 8
N_KV_HEADS = 2
GROUP = N_Q_HEADS // N_KV_HEADS
ATTN_W = N_Q_HEADS * HEAD_DIM
KV_W = N_KV_HEADS * HEAD_DIM
CONV_W = D_MODEL - ATTN_W
IN_W = ATTN_W + 2 * KV_W + 2 * CONV_W
WINDOW = 128
BLOCK = 128
CONV_K = 31
CONV_PAD = CONV_K // 2
N_EXPERTS = 32
TOP_K = 4
D_FF = D_MODEL
SWIGLU_LIMIT = 7.0
SWIGLU_ALPHA = 1.702
RMS_EPS = 1e-6
LN_EPS = 1e-5
NEG_INF = -1e30

LANES = 128
ROW_TILE = 512
EXPERT_TILE = 1024
SC_CORES = 2
SC_SUBCORES = 16
SC_WORKERS = SC_CORES * SC_SUBCORES
SC_CHUNK = 64
CONV_ROWS = 128
FF_CHUNK = 512
TILE_SPANS = (256, 512, 768, 1024)
Q_BLOCKS = 4
COMBINE_PARTS = 2
ROUTE_ROWS = 16
VMEM_LIMIT = 56 * 1024 * 1024


def _rms(x, eps):
    return x * lax.rsqrt(jnp.mean(x * x, axis=-1, keepdims=True) + eps)


def _inproj_kernel(x_ref, g1_ref, w_ref, qg_ref, kg_ref, pq_ref, pk_ref,
                   q_ref, k_ref, v_ref, hc_ref):
    x = x_ref[...]
    n = (_rms(x, RMS_EPS) * g1_ref[...]).astype(BF16)
    proj = jnp.dot(n, w_ref[...], preferred_element_type=F32)
    q = proj[:, :ATTN_W]
    k = proj[:, ATTN_W:ATTN_W + KV_W]
    v = proj[:, ATTN_W + KV_W:ATTN_W + 2 * KV_W]
    a = proj[:, ATTN_W + 2 * KV_W:ATTN_W + 2 * KV_W + CONV_W]
    g = proj[:, ATTN_W + 2 * KV_W + CONV_W:]
    qms = jnp.dot((q * q).astype(BF16), pq_ref[...], preferred_element_type=F32)
    kms = jnp.dot((k * k).astype(BF16), pk_ref[...], preferred_element_type=F32)
    q_ref[...] = (q * lax.rsqrt(qms + RMS_EPS) * qg_ref[...]).astype(BF16)
    kn = k * lax.rsqrt(kms + RMS_EPS) * kg_ref[...]
    lo = lax.broadcasted_iota(jnp.int32, kn.shape, 1) < HEAD_DIM
    ksw = pltpu.roll(kn, HEAD_DIM, 1)
    k_ref[...] = jnp.concatenate([jnp.where(lo, kn, ksw), jnp.where(lo, ksw, kn)], axis=1).astype(BF16)
    vsw = pltpu.roll(v, HEAD_DIM, 1)
    v_ref[...] = jnp.concatenate([jnp.where(lo, v, vsw), jnp.where(lo, vsw, v)], axis=1).astype(BF16)
    hc_ref[...] = a * jax.nn.sigmoid(g)


def _inproj(x2d, g1, w_in_b, qg, kg, pq, pk, tile):
    n = x2d.shape[0]
    const = lambda i: (0, 0)
    row = lambda i: (i, 0)
    return pl.pallas_call(
        _inproj_kernel,
        grid=(n // tile,),
        in_specs=[
            pl.BlockSpec((tile, D_MODEL), row),
            pl.BlockSpec((1, D_MODEL), const),
            pl.BlockSpec((D_MODEL, IN_W), const),
            pl.BlockSpec((1, ATTN_W), const),
            pl.BlockSpec((1, KV_W), const),
            pl.BlockSpec((ATTN_W, ATTN_W), const),
            pl.BlockSpec((KV_W, KV_W), const),
        ],
        out_specs=[
            pl.BlockSpec((tile, ATTN_W), row),
            pl.BlockSpec((tile, 2 * KV_W), row),
            pl.BlockSpec((tile, 2 * KV_W), row),
            pl.BlockSpec((tile, CONV_W), row),
        ],
        out_shape=[
            jax.ShapeDtypeStruct((n, ATTN_W), BF16),
            jax.ShapeDtypeStruct((n, 2 * KV_W), BF16),
            jax.ShapeDtypeStruct((n, 2 * KV_W), BF16),
            jax.ShapeDtypeStruct((n, CONV_W), F32),
        ],
        compiler_params=pltpu.CompilerParams(dimension_semantics=("arbitrary",),
                                             vmem_limit_bytes=VMEM_LIMIT),
        name="inproj",
    )(x2d, g1, w_in_b, qg, kg, pq, pk)


def _conv_fill(pad_ref, hm_ref, hc_ref, seq):
    tail = pad_ref.shape[0] - (N_META + seq)
    pad_ref[0:N_META, :] = hm_ref[...]
    pad_ref[N_META:N_META + seq, :] = hc_ref[0]
    pad_ref[N_META + seq:, :] = jnp.zeros((tail, CONV_W), F32)


def _conv_chunk(pad_ref, y_ref, w_ref, b_ref, lg_ref, lb_ref, base):
    first = N_META - CONV_PAD
    span = CONV_ROWS + 8
    for lt in range(CONV_W // LANES):
        ls = slice(lt * LANES, (lt + 1) * LANES)
        win = pad_ref[pl.ds(base, CONV_ROWS + 32), ls]
        acc = None
        for sub in range(8):
            part = None
            for al in range(4):
                k = 8 * al + sub - first
                if 0 <= k < CONV_K:
                    term = win[8 * al:8 * al + span] * w_ref[k:k + 1, ls]
                    part = term if part is None else part + term
            if sub:
                part = pltpu.roll(part, span - sub, 0)
            acc = part[:CONV_ROWS] if acc is None else acc + part[:CONV_ROWS]
        y_ref[:, ls] = acc
    y = y_ref[...] + b_ref[...]
    mu = jnp.mean(y, axis=-1, keepdims=True)
    yc = y - mu
    var = jnp.mean(yc * yc, axis=-1, keepdims=True)
    z = yc * lax.rsqrt(var + LN_EPS) * lg_ref[...] + lb_ref[...]
    return (z * jax.nn.sigmoid(z)).astype(BF16)


def _alibi_bias():
    qi = np.arange(BLOCK)[:, None]
    kj = np.arange(BLOCK)[None, :]
    dists = [qi + BLOCK - kj, np.abs(qi - kj), kj + BLOCK - qi]
    out = np.zeros((N_KV_HEADS, 4, GROUP * BLOCK, BLOCK), np.float32)
    for h in range(N_KV_HEADS):
        for g in range(GROUP):
            slope = 2.0 ** (-8.0 * (h * GROUP + g + 1) / N_Q_HEADS)
            for p, d in enumerate(dists):
                out[h, p, g * BLOCK:(g + 1) * BLOCK] = np.where(d <= WINDOW, -slope * d, NEG_INF)
    out[:, 3, :, N_META:] = NEG_INF
    return out


def _attn_kernel(sink_ref, q_ref, kp_ref, kc_ref, kn_ref, vp_ref, vc_ref, vn_ref,
                 km_ref, vm_ref, bias_ref, o_ref, *, n_steps):
    i = pl.program_id(1)
    lo = lax.broadcasted_iota(jnp.int32, (BLOCK, LANES), 1) < HEAD_DIM
    edge_first = jnp.where(i == 0, NEG_INF, 0.0).astype(F32)
    edge_last = jnp.where(i == n_steps - 1, NEG_INF, 0.0).astype(F32)
    nt = (((1,), (1,)), ((), ()))
    zero = jnp.zeros((BLOCK, LANES), BF16)

    def key_block(before_ref, here_ref, after_ref, idx, ks):
        if idx < 0:
            return before_ref[0, :, ks]
        if idx >= Q_BLOCKS:
            return after_ref[0, :, ks]
        return here_ref[0, idx * BLOCK:(idx + 1) * BLOCK, ks]

    for qb in range(Q_BLOCKS):
        qr = slice(qb * BLOCK, (qb + 1) * BLOCK)
        for h in range(N_KV_HEADS):
            ks = slice(h * LANES, (h + 1) * LANES)
            rows = []
            for j in range(2):
                pair = q_ref[0, qr, (2 * h + j) * LANES:(2 * h + j + 1) * LANES]
                rows.append(jnp.where(lo, pair, zero))
                rows.append(jnp.where(lo, zero, pair))
            qs = jnp.concatenate(rows, axis=0)
            k_p, k_c, k_n = (key_block(kp_ref, kc_ref, kn_ref, qb + rel, ks) for rel in (-1, 0, 1))
            v_p, v_c, v_n = (key_block(vp_ref, vc_ref, vn_ref, qb + rel, ks) for rel in (-1, 0, 1))
            s_p = lax.dot_general(qs, k_p, nt, preferred_element_type=F32) + bias_ref[h, 0]
            s_c = lax.dot_general(qs, k_c, nt, preferred_element_type=F32) + bias_ref[h, 1]
            s_n = lax.dot_general(qs, k_n, nt, preferred_element_type=F32) + bias_ref[h, 2]
            s_m = lax.dot_general(qs, km_ref[:, ks], nt, preferred_element_type=F32) + bias_ref[h, 3]
            if qb == 0:
                s_p = s_p + edge_first
            if qb == Q_BLOCKS - 1:
                s_n = s_n + edge_last
            sink = jnp.concatenate(
                [jnp.full((BLOCK, 1), sink_ref[h * GROUP + g], F32) for g in range(GROUP)], axis=0)
            m = jnp.max(jnp.maximum(jnp.maximum(s_p, s_c), jnp.maximum(s_n, s_m)), axis=-1, keepdims=True)
            m = jnp.maximum(m, sink)
            p_p = jnp.exp(s_p - m)
            p_c = jnp.exp(s_c - m)
            p_n = jnp.exp(s_n - m)
            p_m = jnp.exp(s_m - m)
            denom = jnp.sum((p_p + p_c) + (p_n + p_m), axis=-1, keepdims=True) + jnp.exp(sink - m)
            o = (jnp.dot(p_p.astype(BF16), v_p, preferred_element_type=F32)
                 + jnp.dot(p_c.astype(BF16), v_c, preferred_element_type=F32)
                 + jnp.dot(p_n.astype(BF16), v_n, preferred_element_type=F32)
                 + jnp.dot(p_m.astype(BF16), vm_ref[:, ks], preferred_element_type=F32))
            o = o / denom
            for j in range(2):
                even = o[(2 * j) * BLOCK:(2 * j + 1) * BLOCK]
                odd = o[(2 * j + 1) * BLOCK:(2 * j + 2) * BLOCK]
                o_ref[0, qr, (2 * h + j) * LANES:(2 * h + j + 1) * LANES] = jnp.where(lo, even, odd).astype(BF16)


def _attention(sink, q, kd, vd, kmeta, vmeta, bias):
    b, s, _ = q.shape
    nb = s // BLOCK
    steps = nb // Q_BLOCKS
    here = lambda bi, i: (bi, i, 0)
    before = lambda bi, i: (bi, jnp.maximum(Q_BLOCKS * i - 1, 0), 0)
    after = lambda bi, i: (bi, jnp.minimum(Q_BLOCKS * i + Q_BLOCKS, nb - 1), 0)
    const2 = lambda bi, i: (0, 0)
    edge_blk = (1, BLOCK, 2 * KV_W)
    here_blk = (1, Q_BLOCKS * BLOCK, 2 * KV_W)
    return pl.pallas_call(
        functools.partial(_attn_kernel, n_steps=steps),
        grid=(b, steps),
        in_specs=[
            pl.BlockSpec(memory_space=pltpu.SMEM),
            pl.BlockSpec((1, Q_BLOCKS * BLOCK, ATTN_W), here),
            pl.BlockSpec(edge_blk, before), pl.BlockSpec(here_blk, here), pl.BlockSpec(edge_blk, after),
            pl.BlockSpec(edge_blk, before), pl.BlockSpec(here_blk, here), pl.BlockSpec(edge_blk, after),
            pl.BlockSpec((BLOCK, 2 * KV_W), const2),
            pl.BlockSpec((BLOCK, 2 * KV_W), const2),
            pl.BlockSpec(bias.shape, lambda bi, i: (0, 0, 0, 0)),
        ],
        out_specs=pl.BlockSpec((1, Q_BLOCKS * BLOCK, ATTN_W), here),
        out_shape=jax.ShapeDtypeStruct((b, s, ATTN_W), BF16),
        compiler_params=pltpu.CompilerParams(dimension_semantics=("arbitrary", "arbitrary"),
                                             vmem_limit_bytes=VMEM_LIMIT),
        name="attention",
    )(sink, q, kd, kd, kd, vd, vd, vd, kmeta, vmeta, bias)


def _outproj_kernel(attn_ref, hc_ref, hm_ref, cw_ref, cb_ref, lg_ref, lb_ref, x_ref, wa_ref, wc_ref, g2_ref,
                    rwh_ref, rwl_ref, rb_ref, tri_ref, h2_ref, xp_ref, route_ref, cnt_ref,
                    run_ref, pad_ref, y_ref, conv_ref, *, seq):
    step = pl.program_id(0)
    tiles_per_seq = seq // ROW_TILE
    part = step % tiles_per_seq

    @pl.when(step == 0)
    def _():
        run_ref[...] = jnp.zeros_like(run_ref)

    @pl.when(part == 0)
    def _():
        _conv_fill(pad_ref, hm_ref, hc_ref, seq)

    for c in range(ROW_TILE // CONV_ROWS):
        base = pl.multiple_of(part * ROW_TILE + c * CONV_ROWS, CONV_ROWS)
        conv_ref[c * CONV_ROWS:(c + 1) * CONV_ROWS, :] = _conv_chunk(pad_ref, y_ref, cw_ref, cb_ref, lg_ref,
                                                                     lb_ref, base)
    mix = (jnp.dot(attn_ref[...], wa_ref[...], preferred_element_type=F32)
           + jnp.dot(conv_ref[...], wc_ref[...], preferred_element_type=F32))
    h2 = x_ref[...] + mix
    h2_ref[...] = h2
    hn = _rms(h2, RMS_EPS) * g2_ref[...]
    hn_hi = hn.astype(BF16)
    half = D_MODEL // 2
    xp_ref[...] = _pack_pair(hn[:, :half], hn[:, half:])
    hn_lo = (hn - hn_hi.astype(F32)).astype(BF16)
    nt = (((1,), (1,)), ((), ()))
    logits = (lax.dot_general(rwh_ref[...], hn_hi, nt, preferred_element_type=F32)
              + lax.dot_general(rwh_ref[...], hn_lo, nt, preferred_element_type=F32)
              + lax.dot_general(rwl_ref[...], hn_hi, nt, preferred_element_type=F32)) + rb_ref[...]
    expert = lax.broadcasted_iota(jnp.int32, logits.shape, 0).astype(F32)
    work = logits
    vals, sels = [], []
    for r in range(TOP_K):
        m = jnp.max(work, axis=0, keepdims=True)
        idx = jnp.min(jnp.where(work == m, expert, float(N_EXPERTS)), axis=0, keepdims=True)
        sel = expert == idx
        work = jnp.where(sel, -jnp.inf, work)
        route_ref[r:r + 1, :] = idx
        vals.append(m)
        sels.append(sel)
    onehot = jnp.where(jnp.logical_or(jnp.logical_or(sels[0], sels[1]), jnp.logical_or(sels[2], sels[3])),
                       1.0, 0.0)
    exps = [jnp.exp(v - vals[0]) for v in vals]
    tot = exps[0] + exps[1] + exps[2] + exps[3]
    before = jnp.dot(onehot.astype(BF16), tri_ref[...], preferred_element_type=F32) + run_ref[...]
    for r in range(TOP_K):
        route_ref[TOP_K + r:TOP_K + r + 1, :] = exps[r] / tot
        route_ref[2 * TOP_K + r:2 * TOP_K + r + 1, :] = jnp.sum(jnp.where(sels[r], before, 0.0), axis=0,
                                                               keepdims=True)
    route_ref[3 * TOP_K:, :] = jnp.zeros((ROUTE_ROWS - 3 * TOP_K, logits.shape[1]), F32)
    run_ref[...] = run_ref[...] + jnp.sum(onehot, axis=1, keepdims=True)
    cnt_ref[...] = run_ref[...]


def _outproj(attn2d, hc, hc_meta, dw_w, dw_b, ln_g, ln_b, x2d, wa, wc, g2, rwh, rwl, rb, tri):
    n = x2d.shape[0]
    seq = hc.shape[1]
    tile = ROW_TILE
    const = lambda i: (0, 0)
    row = lambda i: (i, 0)
    half = D_MODEL // 2
    return pl.pallas_call(
        functools.partial(_outproj_kernel, seq=seq),
        grid=(n // tile,),
        in_specs=[
            pl.BlockSpec((tile, ATTN_W), row),
            pl.BlockSpec((1, seq, CONV_W), lambda i: (i // (seq // tile), 0, 0)),
            pl.BlockSpec((N_META, CONV_W), const),
            pl.BlockSpec((CONV_K, CONV_W), const),
            pl.BlockSpec((1, CONV_W), const),
            pl.BlockSpec((1, CONV_W), const),
            pl.BlockSpec((1, CONV_W), const),
            pl.BlockSpec((tile, D_MODEL), row),
            pl.BlockSpec((ATTN_W, D_MODEL), const),
            pl.BlockSpec((CONV_W, D_MODEL), const),
            pl.BlockSpec((1, D_MODEL), const),
            pl.BlockSpec((N_EXPERTS, D_MODEL), const),
            pl.BlockSpec((N_EXPERTS, D_MODEL), const),
            pl.BlockSpec((N_EXPERTS, 1), const),
            pl.BlockSpec((tile, tile), const),
        ],
        out_specs=[
            pl.BlockSpec((tile, D_MODEL), row),
            pl.BlockSpec((tile, half), row),
            pl.BlockSpec((ROUTE_ROWS, tile), lambda i: (0, i)),
            pl.BlockSpec((N_EXPERTS, 1), const),
        ],
        out_shape=[
            jax.ShapeDtypeStruct((n, D_MODEL), F32),
            jax.ShapeDtypeStruct((n, half), jnp.uint32),
            jax.ShapeDtypeStruct((ROUTE_ROWS, n), F32),
            jax.ShapeDtypeStruct((N_EXPERTS, 1), F32),
        ],
        scratch_shapes=[pltpu.VMEM((N_EXPERTS, 1), F32),
                        pltpu.VMEM((N_META + seq + 32, CONV_W), F32),
                        pltpu.VMEM((CONV_ROWS, CONV_W), F32),
                        pltpu.VMEM((tile, CONV_W), BF16)],
        compiler_params=pltpu.CompilerParams(dimension_semantics=("arbitrary",),
                                             vmem_limit_bytes=VMEM_LIMIT),
        name="outproj",
    )(attn2d, hc, hc_meta, dw_w, dw_b, ln_g, ln_b, x2d, wa, wc, g2, rwh, rwl, rb, tri)


def _sc_mesh():
    return plsc.VectorSubcoreMesh(core_axis_name="c", subcore_axis_name="s",
                                  num_cores=SC_CORES, num_subcores=SC_SUBCORES)


def _sc_worker():
    return lax.axis_index("s") * SC_CORES + lax.axis_index("c")


def _sc_dispatch(pos3, xp, out_rows):
    n, half = xp.shape
    per_worker = n // SC_CHUNK // SC_WORKERS

    @functools.partial(
        pl.kernel, mesh=_sc_mesh(),
        out_type=jax.ShapeDtypeStruct((out_rows, half), jnp.uint32),
        scratch_types=[pltpu.VMEM((2, TOP_K, SC_CHUNK), jnp.int32),
                       pltpu.VMEM((2, SC_CHUNK, half), jnp.uint32),
                       pltpu.SemaphoreType.DMA((2,)),
                       pltpu.SemaphoreType.DMA],
        name="sc_dispatch")
    def run(pos_hbm, xp_hbm, xs_hbm, idx_v, rows_v, load_sem, scatter_sem):
        first = _sc_worker() * per_worker

        def load(j, buf):
            c = first + j
            return (pltpu.async_copy(pos_hbm.at[c], idx_v.at[buf], load_sem.at[buf]),
                    pltpu.async_copy(xp_hbm.at[pl.ds(c * SC_CHUNK, SC_CHUNK)], rows_v.at[buf], load_sem.at[buf]))

        loads = load(0, 0)
        for j in range(per_worker):
            buf = j % 2
            for cp in loads:
                cp.wait()
            if j + 1 < per_worker:
                loads = load(j + 1, 1 - buf)
            copies = [pltpu.async_copy(rows_v.at[buf], xs_hbm.at[idx_v.at[buf].at[k]], scatter_sem)
                      for k in range(TOP_K)]
            for cp in copies:
                cp.wait()

    return run(pos3, xp)


def _sc_collect(pos3, y):
    half = y.shape[1]
    n = pos3.shape[0] * SC_CHUNK
    per_worker = n // SC_CHUNK // SC_WORKERS

    @functools.partial(
        pl.kernel, mesh=_sc_mesh(),
        out_type=jax.ShapeDtypeStruct((TOP_K, n, half), jnp.uint32),
        scratch_types=[pltpu.VMEM((TOP_K, SC_CHUNK), jnp.int32),
                       pltpu.VMEM((2, SC_CHUNK, half), jnp.uint32),
                       pltpu.SemaphoreType.DMA((2,)),
                       pltpu.SemaphoreType.DMA((2,))],
        name="sc_collect")
    def run(pos_hbm, y_hbm, yg_hbm, idx_v, rows_v, gather_sem, store_sem):
        first = _sc_worker() * per_worker

        @pl.loop(0, per_worker)
        def _(j):
            c = first + j
            pltpu.sync_copy(pos_hbm.at[c], idx_v)

            def gather(k):
                return pltpu.async_copy(y_hbm.at[idx_v.at[k]], rows_v.at[k % 2], gather_sem.at[k % 2])

            gathers = [gather(0)]
            stores = []
            for k in range(TOP_K):
                if k + 1 < TOP_K:
                    if k >= 1:
                        stores[k - 1].wait()
                    gathers.append(gather(k + 1))
                gathers[k].wait()
                stores.append(pltpu.async_copy(rows_v.at[k % 2], yg_hbm.at[k, pl.ds(c * SC_CHUNK, SC_CHUNK)],
                                               store_sem.at[k % 2]))
            stores[TOP_K - 2].wait()
            stores[TOP_K - 1].wait()

    return run(pos3, y)


S_TILE, S_EXPERT, S_ROWS, S_NEW, S_SLOT, S_NEXT = range(6)


def _expert_kernel(sched_ref, xs_ref, wgu_hbm, bgu_ref, wd_hbm, bd_ref, y_ref,
                   act_ref, wgu_buf, wd_buf, sem):
    w = pl.program_id(0)
    e = sched_ref[S_EXPERT, w]
    slot = sched_ref[S_SLOT, w]
    n_rows = sched_ref[S_ROWS, w]

    def weight_copies(expert, s):
        return (pltpu.make_async_copy(wgu_hbm.at[expert], wgu_buf.at[s], sem.at[0, s]),
                pltpu.make_async_copy(wd_hbm.at[expert], wd_buf.at[s], sem.at[1, s]))

    @pl.when(sched_ref[S_NEW, w] == 1)
    def _():
        @pl.when(w == 0)
        def _():
            for cp in weight_copies(e, slot):
                cp.start()

        nxt = sched_ref[S_NEXT, w]

        @pl.when(nxt >= 0)
        def _():
            for cp in weight_copies(nxt, 1 - slot):
                cp.start()

        for cp in weight_copies(e, slot):
            cp.wait()

    def tile_path(m):
        half = D_MODEL // 2
        live = lax.broadcasted_iota(jnp.int32, (m, 1), 0) < n_rows
        x_lo, x_hi = _unpack_pair(jnp.where(live, xs_ref[0:m, :], jnp.uint32(0)))
        x = jnp.concatenate([x_lo.astype(BF16), x_hi.astype(BF16)], axis=1)
        for c in range(D_FF // FF_CHUNK):
            gc = slice(c * FF_CHUNK, (c + 1) * FF_CHUNK)
            uc = slice(D_FF + c * FF_CHUNK, D_FF + (c + 1) * FF_CHUNK)
            gate = jnp.dot(x, wgu_buf[slot, :, gc].astype(BF16), preferred_element_type=F32) + bgu_ref[0, :, gc]
            up = jnp.dot(x, wgu_buf[slot, :, uc].astype(BF16), preferred_element_type=F32) + bgu_ref[0, :, uc]
            gate = jnp.minimum(gate, SWIGLU_LIMIT)
            up = jnp.clip(up, -SWIGLU_LIMIT, SWIGLU_LIMIT)
            act_ref[0:m, gc] = (gate * jax.nn.sigmoid(SWIGLU_ALPHA * gate) * (up + 1.0)).astype(BF16)
        for c in range(half // FF_CHUNK):
            lo_c = slice(c * FF_CHUNK, (c + 1) * FF_CHUNK)
            hi_c = slice(half + c * FF_CHUNK, half + (c + 1) * FF_CHUNK)
            y_lo = (jnp.dot(act_ref[0:m, :], wd_buf[slot, :, lo_c].astype(BF16), preferred_element_type=F32)
                    + bd_ref[0, :, lo_c])
            y_hi = (jnp.dot(act_ref[0:m, :], wd_buf[slot, :, hi_c].astype(BF16), preferred_element_type=F32)
                    + bd_ref[0, :, hi_c])
            y_ref[0:m, lo_c] = _pack_pair(y_lo, y_hi)

    for below, m in zip((0,) + TILE_SPANS[:-1], TILE_SPANS):
        pl.when(jnp.logical_and(n_rows > below, n_rows <= m))(functools.partial(tile_path, m))


def _experts(sched, xs, wgu, bgu, wd, bd):
    rows, half = xs.shape
    n_work = sched.shape[1]
    grid_spec = pltpu.PrefetchScalarGridSpec(
        num_scalar_prefetch=1,
        grid=(n_work,),
        in_specs=[
            pl.BlockSpec((EXPERT_TILE, half), lambda w, sc: (sc[S_TILE, w], 0)),
            pl.BlockSpec(memory_space=pl.ANY),
            pl.BlockSpec((1, 1, 2 * D_FF), lambda w, sc: (sc[S_EXPERT, w], 0, 0)),
            pl.BlockSpec(memory_space=pl.ANY),
            pl.BlockSpec((1, 1, D_MODEL), lambda w, sc: (sc[S_EXPERT, w], 0, 0)),
        ],
        out_specs=pl.BlockSpec((EXPERT_TILE, half), lambda w, sc: (sc[S_TILE, w], 0)),
        scratch_shapes=[pltpu.VMEM((EXPERT_TILE, D_FF), BF16),
                        pltpu.VMEM((2, D_MODEL, 2 * D_FF), F32),
                        pltpu.VMEM((2, D_FF, D_MODEL), F32),
                        pltpu.SemaphoreType.DMA((2, 2))],
    )
    return pl.pallas_call(
        _expert_kernel,
        grid_spec=grid_spec,
        out_shape=jax.ShapeDtypeStruct((rows, half), jnp.uint32),
        compiler_params=pltpu.CompilerParams(dimension_semantics=("arbitrary",),
                                             vmem_limit_bytes=VMEM_LIMIT),
        name="experts",
    )(sched, xs, wgu, bgu, wd, bd)


def _unpack_pair(packed):
    lo = lax.bitcast_convert_type(packed << 16, F32)
    hi = lax.bitcast_convert_type(packed & jnp.uint32(0xFFFF0000), F32)
    return lo, hi


def _pack_pair(lo, hi):
    lo_bits = lax.bitcast_convert_type(lo.astype(BF16).astype(F32), jnp.uint32)
    hi_bits = lax.bitcast_convert_type(hi.astype(BF16).astype(F32), jnp.uint32)
    return (lo_bits >> 16) | (hi_bits & jnp.uint32(0xFFFF0000))


def _combine_kernel(yg_ref, h2_ref, gate_ref, o_ref):
    half = D_MODEL // 2
    gates = gate_ref[...]
    lo_sum = h2_ref[:, :half]
    hi_sum = h2_ref[:, half:]
    for k in range(TOP_K):
        gate = gates[:, k:k + 1]
        lo, hi = _unpack_pair(yg_ref[k])
        lo_sum = lo_sum + gate * lo
        hi_sum = hi_sum + gate * hi
    o_ref[:, :half] = lo_sum
    o_ref[:, half:] = hi_sum


def _combine(yg, h2, gates, part):
    n = h2.shape[0]
    half = D_MODEL // 2
    steps = yg.shape[1] // ROW_TILE
    row = lambda i: (i + part * steps, 0)
    return pl.pallas_call(
        _combine_kernel,
        grid=(steps,),
        in_specs=[
            pl.BlockSpec((TOP_K, ROW_TILE, half), lambda i: (0, i, 0)),
            pl.BlockSpec((ROW_TILE, D_MODEL), row),
            pl.BlockSpec((ROW_TILE, TOP_K), row),
        ],
        out_specs=pl.BlockSpec((ROW_TILE, D_MODEL), row),
        out_shape=jax.ShapeDtypeStruct((n, D_MODEL), F32),
        input_output_aliases={1: 0},
        compiler_params=pltpu.CompilerParams(dimension_semantics=("arbitrary",),
                                             vmem_limit_bytes=VMEM_LIMIT),
        name="combine",
    )(yg, h2, gates)


def _work_schedule(counts, n_rows):
    n_work = n_rows // EXPERT_TILE + N_EXPERTS
    tiles = (counts + (EXPERT_TILE - 1)) // EXPERT_TILE
    tile_end = jnp.cumsum(tiles)
    tile_start = tile_end - tiles
    total = tile_end[-1]
    w = jnp.arange(n_work, dtype=jnp.int32)
    wc = jnp.minimum(w, total - 1)
    expert_of = jnp.sum((tile_end[None, :] <= wc[:, None]).astype(jnp.int32), axis=1)
    mine = expert_of[:, None] == jnp.arange(N_EXPERTS, dtype=jnp.int32)[None, :]
    pick = lambda table: jnp.sum(jnp.where(mine, table[None, :], 0), axis=1)
    live = jnp.clip(pick(counts) - (wc - pick(tile_start)) * EXPERT_TILE, 0, EXPERT_TILE)
    live = jnp.where(w < total, live, 0)
    prev_e = jnp.concatenate([jnp.full((1,), -1, jnp.int32), expert_of[:-1]])
    new_e = jnp.logical_and(w < total, expert_of != prev_e)
    slot = (jnp.cumsum(new_e.astype(jnp.int32)) - 1) % 2
    later = jnp.logical_and(new_e[None, :], w[None, :] > w[:, None])
    next_w = jnp.min(jnp.where(later, w[None, :], n_work), axis=1)
    next_e = jnp.sum(jnp.where(next_w[:, None] == w[None, :], expert_of[None, :], 0), axis=1)
    next_e = jnp.where(next_w < n_work, next_e, -1)
    sched = jnp.stack([wc, expert_of, live, new_e.astype(jnp.int32), slot, next_e]).astype(jnp.int32)
    return sched, (tile_start * EXPERT_TILE).astype(jnp.int32)


def kernel(x, meta_tokens, norm1_g, w_in, q_norm_g, k_norm_g, attn_sink, dw_w, dw_b, conv_ln_g,
           conv_ln_b, w_out, norm2_g, router_w, router_b, w_gate_up, b_gate_up, w_down, b_down):
    assert norm1_g.shape[0] == 1, "single-layer trunk: meta-token query rows are not materialised"
    b, s, d = x.shape
    n = b * s
    x2d = x.reshape(n, d)

    scale = HEAD_DIM ** -0.5
    qg = (jnp.tile(q_norm_g[0], N_Q_HEADS) * scale).reshape(1, ATTN_W)
    kg = jnp.tile(k_norm_g[0], N_KV_HEADS).reshape(1, KV_W)
    head_of = np.arange(ATTN_W) // HEAD_DIM
    pq = jnp.asarray((head_of[:, None] == head_of[None, :]) / HEAD_DIM, BF16)
    pk = pq[:KV_W, :KV_W]
    g1 = norm1_g[0].reshape(1, d)
    w_in_b = w_in[0].astype(BF16)

    q, kd, vd, hc = _inproj(x2d, g1, w_in_b, qg, kg, pq, pk, ROW_TILE)
    _, kmeta, vmeta, hc_meta = _inproj(meta_tokens, g1, w_in_b, qg, kg, pq, pk, N_META)

    meta_pad = ((0, BLOCK - N_META), (0, 0))
    attn = _attention(attn_sink[0], q.reshape(b, s, ATTN_W), kd.reshape(b, s, 2 * KV_W),
                      vd.reshape(b, s, 2 * KV_W), jnp.pad(kmeta, meta_pad), jnp.pad(vmeta, meta_pad),
                      jnp.asarray(_alibi_bias()))

    w_out_b = w_out[0].astype(BF16)
    rw = router_w[0].T
    rwh = rw.astype(BF16)
    rwl = (rw - rwh.astype(F32)).astype(BF16)
    rb = router_b[0].reshape(N_EXPERTS, 1)
    tri = jnp.asarray(np.triu(np.ones((ROW_TILE, ROW_TILE), np.float32), 1), BF16)
    h2, xp, route, counts = _outproj(attn.reshape(n, ATTN_W), hc.reshape(b, s, CONV_W), hc_meta,
                                     dw_w[0].reshape(CONV_K, CONV_W), dw_b[0].reshape(1, CONV_W),
                                     conv_ln_g[0].reshape(1, CONV_W), conv_ln_b[0].reshape(1, CONV_W), x2d,
                                     w_out_b[:ATTN_W], w_out_b[ATTN_W:], norm2_g[0].reshape(1, d),
                                     rwh, rwl, rb, tri)

    counts_i = counts[:, 0].astype(jnp.int32)
    idx = route[:TOP_K].astype(jnp.int32)
    rank = route[2 * TOP_K:3 * TOP_K].astype(jnp.int32)
    gates = route[TOP_K:2 * TOP_K].T
    sched, first_row = _work_schedule(counts_i, n * TOP_K)
    chosen = idx[:, :, None] == jnp.arange(N_EXPERTS, dtype=jnp.int32)
    pos = rank + jnp.sum(jnp.where(chosen, first_row, 0), axis=-1)
    pos3 = pos.reshape(TOP_K, n // SC_CHUNK, SC_CHUNK).transpose(1, 0, 2)

    xs = _sc_dispatch(pos3, xp, sched.shape[1] * EXPERT_TILE)
    y = _experts(sched, xs, w_gate_up[0],
                 b_gate_up[0].reshape(N_EXPERTS, 1, 2 * D_FF), w_down[0],
                 b_down[0].reshape(N_EXPERTS, 1, D_MODEL))
    out = h2
    chunks = pos3.shape[0] // COMBINE_PARTS
    for part in range(COMBINE_PARTS):
        out = _combine(_sc_collect(pos3[part * chunks:(part + 1) * chunks], y), out, gates, part)
    return out.reshape(b, s, d)
```

```python
import functools

import numpy as np
import jax
import jax.numpy as jnp
from jax import lax
from jax.experimental import pallas as pl
from jax.experimental.pallas import tpu as pltpu
from jax.experimental.pallas import tpu_sc as plsc

F32 = jnp.float32
BF16 = jnp.bfloat16

D_MODEL = 1024
N_META = 16
HEAD_DIM = 64
N_Q_HEADS = 8
N_KV_HEADS = 2
GROUP = N_Q_HEADS // N_KV_HEADS
ATTN_W = N_Q_HEADS * HEAD_DIM
KV_W = N_KV_HEADS * HEAD_DIM
CONV_W = D_MODEL - ATTN_W
IN_W = ATTN_W + 2 * KV_W + 2 * CONV_W
WINDOW = 128
BLOCK = 128
CONV_K = 31
CONV_PAD = CONV_K // 2
N_EXPERTS = 32
TOP_K = 4
D_FF = D_MODEL
SWIGLU_LIMIT = 7.0
SWIGLU_ALPHA = 1.702
RMS_EPS = 1e-6
LN_EPS = 1e-5
NEG_INF = -1e30

LANES = 128
ROW_TILE = 512
EXPERT_TILE = 1024
SC_CORES = 2
SC_SUBCORES = 16
SC_WORKERS = SC_CORES * SC_SUBCORES
SC_CHUNK = 64
CONV_ROWS = 128
FF_CHUNK = 512
TILE_SPANS = (256, 512, 768, 1024)
Q_BLOCKS = 8
COMBINE_PARTS = 4
ROUTE_ROWS = 16
VMEM_LIMIT = 56 * 1024 * 1024


def _rms(x, eps):
    return x * lax.rsqrt(jnp.mean(x * x, axis=-1, keepdims=True) + eps)


def _inproj_kernel(x_ref, g1_ref, w_ref, qg_ref, kg_ref, pq_ref, pk_ref,
                   q_ref, k_ref, v_ref, hc_ref):
    x = x_ref[...]
    n = (_rms(x, RMS_EPS) * g1_ref[...]).astype(BF16)
    proj = jnp.dot(n, w_ref[...], preferred_element_type=F32)
    q = proj[:, :ATTN_W]
    k = proj[:, ATTN_W:ATTN_W + KV_W]
    v = proj[:, ATTN_W + KV_W:ATTN_W + 2 * KV_W]
    a = proj[:, ATTN_W + 2 * KV_W:ATTN_W + 2 * KV_W + CONV_W]
    g = proj[:, ATTN_W + 2 * KV_W + CONV_W:]
    qms = jnp.dot((q * q).astype(BF16), pq_ref[...], preferred_element_type=F32)
    kms = jnp.dot((k * k).astype(BF16), pk_ref[...], preferred_element_type=F32)
    q_ref[...] = (q * lax.rsqrt(qms + RMS_EPS) * qg_ref[...]).astype(BF16)
    kn = k * lax.rsqrt(kms + RMS_EPS) * kg_ref[...]
    lo = lax.broadcasted_iota(jnp.int32, kn.shape, 1) < HEAD_DIM
    ksw = pltpu.roll(kn, HEAD_DIM, 1)
    k_ref[...] = jnp.concatenate([jnp.where(lo, kn, ksw), jnp.where(lo, ksw, kn)], axis=1).astype(BF16)
    vsw = pltpu.roll(v, HEAD_DIM, 1)
    v_ref[...] = jnp.concatenate([jnp.where(lo, v, vsw), jnp.where(lo, vsw, v)], axis=1).astype(BF16)
    hc_ref[...] = a * jax.nn.sigmoid(g)


def _inproj(x2d, g1, w_in_b, qg, kg, pq, pk, tile):
    n = x2d.shape[0]
    const = lambda i: (0, 0)
    row = lambda i: (i, 0)
    return pl.pallas_call(
        _inproj_kernel,
        grid=(n // tile,),
        in_specs=[
            pl.BlockSpec((tile, D_MODEL), row),
            pl.BlockSpec((1, D_MODEL), const),
            pl.BlockSpec((D_MODEL, IN_W), const),
            pl.BlockSpec((1, ATTN_W), const),
            pl.BlockSpec((1, KV_W), const),
            pl.BlockSpec((ATTN_W, ATTN_W), const),
            pl.BlockSpec((KV_W, KV_W), const),
        ],
        out_specs=[
            pl.BlockSpec((tile, ATTN_W), row),
            pl.BlockSpec((tile, 2 * KV_W), row),
            pl.BlockSpec((tile, 2 * KV_W), row),
            pl.BlockSpec((tile, CONV_W), row),
        ],
        out_shape=[
            jax.ShapeDtypeStruct((n, ATTN_W), BF16),
            jax.ShapeDtypeStruct((n, 2 * KV_W), BF16),
            jax.ShapeDtypeStruct((n, 2 * KV_W), BF16),
            jax.ShapeDtypeStruct((n, CONV_W), F32),
        ],
        compiler_params=pltpu.CompilerParams(dimension_semantics=("arbitrary",),
                                             vmem_limit_bytes=VMEM_LIMIT),
        name="inproj",
    )(x2d, g1, w_in_b, qg, kg, pq, pk)


def _conv_fill(pad_ref, hm_ref, hc_ref, seq):
    tail = pad_ref.shape[0] - (N_META + seq)
    pad_ref[0:N_META, :] = hm_ref[...]
    pad_ref[N_META:N_META + seq, :] = hc_ref[0]
    pad_ref[N_META + seq:, :] = jnp.zeros((tail, CONV_W), F32)


def _conv_chunk(pad_ref, y_ref, w_ref, b_ref, lg_ref, lb_ref, base):
    first = N_META - CONV_PAD
    span = CONV_ROWS + 8
    for lt in range(CONV_W // LANES):
        ls = slice(lt * LANES, (lt + 1) * LANES)
        win = pad_ref[pl.ds(base, CONV_ROWS + 32), ls]
        acc = None
        for sub in range(8):
            part = None
            for al in range(4):
                k = 8 * al + sub - first
                if 0 <= k < CONV_K:
                    term = win[8 * al:8 * al + span] * w_ref[k:k + 1, ls]
                    part = term if part is None else part + term
            if sub:
                part = pltpu.roll(part, span - sub, 0)
            acc = part[:CONV_ROWS] if acc is None else acc + part[:CONV_ROWS]
        y_ref[:, ls] = acc
    y = y_ref[...] + b_ref[...]
    mu = jnp.mean(y, axis=-1, keepdims=True)
    yc = y - mu
    var = jnp.mean(yc * yc, axis=-1, keepdims=True)
    z = yc * lax.rsqrt(var + LN_EPS) * lg_ref[...] + lb_ref[...]
    return (z * jax.nn.sigmoid(z)).astype(BF16)


def _alibi_bias():
    qi = np.arange(BLOCK)[:, None]
    kj = np.arange(BLOCK)[None, :]
    dists = [qi + BLOCK - kj, np.abs(qi - kj), kj + BLOCK - qi]
    out = np.zeros((N_KV_HEADS, 4, GROUP * BLOCK, BLOCK), np.float32)
    for h in range(N_KV_HEADS):
        for g in range(GROUP):
            slope = 2.0 ** (-8.0 * (h * GROUP + g + 1) / N_Q_HEADS)
            for p, d in enumerate(dists):
                out[h, p, g * BLOCK:(g + 1) * BLOCK] = np.where(d <= WINDOW, -slope * d, NEG_INF)
    out[:, 3, :, N_META:] = NEG_INF
    return out


def _attn_kernel(sink_ref, q_ref, kp_ref, kc_ref, kn_ref, vp_ref, vc_ref, vn_ref,
                 km_ref, vm_ref, bias_ref, o_ref, *, n_steps):
    i = pl.program_id(1)
    lo = lax.broadcasted_iota(jnp.int32, (BLOCK, LANES), 1) < HEAD_DIM
    edge_first = jnp.where(i == 0, NEG_INF, 0.0).astype(F32)
    edge_last = jnp.where(i == n_steps - 1, NEG_INF, 0.0).astype(F32)
    nt = (((1,), (1,)), ((), ()))
    zero = jnp.zeros((BLOCK, LANES), BF16)

    def key_block(before_ref, here_ref, after_ref, idx, ks):
        if idx < 0:
            return before_ref[0, :, ks]
        if idx >= Q_BLOCKS:
            return after_ref[0, :, ks]
        return here_ref[0, idx * BLOCK:(idx + 1) * BLOCK, ks]

    for qb in range(Q_BLOCKS):
        qr = slice(qb * BLOCK, (qb + 1) * BLOCK)
        for h in range(N_KV_HEADS):
            ks = slice(h * LANES, (h + 1) * LANES)
            rows = []
            for j in range(2):
                pair = q_ref[0, qr, (2 * h + j) * LANES:(2 * h + j + 1) * LANES]
                rows.append(jnp.where(lo, pair, zero))
                rows.append(jnp.where(lo, zero, pair))
            qs = jnp.concatenate(rows, axis=0)
            k_p, k_c, k_n = (key_block(kp_ref, kc_ref, kn_ref, qb + rel, ks) for rel in (-1, 0, 1))
            v_p, v_c, v_n = (key_block(vp_ref, vc_ref, vn_ref, qb + rel, ks) for rel in (-1, 0, 1))
            s_p = lax.dot_general(qs, k_p, nt, preferred_element_type=F32) + bias_ref[h, 0]
            s_c = lax.dot_general(qs, k_c, nt, preferred_element_type=F32) + bias_ref[h, 1]
            s_n = lax.dot_general(qs, k_n, nt, preferred_element_type=F32) + bias_ref[h, 2]
            s_m = lax.dot_general(qs, km_ref[:, ks], nt, preferred_element_type=F32) + bias_ref[h, 3]
            if qb == 0:
                s_p = s_p + edge_first
            if qb == Q_BLOCKS - 1:
                s_n = s_n + edge_last
            sink = jnp.concatenate(
                [jnp.full((BLOCK, 1), sink_ref[h * GROUP + g], F32) for g in range(GROUP)], axis=0)
            m = jnp.max(jnp.maximum(jnp.maximum(s_p, s_c), jnp.maximum(s_n, s_m)), axis=-1, keepdims=True)
            m = jnp.maximum(m, sink)
            p_p = jnp.exp(s_p - m)
            p_c = jnp.exp(s_c - m)
            p_n = jnp.exp(s_n - m)
            p_m = jnp.exp(s_m - m)
            denom = jnp.sum((p_p + p_c) + (p_n + p_m), axis=-1, keepdims=True) + jnp.exp(sink - m)
            o = (jnp.dot(p_p.astype(BF16), v_p, preferred_element_type=F32)
                 + jnp.dot(p_c.astype(BF16), v_c, preferred_element_type=F32)
                 + jnp.dot(p_n.astype(BF16), v_n, preferred_element_type=F32)
                 + jnp.dot(p_m.astype(BF16), vm_ref[:, ks], preferred_element_type=F32))
            o = o / denom
            for j in range(2):
                even = o[(2 * j) * BLOCK:(2 * j + 1) * BLOCK]
                odd = o[(2 * j + 1) * BLOCK:(2 * j + 2) * BLOCK]
                o_ref[0, qr, (2 * h + j) * LANES:(2 * h + j + 1) * LANES] = jnp.where(lo, even, odd).astype(BF16)


def _attention(sink, q, kd, vd, kmeta, vmeta, bias):
    b, s, _ = q.shape
    nb = s // BLOCK
    steps = nb // Q_BLOCKS
    here = lambda bi, i: (bi, i, 0)
    before = lambda bi, i: (bi, jnp.maximum(Q_BLOCKS * i - 1, 0), 0)
    after = lambda bi, i: (bi, jnp.minimum(Q_BLOCKS * i + Q_BLOCKS, nb - 1), 0)
    const2 = lambda bi, i: (0, 0)
    edge_blk = (1, BLOCK, 2 * KV_W)
    here_blk = (1, Q_BLOCKS * BLOCK, 2 * KV_W)
    return pl.pallas_call(
        functools.partial(_attn_kernel, n_steps=steps),
        grid=(b, steps),
        in_specs=[
            pl.BlockSpec(memory_space=pltpu.SMEM),
            pl.BlockSpec((1, Q_BLOCKS * BLOCK, ATTN_W), here),
            pl.BlockSpec(edge_blk, before), pl.BlockSpec(here_blk, here), pl.BlockSpec(edge_blk, after),
            pl.BlockSpec(edge_blk, before), pl.BlockSpec(here_blk, here), pl.BlockSpec(edge_blk, after),
            pl.BlockSpec((BLOCK, 2 * KV_W), const2),
            pl.BlockSpec((BLOCK, 2 * KV_W), const2),
            pl.BlockSpec(bias.shape, lambda bi, i: (0, 0, 0, 0)),
        ],
        out_specs=pl.BlockSpec((1, Q_BLOCKS * BLOCK, ATTN_W), here),
        out_shape=jax.ShapeDtypeStruct((b, s, ATTN_W), BF16),
        compiler_params=pltpu.CompilerParams(dimension_semantics=("arbitrary", "arbitrary"),
                                             vmem_limit_bytes=VMEM_LIMIT),
        name="attention",
    )(sink, q, kd, kd, kd, vd, vd, vd, kmeta, vmeta, bias)


def _outproj_kernel(attn_ref, hc_ref, hm_ref, cw_ref, cb_ref, lg_ref, lb_ref, x_ref, wa_ref, wc_ref, g2_ref,
                    rwh_ref, rwl_ref, rb_ref, tri_ref, h2_ref, xp_ref, route_ref, cnt_ref,
                    run_ref, pad_ref, y_ref, conv_ref, *, seq):
    step = pl.program_id(0)
    tiles_per_seq = seq // ROW_TILE
    part = step % tiles_per_seq

    @pl.when(step == 0)
    def _():
        run_ref[...] = jnp.zeros_like(run_ref)

    @pl.when(part == 0)
    def _():
        _conv_fill(pad_ref, hm_ref, hc_ref, seq)

    for c in range(ROW_TILE // CONV_ROWS):
        base = pl.multiple_of(part * ROW_TILE + c * CONV_ROWS, CONV_ROWS)
        conv_ref[c * CONV_ROWS:(c + 1) * CONV_ROWS, :] = _conv_chunk(pad_ref, y_ref, cw_ref, cb_ref, lg_ref,
                                                                     lb_ref, base)
    mix = (jnp.dot(attn_ref[...], wa_ref[...], preferred_element_type=F32)
           + jnp.dot(conv_ref[...], wc_ref[...], preferred_element_type=F32))
    h2 = x_ref[...] + mix
    h2_ref[...] = h2
    hn = _rms(h2, RMS_EPS) * g2_ref[...]
    hn_hi = hn.astype(BF16)
    half = D_MODEL // 2
    xp_ref[...] = _pack_pair(hn[:, :half], hn[:, half:])
    hn_lo = (hn - hn_hi.astype(F32)).astype(BF16)
    nt = (((1,), (1,)), ((), ()))
    logits = (lax.dot_general(rwh_ref[...], hn_hi, nt, preferred_element_type=F32)
              + lax.dot_general(rwh_ref[...], hn_lo, nt, preferred_element_type=F32)
              + lax.dot_general(rwl_ref[...], hn_hi, nt, preferred_element_type=F32)) + rb_ref[...]
    expert = lax.broadcasted_iota(jnp.int32, logits.shape, 0).astype(F32)
    work = logits
    vals, sels = [], []
    for r in range(TOP_K):
        m = jnp.max(work, axis=0, keepdims=True)
        idx = jnp.min(jnp.where(work == m, expert, float(N_EXPERTS)), axis=0, keepdims=True)
        sel = expert == idx
        work = jnp.where(sel, -jnp.inf, work)
        route_ref[r:r + 1, :] = idx
        vals.append(m)
        sels.append(sel)
    onehot = jnp.where(jnp.logical_or(jnp.logical_or(sels[0], sels[1]), jnp.logical_or(sels[2], sels[3])),
                       1.0, 0.0)
    exps = [jnp.exp(v - vals[0]) for v in vals]
    tot = exps[0] + exps[1] + exps[2] + exps[3]
    before = jnp.dot(onehot.astype(BF16), tri_ref[...], preferred_element_type=F32) + run_ref[...]
    for r in range(TOP_K):
        route_ref[TOP_K + r:TOP_K + r + 1, :] = exps[r] / tot
        route_ref[2 * TOP_K + r:2 * TOP_K + r + 1, :] = jnp.sum(jnp.where(sels[r], before, 0.0), axis=0,
                                                               keepdims=True)
    route_ref[3 * TOP_K:, :] = jnp.zeros((ROUTE_ROWS - 3 * TOP_K, logits.shape[1]), F32)
    run_ref[...] = run_ref[...] + jnp.sum(onehot, axis=1, keepdims=True)
    cnt_ref[...] = run_ref[...]


def _outproj(attn2d, hc, hc_meta, dw_w, dw_b, ln_g, ln_b, x2d, wa, wc, g2, rwh, rwl, rb, tri):
    n = x2d.shape[0]
    seq = hc.shape[1]
    tile = ROW_TILE
    const = lambda i: (0, 0)
    row = lambda i: (i, 0)
    half = D_MODEL // 2
    return pl.pallas_call(
        functools.partial(_outproj_kernel, seq=seq),
        grid=(n // tile,),
        in_specs=[
            pl.BlockSpec((tile, ATTN_W), row),
            pl.BlockSpec((1, seq, CONV_W), lambda i: (i // (seq // tile), 0, 0)),
            pl.BlockSpec((N_META, CONV_W), const),
            pl.BlockSpec((CONV_K, CONV_W), const),
            pl.BlockSpec((1, CONV_W), const),
            pl.BlockSpec((1, CONV_W), const),
            pl.BlockSpec((1, CONV_W), const),
            pl.BlockSpec((tile, D_MODEL), row),
            pl.BlockSpec((ATTN_W, D_MODEL), const),
            pl.BlockSpec((CONV_W, D_MODEL), const),
            pl.BlockSpec((1, D_MODEL), const),
            pl.BlockSpec((N_EXPERTS, D_MODEL), const),
            pl.BlockSpec((N_EXPERTS, D_MODEL), const),
            pl.BlockSpec((N_EXPERTS, 1), const),
            pl.BlockSpec((tile, tile), const),
        ],
        out_specs=[
            pl.BlockSpec((tile, D_MODEL), row),
            pl.BlockSpec((tile, half), row),
            pl.BlockSpec((ROUTE_ROWS, tile), lambda i: (0, i)),
            pl.BlockSpec((N_EXPERTS, 1), const),
        ],
        out_shape=[
            jax.ShapeDtypeStruct((n, D_MODEL), F32),
            jax.ShapeDtypeStruct((n, half), jnp.uint32),
            jax.ShapeDtypeStruct((ROUTE_ROWS, n), F32),
            jax.ShapeDtypeStruct((N_EXPERTS, 1), F32),
        ],
        scratch_shapes=[pltpu.VMEM((N_EXPERTS, 1), F32),
                        pltpu.VMEM((N_META + seq + 32, CONV_W), F32),
                        pltpu.VMEM((CONV_ROWS, CONV_W), F32),
                        pltpu.VMEM((tile, CONV_W), BF16)],
        compiler_params=pltpu.CompilerParams(dimension_semantics=("arbitrary",),
                                             vmem_limit_bytes=VMEM_LIMIT),
        name="outproj",
    )(attn2d, hc, hc_meta, dw_w, dw_b, ln_g, ln_b, x2d, wa, wc, g2, rwh, rwl, rb, tri)


def _sc_mesh():
    return plsc.VectorSubcoreMesh(core_axis_name="c", subcore_axis_name="s",
                                  num_cores=SC_CORES, num_subcores=SC_SUBCORES)


def _sc_worker():
    return lax.axis_index("s") * SC_CORES + lax.axis_index("c")


def _sc_dispatch(pos3, xp, out_rows):
    n, half = xp.shape
    per_worker = n // SC_CHUNK // SC_WORKERS

    @functools.partial(
        pl.kernel, mesh=_sc_mesh(),
        out_type=jax.ShapeDtypeStruct((out_rows, half), jnp.uint32),
        scratch_types=[pltpu.VMEM((2, TOP_K, SC_CHUNK), jnp.int32),
                       pltpu.VMEM((2, SC_CHUNK, half), jnp.uint32),
                       pltpu.SemaphoreType.DMA((2,)),
                       pltpu.SemaphoreType.DMA],
        name="sc_dispatch")
    def run(pos_hbm, xp_hbm, xs_hbm, idx_v, rows_v, load_sem, scatter_sem):
        first = _sc_worker() * per_worker

        def load(j, buf):
            c = first + j
            return (pltpu.async_copy(pos_hbm.at[c], idx_v.at[buf], load_sem.at[buf]),
                    pltpu.async_copy(xp_hbm.at[pl.ds(c * SC_CHUNK, SC_CHUNK)], rows_v.at[buf], load_sem.at[buf]))

        loads = load(0, 0)
        for j in range(per_worker):
            buf = j % 2
            for cp in loads:
                cp.wait()
            if j + 1 < per_worker:
                loads = load(j + 1, 1 - buf)
            copies = [pltpu.async_copy(rows_v.at[buf], xs_hbm.at[idx_v.at[buf].at[k]], scatter_sem)
                      for k in range(TOP_K)]
            for cp in copies:
                cp.wait()

    return run(pos3, xp)


def _sc_collect(pos3, y):
    half = y.shape[1]
    n = pos3.shape[0] * SC_CHUNK
    per_worker = n // SC_CHUNK // SC_WORKERS

    @functools.partial(
        pl.kernel, mesh=_sc_mesh(),
        out_type=jax.ShapeDtypeStruct((TOP_K, n, half), jnp.uint32),
        scratch_types=[pltpu.VMEM((TOP_K, SC_CHUNK), jnp.int32),
                       pltpu.VMEM((2, SC_CHUNK, half), jnp.uint32),
                       pltpu.SemaphoreType.DMA((2,)),
                       pltpu.SemaphoreType.DMA((2,))],
        name="sc_collect")
    def run(pos_hbm, y_hbm, yg_hbm, idx_v, rows_v, gather_sem, store_sem):
        first = _sc_worker() * per_worker

        @pl.loop(0, per_worker)
        def _(j):
            c = first + j
            pltpu.sync_copy(pos_hbm.at[c], idx_v)

            def gather(k):
                return pltpu.async_copy(y_hbm.at[idx_v.at[k]], rows_v.at[k % 2], gather_sem.at[k % 2])

            gathers = [gather(0)]
            stores = []
            for k in range(TOP_K):
                if k + 1 < TOP_K:
                    if k >= 1:
                        stores[k - 1].wait()
                    gathers.append(gather(k + 1))
                gathers[k].wait()
                stores.append(pltpu.async_copy(rows_v.at[k % 2], yg_hbm.at[k, pl.ds(c * SC_CHUNK, SC_CHUNK)],
                                               store_sem.at[k % 2]))
            stores[TOP_K - 2].wait()
            stores[TOP_K - 1].wait()

    return run(pos3, y)


S_TILE, S_EXPERT, S_ROWS, S_NEW, S_SLOT, S_NEXT = range(6)


def _expert_kernel(sched_ref, xs_ref, wgu_hbm, bgu_ref, wd_hbm, bd_ref, y_ref,
                   act_ref, wgu_buf, wd_buf, sem):
    w = pl.program_id(0)
    e = sched_ref[S_EXPERT, w]
    slot = sched_ref[S_SLOT, w]
    n_rows = sched_ref[S_ROWS, w]

    def weight_copies(expert, s):
        return (pltpu.make_async_copy(wgu_hbm.at[expert], wgu_buf.at[s], sem.at[0, s]),
                pltpu.make_async_copy(wd_hbm.at[expert], wd_buf.at[s], sem.at[1, s]))

    @pl.when(sched_ref[S_NEW, w] == 1)
    def _():
        @pl.when(w == 0)
        def _():
            for cp in weight_copies(e, slot):
                cp.start()

        nxt = sched_ref[S_NEXT, w]

        @pl.when(nxt >= 0)
        def _():
            for cp in weight_copies(nxt, 1 - slot):
                cp.start()

        for cp in weight_copies(e, slot):
            cp.wait()

    def tile_path(m):
        half = D_MODEL // 2
        live = lax.broadcasted_iota(jnp.int32, (m, 1), 0) < n_rows
        x_lo, x_hi = _unpack_pair(jnp.where(live, xs_ref[0:m, :], jnp.uint32(0)))
        x = jnp.concatenate([x_lo.astype(BF16), x_hi.astype(BF16)], axis=1)
        for c in range(D_FF // FF_CHUNK):
            gc = slice(c * FF_CHUNK, (c + 1) * FF_CHUNK)
            uc = slice(D_FF + c * FF_CHUNK, D_FF + (c + 1) * FF_CHUNK)
            gate = jnp.dot(x, wgu_buf[slot, :, gc].astype(BF16), preferred_element_type=F32) + bgu_ref[0, :, gc]
            up = jnp.dot(x, wgu_buf[slot, :, uc].astype(BF16), preferred_element_type=F32) + bgu_ref[0, :, uc]
            gate = jnp.minimum(gate, SWIGLU_LIMIT)
            up = jnp.clip(up, -SWIGLU_LIMIT, SWIGLU_LIMIT)
            act_ref[0:m, gc] = (gate * jax.nn.sigmoid(SWIGLU_ALPHA * gate) * (up + 1.0)).astype(BF16)
        for c in range(half // FF_CHUNK):
            lo_c = slice(c * FF_CHUNK, (c + 1) * FF_CHUNK)
            hi_c = slice(half + c * FF_CHUNK, half + (c + 1) * FF_CHUNK)
            y_lo = (jnp.dot(act_ref[0:m, :], wd_buf[slot, :, lo_c].astype(BF16), preferred_element_type=F32)
                    + bd_ref[0, :, lo_c])
            y_hi = (jnp.dot(act_ref[0:m, :], wd_buf[slot, :, hi_c].astype(BF16), preferred_element_type=F32)
                    + bd_ref[0, :, hi_c])
            y_ref[0:m, lo_c] = _pack_pair(y_lo, y_hi)

    for below, m in zip((0,) + TILE_SPANS[:-1], TILE_SPANS):
        pl.when(jnp.logical_and(n_rows > below, n_rows <= m))(functools.partial(tile_path, m))


def _experts(sched, xs, wgu, bgu, wd, bd):
    rows, half = xs.shape
    n_work = sched.shape[1]
    grid_spec = pltpu.PrefetchScalarGridSpec(
        num_scalar_prefetch=1,
        grid=(n_work,),
        in_specs=[
            pl.BlockSpec((EXPERT_TILE, half), lambda w, sc: (sc[S_TILE, w], 0)),
            pl.BlockSpec(memory_space=pl.ANY),
            pl.BlockSpec((1, 1, 2 * D_FF), lambda w, sc: (sc[S_EXPERT, w], 0, 0)),
            pl.BlockSpec(memory_space=pl.ANY),
            pl.BlockSpec((1, 1, D_MODEL), lambda w, sc: (sc[S_EXPERT, w], 0, 0)),
        ],
        out_specs=pl.BlockSpec((EXPERT_TILE, half), lambda w, sc: (sc[S_TILE, w], 0)),
        scratch_shapes=[pltpu.VMEM((EXPERT_TILE, D_FF), BF16),
                        pltpu.VMEM((2, D_MODEL, 2 * D_FF), F32),
                        pltpu.VMEM((2, D_FF, D_MODEL), F32),
                        pltpu.SemaphoreType.DMA((2, 2))],
    )
    return pl.pallas_call(
        _expert_kernel,
        grid_spec=grid_spec,
        out_shape=jax.ShapeDtypeStruct((rows, half), jnp.uint32),
        compiler_params=pltpu.CompilerParams(dimension_semantics=("arbitrary",),
                                             vmem_limit_bytes=VMEM_LIMIT),
        name="experts",
    )(sched, xs, wgu, bgu, wd, bd)


def _unpack_pair(packed):
    lo = lax.bitcast_convert_type(packed << 16, F32)
    hi = lax.bitcast_convert_type(packed & jnp.uint32(0xFFFF0000), F32)
    return lo, hi


def _pack_pair(lo, hi):
    lo_bits = lax.bitcast_convert_type(lo.astype(BF16).astype(F32), jnp.uint32)
    hi_bits = lax.bitcast_convert_type(hi.astype(BF16).astype(F32), jnp.uint32)
    return (lo_bits >> 16) | (hi_bits & jnp.uint32(0xFFFF0000))


def _combine_kernel(yg_ref, h2_ref, gate_ref, o_ref):
    half = D_MODEL // 2
    gates = gate_ref[...]
    lo_sum = h2_ref[:, :half]
    hi_sum = h2_ref[:, half:]
    for k in range(TOP_K):
        gate = gates[:, k:k + 1]
        lo, hi = _unpack_pair(yg_ref[k])
        lo_sum = lo_sum + gate * lo
        hi_sum = hi_sum + gate * hi
    o_ref[:, :half] = lo_sum
    o_ref[:, half:] = hi_sum


def _combine(yg, h2, gates, part):
    n = h2.shape[0]
    half = D_MODEL // 2
    steps = yg.shape[1] // ROW_TILE
    row = lambda i: (i + part * steps, 0)
    return pl.pallas_call(
        _combine_kernel,
        grid=(steps,),
        in_specs=[
            pl.BlockSpec((TOP_K, ROW_TILE, half), lambda i: (0, i, 0)),
            pl.BlockSpec((ROW_TILE, D_MODEL), row),
            pl.BlockSpec((ROW_TILE, TOP_K), row),
        ],
        out_specs=pl.BlockSpec((ROW_TILE, D_MODEL), row),
        out_shape=jax.ShapeDtypeStruct((n, D_MODEL), F32),
        input_output_aliases={1: 0},
        compiler_params=pltpu.CompilerParams(dimension_semantics=("arbitrary",),
                                             vmem_limit_bytes=VMEM_LIMIT),
        name="combine",
    )(yg, h2, gates)


def _work_schedule(counts, n_rows):
    n_work = n_rows // EXPERT_TILE + N_EXPERTS
    tiles = (counts + (EXPERT_TILE - 1)) // EXPERT_TILE
    tile_end = jnp.cumsum(tiles)
    tile_start = tile_end - tiles
    total = tile_end[-1]
    w = jnp.arange(n_work, dtype=jnp.int32)
    wc = jnp.minimum(w, total - 1)
    expert_of = jnp.sum((tile_end[None, :] <= wc[:, None]).astype(jnp.int32), axis=1)
    mine = expert_of[:, None] == jnp.arange(N_EXPERTS, dtype=jnp.int32)[None, :]
    pick = lambda table: jnp.sum(jnp.where(mine, table[None, :], 0), axis=1)
    live = jnp.clip(pick(counts) - (wc - pick(tile_start)) * EXPERT_TILE, 0, EXPERT_TILE)
    live = jnp.where(w < total, live, 0)
    prev_e = jnp.concatenate([jnp.full((1,), -1, jnp.int32), expert_of[:-1]])
    new_e = jnp.logical_and(w < total, expert_of != prev_e)
    slot = (jnp.cumsum(new_e.astype(jnp.int32)) - 1) % 2
    later = jnp.logical_and(new_e[None, :], w[None, :] > w[:, None])
    next_w = jnp.min(jnp.where(later, w[None, :], n_work), axis=1)
    next_e = jnp.sum(jnp.where(next_w[:, None] == w[None, :], expert_of[None, :], 0), axis=1)
    next_e = jnp.where(next_w < n_work, next_e, -1)
    sched = jnp.stack([wc, expert_of, live, new_e.astype(jnp.int32), slot, next_e]).astype(jnp.int32)
    return sched, (tile_start * EXPERT_TILE).astype(jnp.int32)


def kernel(x, meta_tokens, norm1_g, w_in, q_norm_g, k_norm_g, attn_sink, dw_w, dw_b, conv_ln_g,
           conv_ln_b, w_out, norm2_g, router_w, router_b, w_gate_up, b_gate_up, w_down, b_down):
    assert norm1_g.shape[0] == 1, "single-layer trunk: meta-token query rows are not materialised"
    b, s, d = x.shape
    n = b * s
    x2d = x.reshape(n, d)

    scale = HEAD_DIM ** -0.5
    qg = (jnp.tile(q_norm_g[0], N_Q_HEADS) * scale).reshape(1, ATTN_W)
    kg = jnp.tile(k_norm_g[0], N_KV_HEADS).reshape(1, KV_W)
    head_of = np.arange(ATTN_W) // HEAD_DIM
    pq = jnp.asarray((head_of[:, None] == head_of[None, :]) / HEAD_DIM, BF16)
    pk = pq[:KV_W, :KV_W]
    g1 = norm1_g[0].reshape(1, d)
    w_in_b = w_in[0].astype(BF16)

    q, kd, vd, hc = _inproj(x2d, g1, w_in_b, qg, kg, pq, pk, ROW_TILE)
    _, kmeta, vmeta, hc_meta = _inproj(meta_tokens, g1, w_in_b, qg, kg, pq, pk, N_META)

    meta_pad = ((0, BLOCK - N_META), (0, 0))
    attn = _attention(attn_sink[0], q.reshape(b, s, ATTN_W), kd.reshape(b, s, 2 * KV_W),
                      vd.reshape(b, s, 2 * KV_W), jnp.pad(kmeta, meta_pad), jnp.pad(vmeta, meta_pad),
                      jnp.asarray(_alibi_bias()))

    w_out_b = w_out[0].astype(BF16)
    rw = router_w[0].T
    rwh = rw.astype(BF16)
    rwl = (rw - rwh.astype(F32)).astype(BF16)
    rb = router_b[0].reshape(N_EXPERTS, 1)
    tri = jnp.asarray(np.triu(np.ones((ROW_TILE, ROW_TILE), np.float32), 1), BF16)
    h2, xp, route, counts = _outproj(attn.reshape(n, ATTN_W), hc.reshape(b, s, CONV_W), hc_meta,
                                     dw_w[0].reshape(CONV_K, CONV_W), dw_b[0].reshape(1, CONV_W),
                                     conv_ln_g[0].reshape(1, CONV_W), conv_ln_b[0].reshape(1, CONV_W), x2d,
                                     w_out_b[:ATTN_W], w_out_b[ATTN_W:], norm2_g[0].reshape(1, d),
                                     rwh, rwl, rb, tri)

    counts_i = counts[:, 0].astype(jnp.int32)
    idx = route[:TOP_K].astype(jnp.int32)
    rank = route[2 * TOP_K:3 * TOP_K].astype(jnp.int32)
    gates = route[TOP_K:2 * TOP_K].T
    sched, first_row = _work_schedule(counts_i, n * TOP_K)
    chosen = idx[:, :, None] == jnp.arange(N_EXPERTS, dtype=jnp.int32)
    pos = rank + jnp.sum(jnp.where(chosen, first_row, 0), axis=-1)
    pos3 = pos.reshape(TOP_K, n // SC_CHUNK, SC_CHUNK).transpose(1, 0, 2)

    xs = _sc_dispatch(pos3, xp, sched.shape[1] * EXPERT_TILE)
    y = _experts(sched, xs, w_gate_up[0],
                 b_gate_up[0].reshape(N_EXPERTS, 1, 2 * D_FF), w_down[0],
                 b_down[0].reshape(N_EXPERTS, 1, D_MODEL))
    out = h2
    chunks = pos3.shape[0] // COMBINE_PARTS
    for part in range(COMBINE_PARTS):
        out = _combine(_sc_collect(pos3[part * chunks:(part + 1) * chunks], y), out, gates, part)
    return out.reshape(b, s, d)
```

```python
import functools

import numpy as np
import jax
import jax.numpy as jnp
from jax import lax
from jax.experimental import pallas as pl
from jax.experimental.pallas import tpu as pltpu
from jax.experimental.pallas import tpu_sc as plsc

F32 = jnp.float32
BF16 = jnp.bfloat16

D_MODEL = 1024
N_META = 16
HEAD_DIM = 64
N_Q_HEADS = 8
N_KV_HEADS = 2
GROUP = N_Q_HEADS // N_KV_HEADS
ATTN_W = N_Q_HEADS * HEAD_DIM
KV_W = N_KV_HEADS * HEAD_DIM
CONV_W = D_MODEL - ATTN_W
IN_W = ATTN_W + 2 * KV_W + 2 * CONV_W
WINDOW = 128
BLOCK = 128
CONV_K = 31
CONV_PAD = CONV_K // 2
N_EXPERTS = 32
TOP_K = 4
D_FF = D_MODEL
SWIGLU_LIMIT = 7.0
SWIGLU_ALPHA = 1.702
RMS_EPS = 1e-6
LN_EPS = 1e-5
NEG_INF = -1e30

LANES = 128
ROW_TILE = 512
EXPERT_TILE = 1024
SC_CORES = 2
SC_SUBCORES = 16
SC_WORKERS = SC_CORES * SC_SUBCORES
SC_CHUNK = 64
CONV_ROWS = 128
FF_CHUNK = 512
TILE_SPANS = (256, 512, 768, 1024)
Q_BLOCKS = 16
COMBINE_PARTS = 4
ROUTE_ROWS = 16
VMEM_LIMIT = 56 * 1024 * 1024


def _rms(x, eps):
    return x * lax.rsqrt(jnp.mean(x * x, axis=-1, keepdims=True) + eps)


def _inproj_kernel(x_ref, g1_ref, w_ref, qg_ref, kg_ref, pq_ref, pk_ref,
                   q_ref, k_ref, v_ref, hc_ref):
    x = x_ref[...]
    n = (_rms(x, RMS_EPS) * g1_ref[...]).astype(BF16)
    proj = jnp.dot(n, w_ref[...], preferred_element_type=F32)
    q = proj[:, :ATTN_W]
    k = proj[:, ATTN_W:ATTN_W + KV_W]
    v = proj[:, ATTN_W + KV_W:ATTN_W + 2 * KV_W]
    a = proj[:, ATTN_W + 2 * KV_W:ATTN_W + 2 * KV_W + CONV_W]
    g = proj[:, ATTN_W + 2 * KV_W + CONV_W:]
    qms = jnp.dot((q * q).astype(BF16), pq_ref[...], preferred_element_type=F32)
    kms = jnp.dot((k * k).astype(BF16), pk_ref[...], preferred_element_type=F32)
    q_ref[...] = (q * lax.rsqrt(qms + RMS_EPS) * qg_ref[...]).astype(BF16)
    kn = k * lax.rsqrt(kms + RMS_EPS) * kg_ref[...]
    lo = lax.broadcasted_iota(jnp.int32, kn.shape, 1) < HEAD_DIM
    ksw = pltpu.roll(kn, HEAD_DIM, 1)
    k_ref[...] = jnp.concatenate([jnp.where(lo, kn, ksw), jnp.where(lo, ksw, kn)], axis=1).astype(BF16)
    vsw = pltpu.roll(v, HEAD_DIM, 1)
    v_ref[...] = jnp.concatenate([jnp.where(lo, v, vsw), jnp.where(lo, vsw, v)], axis=1).astype(BF16)
    hc_ref[...] = a * jax.nn.sigmoid(g)


def _inproj(x2d, g1, w_in_b, qg, kg, pq, pk, tile):
    n = x2d.shape[0]
    const = lambda i: (0, 0)
    row = lambda i: (i, 0)
    return pl.pallas_call(
        _inproj_kernel,
        grid=(n // tile,),
        in_specs=[
            pl.BlockSpec((tile, D_MODEL), row),
            pl.BlockSpec((1, D_MODEL), const),
            pl.BlockSpec((D_MODEL, IN_W), const),
            pl.BlockSpec((1, ATTN_W), const),
            pl.BlockSpec((1, KV_W), const),
            pl.BlockSpec((ATTN_W, ATTN_W), const),
            pl.BlockSpec((KV_W, KV_W), const),
        ],
        out_specs=[
            pl.BlockSpec((tile, ATTN_W), row),
            pl.BlockSpec((tile, 2 * KV_W), row),
            pl.BlockSpec((tile, 2 * KV_W), row),
            pl.BlockSpec((tile, CONV_W), row),
        ],
        out_shape=[
            jax.ShapeDtypeStruct((n, ATTN_W), BF16),
            jax.ShapeDtypeStruct((n, 2 * KV_W), BF16),
            jax.ShapeDtypeStruct((n, 2 * KV_W), BF16),
            jax.ShapeDtypeStruct((n, CONV_W), F32),
        ],
        compiler_params=pltpu.CompilerParams(dimension_semantics=("arbitrary",),
                                             vmem_limit_bytes=VMEM_LIMIT),
        name="inproj",
    )(x2d, g1, w_in_b, qg, kg, pq, pk)


def _conv_fill(pad_ref, hm_ref, hc_ref, seq):
    tail = pad_ref.shape[0] - (N_META + seq)
    pad_ref[0:N_META, :] = hm_ref[...]
    pad_ref[N_META:N_META + seq, :] = hc_ref[0]
    pad_ref[N_META + seq:, :] = jnp.zeros((tail, CONV_W), F32)


def _conv_chunk(pad_ref, y_ref, w_ref, b_ref, lg_ref, lb_ref, base):
    first = N_META - CONV_PAD
    span = CONV_ROWS + 8
    for lt in range(CONV_W // LANES):
        ls = slice(lt * LANES, (lt + 1) * LANES)
        win = pad_ref[pl.ds(base, CONV_ROWS + 32), ls]
        acc = None
        for sub in range(8):
            part = None
            for al in range(4):
                k = 8 * al + sub - first
                if 0 <= k < CONV_K:
                    term = win[8 * al:8 * al + span] * w_ref[k:k + 1, ls]
                    part = term if part is None else part + term
            if sub:
                part = pltpu.roll(part, span - sub, 0)
            acc = part[:CONV_ROWS] if acc is None else acc + part[:CONV_ROWS]
        y_ref[:, ls] = acc
    y = y_ref[...] + b_ref[...]
    mu = jnp.mean(y, axis=-1, keepdims=True)
    yc = y - mu
    var = jnp.mean(yc * yc, axis=-1, keepdims=True)
    z = yc * lax.rsqrt(var + LN_EPS) * lg_ref[...] + lb_ref[...]
    return (z * jax.nn.sigmoid(z)).astype(BF16)


def _alibi_bias():
    qi = np.arange(BLOCK)[:, None]
    kj = np.arange(BLOCK)[None, :]
    dists = [qi + BLOCK - kj, np.abs(qi - kj), kj + BLOCK - qi]
    out = np.zeros((N_KV_HEADS, 4, GROUP * BLOCK, BLOCK), np.float32)
    for h in range(N_KV_HEADS):
        for g in range(GROUP):
            slope = 2.0 ** (-8.0 * (h * GROUP + g + 1) / N_Q_HEADS)
            for p, d in enumerate(dists):
                out[h, p, g * BLOCK:(g + 1) * BLOCK] = np.where(d <= WINDOW, -slope * d, NEG_INF)
    out[:, 3, :, N_META:] = NEG_INF
    return out


def _attn_kernel(sink_ref, q_ref, kp_ref, kc_ref, kn_ref, vp_ref, vc_ref, vn_ref,
                 km_ref, vm_ref, bias_ref, o_ref, *, n_steps):
    i = pl.program_id(1)
    lo = lax.broadcasted_iota(jnp.int32, (BLOCK, LANES), 1) < HEAD_DIM
    edge_first = jnp.where(i == 0, NEG_INF, 0.0).astype(F32)
    edge_last = jnp.where(i == n_steps - 1, NEG_INF, 0.0).astype(F32)
    nt = (((1,), (1,)), ((), ()))
    zero = jnp.zeros((BLOCK, LANES), BF16)

    def key_block(before_ref, here_ref, after_ref, idx, ks):
        if idx < 0:
            return before_ref[0, :, ks]
        if idx >= Q_BLOCKS:
            return after_ref[0, :, ks]
        return here_ref[0, idx * BLOCK:(idx + 1) * BLOCK, ks]

    for qb in range(Q_BLOCKS):
        qr = slice(qb * BLOCK, (qb + 1) * BLOCK)
        for h in range(N_KV_HEADS):
            ks = slice(h * LANES, (h + 1) * LANES)
            rows = []
            for j in range(2):
                pair = q_ref[0, qr, (2 * h + j) * LANES:(2 * h + j + 1) * LANES]
                rows.append(jnp.where(lo, pair, zero))
                rows.append(jnp.where(lo, zero, pair))
            qs = jnp.concatenate(rows, axis=0)
            k_p, k_c, k_n = (key_block(kp_ref, kc_ref, kn_ref, qb + rel, ks) for rel in (-1, 0, 1))
            v_p, v_c, v_n = (key_block(vp_ref, vc_ref, vn_ref, qb + rel, ks) for rel in (-1, 0, 1))
            s_p = lax.dot_general(qs, k_p, nt, preferred_element_type=F32) + bias_ref[h, 0]
            s_c = lax.dot_general(qs, k_c, nt, preferred_element_type=F32) + bias_ref[h, 1]
            s_n = lax.dot_general(qs, k_n, nt, preferred_element_type=F32) + bias_ref[h, 2]
            s_m = lax.dot_general(qs, km_ref[:, ks], nt, preferred_element_type=F32) + bias_ref[h, 3]
            if qb == 0:
                s_p = s_p + edge_first
            if qb == Q_BLOCKS - 1:
                s_n = s_n + edge_last
            sink = jnp.concatenate(
                [jnp.full((BLOCK, 1), sink_ref[h * GROUP + g], F32) for g in range(GROUP)], axis=0)
            m = jnp.max(jnp.maximum(jnp.maximum(s_p, s_c), jnp.maximum(s_n, s_m)), axis=-1, keepdims=True)
            m = jnp.maximum(m, sink)
            p_p = jnp.exp(s_p - m)
            p_c = jnp.exp(s_c - m)
            p_n = jnp.exp(s_n - m)
            p_m = jnp.exp(s_m - m)
            denom = jnp.sum((p_p + p_c) + (p_n + p_m), axis=-1, keepdims=True) + jnp.exp(sink - m)
            o = (jnp.dot(p_p.astype(BF16), v_p, preferred_element_type=F32)
                 + jnp.dot(p_c.astype(BF16), v_c, preferred_element_type=F32)
                 + jnp.dot(p_n.astype(BF16), v_n, preferred_element_type=F32)
                 + jnp.dot(p_m.astype(BF16), vm_ref[:, ks], preferred_element_type=F32))
            o = o / denom
            for j in range(2):
                even = o[(2 * j) * BLOCK:(2 * j + 1) * BLOCK]
                odd = o[(2 * j + 1) * BLOCK:(2 * j + 2) * BLOCK]
                o_ref[0, qr, (2 * h + j) * LANES:(2 * h + j + 1) * LANES] = jnp.where(lo, even, odd).astype(BF16)


def _attention(sink, q, kd, vd, kmeta, vmeta, bias):
    b, s, _ = q.shape
    nb = s // BLOCK
    steps = nb // Q_BLOCKS
    here = lambda bi, i: (bi, i, 0)
    before = lambda bi, i: (bi, jnp.maximum(Q_BLOCKS * i - 1, 0), 0)
    after = lambda bi, i: (bi, jnp.minimum(Q_BLOCKS * i + Q_BLOCKS, nb - 1), 0)
    const2 = lambda bi, i: (0, 0)
    edge_blk = (1, BLOCK, 2 * KV_W)
    here_blk = (1, Q_BLOCKS * BLOCK, 2 * KV_W)
    return pl.pallas_call(
        functools.partial(_attn_kernel, n_steps=steps),
        grid=(b, steps),
        in_specs=[
            pl.BlockSpec(memory_space=pltpu.SMEM),
            pl.BlockSpec((1, Q_BLOCKS * BLOCK, ATTN_W), here),
            pl.BlockSpec(edge_blk, before), pl.BlockSpec(here_blk, here), pl.BlockSpec(edge_blk, after),
            pl.BlockSpec(edge_blk, before), pl.BlockSpec(here_blk, here), pl.BlockSpec(edge_blk, after),
            pl.BlockSpec((BLOCK, 2 * KV_W), const2),
            pl.BlockSpec((BLOCK, 2 * KV_W), const2),
            pl.BlockSpec(bias.shape, lambda bi, i: (0, 0, 0, 0)),
        ],
        out_specs=pl.BlockSpec((1, Q_BLOCKS * BLOCK, ATTN_W), here),
        out_shape=jax.ShapeDtypeStruct((b, s, ATTN_W), BF16),
        compiler_params=pltpu.CompilerParams(dimension_semantics=("arbitrary", "arbitrary"),
                                             vmem_limit_bytes=VMEM_LIMIT),
        name="attention",
    )(sink, q, kd, kd, kd, vd, vd, vd, kmeta, vmeta, bias)


def _outproj_kernel(attn_ref, hc_ref, hm_ref, cw_ref, cb_ref, lg_ref, lb_ref, x_ref, wa_ref, wc_ref, g2_ref,
                    rwh_ref, rwl_ref, rb_ref, tri_ref, h2_ref, xp_ref, route_ref, cnt_ref,
                    run_ref, pad_ref, y_ref, conv_ref, *, seq):
    step = pl.program_id(0)
    tiles_per_seq = seq // ROW_TILE
    part = step % tiles_per_seq

    @pl.when(step == 0)
    def _():
        run_ref[...] = jnp.zeros_like(run_ref)

    @pl.when(part == 0)
    def _():
        _conv_fill(pad_ref, hm_ref, hc_ref, seq)

    for c in range(ROW_TILE // CONV_ROWS):
        base = pl.multiple_of(part * ROW_TILE + c * CONV_ROWS, CONV_ROWS)
        conv_ref[c * CONV_ROWS:(c + 1) * CONV_ROWS, :] = _conv_chunk(pad_ref, y_ref, cw_ref, cb_ref, lg_ref,
                                                                     lb_ref, base)
    mix = (jnp.dot(attn_ref[...], wa_ref[...], preferred_element_type=F32)
           + jnp.dot(conv_ref[...], wc_ref[...], preferred_element_type=F32))
    h2 = x_ref[...] + mix
    h2_ref[...] = h2
    hn = _rms(h2, RMS_EPS) * g2_ref[...]
    hn_hi = hn.astype(BF16)
    half = D_MODEL // 2
    xp_ref[...] = _pack_pair(hn[:, :half], hn[:, half:])
    hn_lo = (hn - hn_hi.astype(F32)).astype(BF16)
    nt = (((1,), (1,)), ((), ()))
    logits = (lax.dot_general(rwh_ref[...], hn_hi, nt, preferred_element_type=F32)
              + lax.dot_general(rwh_ref[...], hn_lo, nt, preferred_element_type=F32)
              + lax.dot_general(rwl_ref[...], hn_hi, nt, preferred_element_type=F32)) + rb_ref[...]
    expert = lax.broadcasted_iota(jnp.int32, logits.shape, 0).astype(F32)
    work = logits
    vals, sels = [], []
    for r in range(TOP_K):
        m = jnp.max(work, axis=0, keepdims=True)
        idx = jnp.min(jnp.where(work == m, expert, float(N_EXPERTS)), axis=0, keepdims=True)
        sel = expert == idx
        work = jnp.where(sel, -jnp.inf, work)
        route_ref[r:r + 1, :] = idx
        vals.append(m)
        sels.append(sel)
    onehot = jnp.where(jnp.logical_or(jnp.logical_or(sels[0], sels[1]), jnp.logical_or(sels[2], sels[3])),
                       1.0, 0.0)
    exps = [jnp.exp(v - vals[0]) for v in vals]
    tot = exps[0] + exps[1] + exps[2] + exps[3]
    before = jnp.dot(onehot.astype(BF16), tri_ref[...], preferred_element_type=F32) + run_ref[...]
    for r in range(TOP_K):
        route_ref[TOP_K + r:TOP_K + r + 1, :] = exps[r] / tot
        route_ref[2 * TOP_K + r:2 * TOP_K + r + 1, :] = jnp.sum(jnp.where(sels[r], before, 0.0), axis=0,
                                                               keepdims=True)
    route_ref[3 * TOP_K:, :] = jnp.zeros((ROUTE_ROWS - 3 * TOP_K, logits.shape[1]), F32)
    run_ref[...] = run_ref[...] + jnp.sum(onehot, axis=1, keepdims=True)
    cnt_ref[...] = run_ref[...]


def _outproj(attn2d, hc, hc_meta, dw_w, dw_b, ln_g, ln_b, x2d, wa, wc, g2, rwh, rwl, rb, tri):
    n = x2d.shape[0]
    seq = hc.shape[1]
    tile = ROW_TILE
    const = lambda i: (0, 0)
    row = lambda i: (i, 0)
    half = D_MODEL // 2
    return pl.pallas_call(
        functools.partial(_outproj_kernel, seq=seq),
        grid=(n // tile,),
        in_specs=[
            pl.BlockSpec((tile, ATTN_W), row),
            pl.BlockSpec((1, seq, CONV_W), lambda i: (i // (seq // tile), 0, 0)),
            pl.BlockSpec((N_META, CONV_W), const),
            pl.BlockSpec((CONV_K, CONV_W), const),
            pl.BlockSpec((1, CONV_W), const),
            pl.BlockSpec((1, CONV_W), const),
            pl.BlockSpec((1, CONV_W), const),
            pl.BlockSpec((tile, D_MODEL), row),
            pl.BlockSpec((ATTN_W, D_MODEL), const),
            pl.BlockSpec((CONV_W, D_MODEL), const),
            pl.BlockSpec((1, D_MODEL), const),
            pl.BlockSpec((N_EXPERTS, D_MODEL), const),
            pl.BlockSpec((N_EXPERTS, D_MODEL), const),
            pl.BlockSpec((N_EXPERTS, 1), const),
            pl.BlockSpec((tile, tile), const),
        ],
        out_specs=[
            pl.BlockSpec((tile, D_MODEL), row),
            pl.BlockSpec((tile, half), row),
            pl.BlockSpec((ROUTE_ROWS, tile), lambda i: (0, i)),
            pl.BlockSpec((N_EXPERTS, 1), const),
        ],
        out_shape=[
            jax.ShapeDtypeStruct((n, D_MODEL), F32),
            jax.ShapeDtypeStruct((n, half), jnp.uint32),
            jax.ShapeDtypeStruct((ROUTE_ROWS, n), F32),
            jax.ShapeDtypeStruct((N_EXPERTS, 1), F32),
        ],
        scratch_shapes=[pltpu.VMEM((N_EXPERTS, 1), F32),
                        pltpu.VMEM((N_META + seq + 32, CONV_W), F32),
                        pltpu.VMEM((CONV_ROWS, CONV_W), F32),
                        pltpu.VMEM((tile, CONV_W), BF16)],
        compiler_params=pltpu.CompilerParams(dimension_semantics=("arbitrary",),
                                             vmem_limit_bytes=VMEM_LIMIT),
        name="outproj",
    )(attn2d, hc, hc_meta, dw_w, dw_b, ln_g, ln_b, x2d, wa, wc, g2, rwh, rwl, rb, tri)


def _sc_mesh():
    return plsc.VectorSubcoreMesh(core_axis_name="c", subcore_axis_name="s",
                                  num_cores=SC_CORES, num_subcores=SC_SUBCORES)


def _sc_worker():
    return lax.axis_index("s") * SC_CORES + lax.axis_index("c")


def _sc_dispatch(pos3, xp, out_rows):
    n, half = xp.shape
    per_worker = n // SC_CHUNK // SC_WORKERS

    @functools.partial(
        pl.kernel, mesh=_sc_mesh(),
        out_type=jax.ShapeDtypeStruct((out_rows, half), jnp.uint32),
        scratch_types=[pltpu.VMEM((2, TOP_K, SC_CHUNK), jnp.int32),
                       pltpu.VMEM((2, SC_CHUNK, half), jnp.uint32),
                       pltpu.SemaphoreType.DMA((2,)),
                       pltpu.SemaphoreType.DMA],
        name="sc_dispatch")
    def run(pos_hbm, xp_hbm, xs_hbm, idx_v, rows_v, load_sem, scatter_sem):
        first = _sc_worker() * per_worker

        def load(j, buf):
            c = first + j
            return (pltpu.async_copy(pos_hbm.at[c], idx_v.at[buf], load_sem.at[buf]),
                    pltpu.async_copy(xp_hbm.at[pl.ds(c * SC_CHUNK, SC_CHUNK)], rows_v.at[buf], load_sem.at[buf]))

        loads = load(0, 0)
        for j in range(per_worker):
            buf = j % 2
            for cp in loads:
                cp.wait()
            if j + 1 < per_worker:
                loads = load(j + 1, 1 - buf)
            copies = [pltpu.async_copy(rows_v.at[buf], xs_hbm.at[idx_v.at[buf].at[k]], scatter_sem)
                      for k in range(TOP_K)]
            for cp in copies:
                cp.wait()

    return run(pos3, xp)


def _sc_collect(pos3, y):
    half = y.shape[1]
    n = pos3.shape[0] * SC_CHUNK
    per_worker = n // SC_CHUNK // SC_WORKERS

    @functools.partial(
        pl.kernel, mesh=_sc_mesh(),
        out_type=jax.ShapeDtypeStruct((TOP_K, n, half), jnp.uint32),
        scratch_types=[pltpu.VMEM((TOP_K, SC_CHUNK), jnp.int32),
                       pltpu.VMEM((2, SC_CHUNK, half), jnp.uint32),
                       pltpu.SemaphoreType.DMA((2,)),
                       pltpu.SemaphoreType.DMA((2,))],
        name="sc_collect")
    def run(pos_hbm, y_hbm, yg_hbm, idx_v, rows_v, gather_sem, store_sem):
        first = _sc_worker() * per_worker

        @pl.loop(0, per_worker)
        def _(j):
            c = first + j
            pltpu.sync_copy(pos_hbm.at[c], idx_v)

            def gather(k):
                return pltpu.async_copy(y_hbm.at[idx_v.at[k]], rows_v.at[k % 2], gather_sem.at[k % 2])

            gathers = [gather(0)]
            stores = []
            for k in range(TOP_K):
                if k + 1 < TOP_K:
                    if k >= 1:
                        stores[k - 1].wait()
                    gathers.append(gather(k + 1))
                gathers[k].wait()
                stores.append(pltpu.async_copy(rows_v.at[k % 2], yg_hbm.at[k, pl.ds(c * SC_CHUNK, SC_CHUNK)],
                                               store_sem.at[k % 2]))
            stores[TOP_K - 2].wait()
            stores[TOP_K - 1].wait()

    return run(pos3, y)


S_TILE, S_EXPERT, S_ROWS, S_NEW, S_SLOT, S_NEXT = range(6)


def _expert_kernel(sched_ref, xs_ref, wgu_hbm, bgu_ref, wd_hbm, bd_ref, y_ref,
                   act_ref, wgu_buf, wd_buf, sem):
    w = pl.program_id(0)
    e = sched_ref[S_EXPERT, w]
    slot = sched_ref[S_SLOT, w]
    n_rows = sched_ref[S_ROWS, w]

    def weight_copies(expert, s):
        return (pltpu.make_async_copy(wgu_hbm.at[expert], wgu_buf.at[s], sem.at[0, s]),
                pltpu.make_async_copy(wd_hbm.at[expert], wd_buf.at[s], sem.at[1, s]))

    @pl.when(sched_ref[S_NEW, w] == 1)
    def _():
        @pl.when(w == 0)
        def _():
            for cp in weight_copies(e, slot):
                cp.start()

        nxt = sched_ref[S_NEXT, w]

        @pl.when(nxt >= 0)
        def _():
            for cp in weight_copies(nxt, 1 - slot):
                cp.start()

        for cp in weight_copies(e, slot):
            cp.wait()

    def tile_path(m):
        half = D_MODEL // 2
        live = lax.broadcasted_iota(jnp.int32, (m, 1), 0) < n_rows
        x_lo, x_hi = _unpack_pair(jnp.where(live, xs_ref[0:m, :], jnp.uint32(0)))
        x = jnp.concatenate([x_lo.astype(BF16), x_hi.astype(BF16)], axis=1)
        for c in range(D_FF // FF_CHUNK):
            gc = slice(c * FF_CHUNK, (c + 1) * FF_CHUNK)
            uc = slice(D_FF + c * FF_CHUNK, D_FF + (c + 1) * FF_CHUNK)
            gate = jnp.dot(x, wgu_buf[slot, :, gc].astype(BF16), preferred_element_type=F32) + bgu_ref[0, :, gc]
            up = jnp.dot(x, wgu_buf[slot, :, uc].astype(BF16), preferred_element_type=F32) + bgu_ref[0, :, uc]
            gate = jnp.minimum(gate, SWIGLU_LIMIT)
            up = jnp.clip(up, -SWIGLU_LIMIT, SWIGLU_LIMIT)
            act_ref[0:m, gc] = (gate * jax.nn.sigmoid(SWIGLU_ALPHA * gate) * (up + 1.0)).astype(BF16)
        for c in range(half // FF_CHUNK):
            lo_c = slice(c * FF_CHUNK, (c + 1) * FF_CHUNK)
            hi_c = slice(half + c * FF_CHUNK, half + (c + 1) * FF_CHUNK)
            y_lo = (jnp.dot(act_ref[0:m, :], wd_buf[slot, :, lo_c].astype(BF16), preferred_element_type=F32)
                    + bd_ref[0, :, lo_c])
            y_hi = (jnp.dot(act_ref[0:m, :], wd_buf[slot, :, hi_c].astype(BF16), preferred_element_type=F32)
                    + bd_ref[0, :, hi_c])
            y_ref[0:m, lo_c] = _pack_pair(y_lo, y_hi)

    for below, m in zip((0,) + TILE_SPANS[:-1], TILE_SPANS):
        pl.when(jnp.logical_and(n_rows > below, n_rows <= m))(functools.partial(tile_path, m))


def _experts(sched, xs, wgu, bgu, wd, bd):
    rows, half = xs.shape
    n_work = sched.shape[1]
    grid_spec = pltpu.PrefetchScalarGridSpec(
        num_scalar_prefetch=1,
        grid=(n_work,),
        in_specs=[
            pl.BlockSpec((EXPERT_TILE, half), lambda w, sc: (sc[S_TILE, w], 0)),
            pl.BlockSpec(memory_space=pl.ANY),
            pl.BlockSpec((1, 1, 2 * D_FF), lambda w, sc: (sc[S_EXPERT, w], 0, 0)),
            pl.BlockSpec(memory_space=pl.ANY),
            pl.BlockSpec((1, 1, D_MODEL), lambda w, sc: (sc[S_EXPERT, w], 0, 0)),
        ],
        out_specs=pl.BlockSpec((EXPERT_TILE, half), lambda w, sc: (sc[S_TILE, w], 0)),
        scratch_shapes=[pltpu.VMEM((EXPERT_TILE, D_FF), BF16),
                        pltpu.VMEM((2, D_MODEL, 2 * D_FF), F32),
                        pltpu.VMEM((2, D_FF, D_MODEL), F32),
                        pltpu.SemaphoreType.DMA((2, 2))],
    )
    return pl.pallas_call(
        _expert_kernel,
        grid_spec=grid_spec,
        out_shape=jax.ShapeDtypeStruct((rows, half), jnp.uint32),
        compiler_params=pltpu.CompilerParams(dimension_semantics=("arbitrary",),
                                             vmem_limit_bytes=VMEM_LIMIT),
        name="experts",
    )(sched, xs, wgu, bgu, wd, bd)


def _unpack_pair(packed):
    lo = lax.bitcast_convert_type(packed << 16, F32)
    hi = lax.bitcast_convert_type(packed & jnp.uint32(0xFFFF0000), F32)
    return lo, hi


def _pack_pair(lo, hi):
    lo_bits = lax.bitcast_convert_type(lo.astype(BF16).astype(F32), jnp.uint32)
    hi_bits = lax.bitcast_convert_type(hi.astype(BF16).astype(F32), jnp.uint32)
    return (lo_bits >> 16) | (hi_bits & jnp.uint32(0xFFFF0000))


def _combine_kernel(yg_ref, h2_ref, route_ref, o_ref):
    half = D_MODEL // 2
    by_token = jnp.transpose(route_ref[...])
    lo_sum = h2_ref[:, :half]
    hi_sum = h2_ref[:, half:]
    for k in range(TOP_K):
        gate = by_token[:, TOP_K + k:TOP_K + k + 1]
        lo, hi = _unpack_pair(yg_ref[k])
        lo_sum = lo_sum + gate * lo
        hi_sum = hi_sum + gate * hi
    o_ref[:, :half] = lo_sum
    o_ref[:, half:] = hi_sum


def _combine(yg, h2, route, part):
    n = h2.shape[0]
    half = D_MODEL // 2
    steps = yg.shape[1] // ROW_TILE
    row = lambda i: (i + part * steps, 0)
    return pl.pallas_call(
        _combine_kernel,
        grid=(steps,),
        in_specs=[
            pl.BlockSpec((TOP_K, ROW_TILE, half), lambda i: (0, i, 0)),
            pl.BlockSpec((ROW_TILE, D_MODEL), row),
            pl.BlockSpec((2 * TOP_K, ROW_TILE), lambda i: (0, i + part * steps)),
        ],
        out_specs=pl.BlockSpec((ROW_TILE, D_MODEL), row),
        out_shape=jax.ShapeDtypeStruct((n, D_MODEL), F32),
        input_output_aliases={1: 0},
        compiler_params=pltpu.CompilerParams(dimension_semantics=("arbitrary",),
                                             vmem_limit_bytes=VMEM_LIMIT),
        name="combine",
    )(yg, h2, route)


def _work_schedule(counts, n_rows):
    n_work = n_rows // EXPERT_TILE + N_EXPERTS
    tiles = (counts + (EXPERT_TILE - 1)) // EXPERT_TILE
    tile_end = jnp.cumsum(tiles)
    tile_start = tile_end - tiles
    total = tile_end[-1]
    w = jnp.arange(n_work, dtype=jnp.int32)
    wc = jnp.minimum(w, total - 1)
    expert_of = jnp.sum((tile_end[None, :] <= wc[:, None]).astype(jnp.int32), axis=1)
    mine = expert_of[:, None] == jnp.arange(N_EXPERTS, dtype=jnp.int32)[None, :]
    pick = lambda table: jnp.sum(jnp.where(mine, table[None, :], 0), axis=1)
    live = jnp.clip(pick(counts) - (wc - pick(tile_start)) * EXPERT_TILE, 0, EXPERT_TILE)
    live = jnp.where(w < total, live, 0)
    prev_e = jnp.concatenate([jnp.full((1,), -1, jnp.int32), expert_of[:-1]])
    new_e = jnp.logical_and(w < total, expert_of != prev_e)
    slot = (jnp.cumsum(new_e.astype(jnp.int32)) - 1) % 2
    later = jnp.logical_and(new_e[None, :], w[None, :] > w[:, None])
    next_w = jnp.min(jnp.where(later, w[None, :], n_work), axis=1)
    next_e = jnp.sum(jnp.where(next_w[:, None] == w[None, :], expert_of[None, :], 0), axis=1)
    next_e = jnp.where(next_w < n_work, next_e, -1)
    sched = jnp.stack([wc, expert_of, live, new_e.astype(jnp.int32), slot, next_e]).astype(jnp.int32)
    return sched, (tile_start * EXPERT_TILE).astype(jnp.int32)


def kernel(x, meta_tokens, norm1_g, w_in, q_norm_g, k_norm_g, attn_sink, dw_w, dw_b, conv_ln_g,
           conv_ln_b, w_out, norm2_g, router_w, router_b, w_gate_up, b_gate_up, w_down, b_down):
    assert norm1_g.shape[0] == 1, "single-layer trunk: meta-token query rows are not materialised"
    b, s, d = x.shape
    n = b * s
    x2d = x.reshape(n, d)

    scale = HEAD_DIM ** -0.5
    qg = (jnp.tile(q_norm_g[0], N_Q_HEADS) * scale).reshape(1, ATTN_W)
    kg = jnp.tile(k_norm_g[0], N_KV_HEADS).reshape(1, KV_W)
    head_of = np.arange(ATTN_W) // HEAD_DIM
    pq = jnp.asarray((head_of[:, None] == head_of[None, :]) / HEAD_DIM, BF16)
    pk = pq[:KV_W, :KV_W]
    g1 = norm1_g[0].reshape(1, d)
    w_in_b = w_in[0].astype(BF16)

    q, kd, vd, hc = _inproj(x2d, g1, w_in_b, qg, kg, pq, pk, ROW_TILE)
    _, kmeta, vmeta, hc_meta = _inproj(meta_tokens, g1, w_in_b, qg, kg, pq, pk, N_META)

    meta_pad = ((0, BLOCK - N_META), (0, 0))
    attn = _attention(attn_sink[0], q.reshape(b, s, ATTN_W), kd.reshape(b, s, 2 * KV_W),
                      vd.reshape(b, s, 2 * KV_W), jnp.pad(kmeta, meta_pad), jnp.pad(vmeta, meta_pad),
                      jnp.asarray(_alibi_bias()))

    w_out_b = w_out[0].astype(BF16)
    rw = router_w[0].T
    rwh = rw.astype(BF16)
    rwl = (rw - rwh.astype(F32)).astype(BF16)
    rb = router_b[0].reshape(N_EXPERTS, 1)
    tri = jnp.asarray(np.triu(np.ones((ROW_TILE, ROW_TILE), np.float32), 1), BF16)
    h2, xp, route, counts = _outproj(attn.reshape(n, ATTN_W), hc.reshape(b, s, CONV_W), hc_meta,
                                     dw_w[0].reshape(CONV_K, CONV_W), dw_b[0].reshape(1, CONV_W),
                                     conv_ln_g[0].reshape(1, CONV_W), conv_ln_b[0].reshape(1, CONV_W), x2d,
                                     w_out_b[:ATTN_W], w_out_b[ATTN_W:], norm2_g[0].reshape(1, d),
                                     rwh, rwl, rb, tri)

    counts_i = counts[:, 0].astype(jnp.int32)
    idx = route[:TOP_K].astype(jnp.int32)
    rank = route[2 * TOP_K:3 * TOP_K].astype(jnp.int32)
    sched, first_row = _work_schedule(counts_i, n * TOP_K)
    chosen = idx[:, :, None] == jnp.arange(N_EXPERTS, dtype=jnp.int32)
    pos = rank + jnp.sum(jnp.where(chosen, first_row, 0), axis=-1)
    pos3 = pos.reshape(TOP_K, n // SC_CHUNK, SC_CHUNK).transpose(1, 0, 2)

    xs = _sc_dispatch(pos3, xp, sched.shape[1] * EXPERT_TILE)
    y = _experts(sched, xs, w_gate_up[0],
                 b_gate_up[0].reshape(N_EXPERTS, 1, 2 * D_FF), w_down[0],
                 b_down[0].reshape(N_EXPERTS, 1, D_MODEL))
    out = h2
    chunks = pos3.shape[0] // COMBINE_PARTS
    for part in range(COMBINE_PARTS):
        out = _combine(_sc_collect(pos3[part * chunks:(part + 1) * chunks], y), out, route, part)
    return out.reshape(b, s, d)
```

```python
import functools

import numpy as np
import jax
import jax.numpy as jnp
from jax import lax
from jax.experimental import pallas as pl
from jax.experimental.pallas import tpu as pltpu
from jax.experimental.pallas import tpu_sc as plsc

F32 = jnp.float32
BF16 = jnp.bfloat16

D_MODEL = 1024
N_META = 16
HEAD_DIM = 64
N_Q_HEADS = 8
N_KV_HEADS = 2
GROUP = N_Q_HEADS // N_KV_HEADS
ATTN_W = N_Q_HEADS * HEAD_DIM
KV_W = N_KV_HEADS * HEAD_DIM
CONV_W = D_MODEL - ATTN_W
IN_W = ATTN_W + 2 * KV_W + 2 * CONV_W
WINDOW = 128
BLOCK = 128
CONV_K = 31
CONV_PAD = CONV_K // 2
N_EXPERTS = 32
TOP_K = 4
D_FF = D_MODEL
SWIGLU_LIMIT = 7.0
SWIGLU_ALPHA = 1.702
RMS_EPS = 1e-6
LN_EPS = 1e-5
NEG_INF = -1e30

LANES = 128
ROW_TILE = 1024
EXPERT_TILE = 1024
SC_CORES = 2
SC_SUBCORES = 16
SC_WORKERS = SC_CORES * SC_SUBCORES
SC_CHUNK = 64
CONV_ROWS = 128
FF_CHUNK = 512
TILE_SPANS = (256, 512, 768, 1024)
Q_BLOCKS = 16
COMBINE_PARTS = 8
ROUTE_ROWS = 16
VMEM_LIMIT = 56 * 1024 * 1024


def _rms(x, eps):
    return x * lax.rsqrt(jnp.mean(x * x, axis=-1, keepdims=True) + eps)


def _inproj_kernel(x_ref, g1_ref, w_ref, qg_ref, kg_ref, pq_ref, pk_ref,
                   q_ref, k_ref, v_ref, hc_ref):
    x = x_ref[...]
    n = (_rms(x, RMS_EPS) * g1_ref[...]).astype(BF16)
    proj = jnp.dot(n, w_ref[...], preferred_element_type=F32)
    q = proj[:, :ATTN_W]
    k = proj[:, ATTN_W:ATTN_W + KV_W]
    v = proj[:, ATTN_W + KV_W:ATTN_W + 2 * KV_W]
    a = proj[:, ATTN_W + 2 * KV_W:ATTN_W + 2 * KV_W + CONV_W]
    g = proj[:, ATTN_W + 2 * KV_W + CONV_W:]
    qms = jnp.dot((q * q).astype(BF16), pq_ref[...], preferred_element_type=F32)
    kms = jnp.dot((k * k).astype(BF16), pk_ref[...], preferred_element_type=F32)
    q_ref[...] = (q * lax.rsqrt(qms + RMS_EPS) * qg_ref[...]).astype(BF16)
    kn = k * lax.rsqrt(kms + RMS_EPS) * kg_ref[...]
    lo = lax.broadcasted_iota(jnp.int32, kn.shape, 1) < HEAD_DIM
    ksw = pltpu.roll(kn, HEAD_DIM, 1)
    k_ref[...] = jnp.concatenate([jnp.where(lo, kn, ksw), jnp.where(lo, ksw, kn)], axis=1).astype(BF16)
    vsw = pltpu.roll(v, HEAD_DIM, 1)
    v_ref[...] = jnp.concatenate([jnp.where(lo, v, vsw), jnp.where(lo, vsw, v)], axis=1).astype(BF16)
    hc_ref[...] = a * jax.nn.sigmoid(g)


def _inproj(x2d, g1, w_in_b, qg, kg, pq, pk, tile):
    n = x2d.shape[0]
    const = lambda i: (0, 0)
    row = lambda i: (i, 0)
    return pl.pallas_call(
        _inproj_kernel,
        grid=(n // tile,),
        in_specs=[
            pl.BlockSpec((tile, D_MODEL), row),
            pl.BlockSpec((1, D_MODEL), const),
            pl.BlockSpec((D_MODEL, IN_W), const),
            pl.BlockSpec((1, ATTN_W), const),
            pl.BlockSpec((1, KV_W), const),
            pl.BlockSpec((ATTN_W, ATTN_W), const),
            pl.BlockSpec((KV_W, KV_W), const),
        ],
        out_specs=[
            pl.BlockSpec((tile, ATTN_W), row),
            pl.BlockSpec((tile, 2 * KV_W), row),
            pl.BlockSpec((tile, 2 * KV_W), row),
            pl.BlockSpec((tile, CONV_W), row),
        ],
        out_shape=[
            jax.ShapeDtypeStruct((n, ATTN_W), BF16),
            jax.ShapeDtypeStruct((n, 2 * KV_W), BF16),
            jax.ShapeDtypeStruct((n, 2 * KV_W), BF16),
            jax.ShapeDtypeStruct((n, CONV_W), F32),
        ],
        compiler_params=pltpu.CompilerParams(dimension_semantics=("arbitrary",),
                                             vmem_limit_bytes=VMEM_LIMIT),
        name="inproj",
    )(x2d, g1, w_in_b, qg, kg, pq, pk)


def _conv_fill(pad_ref, hm_ref, hc_ref, seq):
    tail = pad_ref.shape[0] - (N_META + seq)
    pad_ref[0:N_META, :] = hm_ref[...]
    pad_ref[N_META:N_META + seq, :] = hc_ref[0]
    pad_ref[N_META + seq:, :] = jnp.zeros((tail, CONV_W), F32)


def _conv_chunk(pad_ref, y_ref, w_ref, b_ref, lg_ref, lb_ref, base):
    first = N_META - CONV_PAD
    span = CONV_ROWS + 8
    for lt in range(CONV_W // LANES):
        ls = slice(lt * LANES, (lt + 1) * LANES)
        win = pad_ref[pl.ds(base, CONV_ROWS + 32), ls]
        acc = None
        for sub in range(8):
            part = None
            for al in range(4):
                k = 8 * al + sub - first
                if 0 <= k < CONV_K:
                    term = win[8 * al:8 * al + span] * w_ref[k:k + 1, ls]
                    part = term if part is None else part + term
            if sub:
                part = pltpu.roll(part, span - sub, 0)
            acc = part[:CONV_ROWS] if acc is None else acc + part[:CONV_ROWS]
        y_ref[:, ls] = acc
    y = y_ref[...] + b_ref[...]
    mu = jnp.mean(y, axis=-1, keepdims=True)
    yc = y - mu
    var = jnp.mean(yc * yc, axis=-1, keepdims=True)
    z = yc * lax.rsqrt(var + LN_EPS) * lg_ref[...] + lb_ref[...]
    return (z * jax.nn.sigmoid(z)).astype(BF16)


def _alibi_bias():
    qi = np.arange(BLOCK)[:, None]
    kj = np.arange(BLOCK)[None, :]
    dists = [qi + BLOCK - kj, np.abs(qi - kj), kj + BLOCK - qi]
    out = np.zeros((N_KV_HEADS, 4, GROUP * BLOCK, BLOCK), np.float32)
    for h in range(N_KV_HEADS):
        for g in range(GROUP):
            slope = 2.0 ** (-8.0 * (h * GROUP + g + 1) / N_Q_HEADS)
            for p, d in enumerate(dists):
                out[h, p, g * BLOCK:(g + 1) * BLOCK] = np.where(d <= WINDOW, -slope * d, NEG_INF)
    out[:, 3, :, N_META:] = NEG_INF
    return out


def _attn_kernel(sink_ref, q_ref, kp_ref, kc_ref, kn_ref, vp_ref, vc_ref, vn_ref,
                 km_ref, vm_ref, bias_ref, o_ref, *, n_steps):
    i = pl.program_id(1)
    lo = lax.broadcasted_iota(jnp.int32, (BLOCK, LANES), 1) < HEAD_DIM
    edge_first = jnp.where(i == 0, NEG_INF, 0.0).astype(F32)
    edge_last = jnp.where(i == n_steps - 1, NEG_INF, 0.0).astype(F32)
    nt = (((1,), (1,)), ((), ()))
    zero = jnp.zeros((BLOCK, LANES), BF16)

    def key_block(before_ref, here_ref, after_ref, idx, ks):
        if idx < 0:
            return before_ref[0, :, ks]
        if idx >= Q_BLOCKS:
            return after_ref[0, :, ks]
        return here_ref[0, idx * BLOCK:(idx + 1) * BLOCK, ks]

    for qb in range(Q_BLOCKS):
        qr = slice(qb * BLOCK, (qb + 1) * BLOCK)
        for h in range(N_KV_HEADS):
            ks = slice(h * LANES, (h + 1) * LANES)
            rows = []
            for j in range(2):
                pair = q_ref[0, qr, (2 * h + j) * LANES:(2 * h + j + 1) * LANES]
                rows.append(jnp.where(lo, pair, zero))
                rows.append(jnp.where(lo, zero, pair))
            qs = jnp.concatenate(rows, axis=0)
            k_p, k_c, k_n = (key_block(kp_ref, kc_ref, kn_ref, qb + rel, ks) for rel in (-1, 0, 1))
            v_p, v_c, v_n = (key_block(vp_ref, vc_ref, vn_ref, qb + rel, ks) for rel in (-1, 0, 1))
            s_p = lax.dot_general(qs, k_p, nt, preferred_element_type=F32) + bias_ref[h, 0]
            s_c = lax.dot_general(qs, k_c, nt, preferred_element_type=F32) + bias_ref[h, 1]
            s_n = lax.dot_general(qs, k_n, nt, preferred_element_type=F32) + bias_ref[h, 2]
            s_m = lax.dot_general(qs, km_ref[:, ks], nt, preferred_element_type=F32) + bias_ref[h, 3]
            if qb == 0:
                s_p = s_p + edge_first
            if qb == Q_BLOCKS - 1:
                s_n = s_n + edge_last
            sink = jnp.concatenate(
                [jnp.full((BLOCK, 1), sink_ref[h * GROUP + g], F32) for g in range(GROUP)], axis=0)
            m = jnp.max(jnp.maximum(jnp.maximum(s_p, s_c), jnp.maximum(s_n, s_m)), axis=-1, keepdims=True)
            m = jnp.maximum(m, sink)
            p_p = jnp.exp(s_p - m)
            p_c = jnp.exp(s_c - m)
            p_n = jnp.exp(s_n - m)
            p_m = jnp.exp(s_m - m)
            denom = jnp.sum((p_p + p_c) + (p_n + p_m), axis=-1, keepdims=True) + jnp.exp(sink - m)
            o = (jnp.dot(p_p.astype(BF16), v_p, preferred_element_type=F32)
                 + jnp.dot(p_c.astype(BF16), v_c, preferred_element_type=F32)
                 + jnp.dot(p_n.astype(BF16), v_n, preferred_element_type=F32)
                 + jnp.dot(p_m.astype(BF16), vm_ref[:, ks], preferred_element_type=F32))
            o = o / denom
            for j in range(2):
                even = o[(2 * j) * BLOCK:(2 * j + 1) * BLOCK]
                odd = o[(2 * j + 1) * BLOCK:(2 * j + 2) * BLOCK]
                o_ref[0, qr, (2 * h + j) * LANES:(2 * h + j + 1) * LANES] = jnp.where(lo, even, odd).astype(BF16)


def _attention(sink, q, kd, vd, kmeta, vmeta, bias):
    b, s, _ = q.shape
    nb = s // BLOCK
    steps = nb // Q_BLOCKS
    here = lambda bi, i: (bi, i, 0)
    before = lambda bi, i: (bi, jnp.maximum(Q_BLOCKS * i - 1, 0), 0)
    after = lambda bi, i: (bi, jnp.minimum(Q_BLOCKS * i + Q_BLOCKS, nb - 1), 0)
    const2 = lambda bi, i: (0, 0)
    edge_blk = (1, BLOCK, 2 * KV_W)
    here_blk = (1, Q_BLOCKS * BLOCK, 2 * KV_W)
    return pl.pallas_call(
        functools.partial(_attn_kernel, n_steps=steps),
        grid=(b, steps),
        in_specs=[
            pl.BlockSpec(memory_space=pltpu.SMEM),
            pl.BlockSpec((1, Q_BLOCKS * BLOCK, ATTN_W), here),
            pl.BlockSpec(edge_blk, before), pl.BlockSpec(here_blk, here), pl.BlockSpec(edge_blk, after),
            pl.BlockSpec(edge_blk, before), pl.BlockSpec(here_blk, here), pl.BlockSpec(edge_blk, after),
            pl.BlockSpec((BLOCK, 2 * KV_W), const2),
            pl.BlockSpec((BLOCK, 2 * KV_W), const2),
            pl.BlockSpec(bias.shape, lambda bi, i: (0, 0, 0, 0)),
        ],
        out_specs=pl.BlockSpec((1, Q_BLOCKS * BLOCK, ATTN_W), here),
        out_shape=jax.ShapeDtypeStruct((b, s, ATTN_W), BF16),
        compiler_params=pltpu.CompilerParams(dimension_semantics=("arbitrary", "arbitrary"),
                                             vmem_limit_bytes=VMEM_LIMIT),
        name="attention",
    )(sink, q, kd, kd, kd, vd, vd, vd, kmeta, vmeta, bias)


def _outproj_kernel(attn_ref, hc_ref, hm_ref, cw_ref, cb_ref, lg_ref, lb_ref, x_ref, wa_ref, wc_ref, g2_ref,
                    rwh_ref, rwl_ref, rb_ref, tri_ref, h2_ref, xp_ref, route_ref, cnt_ref,
                    run_ref, pad_ref, y_ref, conv_ref, *, seq):
    step = pl.program_id(0)
    tiles_per_seq = seq // ROW_TILE
    part = step % tiles_per_seq

    @pl.when(step == 0)
    def _():
        run_ref[...] = jnp.zeros_like(run_ref)

    @pl.when(part == 0)
    def _():
        _conv_fill(pad_ref, hm_ref, hc_ref, seq)

    for c in range(ROW_TILE // CONV_ROWS):
        base = pl.multiple_of(part * ROW_TILE + c * CONV_ROWS, CONV_ROWS)
        conv_ref[c * CONV_ROWS:(c + 1) * CONV_ROWS, :] = _conv_chunk(pad_ref, y_ref, cw_ref, cb_ref, lg_ref,
                                                                     lb_ref, base)
    mix = (jnp.dot(attn_ref[...], wa_ref[...], preferred_element_type=F32)
           + jnp.dot(conv_ref[...], wc_ref[...], preferred_element_type=F32))
    h2 = x_ref[...] + mix
    h2_ref[...] = h2
    hn = _rms(h2, RMS_EPS) * g2_ref[...]
    hn_hi = hn.astype(BF16)
    half = D_MODEL // 2
    xp_ref[...] = _pack_pair(hn[:, :half], hn[:, half:])
    hn_lo = (hn - hn_hi.astype(F32)).astype(BF16)
    nt = (((1,), (1,)), ((), ()))
    logits = (lax.dot_general(rwh_ref[...], hn_hi, nt, preferred_element_type=F32)
              + lax.dot_general(rwh_ref[...], hn_lo, nt, preferred_element_type=F32)
              + lax.dot_general(rwl_ref[...], hn_hi, nt, preferred_element_type=F32)) + rb_ref[...]
    expert = lax.broadcasted_iota(jnp.int32, logits.shape, 0).astype(F32)
    work = logits
    vals, sels = [], []
    for r in range(TOP_K):
        m = jnp.max(work, axis=0, keepdims=True)
        idx = jnp.min(jnp.where(work == m, expert, float(N_EXPERTS)), axis=0, keepdims=True)
        sel = expert == idx
        work = jnp.where(sel, -jnp.inf, work)
        route_ref[r:r + 1, :] = idx
        vals.append(m)
        sels.append(sel)
    onehot = jnp.where(jnp.logical_or(jnp.logical_or(sels[0], sels[1]), jnp.logical_or(sels[2], sels[3])),
                       1.0, 0.0)
    exps = [jnp.exp(v - vals[0]) for v in vals]
    tot = exps[0] + exps[1] + exps[2] + exps[3]
    before = jnp.dot(onehot.astype(BF16), tri_ref[...], preferred_element_type=F32) + run_ref[...]
    for r in range(TOP_K):
        route_ref[TOP_K + r:TOP_K + r + 1, :] = exps[r] / tot
        route_ref[2 * TOP_K + r:2 * TOP_K + r + 1, :] = jnp.sum(jnp.where(sels[r], before, 0.0), axis=0,
                                                               keepdims=True)
    route_ref[3 * TOP_K:, :] = jnp.zeros((ROUTE_ROWS - 3 * TOP_K, logits.shape[1]), F32)
    run_ref[...] = run_ref[...] + jnp.sum(onehot, axis=1, keepdims=True)
    cnt_ref[...] = run_ref[...]


def _outproj(attn2d, hc, hc_meta, dw_w, dw_b, ln_g, ln_b, x2d, wa, wc, g2, rwh, rwl, rb, tri):
    n = x2d.shape[0]
    seq = hc.shape[1]
    tile = ROW_TILE
    const = lambda i: (0, 0)
    row = lambda i: (i, 0)
    half = D_MODEL // 2
    return pl.pallas_call(
        functools.partial(_outproj_kernel, seq=seq),
        grid=(n // tile,),
        in_specs=[
            pl.BlockSpec((tile, ATTN_W), row),
            pl.BlockSpec((1, seq, CONV_W), lambda i: (i // (seq // tile), 0, 0)),
            pl.BlockSpec((N_META, CONV_W), const),
            pl.BlockSpec((CONV_K, CONV_W), const),
            pl.BlockSpec((1, CONV_W), const),
            pl.BlockSpec((1, CONV_W), const),
            pl.BlockSpec((1, CONV_W), const),
            pl.BlockSpec((tile, D_MODEL), row),
            pl.BlockSpec((ATTN_W, D_MODEL), const),
            pl.BlockSpec((CONV_W, D_MODEL), const),
            pl.BlockSpec((1, D_MODEL), const),
            pl.BlockSpec((N_EXPERTS, D_MODEL), const),
            pl.BlockSpec((N_EXPERTS, D_MODEL), const),
            pl.BlockSpec((N_EXPERTS, 1), const),
            pl.BlockSpec((tile, tile), const),
        ],
        out_specs=[
            pl.BlockSpec((tile, D_MODEL), row),
            pl.BlockSpec((tile, half), row),
            pl.BlockSpec((ROUTE_ROWS, tile), lambda i: (0, i)),
            pl.BlockSpec((N_EXPERTS, 1), const),
        ],
        out_shape=[
            jax.ShapeDtypeStruct((n, D_MODEL), F32),
            jax.ShapeDtypeStruct((n, half), jnp.uint32),
            jax.ShapeDtypeStruct((ROUTE_ROWS, n), F32),
            jax.ShapeDtypeStruct((N_EXPERTS, 1), F32),
        ],
        scratch_shapes=[pltpu.VMEM((N_EXPERTS, 1), F32),
                        pltpu.VMEM((N_META + seq + 32, CONV_W), F32),
                        pltpu.VMEM((CONV_ROWS, CONV_W), F32),
                        pltpu.VMEM((tile, CONV_W), BF16)],
        compiler_params=pltpu.CompilerParams(dimension_semantics=("arbitrary",),
                                             vmem_limit_bytes=VMEM_LIMIT),
        name="outproj",
    )(attn2d, hc, hc_meta, dw_w, dw_b, ln_g, ln_b, x2d, wa, wc, g2, rwh, rwl, rb, tri)


def _sc_mesh():
    return plsc.VectorSubcoreMesh(core_axis_name="c", subcore_axis_name="s",
                                  num_cores=SC_CORES, num_subcores=SC_SUBCORES)


def _sc_worker():
    return lax.axis_index("s") * SC_CORES + lax.axis_index("c")


def _sc_dispatch(pos3, xp, out_rows):
    n, half = xp.shape
    per_worker = n // SC_CHUNK // SC_WORKERS

    @functools.partial(
        pl.kernel, mesh=_sc_mesh(),
        out_type=jax.ShapeDtypeStruct((out_rows, half), jnp.uint32),
        scratch_types=[pltpu.VMEM((2, TOP_K, SC_CHUNK), jnp.int32),
                       pltpu.VMEM((2, SC_CHUNK, half), jnp.uint32),
                       pltpu.SemaphoreType.DMA((2,)),
                       pltpu.SemaphoreType.DMA],
        name="sc_dispatch")
    def run(pos_hbm, xp_hbm, xs_hbm, idx_v, rows_v, load_sem, scatter_sem):
        first = _sc_worker() * per_worker

        def load(j, buf):
            c = first + j
            return (pltpu.async_copy(pos_hbm.at[c], idx_v.at[buf], load_sem.at[buf]),
                    pltpu.async_copy(xp_hbm.at[pl.ds(c * SC_CHUNK, SC_CHUNK)], rows_v.at[buf], load_sem.at[buf]))

        loads = load(0, 0)
        for j in range(per_worker):
            buf = j % 2
            for cp in loads:
                cp.wait()
            if j + 1 < per_worker:
                loads = load(j + 1, 1 - buf)
            copies = [pltpu.async_copy(rows_v.at[buf], xs_hbm.at[idx_v.at[buf].at[k]], scatter_sem)
                      for k in range(TOP_K)]
            for cp in copies:
                cp.wait()

    return run(pos3, xp)


def _sc_collect(pos3, y):
    half = y.shape[1]
    n = pos3.shape[0] * SC_CHUNK
    per_worker = n // SC_CHUNK // SC_WORKERS

    @functools.partial(
        pl.kernel, mesh=_sc_mesh(),
        out_type=jax.ShapeDtypeStruct((TOP_K, n, half), jnp.uint32),
        scratch_types=[pltpu.VMEM((TOP_K, SC_CHUNK), jnp.int32),
                       pltpu.VMEM((2, SC_CHUNK, half), jnp.uint32),
                       pltpu.SemaphoreType.DMA((2,)),
                       pltpu.SemaphoreType.DMA((2,))],
        name="sc_collect")
    def run(pos_hbm, y_hbm, yg_hbm, idx_v, rows_v, gather_sem, store_sem):
        first = _sc_worker() * per_worker

        @pl.loop(0, per_worker)
        def _(j):
            c = first + j
            pltpu.sync_copy(pos_hbm.at[c], idx_v)

            def gather(k):
                return pltpu.async_copy(y_hbm.at[idx_v.at[k]], rows_v.at[k % 2], gather_sem.at[k % 2])

            gathers = [gather(0)]
            stores = []
            for k in range(TOP_K):
                if k + 1 < TOP_K:
                    if k >= 1:
                        stores[k - 1].wait()
                    gathers.append(gather(k + 1))
                gathers[k].wait()
                stores.append(pltpu.async_copy(rows_v.at[k % 2], yg_hbm.at[k, pl.ds(c * SC_CHUNK, SC_CHUNK)],
                                               store_sem.at[k % 2]))
            stores[TOP_K - 2].wait()
            stores[TOP_K - 1].wait()

    return run(pos3, y)


S_TILE, S_EXPERT, S_ROWS, S_NEW, S_SLOT, S_NEXT = range(6)


def _expert_kernel(sched_ref, xs_ref, wgu_hbm, bgu_ref, wd_hbm, bd_ref, y_ref,
                   act_ref, wgu_buf, wd_buf, sem):
    w = pl.program_id(0)
    e = sched_ref[S_EXPERT, w]
    slot = sched_ref[S_SLOT, w]
    n_rows = sched_ref[S_ROWS, w]

    def weight_copies(expert, s):
        return (pltpu.make_async_copy(wgu_hbm.at[expert], wgu_buf.at[s], sem.at[0, s]),
                pltpu.make_async_copy(wd_hbm.at[expert], wd_buf.at[s], sem.at[1, s]))

    @pl.when(sched_ref[S_NEW, w] == 1)
    def _():
        @pl.when(w == 0)
        def _():
            for cp in weight_copies(e, slot):
                cp.start()

        nxt = sched_ref[S_NEXT, w]

        @pl.when(nxt >= 0)
        def _():
            for cp in weight_copies(nxt, 1 - slot):
                cp.start()

        for cp in weight_copies(e, slot):
            cp.wait()

    def tile_path(m):
        half = D_MODEL // 2
        live = lax.broadcasted_iota(jnp.int32, (m, 1), 0) < n_rows
        x_lo, x_hi = _unpack_pair(jnp.where(live, xs_ref[0:m, :], jnp.uint32(0)))
        x = jnp.concatenate([x_lo.astype(BF16), x_hi.astype(BF16)], axis=1)
        for c in range(D_FF // FF_CHUNK):
            gc = slice(c * FF_CHUNK, (c + 1) * FF_CHUNK)
            uc = slice(D_FF + c * FF_CHUNK, D_FF + (c + 1) * FF_CHUNK)
            gate = jnp.dot(x, wgu_buf[slot, :, gc].astype(BF16), preferred_element_type=F32) + bgu_ref[0, :, gc]
            up = jnp.dot(x, wgu_buf[slot, :, uc].astype(BF16), preferred_element_type=F32) + bgu_ref[0, :, uc]
            gate = jnp.minimum(gate, SWIGLU_LIMIT)
            up = jnp.clip(up, -SWIGLU_LIMIT, SWIGLU_LIMIT)
            act_ref[0:m, gc] = (gate * jax.nn.sigmoid(SWIGLU_ALPHA * gate) * (up + 1.0)).astype(BF16)
        for c in range(half // FF_CHUNK):
            lo_c = slice(c * FF_CHUNK, (c + 1) * FF_CHUNK)
            hi_c = slice(half + c * FF_CHUNK, half + (c + 1) * FF_CHUNK)
            y_lo = (jnp.dot(act_ref[0:m, :], wd_buf[slot, :, lo_c].astype(BF16), preferred_element_type=F32)
                    + bd_ref[0, :, lo_c])
            y_hi = (jnp.dot(act_ref[0:m, :], wd_buf[slot, :, hi_c].astype(BF16), preferred_element_type=F32)
                    + bd_ref[0, :, hi_c])
            y_ref[0:m, lo_c] = _pack_pair(y_lo, y_hi)

    for below, m in zip((0,) + TILE_SPANS[:-1], TILE_SPANS):
        pl.when(jnp.logical_and(n_rows > below, n_rows <= m))(functools.partial(tile_path, m))


def _experts(sched, xs, wgu, bgu, wd, bd):
    rows, half = xs.shape
    n_work = sched.shape[1]
    grid_spec = pltpu.PrefetchScalarGridSpec(
        num_scalar_prefetch=1,
        grid=(n_work,),
        in_specs=[
            pl.BlockSpec((EXPERT_TILE, half), lambda w, sc: (sc[S_TILE, w], 0)),
            pl.BlockSpec(memory_space=pl.ANY),
            pl.BlockSpec((1, 1, 2 * D_FF), lambda w, sc: (sc[S_EXPERT, w], 0, 0)),
            pl.BlockSpec(memory_space=pl.ANY),
            pl.BlockSpec((1, 1, D_MODEL), lambda w, sc: (sc[S_EXPERT, w], 0, 0)),
        ],
        out_specs=pl.BlockSpec((EXPERT_TILE, half), lambda w, sc: (sc[S_TILE, w], 0)),
        scratch_shapes=[pltpu.VMEM((EXPERT_TILE, D_FF), BF16),
                        pltpu.VMEM((2, D_MODEL, 2 * D_FF), F32),
                        pltpu.VMEM((2, D_FF, D_MODEL), F32),
                        pltpu.SemaphoreType.DMA((2, 2))],
    )
    return pl.pallas_call(
        _expert_kernel,
        grid_spec=grid_spec,
        out_shape=jax.ShapeDtypeStruct((rows, half), jnp.uint32),
        compiler_params=pltpu.CompilerParams(dimension_semantics=("arbitrary",),
                                             vmem_limit_bytes=VMEM_LIMIT),
        name="experts",
    )(sched, xs, wgu, bgu, wd, bd)


def _unpack_pair(packed):
    lo = lax.bitcast_convert_type(packed << 16, F32)
    hi = lax.bitcast_convert_type(packed & jnp.uint32(0xFFFF0000), F32)
    return lo, hi


def _pack_pair(lo, hi):
    lo_bits = lax.bitcast_convert_type(lo.astype(BF16).astype(F32), jnp.uint32)
    hi_bits = lax.bitcast_convert_type(hi.astype(BF16).astype(F32), jnp.uint32)
    return (lo_bits >> 16) | (hi_bits & jnp.uint32(0xFFFF0000))


def _combine_kernel(yg_ref, h2_ref, route_ref, o_ref):
    half = D_MODEL // 2
    by_token = jnp.transpose(route_ref[...])
    lo_sum = h2_ref[:, :half]
    hi_sum = h2_ref[:, half:]
    for k in range(TOP_K):
        gate = by_token[:, TOP_K + k:TOP_K + k + 1]
        lo, hi = _unpack_pair(yg_ref[k])
        lo_sum = lo_sum + gate * lo
        hi_sum = hi_sum + gate * hi
    o_ref[:, :half] = lo_sum
    o_ref[:, half:] = hi_sum


def _combine(yg, h2, route, part):
    n = h2.shape[0]
    half = D_MODEL // 2
    steps = yg.shape[1] // ROW_TILE
    row = lambda i: (i + part * steps, 0)
    return pl.pallas_call(
        _combine_kernel,
        grid=(steps,),
        in_specs=[
            pl.BlockSpec((TOP_K, ROW_TILE, half), lambda i: (0, i, 0)),
            pl.BlockSpec((ROW_TILE, D_MODEL), row),
            pl.BlockSpec((2 * TOP_K, ROW_TILE), lambda i: (0, i + part * steps)),
        ],
        out_specs=pl.BlockSpec((ROW_TILE, D_MODEL), row),
        out_shape=jax.ShapeDtypeStruct((n, D_MODEL), F32),
        input_output_aliases={1: 0},
        compiler_params=pltpu.CompilerParams(dimension_semantics=("arbitrary",),
                                             vmem_limit_bytes=VMEM_LIMIT),
        name="combine",
    )(yg, h2, route)


def _work_schedule(counts, n_rows):
    n_work = n_rows // EXPERT_TILE + N_EXPERTS
    tiles = (counts + (EXPERT_TILE - 1)) // EXPERT_TILE
    tile_end = jnp.cumsum(tiles)
    tile_start = tile_end - tiles
    total = tile_end[-1]
    w = jnp.arange(n_work, dtype=jnp.int32)
    wc = jnp.minimum(w, total - 1)
    expert_of = jnp.sum((tile_end[None, :] <= wc[:, None]).astype(jnp.int32), axis=1)
    mine = expert_of[:, None] == jnp.arange(N_EXPERTS, dtype=jnp.int32)[None, :]
    pick = lambda table: jnp.sum(jnp.where(mine, table[None, :], 0), axis=1)
    live = jnp.clip(pick(counts) - (wc - pick(tile_start)) * EXPERT_TILE, 0, EXPERT_TILE)
    live = jnp.where(w < total, live, 0)
    prev_e = jnp.concatenate([jnp.full((1,), -1, jnp.int32), expert_of[:-1]])
    new_e = jnp.logical_and(w < total, expert_of != prev_e)
    slot = (jnp.cumsum(new_e.astype(jnp.int32)) - 1) % 2
    later = jnp.logical_and(new_e[None, :], w[None, :] > w[:, None])
    next_w = jnp.min(jnp.where(later, w[None, :], n_work), axis=1)
    next_e = jnp.sum(jnp.where(next_w[:, None] == w[None, :], expert_of[None, :], 0), axis=1)
    next_e = jnp.where(next_w < n_work, next_e, -1)
    sched = jnp.stack([wc, expert_of, live, new_e.astype(jnp.int32), slot, next_e]).astype(jnp.int32)
    return sched, (tile_start * EXPERT_TILE).astype(jnp.int32)


def kernel(x, meta_tokens, norm1_g, w_in, q_norm_g, k_norm_g, attn_sink, dw_w, dw_b, conv_ln_g,
           conv_ln_b, w_out, norm2_g, router_w, router_b, w_gate_up, b_gate_up, w_down, b_down):
    assert norm1_g.shape[0] == 1, "single-layer trunk: meta-token query rows are not materialised"
    b, s, d = x.shape
    n = b * s
    x2d = x.reshape(n, d)

    scale = HEAD_DIM ** -0.5
    qg = (jnp.tile(q_norm_g[0], N_Q_HEADS) * scale).reshape(1, ATTN_W)
    kg = jnp.tile(k_norm_g[0], N_KV_HEADS).reshape(1, KV_W)
    head_of = np.arange(ATTN_W) // HEAD_DIM
    pq = jnp.asarray((head_of[:, None] == head_of[None, :]) / HEAD_DIM, BF16)
    pk = pq[:KV_W, :KV_W]
    g1 = norm1_g[0].reshape(1, d)
    w_in_b = w_in[0].astype(BF16)

    q, kd, vd, hc = _inproj(x2d, g1, w_in_b, qg, kg, pq, pk, ROW_TILE)
    _, kmeta, vmeta, hc_meta = _inproj(meta_tokens, g1, w_in_b, qg, kg, pq, pk, N_META)

    meta_pad = ((0, BLOCK - N_META), (0, 0))
    attn = _attention(attn_sink[0], q.reshape(b, s, ATTN_W), kd.reshape(b, s, 2 * KV_W),
                      vd.reshape(b, s, 2 * KV_W), jnp.pad(kmeta, meta_pad), jnp.pad(vmeta, meta_pad),
                      jnp.asarray(_alibi_bias()))

    w_out_b = w_out[0].astype(BF16)
    rw = router_w[0].T
    rwh = rw.astype(BF16)
    rwl = (rw - rwh.astype(F32)).astype(BF16)
    rb = router_b[0].reshape(N_EXPERTS, 1)
    tri = jnp.asarray(np.triu(np.ones((ROW_TILE, ROW_TILE), np.float32), 1), BF16)
    h2, xp, route, counts = _outproj(attn.reshape(n, ATTN_W), hc.reshape(b, s, CONV_W), hc_meta,
                                     dw_w[0].reshape(CONV_K, CONV_W), dw_b[0].reshape(1, CONV_W),
                                     conv_ln_g[0].reshape(1, CONV_W), conv_ln_b[0].reshape(1, CONV_W), x2d,
                                     w_out_b[:ATTN_W], w_out_b[ATTN_W:], norm2_g[0].reshape(1, d),
                                     rwh, rwl, rb, tri)

    counts_i = counts[:, 0].astype(jnp.int32)
    idx = route[:TOP_K].astype(jnp.int32)
    rank = route[2 * TOP_K:3 * TOP_K].astype(jnp.int32)
    sched, first_row = _work_schedule(counts_i, n * TOP_K)
    chosen = idx[:, :, None] == jnp.arange(N_EXPERTS, dtype=jnp.int32)
    pos = rank + jnp.sum(jnp.where(chosen, first_row, 0), axis=-1)
    pos3 = pos.reshape(TOP_K, n // SC_CHUNK, SC_CHUNK).transpose(1, 0, 2)

    xs = _sc_dispatch(pos3, xp, sched.shape[1] * EXPERT_TILE)
    y = _experts(sched, xs, w_gate_up[0],
                 b_gate_up[0].reshape(N_EXPERTS, 1, 2 * D_FF), w_down[0],
                 b_down[0].reshape(N_EXPERTS, 1, D_MODEL))
    out = h2
    chunks = pos3.shape[0] // COMBINE_PARTS
    for part in range(COMBINE_PARTS):
        out = _combine(_sc_collect(pos3[part * chunks:(part + 1) * chunks], y), out, route, part)
    return out.reshape(b, s, d)
```

```python
import functools

import numpy as np
import jax
import jax.numpy as jnp
from jax import lax
from jax.experimental import pallas as pl
from jax.experimental.pallas import tpu as pltpu
from jax.experimental.pallas import tpu_sc as plsc

F32 = jnp.float32
BF16 = jnp.bfloat16

D_MODEL = 1024
N_META = 16
HEAD_DIM = 64
N_Q_HEADS = 8
N_KV_HEADS = 2
GROUP = N_Q_HEADS // N_KV_HEADS
ATTN_W = N_Q_HEADS * HEAD_DIM
KV_W = N_KV_HEADS * HEAD_DIM
CONV_W = D_MODEL - ATTN_W
IN_W = ATTN_W + 2 * KV_W + 2 * CONV_W
WINDOW = 128
BLOCK = 128
CONV_K = 31
CONV_PAD = CONV_K // 2
N_EXPERTS = 32
TOP_K = 4
D_FF = D_MODEL
SWIGLU_LIMIT = 7.0
SWIGLU_ALPHA = 1.702
RMS_EPS = 1e-6
LN_EPS = 1e-5
NEG_INF = -1e30

LANES = 128
ROW_TILE = 1024
EXPERT_TILE = 1024
SC_CORES = 2
SC_SUBCORES = 16
SC_WORKERS = SC_CORES * SC_SUBCORES
SC_CHUNK = 64
CONV_ROWS = 128
FF_CHUNK = 512
TILE_SPANS = (128, 256, 512, 768, 1024)
Q_BLOCKS = 16
COMBINE_PARTS = 8
ROUTE_ROWS = 16
VMEM_LIMIT = 56 * 1024 * 1024


def _rms(x, eps):
    return x * lax.rsqrt(jnp.mean(x * x, axis=-1, keepdims=True) + eps)


def _inproj_kernel(x_ref, g1_ref, w_ref, qg_ref, kg_ref, pq_ref, pk_ref,
                   q_ref, k_ref, v_ref, hc_ref):
    x = x_ref[...]
    n = (_rms(x, RMS_EPS) * g1_ref[...]).astype(BF16)
    proj = jnp.dot(n, w_ref[...], preferred_element_type=F32)
    q = proj[:, :ATTN_W]
    k = proj[:, ATTN_W:ATTN_W + KV_W]
    v = proj[:, ATTN_W + KV_W:ATTN_W + 2 * KV_W]
    a = proj[:, ATTN_W + 2 * KV_W:ATTN_W + 2 * KV_W + CONV_W]
    g = proj[:, ATTN_W + 2 * KV_W + CONV_W:]
    qms = jnp.dot((q * q).astype(BF16), pq_ref[...], preferred_element_type=F32)
    kms = jnp.dot((k * k).astype(BF16), pk_ref[...], preferred_element_type=F32)
    q_ref[...] = (q * lax.rsqrt(qms + RMS_EPS) * qg_ref[...]).astype(BF16)
    kn = k * lax.rsqrt(kms + RMS_EPS) * kg_ref[...]
    lo = lax.broadcasted_iota(jnp.int32, kn.shape, 1) < HEAD_DIM
    ksw = pltpu.roll(kn, HEAD_DIM, 1)
    k_ref[...] = jnp.concatenate([jnp.where(lo, kn, ksw), jnp.where(lo, ksw, kn)], axis=1).astype(BF16)
    vsw = pltpu.roll(v, HEAD_DIM, 1)
    v_ref[...] = jnp.concatenate([jnp.where(lo, v, vsw), jnp.where(lo, vsw, v)], axis=1).astype(BF16)
    hc_ref[...] = a * jax.nn.sigmoid(g)


def _inproj(x2d, g1, w_in_b, qg, kg, pq, pk, tile):
    n = x2d.shape[0]
    const = lambda i: (0, 0)
    row = lambda i: (i, 0)
    return pl.pallas_call(
        _inproj_kernel,
        grid=(n // tile,),
        in_specs=[
            pl.BlockSpec((tile, D_MODEL), row),
            pl.BlockSpec((1, D_MODEL), const),
            pl.BlockSpec((D_MODEL, IN_W), const),
            pl.BlockSpec((1, ATTN_W), const),
            pl.BlockSpec((1, KV_W), const),
            pl.BlockSpec((ATTN_W, ATTN_W), const),
            pl.BlockSpec((KV_W, KV_W), const),
        ],
        out_specs=[
            pl.BlockSpec((tile, ATTN_W), row),
            pl.BlockSpec((tile, 2 * KV_W), row),
            pl.BlockSpec((tile, 2 * KV_W), row),
            pl.BlockSpec((tile, CONV_W), row),
        ],
        out_shape=[
            jax.ShapeDtypeStruct((n, ATTN_W), BF16),
            jax.ShapeDtypeStruct((n, 2 * KV_W), BF16),
            jax.ShapeDtypeStruct((n, 2 * KV_W), BF16),
            jax.ShapeDtypeStruct((n, CONV_W), F32),
        ],
        compiler_params=pltpu.CompilerParams(dimension_semantics=("arbitrary",),
                                             vmem_limit_bytes=VMEM_LIMIT),
        name="inproj",
    )(x2d, g1, w_in_b, qg, kg, pq, pk)


def _conv_fill(pad_ref, hm_ref, hc_ref, seq):
    tail = pad_ref.shape[0] - (N_META + seq)
    pad_ref[0:N_META, :] = hm_ref[...]
    pad_ref[N_META:N_META + seq, :] = hc_ref[0]
    pad_ref[N_META + seq:, :] = jnp.zeros((tail, CONV_W), F32)


def _conv_chunk(pad_ref, y_ref, w_ref, b_ref, lg_ref, lb_ref, base):
    first = N_META - CONV_PAD
    span = CONV_ROWS + 8
    for lt in range(CONV_W // LANES):
        ls = slice(lt * LANES, (lt + 1) * LANES)
        win = pad_ref[pl.ds(base, CONV_ROWS + 32), ls]
        acc = None
        for sub in range(8):
            part = None
            for al in range(4):
                k = 8 * al + sub - first
                if 0 <= k < CONV_K:
                    term = win[8 * al:8 * al + span] * w_ref[k:k + 1, ls]
                    part = term if part is None else part + term
            if sub:
                part = pltpu.roll(part, span - sub, 0)
            acc = part[:CONV_ROWS] if acc is None else acc + part[:CONV_ROWS]
        y_ref[:, ls] = acc
    y = y_ref[...] + b_ref[...]
    mu = jnp.mean(y, axis=-1, keepdims=True)
    yc = y - mu
    var = jnp.mean(yc * yc, axis=-1, keepdims=True)
    z = yc * lax.rsqrt(var + LN_EPS) * lg_ref[...] + lb_ref[...]
    return (z * jax.nn.sigmoid(z)).astype(BF16)


def _alibi_bias():
    qi = np.arange(BLOCK)[:, None]
    kj = np.arange(BLOCK)[None, :]
    dists = [qi + BLOCK - kj, np.abs(qi - kj), kj + BLOCK - qi]
    out = np.zeros((N_KV_HEADS, 4, GROUP * BLOCK, BLOCK), np.float32)
    for h in range(N_KV_HEADS):
        for g in range(GROUP):
            slope = 2.0 ** (-8.0 * (h * GROUP + g + 1) / N_Q_HEADS)
            for p, d in enumerate(dists):
                out[h, p, g * BLOCK:(g + 1) * BLOCK] = np.where(d <= WINDOW, -slope * d, NEG_INF)
    out[:, 3, :, N_META:] = NEG_INF
    return out


def _attn_kernel(sink_ref, q_ref, kp_ref, kc_ref, kn_ref, vp_ref, vc_ref, vn_ref,
                 km_ref, vm_ref, bias_ref, o_ref, *, n_steps):
    i = pl.program_id(1)
    lo = lax.broadcasted_iota(jnp.int32, (BLOCK, LANES), 1) < HEAD_DIM
    edge_first = jnp.where(i == 0, NEG_INF, 0.0).astype(F32)
    edge_last = jnp.where(i == n_steps - 1, NEG_INF, 0.0).astype(F32)
    nt = (((1,), (1,)), ((), ()))
    zero = jnp.zeros((BLOCK, LANES), BF16)

    def key_block(before_ref, here_ref, after_ref, idx, ks):
        if idx < 0:
            return before_ref[0, :, ks]
        if idx >= Q_BLOCKS:
            return after_ref[0, :, ks]
        return here_ref[0, idx * BLOCK:(idx + 1) * BLOCK, ks]

    for qb in range(Q_BLOCKS):
        qr = slice(qb * BLOCK, (qb + 1) * BLOCK)
        for h in range(N_KV_HEADS):
            ks = slice(h * LANES, (h + 1) * LANES)
            rows = []
            for j in range(2):
                pair = q_ref[0, qr, (2 * h + j) * LANES:(2 * h + j + 1) * LANES]
                rows.append(jnp.where(lo, pair, zero))
                rows.append(jnp.where(lo, zero, pair))
            qs = jnp.concatenate(rows, axis=0)
            k_p, k_c, k_n = (key_block(kp_ref, kc_ref, kn_ref, qb + rel, ks) for rel in (-1, 0, 1))
            v_p, v_c, v_n = (key_block(vp_ref, vc_ref, vn_ref, qb + rel, ks) for rel in (-1, 0, 1))
            s_p = lax.dot_general(qs, k_p, nt, preferred_element_type=F32) + bias_ref[h, 0]
            s_c = lax.dot_general(qs, k_c, nt, preferred_element_type=F32) + bias_ref[h, 1]
            s_n = lax.dot_general(qs, k_n, nt, preferred_element_type=F32) + bias_ref[h, 2]
            s_m = lax.dot_general(qs, km_ref[:, ks], nt, preferred_element_type=F32) + bias_ref[h, 3]
            if qb == 0:
                s_p = s_p + edge_first
            if qb == Q_BLOCKS - 1:
                s_n = s_n + edge_last
            sink = jnp.concatenate(
                [jnp.full((BLOCK, 1), sink_ref[h * GROUP + g], F32) for g in range(GROUP)], axis=0)
            m = jnp.max(jnp.maximum(jnp.maximum(s_p, s_c), jnp.maximum(s_n, s_m)), axis=-1, keepdims=True)
            m = jnp.maximum(m, sink)
            p_p = jnp.exp(s_p - m)
            p_c = jnp.exp(s_c - m)
            p_n = jnp.exp(s_n - m)
            p_m = jnp.exp(s_m - m)
            denom = jnp.sum((p_p + p_c) + (p_n + p_m), axis=-1, keepdims=True) + jnp.exp(sink - m)
            o = (jnp.dot(p_p.astype(BF16), v_p, preferred_element_type=F32)
                 + jnp.dot(p_c.astype(BF16), v_c, preferred_element_type=F32)
                 + jnp.dot(p_n.astype(BF16), v_n, preferred_element_type=F32)
                 + jnp.dot(p_m.astype(BF16), vm_ref[:, ks], preferred_element_type=F32))
            o = o / denom
            for j in range(2):
                even = o[(2 * j) * BLOCK:(2 * j + 1) * BLOCK]
                odd = o[(2 * j + 1) * BLOCK:(2 * j + 2) * BLOCK]
                o_ref[0, qr, (2 * h + j) * LANES:(2 * h + j + 1) * LANES] = jnp.where(lo, even, odd).astype(BF16)


def _attention(sink, q, kd, vd, kmeta, vmeta, bias):
    b, s, _ = q.shape
    nb = s // BLOCK
    steps = nb // Q_BLOCKS
    here = lambda bi, i: (bi, i, 0)
    before = lambda bi, i: (bi, jnp.maximum(Q_BLOCKS * i - 1, 0), 0)
    after = lambda bi, i: (bi, jnp.minimum(Q_BLOCKS * i + Q_BLOCKS, nb - 1), 0)
    const2 = lambda bi, i: (0, 0)
    edge_blk = (1, BLOCK, 2 * KV_W)
    here_blk = (1, Q_BLOCKS * BLOCK, 2 * KV_W)
    return pl.pallas_call(
        functools.partial(_attn_kernel, n_steps=steps),
        grid=(b, steps),
        in_specs=[
            pl.BlockSpec(memory_space=pltpu.SMEM),
            pl.BlockSpec((1, Q_BLOCKS * BLOCK, ATTN_W), here),
            pl.BlockSpec(edge_blk, before), pl.BlockSpec(here_blk, here), pl.BlockSpec(edge_blk, after),
            pl.BlockSpec(edge_blk, before), pl.BlockSpec(here_blk, here), pl.BlockSpec(edge_blk, after),
            pl.BlockSpec((BLOCK, 2 * KV_W), const2),
            pl.BlockSpec((BLOCK, 2 * KV_W), const2),
            pl.BlockSpec(bias.shape, lambda bi, i: (0, 0, 0, 0)),
        ],
        out_specs=pl.BlockSpec((1, Q_BLOCKS * BLOCK, ATTN_W), here),
        out_shape=jax.ShapeDtypeStruct((b, s, ATTN_W), BF16),
        compiler_params=pltpu.CompilerParams(dimension_semantics=("arbitrary", "arbitrary"),
                                             vmem_limit_bytes=VMEM_LIMIT),
        name="attention",
    )(sink, q, kd, kd, kd, vd, vd, vd, kmeta, vmeta, bias)


def _outproj_kernel(attn_ref, hc_ref, hm_ref, cw_ref, cb_ref, lg_ref, lb_ref, x_ref, wa_ref, wc_ref, g2_ref,
                    rwh_ref, rwl_ref, rb_ref, tri_ref, h2_ref, xp_ref, route_ref, cnt_ref,
                    run_ref, pad_ref, y_ref, conv_ref, *, seq):
    step = pl.program_id(0)
    tiles_per_seq = seq // ROW_TILE
    part = step % tiles_per_seq

    @pl.when(step == 0)
    def _():
        run_ref[...] = jnp.zeros_like(run_ref)

    @pl.when(part == 0)
    def _():
        _conv_fill(pad_ref, hm_ref, hc_ref, seq)

    for c in range(ROW_TILE // CONV_ROWS):
        base = pl.multiple_of(part * ROW_TILE + c * CONV_ROWS, CONV_ROWS)
        conv_ref[c * CONV_ROWS:(c + 1) * CONV_ROWS, :] = _conv_chunk(pad_ref, y_ref, cw_ref, cb_ref, lg_ref,
                                                                     lb_ref, base)
    mix = (jnp.dot(attn_ref[...], wa_ref[...], preferred_element_type=F32)
           + jnp.dot(conv_ref[...], wc_ref[...], preferred_element_type=F32))
    h2 = x_ref[...] + mix
    h2_ref[...] = h2
    hn = _rms(h2, RMS_EPS) * g2_ref[...]
    hn_hi = hn.astype(BF16)
    half = D_MODEL // 2
    xp_ref[...] = _pack_pair(hn[:, :half], hn[:, half:])
    hn_lo = (hn - hn_hi.astype(F32)).astype(BF16)
    nt = (((1,), (1,)), ((), ()))
    logits = (lax.dot_general(rwh_ref[...], hn_hi, nt, preferred_element_type=F32)
              + lax.dot_general(rwh_ref[...], hn_lo, nt, preferred_element_type=F32)
              + lax.dot_general(rwl_ref[...], hn_hi, nt, preferred_element_type=F32)) + rb_ref[...]
    expert = lax.broadcasted_iota(jnp.int32, logits.shape, 0).astype(F32)
    work = logits
    vals, sels = [], []
    for r in range(TOP_K):
        m = jnp.max(work, axis=0, keepdims=True)
        idx = jnp.min(jnp.where(work == m, expert, float(N_EXPERTS)), axis=0, keepdims=True)
        sel = expert == idx
        work = jnp.where(sel, -jnp.inf, work)
        route_ref[r:r + 1, :] = idx
        vals.append(m)
        sels.append(sel)
    onehot = jnp.where(jnp.logical_or(jnp.logical_or(sels[0], sels[1]), jnp.logical_or(sels[2], sels[3])),
                       1.0, 0.0)
    exps = [jnp.exp(v - vals[0]) for v in vals]
    tot = exps[0] + exps[1] + exps[2] + exps[3]
    before = jnp.dot(onehot.astype(BF16), tri_ref[...], preferred_element_type=F32) + run_ref[...]
    for r in range(TOP_K):
        route_ref[TOP_K + r:TOP_K + r + 1, :] = exps[r] / tot
        route_ref[2 * TOP_K + r:2 * TOP_K + r + 1, :] = jnp.sum(jnp.where(sels[r], before, 0.0), axis=0,
                                                               keepdims=True)
    route_ref[3 * TOP_K:, :] = jnp.zeros((ROUTE_ROWS - 3 * TOP_K, logits.shape[1]), F32)
    run_ref[...] = run_ref[...] + jnp.sum(onehot, axis=1, keepdims=True)
    cnt_ref[...] = run_ref[...]


def _outproj(attn2d, hc, hc_meta, dw_w, dw_b, ln_g, ln_b, x2d, wa, wc, g2, rwh, rwl, rb, tri):
    n = x2d.shape[0]
    seq = hc.shape[1]
    tile = ROW_TILE
    const = lambda i: (0, 0)
    row = lambda i: (i, 0)
    half = D_MODEL // 2
    return pl.pallas_call(
        functools.partial(_outproj_kernel, seq=seq),
        grid=(n // tile,),
        in_specs=[
            pl.BlockSpec((tile, ATTN_W), row),
            pl.BlockSpec((1, seq, CONV_W), lambda i: (i // (seq // tile), 0, 0)),
            pl.BlockSpec((N_META, CONV_W), const),
            pl.BlockSpec((CONV_K, CONV_W), const),
            pl.BlockSpec((1, CONV_W), const),
            pl.BlockSpec((1, CONV_W), const),
            pl.BlockSpec((1, CONV_W), const),
            pl.BlockSpec((tile, D_MODEL), row),
            pl.BlockSpec((ATTN_W, D_MODEL), const),
            pl.BlockSpec((CONV_W, D_MODEL), lambda i: (ATTN_W // CONV_W, 0)),
            pl.BlockSpec((1, D_MODEL), const),
            pl.BlockSpec((N_EXPERTS, D_MODEL), const),
            pl.BlockSpec((N_EXPERTS, D_MODEL), const),
            pl.BlockSpec((N_EXPERTS, 1), const),
            pl.BlockSpec((tile, tile), const),
        ],
        out_specs=[
            pl.BlockSpec((tile, D_MODEL), row),
            pl.BlockSpec((tile, half), row),
            pl.BlockSpec((ROUTE_ROWS, tile), lambda i: (0, i)),
            pl.BlockSpec((N_EXPERTS, 1), const),
        ],
        out_shape=[
            jax.ShapeDtypeStruct((n, D_MODEL), F32),
            jax.ShapeDtypeStruct((n, half), jnp.uint32),
            jax.ShapeDtypeStruct((ROUTE_ROWS, n), F32),
            jax.ShapeDtypeStruct((N_EXPERTS, 1), F32),
        ],
        scratch_shapes=[pltpu.VMEM((N_EXPERTS, 1), F32),
                        pltpu.VMEM((N_META + seq + 32, CONV_W), F32),
                        pltpu.VMEM((CONV_ROWS, CONV_W), F32),
                        pltpu.VMEM((tile, CONV_W), BF16)],
        compiler_params=pltpu.CompilerParams(dimension_semantics=("arbitrary",),
                                             vmem_limit_bytes=VMEM_LIMIT),
        name="outproj",
    )(attn2d, hc, hc_meta, dw_w, dw_b, ln_g, ln_b, x2d, wa, wc, g2, rwh, rwl, rb, tri)


def _sc_mesh():
    return plsc.VectorSubcoreMesh(core_axis_name="c", subcore_axis_name="s",
                                  num_cores=SC_CORES, num_subcores=SC_SUBCORES)


def _sc_worker():
    return lax.axis_index("s") * SC_CORES + lax.axis_index("c")


def _sc_dispatch(pos3, xp, out_rows):
    n, half = xp.shape
    per_worker = n // SC_CHUNK // SC_WORKERS

    @functools.partial(
        pl.kernel, mesh=_sc_mesh(),
        out_type=jax.ShapeDtypeStruct((out_rows, half), jnp.uint32),
        scratch_types=[pltpu.VMEM((2, TOP_K, SC_CHUNK), jnp.int32),
                       pltpu.VMEM((2, SC_CHUNK, half), jnp.uint32),
                       pltpu.SemaphoreType.DMA((2,)),
                       pltpu.SemaphoreType.DMA],
        name="sc_dispatch")
    def run(pos_hbm, xp_hbm, xs_hbm, idx_v, rows_v, load_sem, scatter_sem):
        first = _sc_worker() * per_worker

        def load(j, buf):
            c = first + j
            return (pltpu.async_copy(pos_hbm.at[c], idx_v.at[buf], load_sem.at[buf]),
                    pltpu.async_copy(xp_hbm.at[pl.ds(c * SC_CHUNK, SC_CHUNK)], rows_v.at[buf], load_sem.at[buf]))

        loads = load(0, 0)
        for j in range(per_worker):
            buf = j % 2
            for cp in loads:
                cp.wait()
            if j + 1 < per_worker:
                loads = load(j + 1, 1 - buf)
            copies = [pltpu.async_copy(rows_v.at[buf], xs_hbm.at[idx_v.at[buf].at[k]], scatter_sem)
                      for k in range(TOP_K)]
            for cp in copies:
                cp.wait()

    return run(pos3, xp)


def _sc_collect(pos3, y):
    half = y.shape[1]
    n = pos3.shape[0] * SC_CHUNK
    per_worker = n // SC_CHUNK // SC_WORKERS

    @functools.partial(
        pl.kernel, mesh=_sc_mesh(),
        out_type=jax.ShapeDtypeStruct((TOP_K, n, half), jnp.uint32),
        scratch_types=[pltpu.VMEM((TOP_K, SC_CHUNK), jnp.int32),
                       pltpu.VMEM((2, SC_CHUNK, half), jnp.uint32),
                       pltpu.SemaphoreType.DMA((2,)),
                       pltpu.SemaphoreType.DMA((2,))],
        name="sc_collect")
    def run(pos_hbm, y_hbm, yg_hbm, idx_v, rows_v, gather_sem, store_sem):
        first = _sc_worker() * per_worker

        @pl.loop(0, per_worker)
        def _(j):
            c = first + j
            pltpu.sync_copy(pos_hbm.at[c], idx_v)

            def gather(k):
                return pltpu.async_copy(y_hbm.at[idx_v.at[k]], rows_v.at[k % 2], gather_sem.at[k % 2])

            gathers = [gather(0)]
            stores = []
            for k in range(TOP_K):
                if k + 1 < TOP_K:
                    if k >= 1:
                        stores[k - 1].wait()
                    gathers.append(gather(k + 1))
                gathers[k].wait()
                stores.append(pltpu.async_copy(rows_v.at[k % 2], yg_hbm.at[k, pl.ds(c * SC_CHUNK, SC_CHUNK)],
                                               store_sem.at[k % 2]))
            stores[TOP_K - 2].wait()
            stores[TOP_K - 1].wait()

    return run(pos3, y)


S_TILE, S_EXPERT, S_ROWS, S_NEW, S_SLOT, S_NEXT = range(6)


def _expert_kernel(sched_ref, xs_ref, wgu_hbm, bgu_ref, wd_hbm, bd_ref, y_ref,
                   act_ref, wgu_buf, wd_buf, sem):
    w = pl.program_id(0)
    e = sched_ref[S_EXPERT, w]
    slot = sched_ref[S_SLOT, w]
    n_rows = sched_ref[S_ROWS, w]

    def weight_copies(expert, s):
        return (pltpu.make_async_copy(wgu_hbm.at[expert], wgu_buf.at[s], sem.at[0, s]),
                pltpu.make_async_copy(wd_hbm.at[expert], wd_buf.at[s], sem.at[1, s]))

    @pl.when(sched_ref[S_NEW, w] == 1)
    def _():
        @pl.when(w == 0)
        def _():
            for cp in weight_copies(e, slot):
                cp.start()

        nxt = sched_ref[S_NEXT, w]

        @pl.when(nxt >= 0)
        def _():
            for cp in weight_copies(nxt, 1 - slot):
                cp.start()

        for cp in weight_copies(e, slot):
            cp.wait()

    def tile_path(m):
        half = D_MODEL // 2
        live = lax.broadcasted_iota(jnp.int32, (m, 1), 0) < n_rows
        x_lo, x_hi = _unpack_pair(jnp.where(live, xs_ref[0:m, :], jnp.uint32(0)))
        x = jnp.concatenate([x_lo.astype(BF16), x_hi.astype(BF16)], axis=1)
        for c in range(D_FF // FF_CHUNK):
            gc = slice(c * FF_CHUNK, (c + 1) * FF_CHUNK)
            uc = slice(D_FF + c * FF_CHUNK, D_FF + (c + 1) * FF_CHUNK)
            gate = jnp.dot(x, wgu_buf[slot, :, gc].astype(BF16), preferred_element_type=F32) + bgu_ref[0, :, gc]
            up = jnp.dot(x, wgu_buf[slot, :, uc].astype(BF16), preferred_element_type=F32) + bgu_ref[0, :, uc]
            gate = jnp.minimum(gate, SWIGLU_LIMIT)
            up = jnp.clip(up, -SWIGLU_LIMIT, SWIGLU_LIMIT)
            act_ref[0:m, gc] = (gate * jax.nn.sigmoid(SWIGLU_ALPHA * gate) * (up + 1.0)).astype(BF16)
        for c in range(half // FF_CHUNK):
            lo_c = slice(c * FF_CHUNK, (c + 1) * FF_CHUNK)
            hi_c = slice(half + c * FF_CHUNK, half + (c + 1) * FF_CHUNK)
            y_lo = (jnp.dot(act_ref[0:m, :], wd_buf[slot, :, lo_c].astype(BF16), preferred_element_type=F32)
                    + bd_ref[0, :, lo_c])
            y_hi = (jnp.dot(act_ref[0:m, :], wd_buf[slot, :, hi_c].astype(BF16), preferred_element_type=F32)
                    + bd_ref[0, :, hi_c])
            y_ref[0:m, lo_c] = _pack_pair(y_lo, y_hi)

    for below, m in zip((0,) + TILE_SPANS[:-1], TILE_SPANS):
        pl.when(jnp.logical_and(n_rows > below, n_rows <= m))(functools.partial(tile_path, m))


def _experts(sched, xs, wgu, bgu, wd, bd):
    rows, half = xs.shape
    n_work = sched.shape[1]
    grid_spec = pltpu.PrefetchScalarGridSpec(
        num_scalar_prefetch=1,
        grid=(n_work,),
        in_specs=[
            pl.BlockSpec((EXPERT_TILE, half), lambda w, sc: (sc[S_TILE, w], 0)),
            pl.BlockSpec(memory_space=pl.ANY),
            pl.BlockSpec((1, 1, 2 * D_FF), lambda w, sc: (sc[S_EXPERT, w], 0, 0)),
            pl.BlockSpec(memory_space=pl.ANY),
            pl.BlockSpec((1, 1, D_MODEL), lambda w, sc: (sc[S_EXPERT, w], 0, 0)),
        ],
        out_specs=pl.BlockSpec((EXPERT_TILE, half), lambda w, sc: (sc[S_TILE, w], 0)),
        scratch_shapes=[pltpu.VMEM((EXPERT_TILE, D_FF), BF16),
                        pltpu.VMEM((2, D_MODEL, 2 * D_FF), F32),
                        pltpu.VMEM((2, D_FF, D_MODEL), F32),
                        pltpu.SemaphoreType.DMA((2, 2))],
    )
    return pl.pallas_call(
        _expert_kernel,
        grid_spec=grid_spec,
        out_shape=jax.ShapeDtypeStruct((rows, half), jnp.uint32),
        compiler_params=pltpu.CompilerParams(dimension_semantics=("arbitrary",),
                                             vmem_limit_bytes=VMEM_LIMIT),
        name="experts",
    )(sched, xs, wgu, bgu, wd, bd)


def _unpack_pair(packed):
    lo = lax.bitcast_convert_type(packed << 16, F32)
    hi = lax.bitcast_convert_type(packed & jnp.uint32(0xFFFF0000), F32)
    return lo, hi


def _pack_pair(lo, hi):
    lo_bits = lax.bitcast_convert_type(lo.astype(BF16).astype(F32), jnp.uint32)
    hi_bits = lax.bitcast_convert_type(hi.astype(BF16).astype(F32), jnp.uint32)
    return (lo_bits >> 16) | (hi_bits & jnp.uint32(0xFFFF0000))


def _combine_kernel(yg_ref, h2_ref, route_ref, o_ref):
    half = D_MODEL // 2
    by_token = jnp.transpose(route_ref[...])
    lo_sum = h2_ref[:, :half]
    hi_sum = h2_ref[:, half:]
    for k in range(TOP_K):
        gate = by_token[:, TOP_K + k:TOP_K + k + 1]
        lo, hi = _unpack_pair(yg_ref[k])
        lo_sum = lo_sum + gate * lo
        hi_sum = hi_sum + gate * hi
    o_ref[:, :half] = lo_sum
    o_ref[:, half:] = hi_sum


def _combine(yg, h2, route, part):
    n = h2.shape[0]
    half = D_MODEL // 2
    steps = yg.shape[1] // ROW_TILE
    row = lambda i: (i + part * steps, 0)
    return pl.pallas_call(
        _combine_kernel,
        grid=(steps,),
        in_specs=[
            pl.BlockSpec((TOP_K, ROW_TILE, half), lambda i: (0, i, 0)),
            pl.BlockSpec((ROW_TILE, D_MODEL), row),
            pl.BlockSpec((2 * TOP_K, ROW_TILE), lambda i: (0, i + part * steps)),
        ],
        out_specs=pl.BlockSpec((ROW_TILE, D_MODEL), row),
        out_shape=jax.ShapeDtypeStruct((n, D_MODEL), F32),
        input_output_aliases={1: 0},
        compiler_params=pltpu.CompilerParams(dimension_semantics=("arbitrary",),
                                             vmem_limit_bytes=VMEM_LIMIT),
        name="combine",
    )(yg, h2, route)


def _work_schedule(counts, n_rows):
    n_work = n_rows // EXPERT_TILE + N_EXPERTS
    tiles = (counts + (EXPERT_TILE - 1)) // EXPERT_TILE
    tile_end = jnp.cumsum(tiles)
    tile_start = tile_end - tiles
    total = tile_end[-1]
    w = jnp.arange(n_work, dtype=jnp.int32)
    wc = jnp.minimum(w, total - 1)
    expert_of = jnp.sum((tile_end[None, :] <= wc[:, None]).astype(jnp.int32), axis=1)
    mine = expert_of[:, None] == jnp.arange(N_EXPERTS, dtype=jnp.int32)[None, :]
    pick = lambda table: jnp.sum(jnp.where(mine, table[None, :], 0), axis=1)
    live = jnp.clip(pick(counts) - (wc - pick(tile_start)) * EXPERT_TILE, 0, EXPERT_TILE)
    live = jnp.where(w < total, live, 0)
    prev_e = jnp.concatenate([jnp.full((1,), -1, jnp.int32), expert_of[:-1]])
    new_e = jnp.logical_and(w < total, expert_of != prev_e)
    slot = (jnp.cumsum(new_e.astype(jnp.int32)) - 1) % 2
    later = jnp.logical_and(new_e[None, :], w[None, :] > w[:, None])
    next_w = jnp.min(jnp.where(later, w[None, :], n_work), axis=1)
    next_e = jnp.sum(jnp.where(next_w[:, None] == w[None, :], expert_of[None, :], 0), axis=1)
    next_e = jnp.where(next_w < n_work, next_e, -1)
    sched = jnp.stack([wc, expert_of, live, new_e.astype(jnp.int32), slot, next_e]).astype(jnp.int32)
    return sched, (tile_start * EXPERT_TILE).astype(jnp.int32)


def kernel(x, meta_tokens, norm1_g, w_in, q_norm_g, k_norm_g, attn_sink, dw_w, dw_b, conv_ln_g,
           conv_ln_b, w_out, norm2_g, router_w, router_b, w_gate_up, b_gate_up, w_down, b_down):
    assert norm1_g.shape[0] == 1, "single-layer trunk: meta-token query rows are not materialised"
    b, s, d = x.shape
    n = b * s
    x2d = x.reshape(n, d)

    scale = HEAD_DIM ** -0.5
    qg = (jnp.tile(q_norm_g[0], N_Q_HEADS) * scale).reshape(1, ATTN_W)
    kg = jnp.tile(k_norm_g[0], N_KV_HEADS).reshape(1, KV_W)
    head_of = np.arange(ATTN_W) // HEAD_DIM
    pq = jnp.asarray((head_of[:, None] == head_of[None, :]) / HEAD_DIM, BF16)
    pk = pq[:KV_W, :KV_W]
    g1 = norm1_g[0].reshape(1, d)
    w_in_b = w_in[0].astype(BF16)

    q, kd, vd, hc = _inproj(x2d, g1, w_in_b, qg, kg, pq, pk, ROW_TILE)
    _, kmeta, vmeta, hc_meta = _inproj(meta_tokens, g1, w_in_b, qg, kg, pq, pk, N_META)

    meta_pad = ((0, BLOCK - N_META), (0, 0))
    attn = _attention(attn_sink[0], q.reshape(b, s, ATTN_W), kd.reshape(b, s, 2 * KV_W),
                      vd.reshape(b, s, 2 * KV_W), jnp.pad(kmeta, meta_pad), jnp.pad(vmeta, meta_pad),
                      jnp.asarray(_alibi_bias()))

    w_out_b = w_out[0].astype(BF16)
    rw = router_w[0].T
    rwh = rw.astype(BF16)
    rwl = (rw - rwh.astype(F32)).astype(BF16)
    rb = router_b[0].reshape(N_EXPERTS, 1)
    tri = jnp.asarray(np.triu(np.ones((ROW_TILE, ROW_TILE), np.float32), 1), BF16)
    h2, xp, route, counts = _outproj(attn.reshape(n, ATTN_W), hc.reshape(b, s, CONV_W), hc_meta,
                                     dw_w[0].reshape(CONV_K, CONV_W), dw_b[0].reshape(1, CONV_W),
                                     conv_ln_g[0].reshape(1, CONV_W), conv_ln_b[0].reshape(1, CONV_W), x2d,
                                     w_out_b, w_out_b, norm2_g[0].reshape(1, d),
                                     rwh, rwl, rb, tri)

    counts_i = counts[:, 0].astype(jnp.int32)
    idx = route[:TOP_K].astype(jnp.int32)
    rank = route[2 * TOP_K:3 * TOP_K].astype(jnp.int32)
    sched, first_row = _work_schedule(counts_i, n * TOP_K)
    chosen = idx[:, :, None] == jnp.arange(N_EXPERTS, dtype=jnp.int32)
    pos = rank + jnp.sum(jnp.where(chosen, first_row, 0), axis=-1)
    pos3 = pos.reshape(TOP_K, n // SC_CHUNK, SC_CHUNK).transpose(1, 0, 2)

    xs = _sc_dispatch(pos3, xp, sched.shape[1] * EXPERT_TILE)
    y = _experts(sched, xs, w_gate_up[0],
                 b_gate_up[0].reshape(N_EXPERTS, 1, 2 * D_FF), w_down[0],
                 b_down[0].reshape(N_EXPERTS, 1, D_MODEL))
    out = h2
    chunks = pos3.shape[0] // COMBINE_PARTS
    for part in range(COMBINE_PARTS):
        out = _combine(_sc_collect(pos3[part * chunks:(part + 1) * chunks], y), out, route, part)
    return out.reshape(b, s, d)
```

```python
import functools

import numpy as np
import jax
import jax.numpy as jnp
from jax import lax
from jax.experimental import pallas as pl
from jax.experimental.pallas import tpu as pltpu
from jax.experimental.pallas import tpu_sc as plsc

F32 = jnp.float32
BF16 = jnp.bfloat16

D_MODEL = 1024
N_META = 16
HEAD_DIM = 64
N_Q_HEADS = 8
N_KV_HEADS = 2
GROUP = N_Q_HEADS // N_KV_HEADS
ATTN_W = N_Q_HEADS * HEAD_DIM
KV_W = N_KV_HEADS * HEAD_DIM
CONV_W = D_MODEL - ATTN_W
IN_W = ATTN_W + 2 * KV_W + 2 * CONV_W
WINDOW = 128
BLOCK = 128
CONV_K = 31
CONV_PAD = CONV_K // 2
N_EXPERTS = 32
TOP_K = 4
D_FF = D_MODEL
SWIGLU_LIMIT = 7.0
SWIGLU_ALPHA = 1.702
RMS_EPS = 1e-6
LN_EPS = 1e-5
NEG_INF = -1e30
LOG2_E = 1.4426950408889634

LANES = 128
ROW_TILE = 1024
EXPERT_TILE = 1024
SC_CORES = 2
SC_SUBCORES = 16
SC_WORKERS = SC_CORES * SC_SUBCORES
SC_CHUNK = 64
CONV_ROWS = 128
FF_CHUNK = 512
TILE_SPANS = (128, 256, 512, 768, 1024)
Q_BLOCKS = 16
COMBINE_PARTS = 8
ROUTE_ROWS = 16
VMEM_LIMIT = 56 * 1024 * 1024


def _rms(x, eps):
    return x * lax.rsqrt(jnp.mean(x * x, axis=-1, keepdims=True) + eps)


def _inproj_kernel(x_ref, g1_ref, w_ref, qg_ref, kg_ref, pq_ref, pk_ref,
                   q_ref, k_ref, v_ref, hc_ref):
    x = x_ref[...]
    n = (_rms(x, RMS_EPS) * g1_ref[...]).astype(BF16)
    proj = jnp.dot(n, w_ref[...], preferred_element_type=F32)
    q = proj[:, :ATTN_W]
    k = proj[:, ATTN_W:ATTN_W + KV_W]
    v = proj[:, ATTN_W + KV_W:ATTN_W + 2 * KV_W]
    a = proj[:, ATTN_W + 2 * KV_W:ATTN_W + 2 * KV_W + CONV_W]
    g = proj[:, ATTN_W + 2 * KV_W + CONV_W:]
    qms = jnp.dot((q * q).astype(BF16), pq_ref[...], preferred_element_type=F32)
    kms = jnp.dot((k * k).astype(BF16), pk_ref[...], preferred_element_type=F32)
    q_ref[...] = (q * lax.rsqrt(qms + RMS_EPS) * qg_ref[...]).astype(BF16)
    kn = k * lax.rsqrt(kms + RMS_EPS) * kg_ref[...]
    lo = lax.broadcasted_iota(jnp.int32, kn.shape, 1) < HEAD_DIM
    ksw = pltpu.roll(kn, HEAD_DIM, 1)
    k_ref[...] = jnp.concatenate([jnp.where(lo, kn, ksw), jnp.where(lo, ksw, kn)], axis=1).astype(BF16)
    vsw = pltpu.roll(v, HEAD_DIM, 1)
    v_ref[...] = jnp.concatenate([jnp.where(lo, v, vsw), jnp.where(lo, vsw, v)], axis=1).astype(BF16)
    hc_ref[...] = a * jax.nn.sigmoid(g)


def _inproj(x2d, g1, w_in_b, qg, kg, pq, pk, tile):
    n = x2d.shape[0]
    const = lambda i: (0, 0)
    row = lambda i: (i, 0)
    return pl.pallas_call(
        _inproj_kernel,
        grid=(n // tile,),
        in_specs=[
            pl.BlockSpec((tile, D_MODEL), row),
            pl.BlockSpec((1, D_MODEL), const),
            pl.BlockSpec((D_MODEL, IN_W), const),
            pl.BlockSpec((1, ATTN_W), const),
            pl.BlockSpec((1, KV_W), const),
            pl.BlockSpec((ATTN_W, ATTN_W), const),
            pl.BlockSpec((KV_W, KV_W), const),
        ],
        out_specs=[
            pl.BlockSpec((tile, ATTN_W), row),
            pl.BlockSpec((tile, 2 * KV_W), row),
            pl.BlockSpec((tile, 2 * KV_W), row),
            pl.BlockSpec((tile, CONV_W), row),
        ],
        out_shape=[
            jax.ShapeDtypeStruct((n, ATTN_W), BF16),
            jax.ShapeDtypeStruct((n, 2 * KV_W), BF16),
            jax.ShapeDtypeStruct((n, 2 * KV_W), BF16),
            jax.ShapeDtypeStruct((n, CONV_W), F32),
        ],
        compiler_params=pltpu.CompilerParams(dimension_semantics=("arbitrary",),
                                             vmem_limit_bytes=VMEM_LIMIT),
        name="inproj",
    )(x2d, g1, w_in_b, qg, kg, pq, pk)


def _conv_fill(pad_ref, hm_ref, hc_ref, seq):
    tail = pad_ref.shape[0] - (N_META + seq)
    pad_ref[0:N_META, :] = hm_ref[...]
    pad_ref[N_META:N_META + seq, :] = hc_ref[0]
    pad_ref[N_META + seq:, :] = jnp.zeros((tail, CONV_W), F32)


def _conv_chunk(pad_ref, y_ref, w_ref, b_ref, lg_ref, lb_ref, base):
    first = N_META - CONV_PAD
    span = CONV_ROWS + 8
    for lt in range(CONV_W // LANES):
        ls = slice(lt * LANES, (lt + 1) * LANES)
        win = pad_ref[pl.ds(base, CONV_ROWS + 32), ls]
        acc = None
        for sub in range(8):
            part = None
            for al in range(4):
                k = 8 * al + sub - first
                if 0 <= k < CONV_K:
                    term = win[8 * al:8 * al + span] * w_ref[k:k + 1, ls]
                    part = term if part is None else part + term
            if sub:
                part = pltpu.roll(part, span - sub, 0)
            acc = part[:CONV_ROWS] if acc is None else acc + part[:CONV_ROWS]
        y_ref[:, ls] = acc
    y = y_ref[...] + b_ref[...]
    mu = jnp.mean(y, axis=-1, keepdims=True)
    yc = y - mu
    var = jnp.mean(yc * yc, axis=-1, keepdims=True)
    z = yc * lax.rsqrt(var + LN_EPS) * lg_ref[...] + lb_ref[...]
    return (z * jax.nn.sigmoid(z)).astype(BF16)


def _alibi_bias():
    qi = np.arange(BLOCK)[:, None]
    kj = np.arange(BLOCK)[None, :]
    dists = [qi + BLOCK - kj, np.abs(qi - kj), kj + BLOCK - qi]
    out = np.zeros((N_KV_HEADS, 4, GROUP * BLOCK, BLOCK), np.float32)
    for h in range(N_KV_HEADS):
        for g in range(GROUP):
            slope = LOG2_E * 2.0 ** (-8.0 * (h * GROUP + g + 1) / N_Q_HEADS)
            for p, d in enumerate(dists):
                out[h, p, g * BLOCK:(g + 1) * BLOCK] = np.where(d <= WINDOW, -slope * d, NEG_INF)
    out[:, 3, :, N_META:] = NEG_INF
    return out


def _attn_kernel(sink_ref, q_ref, kp_ref, kc_ref, kn_ref, vp_ref, vc_ref, vn_ref,
                 km_ref, vm_ref, bias_ref, o_ref, *, n_steps):
    i = pl.program_id(1)
    lo = lax.broadcasted_iota(jnp.int32, (BLOCK, LANES), 1) < HEAD_DIM
    edge_first = jnp.where(i == 0, NEG_INF, 0.0).astype(F32)
    edge_last = jnp.where(i == n_steps - 1, NEG_INF, 0.0).astype(F32)
    nt = (((1,), (1,)), ((), ()))
    zero = jnp.zeros((BLOCK, LANES), BF16)

    def key_block(before_ref, here_ref, after_ref, idx, ks):
        if idx < 0:
            return before_ref[0, :, ks]
        if idx >= Q_BLOCKS:
            return after_ref[0, :, ks]
        return here_ref[0, idx * BLOCK:(idx + 1) * BLOCK, ks]

    for qb in range(Q_BLOCKS):
        qr = slice(qb * BLOCK, (qb + 1) * BLOCK)
        for h in range(N_KV_HEADS):
            ks = slice(h * LANES, (h + 1) * LANES)
            rows = []
            for j in range(2):
                pair = q_ref[0, qr, (2 * h + j) * LANES:(2 * h + j + 1) * LANES]
                rows.append(jnp.where(lo, pair, zero))
                rows.append(jnp.where(lo, zero, pair))
            qs = jnp.concatenate(rows, axis=0)
            k_p, k_c, k_n = (key_block(kp_ref, kc_ref, kn_ref, qb + rel, ks) for rel in (-1, 0, 1))
            v_p, v_c, v_n = (key_block(vp_ref, vc_ref, vn_ref, qb + rel, ks) for rel in (-1, 0, 1))
            s_p = lax.dot_general(qs, k_p, nt, preferred_element_type=F32) + bias_ref[h, 0]
            s_c = lax.dot_general(qs, k_c, nt, preferred_element_type=F32) + bias_ref[h, 1]
            s_n = lax.dot_general(qs, k_n, nt, preferred_element_type=F32) + bias_ref[h, 2]
            s_m = lax.dot_general(qs, km_ref[:, ks], nt, preferred_element_type=F32) + bias_ref[h, 3]
            if qb == 0:
                s_p = s_p + edge_first
            if qb == Q_BLOCKS - 1:
                s_n = s_n + edge_last
            sink = jnp.concatenate(
                [jnp.full((BLOCK, 1), sink_ref[h * GROUP + g], F32) for g in range(GROUP)], axis=0)
            m = jnp.max(jnp.maximum(jnp.maximum(s_p, s_c), jnp.maximum(s_n, s_m)), axis=-1, keepdims=True)
            m = jnp.maximum(m, sink)
            p_p = jnp.exp2(s_p - m)
            p_c = jnp.exp2(s_c - m)
            p_n = jnp.exp2(s_n - m)
            p_m = jnp.exp2(s_m - m)
            denom = jnp.sum((p_p + p_c) + (p_n + p_m), axis=-1, keepdims=True) + jnp.exp2(sink - m)
            o = (jnp.dot(p_p.astype(BF16), v_p, preferred_element_type=F32)
                 + jnp.dot(p_c.astype(BF16), v_c, preferred_element_type=F32)
                 + jnp.dot(p_n.astype(BF16), v_n, preferred_element_type=F32)
                 + jnp.dot(p_m.astype(BF16), vm_ref[:, ks], preferred_element_type=F32))
            o = o / denom
            for j in range(2):
                even = o[(2 * j) * BLOCK:(2 * j + 1) * BLOCK]
                odd = o[(2 * j + 1) * BLOCK:(2 * j + 2) * BLOCK]
                o_ref[0, qr, (2 * h + j) * LANES:(2 * h + j + 1) * LANES] = jnp.where(lo, even, odd).astype(BF16)


def _attention(sink, q, kd, vd, kmeta, vmeta, bias):
    b, s, _ = q.shape
    nb = s // BLOCK
    steps = nb // Q_BLOCKS
    here = lambda bi, i: (bi, i, 0)
    before = lambda bi, i: (bi, jnp.maximum(Q_BLOCKS * i - 1, 0), 0)
    after = lambda bi, i: (bi, jnp.minimum(Q_BLOCKS * i + Q_BLOCKS, nb - 1), 0)
    const2 = lambda bi, i: (0, 0)
    edge_blk = (1, BLOCK, 2 * KV_W)
    here_blk = (1, Q_BLOCKS * BLOCK, 2 * KV_W)
    return pl.pallas_call(
        functools.partial(_attn_kernel, n_steps=steps),
        grid=(b, steps),
        in_specs=[
            pl.BlockSpec(memory_space=pltpu.SMEM),
            pl.BlockSpec((1, Q_BLOCKS * BLOCK, ATTN_W), here),
            pl.BlockSpec(edge_blk, before), pl.BlockSpec(here_blk, here), pl.BlockSpec(edge_blk, after),
            pl.BlockSpec(edge_blk, before), pl.BlockSpec(here_blk, here), pl.BlockSpec(edge_blk, after),
            pl.BlockSpec((BLOCK, 2 * KV_W), const2),
            pl.BlockSpec((BLOCK, 2 * KV_W), const2),
            pl.BlockSpec(bias.shape, lambda bi, i: (0, 0, 0, 0)),
        ],
        out_specs=pl.BlockSpec((1, Q_BLOCKS * BLOCK, ATTN_W), here),
        out_shape=jax.ShapeDtypeStruct((b, s, ATTN_W), BF16),
        compiler_params=pltpu.CompilerParams(dimension_semantics=("arbitrary", "arbitrary"),
                                             vmem_limit_bytes=VMEM_LIMIT),
        name="attention",
    )(sink, q, kd, kd, kd, vd, vd, vd, kmeta, vmeta, bias)


def _outproj_kernel(attn_ref, hc_ref, hm_ref, cw_ref, cb_ref, lg_ref, lb_ref, x_ref, wa_ref, wc_ref, g2_ref,
                    rwh_ref, rwl_ref, rb_ref, tri_ref, h2_ref, xp_ref, route_ref, cnt_ref,
                    run_ref, pad_ref, y_ref, conv_ref, *, seq):
    step = pl.program_id(0)
    tiles_per_seq = seq // ROW_TILE
    part = step % tiles_per_seq

    @pl.when(step == 0)
    def _():
        run_ref[...] = jnp.zeros_like(run_ref)

    @pl.when(part == 0)
    def _():
        _conv_fill(pad_ref, hm_ref, hc_ref, seq)

    for c in range(ROW_TILE // CONV_ROWS):
        base = pl.multiple_of(part * ROW_TILE + c * CONV_ROWS, CONV_ROWS)
        conv_ref[c * CONV_ROWS:(c + 1) * CONV_ROWS, :] = _conv_chunk(pad_ref, y_ref, cw_ref, cb_ref, lg_ref,
                                                                     lb_ref, base)
    mix = (jnp.dot(attn_ref[...], wa_ref[...], preferred_element_type=F32)
           + jnp.dot(conv_ref[...], wc_ref[...], preferred_element_type=F32))
    h2 = x_ref[...] + mix
    h2_ref[...] = h2
    hn = _rms(h2, RMS_EPS) * g2_ref[...]
    hn_hi = hn.astype(BF16)
    half = D_MODEL // 2
    xp_ref[...] = _pack_pair(hn[:, :half], hn[:, half:])
    hn_lo = (hn - hn_hi.astype(F32)).astype(BF16)
    nt = (((1,), (1,)), ((), ()))
    logits = (lax.dot_general(rwh_ref[...], hn_hi, nt, preferred_element_type=F32)
              + lax.dot_general(rwh_ref[...], hn_lo, nt, preferred_element_type=F32)
              + lax.dot_general(rwl_ref[...], hn_hi, nt, preferred_element_type=F32)) + rb_ref[...]
    expert = lax.broadcasted_iota(jnp.int32, logits.shape, 0).astype(F32)
    work = logits
    vals, sels = [], []
    for r in range(TOP_K):
        m = jnp.max(work, axis=0, keepdims=True)
        idx = jnp.min(jnp.where(work == m, expert, float(N_EXPERTS)), axis=0, keepdims=True)
        sel = expert == idx
        work = jnp.where(sel, -jnp.inf, work)
        route_ref[r:r + 1, :] = idx
        vals.append(m)
        sels.append(sel)
    onehot = jnp.where(jnp.logical_or(jnp.logical_or(sels[0], sels[1]), jnp.logical_or(sels[2], sels[3])),
                       1.0, 0.0)
    exps = [jnp.exp(v - vals[0]) for v in vals]
    tot = exps[0] + exps[1] + exps[2] + exps[3]
    before = jnp.dot(onehot.astype(BF16), tri_ref[...], preferred_element_type=F32) + run_ref[...]
    for r in range(TOP_K):
        route_ref[TOP_K + r:TOP_K + r + 1, :] = exps[r] / tot
        route_ref[2 * TOP_K + r:2 * TOP_K + r + 1, :] = jnp.sum(jnp.where(sels[r], before, 0.0), axis=0,
                                                               keepdims=True)
    route_ref[3 * TOP_K:, :] = jnp.zeros((ROUTE_ROWS - 3 * TOP_K, logits.shape[1]), F32)
    run_ref[...] = run_ref[...] + jnp.sum(onehot, axis=1, keepdims=True)
    cnt_ref[...] = run_ref[...]


def _outproj(attn2d, hc, hc_meta, dw_w, dw_b, ln_g, ln_b, x2d, wa, wc, g2, rwh, rwl, rb, tri):
    n = x2d.shape[0]
    seq = hc.shape[1]
    tile = ROW_TILE
    const = lambda i: (0, 0)
    row = lambda i: (i, 0)
    half = D_MODEL // 2
    return pl.pallas_call(
        functools.partial(_outproj_kernel, seq=seq),
        grid=(n // tile,),
        in_specs=[
            pl.BlockSpec((tile, ATTN_W), row),
            pl.BlockSpec((1, seq, CONV_W), lambda i: (i // (seq // tile), 0, 0)),
            pl.BlockSpec((N_META, CONV_W), const),
            pl.BlockSpec((CONV_K, CONV_W), const),
            pl.BlockSpec((1, CONV_W), const),
            pl.BlockSpec((1, CONV_W), const),
            pl.BlockSpec((1, CONV_W), const),
            pl.BlockSpec((tile, D_MODEL), row),
            pl.BlockSpec((ATTN_W, D_MODEL), const),
            pl.BlockSpec((CONV_W, D_MODEL), lambda i: (ATTN_W // CONV_W, 0)),
            pl.BlockSpec((1, D_MODEL), const),
            pl.BlockSpec((N_EXPERTS, D_MODEL), const),
            pl.BlockSpec((N_EXPERTS, D_MODEL), const),
            pl.BlockSpec((N_EXPERTS, 1), const),
            pl.BlockSpec((tile, tile), const),
        ],
        out_specs=[
            pl.BlockSpec((tile, D_MODEL), row),
            pl.BlockSpec((tile, half), row),
            pl.BlockSpec((ROUTE_ROWS, tile), lambda i: (0, i)),
            pl.BlockSpec((N_EXPERTS, 1), const),
        ],
        out_shape=[
            jax.ShapeDtypeStruct((n, D_MODEL), F32),
            jax.ShapeDtypeStruct((n, half), jnp.uint32),
            jax.ShapeDtypeStruct((ROUTE_ROWS, n), F32),
            jax.ShapeDtypeStruct((N_EXPERTS, 1), F32),
        ],
        scratch_shapes=[pltpu.VMEM((N_EXPERTS, 1), F32),
                        pltpu.VMEM((N_META + seq + 32, CONV_W), F32),
                        pltpu.VMEM((CONV_ROWS, CONV_W), F32),
                        pltpu.VMEM((tile, CONV_W), BF16)],
        compiler_params=pltpu.CompilerParams(dimension_semantics=("arbitrary",),
                                             vmem_limit_bytes=VMEM_LIMIT),
        name="outproj",
    )(attn2d, hc, hc_meta, dw_w, dw_b, ln_g, ln_b, x2d, wa, wc, g2, rwh, rwl, rb, tri)


def _sc_mesh():
    return plsc.VectorSubcoreMesh(core_axis_name="c", subcore_axis_name="s",
                                  num_cores=SC_CORES, num_subcores=SC_SUBCORES)


def _sc_worker():
    return lax.axis_index("s") * SC_CORES + lax.axis_index("c")


def _sc_dispatch(pos3, xp, out_rows):
    n, half = xp.shape
    per_worker = n // SC_CHUNK // SC_WORKERS

    @functools.partial(
        pl.kernel, mesh=_sc_mesh(),
        out_type=jax.ShapeDtypeStruct((out_rows, half), jnp.uint32),
        scratch_types=[pltpu.VMEM((2, TOP_K, SC_CHUNK), jnp.int32),
                       pltpu.VMEM((2, SC_CHUNK, half), jnp.uint32),
                       pltpu.SemaphoreType.DMA((2,)),
                       pltpu.SemaphoreType.DMA],
        name="sc_dispatch")
    def run(pos_hbm, xp_hbm, xs_hbm, idx_v, rows_v, load_sem, scatter_sem):
        first = _sc_worker() * per_worker

        def load(j, buf):
            c = first + j
            return (pltpu.async_copy(pos_hbm.at[c], idx_v.at[buf], load_sem.at[buf]),
                    pltpu.async_copy(xp_hbm.at[pl.ds(c * SC_CHUNK, SC_CHUNK)], rows_v.at[buf], load_sem.at[buf]))

        loads = load(0, 0)
        for j in range(per_worker):
            buf = j % 2
            for cp in loads:
                cp.wait()
            if j + 1 < per_worker:
                loads = load(j + 1, 1 - buf)
            copies = [pltpu.async_copy(rows_v.at[buf], xs_hbm.at[idx_v.at[buf].at[k]], scatter_sem)
                      for k in range(TOP_K)]
            for cp in copies:
                cp.wait()

    return run(pos3, xp)


def _sc_collect(pos3, y):
    half = y.shape[1]
    n = pos3.shape[0] * SC_CHUNK
    per_worker = n // SC_CHUNK // SC_WORKERS

    @functools.partial(
        pl.kernel, mesh=_sc_mesh(),
        out_type=jax.ShapeDtypeStruct((TOP_K, n, half), jnp.uint32),
        scratch_types=[pltpu.VMEM((TOP_K, SC_CHUNK), jnp.int32),
                       pltpu.VMEM((2, SC_CHUNK, half), jnp.uint32),
                       pltpu.SemaphoreType.DMA((2,)),
                       pltpu.SemaphoreType.DMA((2,))],
        name="sc_collect")
    def run(pos_hbm, y_hbm, yg_hbm, idx_v, rows_v, gather_sem, store_sem):
        first = _sc_worker() * per_worker

        @pl.loop(0, per_worker)
        def _(j):
            c = first + j
            pltpu.sync_copy(pos_hbm.at[c], idx_v)

            def gather(k):
                return pltpu.async_copy(y_hbm.at[idx_v.at[k]], rows_v.at[k % 2], gather_sem.at[k % 2])

            gathers = [gather(0)]
            stores = []
            for k in range(TOP_K):
                if k + 1 < TOP_K:
                    if k >= 1:
                        stores[k - 1].wait()
                    gathers.append(gather(k + 1))
                gathers[k].wait()
                stores.append(pltpu.async_copy(rows_v.at[k % 2], yg_hbm.at[k, pl.ds(c * SC_CHUNK, SC_CHUNK)],
                                               store_sem.at[k % 2]))
            stores[TOP_K - 2].wait()
            stores[TOP_K - 1].wait()

    return run(pos3, y)


S_TILE, S_EXPERT, S_ROWS, S_NEW, S_SLOT, S_NEXT = range(6)


def _expert_kernel(sched_ref, xs_ref, wgu_hbm, bgu_ref, wd_hbm, bd_ref, y_ref,
                   act_ref, wgu_buf, wd_buf, sem):
    w = pl.program_id(0)
    e = sched_ref[S_EXPERT, w]
    slot = sched_ref[S_SLOT, w]
    n_rows = sched_ref[S_ROWS, w]

    def weight_copies(expert, s):
        return (pltpu.make_async_copy(wgu_hbm.at[expert], wgu_buf.at[s], sem.at[0, s]),
                pltpu.make_async_copy(wd_hbm.at[expert], wd_buf.at[s], sem.at[1, s]))

    @pl.when(sched_ref[S_NEW, w] == 1)
    def _():
        @pl.when(w == 0)
        def _():
            for cp in weight_copies(e, slot):
                cp.start()

        nxt = sched_ref[S_NEXT, w]

        @pl.when(nxt >= 0)
        def _():
            for cp in weight_copies(nxt, 1 - slot):
                cp.start()

        for cp in weight_copies(e, slot):
            cp.wait()

    def tile_path(m):
        half = D_MODEL // 2
        live = lax.broadcasted_iota(jnp.int32, (m, 1), 0) < n_rows
        x_lo, x_hi = _unpack_pair(jnp.where(live, xs_ref[0:m, :], jnp.uint32(0)))
        x = jnp.concatenate([x_lo.astype(BF16), x_hi.astype(BF16)], axis=1)
        for c in range(D_FF // FF_CHUNK):
            gc = slice(c * FF_CHUNK, (c + 1) * FF_CHUNK)
            uc = slice(D_FF + c * FF_CHUNK, D_FF + (c + 1) * FF_CHUNK)
            gate = jnp.dot(x, wgu_buf[slot, :, gc].astype(BF16), preferred_element_type=F32) + bgu_ref[0, :, gc]
            up = jnp.dot(x, wgu_buf[slot, :, uc].astype(BF16), preferred_element_type=F32) + bgu_ref[0, :, uc]
            gate = jnp.minimum(gate, SWIGLU_LIMIT)
            up = jnp.clip(up, -SWIGLU_LIMIT, SWIGLU_LIMIT)
            act_ref[0:m, gc] = (gate * jax.nn.sigmoid(SWIGLU_ALPHA * gate) * (up + 1.0)).astype(BF16)
        for c in range(half // FF_CHUNK):
            lo_c = slice(c * FF_CHUNK, (c + 1) * FF_CHUNK)
            hi_c = slice(half + c * FF_CHUNK, half + (c + 1) * FF_CHUNK)
            y_lo = (jnp.dot(act_ref[0:m, :], wd_buf[slot, :, lo_c].astype(BF16), preferred_element_type=F32)
                    + bd_ref[0, :, lo_c])
            y_hi = (jnp.dot(act_ref[0:m, :], wd_buf[slot, :, hi_c].astype(BF16), preferred_element_type=F32)
                    + bd_ref[0, :, hi_c])
            y_ref[0:m, lo_c] = _pack_pair(y_lo, y_hi)

    for below, m in zip((0,) + TILE_SPANS[:-1], TILE_SPANS):
        pl.when(jnp.logical_and(n_rows > below, n_rows <= m))(functools.partial(tile_path, m))


def _experts(sched, xs, wgu, bgu, wd, bd):
    rows, half = xs.shape
    n_work = sched.shape[1]
    grid_spec = pltpu.PrefetchScalarGridSpec(
        num_scalar_prefetch=1,
        grid=(n_work,),
        in_specs=[
            pl.BlockSpec((EXPERT_TILE, half), lambda w, sc: (sc[S_TILE, w], 0)),
            pl.BlockSpec(memory_space=pl.ANY),
            pl.BlockSpec((1, 1, 2 * D_FF), lambda w, sc: (sc[S_EXPERT, w], 0, 0)),
            pl.BlockSpec(memory_space=pl.ANY),
            pl.BlockSpec((1, 1, D_MODEL), lambda w, sc: (sc[S_EXPERT, w], 0, 0)),
        ],
        out_specs=pl.BlockSpec((EXPERT_TILE, half), lambda w, sc: (sc[S_TILE, w], 0)),
        scratch_shapes=[pltpu.VMEM((EXPERT_TILE, D_FF), BF16),
                        pltpu.VMEM((2, D_MODEL, 2 * D_FF), F32),
                        pltpu.VMEM((2, D_FF, D_MODEL), F32),
                        pltpu.SemaphoreType.DMA((2, 2))],
    )
    return pl.pallas_call(
        _expert_kernel,
        grid_spec=grid_spec,
        out_shape=jax.ShapeDtypeStruct((rows, half), jnp.uint32),
        compiler_params=pltpu.CompilerParams(dimension_semantics=("arbitrary",),
                                             vmem_limit_bytes=VMEM_LIMIT),
        name="experts",
    )(sched, xs, wgu, bgu, wd, bd)


def _unpack_pair(packed):
    lo = lax.bitcast_convert_type(packed << 16, F32)
    hi = lax.bitcast_convert_type(packed & jnp.uint32(0xFFFF0000), F32)
    return lo, hi


def _pack_pair(lo, hi):
    lo_bits = lax.bitcast_convert_type(lo.astype(BF16).astype(F32), jnp.uint32)
    hi_bits = lax.bitcast_convert_type(hi.astype(BF16).astype(F32), jnp.uint32)
    return (lo_bits >> 16) | (hi_bits & jnp.uint32(0xFFFF0000))


def _combine_kernel(yg_ref, h2_ref, route_ref, o_ref):
    half = D_MODEL // 2
    by_token = jnp.transpose(route_ref[...])
    lo_sum = h2_ref[:, :half]
    hi_sum = h2_ref[:, half:]
    for k in range(TOP_K):
        gate = by_token[:, TOP_K + k:TOP_K + k + 1]
        lo, hi = _unpack_pair(yg_ref[k])
        lo_sum = lo_sum + gate * lo
        hi_sum = hi_sum + gate * hi
    o_ref[:, :half] = lo_sum
    o_ref[:, half:] = hi_sum


def _combine(yg, h2, route, part):
    n = h2.shape[0]
    half = D_MODEL // 2
    steps = yg.shape[1] // ROW_TILE
    row = lambda i: (i + part * steps, 0)
    return pl.pallas_call(
        _combine_kernel,
        grid=(steps,),
        in_specs=[
            pl.BlockSpec((TOP_K, ROW_TILE, half), lambda i: (0, i, 0)),
            pl.BlockSpec((ROW_TILE, D_MODEL), row),
            pl.BlockSpec((2 * TOP_K, ROW_TILE), lambda i: (0, i + part * steps)),
        ],
        out_specs=pl.BlockSpec((ROW_TILE, D_MODEL), row),
        out_shape=jax.ShapeDtypeStruct((n, D_MODEL), F32),
        input_output_aliases={1: 0},
        compiler_params=pltpu.CompilerParams(dimension_semantics=("arbitrary",),
                                             vmem_limit_bytes=VMEM_LIMIT),
        name="combine",
    )(yg, h2, route)


def _work_schedule(counts, n_rows):
    n_work = n_rows // EXPERT_TILE + N_EXPERTS
    tiles = (counts + (EXPERT_TILE - 1)) // EXPERT_TILE
    tile_end = jnp.cumsum(tiles)
    tile_start = tile_end - tiles
    total = tile_end[-1]
    w = jnp.arange(n_work, dtype=jnp.int32)
    wc = jnp.minimum(w, total - 1)
    expert_of = jnp.sum((tile_end[None, :] <= wc[:, None]).astype(jnp.int32), axis=1)
    mine = expert_of[:, None] == jnp.arange(N_EXPERTS, dtype=jnp.int32)[None, :]
    pick = lambda table: jnp.sum(jnp.where(mine, table[None, :], 0), axis=1)
    live = jnp.clip(pick(counts) - (wc - pick(tile_start)) * EXPERT_TILE, 0, EXPERT_TILE)
    live = jnp.where(w < total, live, 0)
    prev_e = jnp.concatenate([jnp.full((1,), -1, jnp.int32), expert_of[:-1]])
    new_e = jnp.logical_and(w < total, expert_of != prev_e)
    slot = (jnp.cumsum(new_e.astype(jnp.int32)) - 1) % 2
    later = jnp.logical_and(new_e[None, :], w[None, :] > w[:, None])
    next_w = jnp.min(jnp.where(later, w[None, :], n_work), axis=1)
    next_e = jnp.sum(jnp.where(next_w[:, None] == w[None, :], expert_of[None, :], 0), axis=1)
    next_e = jnp.where(next_w < n_work, next_e, -1)
    sched = jnp.stack([wc, expert_of, live, new_e.astype(jnp.int32), slot, next_e]).astype(jnp.int32)
    return sched, (tile_start * EXPERT_TILE).astype(jnp.int32)


def kernel(x, meta_tokens, norm1_g, w_in, q_norm_g, k_norm_g, attn_sink, dw_w, dw_b, conv_ln_g,
           conv_ln_b, w_out, norm2_g, router_w, router_b, w_gate_up, b_gate_up, w_down, b_down):
    assert norm1_g.shape[0] == 1, "single-layer trunk: meta-token query rows are not materialised"
    b, s, d = x.shape
    n = b * s
    x2d = x.reshape(n, d)

    scale = HEAD_DIM ** -0.5 * LOG2_E
    qg = (jnp.tile(q_norm_g[0], N_Q_HEADS) * scale).reshape(1, ATTN_W)
    kg = jnp.tile(k_norm_g[0], N_KV_HEADS).reshape(1, KV_W)
    head_of = np.arange(ATTN_W) // HEAD_DIM
    pq = jnp.asarray((head_of[:, None] == head_of[None, :]) / HEAD_DIM, BF16)
    pk = pq[:KV_W, :KV_W]
    g1 = norm1_g[0].reshape(1, d)
    w_in_b = w_in[0].astype(BF16)

    q, kd, vd, hc = _inproj(x2d, g1, w_in_b, qg, kg, pq, pk, ROW_TILE)
    _, kmeta, vmeta, hc_meta = _inproj(meta_tokens, g1, w_in_b, qg, kg, pq, pk, N_META)

    meta_pad = ((0, BLOCK - N_META), (0, 0))
    attn = _attention(attn_sink[0] * LOG2_E, q.reshape(b, s, ATTN_W), kd.reshape(b, s, 2 * KV_W),
                      vd.reshape(b, s, 2 * KV_W), jnp.pad(kmeta, meta_pad), jnp.pad(vmeta, meta_pad),
                      jnp.asarray(_alibi_bias()))

    w_out_b = w_out[0].astype(BF16)
    rw = router_w[0].T
    rwh = rw.astype(BF16)
    rwl = (rw - rwh.astype(F32)).astype(BF16)
    rb = router_b[0].reshape(N_EXPERTS, 1)
    tri = jnp.asarray(np.triu(np.ones((ROW_TILE, ROW_TILE), np.float32), 1), BF16)
    h2, xp, route, counts = _outproj(attn.reshape(n, ATTN_W), hc.reshape(b, s, CONV_W), hc_meta,
                                     dw_w[0].reshape(CONV_K, CONV_W), dw_b[0].reshape(1, CONV_W),
                                     conv_ln_g[0].reshape(1, CONV_W), conv_ln_b[0].reshape(1, CONV_W), x2d,
                                     w_out_b, w_out_b, norm2_g[0].reshape(1, d),
                                     rwh, rwl, rb, tri)

    counts_i = counts[:, 0].astype(jnp.int32)
    idx = route[:TOP_K].astype(jnp.int32)
    rank = route[2 * TOP_K:3 * TOP_K].astype(jnp.int32)
    sched, first_row = _work_schedule(counts_i, n * TOP_K)
    chosen = idx[:, :, None] == jnp.arange(N_EXPERTS, dtype=jnp.int32)
    pos = rank + jnp.sum(jnp.where(chosen, first_row, 0), axis=-1)
    pos3 = pos.reshape(TOP_K, n // SC_CHUNK, SC_CHUNK).transpose(1, 0, 2)

    xs = _sc_dispatch(pos3, xp, sched.shape[1] * EXPERT_TILE)
    y = _experts(sched, xs, w_gate_up[0],
                 b_gate_up[0].reshape(N_EXPERTS, 1, 2 * D_FF), w_down[0],
                 b_down[0].reshape(N_EXPERTS, 1, D_MODEL))
    out = h2
    chunks = pos3.shape[0] // COMBINE_PARTS
    for part in range(COMBINE_PARTS):
        out = _combine(_sc_collect(pos3[part * chunks:(part + 1) * chunks], y), out, route, part)
    return out.reshape(b, s, d)
```

```python
import functools

import numpy as np
import jax
import jax.numpy as jnp
from jax import lax
from jax.experimental import pallas as pl
from jax.experimental.pallas import tpu as pltpu
from jax.experimental.pallas import tpu_sc as plsc

F32 = jnp.float32
BF16 = jnp.bfloat16

D_MODEL = 1024
N_META = 16
HEAD_DIM = 64
N_Q_HEADS = 8
N_KV_HEADS = 2
GROUP = N_Q_HEADS // N_KV_HEADS
ATTN_W = N_Q_HEADS * HEAD_DIM
KV_W = N_KV_HEADS * HEAD_DIM
CONV_W = D_MODEL - ATTN_W
IN_W = ATTN_W + 2 * KV_W + 2 * CONV_W
WINDOW = 128
BLOCK = 128
CONV_K = 31
CONV_PAD = CONV_K // 2
N_EXPERTS = 32
TOP_K = 4
D_FF = D_MODEL
SWIGLU_LIMIT = 7.0
SWIGLU_ALPHA = 1.702
RMS_EPS = 1e-6
LN_EPS = 1e-5
NEG_INF = -1e30
LOG2_E = 1.4426950408889634

LANES = 128
ROW_TILE = 1024
EXPERT_TILE = 1024
SC_CORES = 2
SC_SUBCORES = 16
SC_WORKERS = SC_CORES * SC_SUBCORES
SC_CHUNK = 64
CONV_ROWS = 128
FF_CHUNK = 512
TILE_SPANS = (128, 256, 512, 768, 1024)
Q_BLOCKS = 16
COMBINE_PARTS = 8
ROUTE_ROWS = 16
VMEM_LIMIT = 56 * 1024 * 1024


def _rms(x, eps):
    return x * lax.rsqrt(jnp.mean(x * x, axis=-1, keepdims=True) + eps)


def _inproj_kernel(x_ref, g1_ref, w_ref, qg_ref, kg_ref, pq_ref, pk_ref,
                   q_ref, k_ref, v_ref, hc_ref, wb_ref):
    @pl.when(pl.program_id(0) == 0)
    def _():
        wb_ref[...] = w_ref[...].astype(BF16)

    x = x_ref[...]
    n = (_rms(x, RMS_EPS) * g1_ref[...]).astype(BF16)
    proj = jnp.dot(n, wb_ref[...], preferred_element_type=F32)
    q = proj[:, :ATTN_W]
    k = proj[:, ATTN_W:ATTN_W + KV_W]
    v = proj[:, ATTN_W + KV_W:ATTN_W + 2 * KV_W]
    a = proj[:, ATTN_W + 2 * KV_W:ATTN_W + 2 * KV_W + CONV_W]
    g = proj[:, ATTN_W + 2 * KV_W + CONV_W:]
    qms = jnp.dot((q * q).astype(BF16), pq_ref[...], preferred_element_type=F32)
    kms = jnp.dot((k * k).astype(BF16), pk_ref[...], preferred_element_type=F32)
    q_ref[...] = (q * lax.rsqrt(qms + RMS_EPS) * qg_ref[...]).astype(BF16)
    kn = k * lax.rsqrt(kms + RMS_EPS) * kg_ref[...]
    lo = lax.broadcasted_iota(jnp.int32, kn.shape, 1) < HEAD_DIM
    ksw = pltpu.roll(kn, HEAD_DIM, 1)
    k_ref[...] = jnp.concatenate([jnp.where(lo, kn, ksw), jnp.where(lo, ksw, kn)], axis=1).astype(BF16)
    vsw = pltpu.roll(v, HEAD_DIM, 1)
    v_ref[...] = jnp.concatenate([jnp.where(lo, v, vsw), jnp.where(lo, vsw, v)], axis=1).astype(BF16)
    hc_ref[...] = a * jax.nn.sigmoid(g)


def _inproj(x2d, g1, w_in, qg, kg, pq, pk, tile):
    n = x2d.shape[0]
    const = lambda i: (0, 0)
    row = lambda i: (i, 0)
    return pl.pallas_call(
        _inproj_kernel,
        grid=(n // tile,),
        in_specs=[
            pl.BlockSpec((tile, D_MODEL), row),
            pl.BlockSpec((1, D_MODEL), const),
            pl.BlockSpec((D_MODEL, IN_W), const),
            pl.BlockSpec((1, ATTN_W), const),
            pl.BlockSpec((1, KV_W), const),
            pl.BlockSpec((ATTN_W, ATTN_W), const),
            pl.BlockSpec((KV_W, KV_W), const),
        ],
        out_specs=[
            pl.BlockSpec((tile, ATTN_W), row),
            pl.BlockSpec((tile, 2 * KV_W), row),
            pl.BlockSpec((tile, 2 * KV_W), row),
            pl.BlockSpec((tile, CONV_W), row),
        ],
        out_shape=[
            jax.ShapeDtypeStruct((n, ATTN_W), BF16),
            jax.ShapeDtypeStruct((n, 2 * KV_W), BF16),
            jax.ShapeDtypeStruct((n, 2 * KV_W), BF16),
            jax.ShapeDtypeStruct((n, CONV_W), F32),
        ],
        scratch_shapes=[pltpu.VMEM((D_MODEL, IN_W), BF16)],
        compiler_params=pltpu.CompilerParams(dimension_semantics=("arbitrary",),
                                             vmem_limit_bytes=VMEM_LIMIT),
        name="inproj",
    )(x2d, g1, w_in, qg, kg, pq, pk)


def _conv_fill(pad_ref, hm_ref, hc_ref, seq):
    tail = pad_ref.shape[0] - (N_META + seq)
    pad_ref[0:N_META, :] = hm_ref[...]
    pad_ref[N_META:N_META + seq, :] = hc_ref[0]
    pad_ref[N_META + seq:, :] = jnp.zeros((tail, CONV_W), F32)


def _conv_chunk(pad_ref, y_ref, w_ref, b_ref, lg_ref, lb_ref, base):
    first = N_META - CONV_PAD
    span = CONV_ROWS + 8
    for lt in range(CONV_W // LANES):
        ls = slice(lt * LANES, (lt + 1) * LANES)
        win = pad_ref[pl.ds(base, CONV_ROWS + 32), ls]
        acc = None
        for sub in range(8):
            part = None
            for al in range(4):
                k = 8 * al + sub - first
                if 0 <= k < CONV_K:
                    term = win[8 * al:8 * al + span] * w_ref[k:k + 1, ls]
                    part = term if part is None else part + term
            if sub:
                part = pltpu.roll(part, span - sub, 0)
            acc = part[:CONV_ROWS] if acc is None else acc + part[:CONV_ROWS]
        y_ref[:, ls] = acc
    y = y_ref[...] + b_ref[...]
    mu = jnp.mean(y, axis=-1, keepdims=True)
    yc = y - mu
    var = jnp.mean(yc * yc, axis=-1, keepdims=True)
    z = yc * lax.rsqrt(var + LN_EPS) * lg_ref[...] + lb_ref[...]
    return (z * jax.nn.sigmoid(z)).astype(BF16)


def _alibi_bias():
    qi = np.arange(BLOCK)[:, None]
    kj = np.arange(BLOCK)[None, :]
    dists = [qi + BLOCK - kj, np.abs(qi - kj), kj + BLOCK - qi]
    out = np.zeros((N_KV_HEADS, 4, GROUP * BLOCK, BLOCK), np.float32)
    for h in range(N_KV_HEADS):
        for g in range(GROUP):
            slope = LOG2_E * 2.0 ** (-8.0 * (h * GROUP + g + 1) / N_Q_HEADS)
            for p, d in enumerate(dists):
                out[h, p, g * BLOCK:(g + 1) * BLOCK] = np.where(d <= WINDOW, -slope * d, NEG_INF)
    out[:, 3, :, N_META:] = NEG_INF
    return out


def _attn_kernel(sink_ref, q_ref, kp_ref, kc_ref, kn_ref, vp_ref, vc_ref, vn_ref,
                 km_ref, vm_ref, bias_ref, o_ref, *, n_steps):
    i = pl.program_id(1)
    lo = lax.broadcasted_iota(jnp.int32, (BLOCK, LANES), 1) < HEAD_DIM
    edge_first = jnp.where(i == 0, NEG_INF, 0.0).astype(F32)
    edge_last = jnp.where(i == n_steps - 1, NEG_INF, 0.0).astype(F32)
    nt = (((1,), (1,)), ((), ()))
    zero = jnp.zeros((BLOCK, LANES), BF16)

    def key_block(before_ref, here_ref, after_ref, idx, ks):
        if idx < 0:
            return before_ref[0, :, ks]
        if idx >= Q_BLOCKS:
            return after_ref[0, :, ks]
        return here_ref[0, idx * BLOCK:(idx + 1) * BLOCK, ks]

    for qb in range(Q_BLOCKS):
        qr = slice(qb * BLOCK, (qb + 1) * BLOCK)
        for h in range(N_KV_HEADS):
            ks = slice(h * LANES, (h + 1) * LANES)
            rows = []
            for j in range(2):
                pair = q_ref[0, qr, (2 * h + j) * LANES:(2 * h + j + 1) * LANES]
                rows.append(jnp.where(lo, pair, zero))
                rows.append(jnp.where(lo, zero, pair))
            qs = jnp.concatenate(rows, axis=0)
            k_p, k_c, k_n = (key_block(kp_ref, kc_ref, kn_ref, qb + rel, ks) for rel in (-1, 0, 1))
            v_p, v_c, v_n = (key_block(vp_ref, vc_ref, vn_ref, qb + rel, ks) for rel in (-1, 0, 1))
            s_p = lax.dot_general(qs, k_p, nt, preferred_element_type=F32) + bias_ref[h, 0]
            s_c = lax.dot_general(qs, k_c, nt, preferred_element_type=F32) + bias_ref[h, 1]
            s_n = lax.dot_general(qs, k_n, nt, preferred_element_type=F32) + bias_ref[h, 2]
            s_m = lax.dot_general(qs, km_ref[:, ks], nt, preferred_element_type=F32) + bias_ref[h, 3]
            if qb == 0:
                s_p = s_p + edge_first
            if qb == Q_BLOCKS - 1:
                s_n = s_n + edge_last
            sink = jnp.concatenate(
                [jnp.full((BLOCK, 1), sink_ref[h * GROUP + g], F32) for g in range(GROUP)], axis=0)
            m = jnp.max(jnp.maximum(jnp.maximum(s_p, s_c), jnp.maximum(s_n, s_m)), axis=-1, keepdims=True)
            m = jnp.maximum(m, sink)
            p_p = jnp.exp2(s_p - m)
            p_c = jnp.exp2(s_c - m)
            p_n = jnp.exp2(s_n - m)
            p_m = jnp.exp2(s_m - m)
            denom = jnp.sum((p_p + p_c) + (p_n + p_m), axis=-1, keepdims=True) + jnp.exp2(sink - m)
            o = (jnp.dot(p_p.astype(BF16), v_p, preferred_element_type=F32)
                 + jnp.dot(p_c.astype(BF16), v_c, preferred_element_type=F32)
                 + jnp.dot(p_n.astype(BF16), v_n, preferred_element_type=F32)
                 + jnp.dot(p_m.astype(BF16), vm_ref[:, ks], preferred_element_type=F32))
            o = o / denom
            for j in range(2):
                even = o[(2 * j) * BLOCK:(2 * j + 1) * BLOCK]
                odd = o[(2 * j + 1) * BLOCK:(2 * j + 2) * BLOCK]
                o_ref[0, qr, (2 * h + j) * LANES:(2 * h + j + 1) * LANES] = jnp.where(lo, even, odd).astype(BF16)


def _attention(sink, q, kd, vd, kmeta, vmeta, bias):
    b, s, _ = q.shape
    nb = s // BLOCK
    steps = nb // Q_BLOCKS
    here = lambda bi, i: (bi, i, 0)
    before = lambda bi, i: (bi, jnp.maximum(Q_BLOCKS * i - 1, 0), 0)
    after = lambda bi, i: (bi, jnp.minimum(Q_BLOCKS * i + Q_BLOCKS, nb - 1), 0)
    const2 = lambda bi, i: (0, 0)
    edge_blk = (1, BLOCK, 2 * KV_W)
    here_blk = (1, Q_BLOCKS * BLOCK, 2 * KV_W)
    return pl.pallas_call(
        functools.partial(_attn_kernel, n_steps=steps),
        grid=(b, steps),
        in_specs=[
            pl.BlockSpec(memory_space=pltpu.SMEM),
            pl.BlockSpec((1, Q_BLOCKS * BLOCK, ATTN_W), here),
            pl.BlockSpec(edge_blk, before), pl.BlockSpec(here_blk, here), pl.BlockSpec(edge_blk, after),
            pl.BlockSpec(edge_blk, before), pl.BlockSpec(here_blk, here), pl.BlockSpec(edge_blk, after),
            pl.BlockSpec((BLOCK, 2 * KV_W), const2),
            pl.BlockSpec((BLOCK, 2 * KV_W), const2),
            pl.BlockSpec(bias.shape, lambda bi, i: (0, 0, 0, 0)),
        ],
        out_specs=pl.BlockSpec((1, Q_BLOCKS * BLOCK, ATTN_W), here),
        out_shape=jax.ShapeDtypeStruct((b, s, ATTN_W), BF16),
        compiler_params=pltpu.CompilerParams(dimension_semantics=("arbitrary", "arbitrary"),
                                             vmem_limit_bytes=VMEM_LIMIT),
        name="attention",
    )(sink, q, kd, kd, kd, vd, vd, vd, kmeta, vmeta, bias)


def _outproj_kernel(attn_ref, hc_ref, hm_ref, cw_ref, cb_ref, lg_ref, lb_ref, x_ref, wa_ref, wc_ref, g2_ref,
                    rwh_ref, rwl_ref, rb_ref, tri_ref, h2_ref, xp_ref, route_ref, cnt_ref,
                    run_ref, pad_ref, y_ref, conv_ref, wab_ref, wcb_ref, *, seq):
    step = pl.program_id(0)
    tiles_per_seq = seq // ROW_TILE
    part = step % tiles_per_seq

    @pl.when(step == 0)
    def _():
        run_ref[...] = jnp.zeros_like(run_ref)
        wab_ref[...] = wa_ref[...].astype(BF16)
        wcb_ref[...] = wc_ref[...].astype(BF16)

    @pl.when(part == 0)
    def _():
        _conv_fill(pad_ref, hm_ref, hc_ref, seq)

    for c in range(ROW_TILE // CONV_ROWS):
        base = pl.multiple_of(part * ROW_TILE + c * CONV_ROWS, CONV_ROWS)
        conv_ref[c * CONV_ROWS:(c + 1) * CONV_ROWS, :] = _conv_chunk(pad_ref, y_ref, cw_ref, cb_ref, lg_ref,
                                                                     lb_ref, base)
    mix = (jnp.dot(attn_ref[...], wab_ref[...], preferred_element_type=F32)
           + jnp.dot(conv_ref[...], wcb_ref[...], preferred_element_type=F32))
    h2 = x_ref[...] + mix
    h2_ref[...] = h2
    hn = _rms(h2, RMS_EPS) * g2_ref[...]
    hn_hi = hn.astype(BF16)
    half = D_MODEL // 2
    xp_ref[...] = _pack_pair(hn[:, :half], hn[:, half:])
    hn_lo = (hn - hn_hi.astype(F32)).astype(BF16)
    nt = (((1,), (1,)), ((), ()))
    logits = (lax.dot_general(rwh_ref[...], hn_hi, nt, preferred_element_type=F32)
              + lax.dot_general(rwh_ref[...], hn_lo, nt, preferred_element_type=F32)
              + lax.dot_general(rwl_ref[...], hn_hi, nt, preferred_element_type=F32)) + rb_ref[...]
    expert = lax.broadcasted_iota(jnp.int32, logits.shape, 0).astype(F32)
    work = logits
    vals, sels = [], []
    for r in range(TOP_K):
        m = jnp.max(work, axis=0, keepdims=True)
        idx = jnp.min(jnp.where(work == m, expert, float(N_EXPERTS)), axis=0, keepdims=True)
        sel = expert == idx
        work = jnp.where(sel, -jnp.inf, work)
        route_ref[r:r + 1, :] = idx
        vals.append(m)
        sels.append(sel)
    onehot = jnp.where(jnp.logical_or(jnp.logical_or(sels[0], sels[1]), jnp.logical_or(sels[2], sels[3])),
                       1.0, 0.0)
    exps = [jnp.exp(v - vals[0]) for v in vals]
    tot = exps[0] + exps[1] + exps[2] + exps[3]
    before = jnp.dot(onehot.astype(BF16), tri_ref[...], preferred_element_type=F32) + run_ref[...]
    for r in range(TOP_K):
        route_ref[TOP_K + r:TOP_K + r + 1, :] = exps[r] / tot
        route_ref[2 * TOP_K + r:2 * TOP_K + r + 1, :] = jnp.sum(jnp.where(sels[r], before, 0.0), axis=0,
                                                               keepdims=True)
    route_ref[3 * TOP_K:, :] = jnp.zeros((ROUTE_ROWS - 3 * TOP_K, logits.shape[1]), F32)
    run_ref[...] = run_ref[...] + jnp.sum(onehot, axis=1, keepdims=True)
    cnt_ref[...] = run_ref[...]


def _outproj(attn2d, hc, hc_meta, dw_w, dw_b, ln_g, ln_b, x2d, wa, wc, g2, rwh, rwl, rb, tri):
    n = x2d.shape[0]
    seq = hc.shape[1]
    tile = ROW_TILE
    const = lambda i: (0, 0)
    row = lambda i: (i, 0)
    half = D_MODEL // 2
    return pl.pallas_call(
        functools.partial(_outproj_kernel, seq=seq),
        grid=(n // tile,),
        in_specs=[
            pl.BlockSpec((tile, ATTN_W), row),
            pl.BlockSpec((1, seq, CONV_W), lambda i: (i // (seq // tile), 0, 0)),
            pl.BlockSpec((N_META, CONV_W), const),
            pl.BlockSpec((CONV_K, CONV_W), const),
            pl.BlockSpec((1, CONV_W), const),
            pl.BlockSpec((1, CONV_W), const),
            pl.BlockSpec((1, CONV_W), const),
            pl.BlockSpec((tile, D_MODEL), row),
            pl.BlockSpec((ATTN_W, D_MODEL), const),
            pl.BlockSpec((CONV_W, D_MODEL), lambda i: (ATTN_W // CONV_W, 0)),
            pl.BlockSpec((1, D_MODEL), const),
            pl.BlockSpec((N_EXPERTS, D_MODEL), const),
            pl.BlockSpec((N_EXPERTS, D_MODEL), const),
            pl.BlockSpec((N_EXPERTS, 1), const),
            pl.BlockSpec((tile, tile), const),
        ],
        out_specs=[
            pl.BlockSpec((tile, D_MODEL), row),
            pl.BlockSpec((tile, half), row),
            pl.BlockSpec((ROUTE_ROWS, tile), lambda i: (0, i)),
            pl.BlockSpec((N_EXPERTS, 1), const),
        ],
        out_shape=[
            jax.ShapeDtypeStruct((n, D_MODEL), F32),
            jax.ShapeDtypeStruct((n, half), jnp.uint32),
            jax.ShapeDtypeStruct((ROUTE_ROWS, n), F32),
            jax.ShapeDtypeStruct((N_EXPERTS, 1), F32),
        ],
        scratch_shapes=[pltpu.VMEM((N_EXPERTS, 1), F32),
                        pltpu.VMEM((N_META + seq + 32, CONV_W), F32),
                        pltpu.VMEM((CONV_ROWS, CONV_W), F32),
                        pltpu.VMEM((tile, CONV_W), BF16),
                        pltpu.VMEM((ATTN_W, D_MODEL), BF16),
                        pltpu.VMEM((CONV_W, D_MODEL), BF16)],
        compiler_params=pltpu.CompilerParams(dimension_semantics=("arbitrary",),
                                             vmem_limit_bytes=VMEM_LIMIT),
        name="outproj",
    )(attn2d, hc, hc_meta, dw_w, dw_b, ln_g, ln_b, x2d, wa, wc, g2, rwh, rwl, rb, tri)


def _sc_mesh():
    return plsc.VectorSubcoreMesh(core_axis_name="c", subcore_axis_name="s",
                                  num_cores=SC_CORES, num_subcores=SC_SUBCORES)


def _sc_worker():
    return lax.axis_index("s") * SC_CORES + lax.axis_index("c")


def _sc_dispatch(pos3, xp, out_rows):
    n, half = xp.shape
    per_worker = n // SC_CHUNK // SC_WORKERS

    @functools.partial(
        pl.kernel, mesh=_sc_mesh(),
        out_type=jax.ShapeDtypeStruct((out_rows, half), jnp.uint32),
        scratch_types=[pltpu.VMEM((2, TOP_K, SC_CHUNK), jnp.int32),
                       pltpu.VMEM((2, SC_CHUNK, half), jnp.uint32),
                       pltpu.SemaphoreType.DMA((2,)),
                       pltpu.SemaphoreType.DMA],
        name="sc_dispatch")
    def run(pos_hbm, xp_hbm, xs_hbm, idx_v, rows_v, load_sem, scatter_sem):
        first = _sc_worker() * per_worker

        def load(j, buf):
            c = first + j
            return (pltpu.async_copy(pos_hbm.at[c], idx_v.at[buf], load_sem.at[buf]),
                    pltpu.async_copy(xp_hbm.at[pl.ds(c * SC_CHUNK, SC_CHUNK)], rows_v.at[buf], load_sem.at[buf]))

        loads = load(0, 0)
        for j in range(per_worker):
            buf = j % 2
            for cp in loads:
                cp.wait()
            if j + 1 < per_worker:
                loads = load(j + 1, 1 - buf)
            copies = [pltpu.async_copy(rows_v.at[buf], xs_hbm.at[idx_v.at[buf].at[k]], scatter_sem)
                      for k in range(TOP_K)]
            for cp in copies:
                cp.wait()

    return run(pos3, xp)


def _sc_collect(pos3, y):
    half = y.shape[1]
    n = pos3.shape[0] * SC_CHUNK
    per_worker = n // SC_CHUNK // SC_WORKERS

    @functools.partial(
        pl.kernel, mesh=_sc_mesh(),
        out_type=jax.ShapeDtypeStruct((TOP_K, n, half), jnp.uint32),
        scratch_types=[pltpu.VMEM((TOP_K, SC_CHUNK), jnp.int32),
                       pltpu.VMEM((2, SC_CHUNK, half), jnp.uint32),
                       pltpu.SemaphoreType.DMA((2,)),
                       pltpu.SemaphoreType.DMA((2,))],
        name="sc_collect")
    def run(pos_hbm, y_hbm, yg_hbm, idx_v, rows_v, gather_sem, store_sem):
        first = _sc_worker() * per_worker

        @pl.loop(0, per_worker)
        def _(j):
            c = first + j
            pltpu.sync_copy(pos_hbm.at[c], idx_v)

            def gather(k):
                return pltpu.async_copy(y_hbm.at[idx_v.at[k]], rows_v.at[k % 2], gather_sem.at[k % 2])

            gathers = [gather(0)]
            stores = []
            for k in range(TOP_K):
                if k + 1 < TOP_K:
                    if k >= 1:
                        stores[k - 1].wait()
                    gathers.append(gather(k + 1))
                gathers[k].wait()
                stores.append(pltpu.async_copy(rows_v.at[k % 2], yg_hbm.at[k, pl.ds(c * SC_CHUNK, SC_CHUNK)],
                                               store_sem.at[k % 2]))
            stores[TOP_K - 2].wait()
            stores[TOP_K - 1].wait()

    return run(pos3, y)


S_TILE, S_EXPERT, S_ROWS, S_NEW, S_SLOT, S_NEXT = range(6)


def _expert_kernel(sched_ref, xs_ref, wgu_hbm, bgu_ref, wd_hbm, bd_ref, y_ref,
                   act_ref, wgu_buf, wd_buf, sem):
    w = pl.program_id(0)
    e = sched_ref[S_EXPERT, w]
    slot = sched_ref[S_SLOT, w]
    n_rows = sched_ref[S_ROWS, w]

    def weight_copies(expert, s):
        return (pltpu.make_async_copy(wgu_hbm.at[expert], wgu_buf.at[s], sem.at[0, s]),
                pltpu.make_async_copy(wd_hbm.at[expert], wd_buf.at[s], sem.at[1, s]))

    @pl.when(sched_ref[S_NEW, w] == 1)
    def _():
        @pl.when(w == 0)
        def _():
            for cp in weight_copies(e, slot):
                cp.start()

        nxt = sched_ref[S_NEXT, w]

        @pl.when(nxt >= 0)
        def _():
            for cp in weight_copies(nxt, 1 - slot):
                cp.start()

        for cp in weight_copies(e, slot):
            cp.wait()

    def tile_path(m):
        half = D_MODEL // 2
        live = lax.broadcasted_iota(jnp.int32, (m, 1), 0) < n_rows
        x_lo, x_hi = _unpack_pair(jnp.where(live, xs_ref[0:m, :], jnp.uint32(0)))
        x = jnp.concatenate([x_lo.astype(BF16), x_hi.astype(BF16)], axis=1)
        for c in range(D_FF // FF_CHUNK):
            gc = slice(c * FF_CHUNK, (c + 1) * FF_CHUNK)
            uc = slice(D_FF + c * FF_CHUNK, D_FF + (c + 1) * FF_CHUNK)
            gate = jnp.dot(x, wgu_buf[slot, :, gc].astype(BF16), preferred_element_type=F32) + bgu_ref[0, :, gc]
            up = jnp.dot(x, wgu_buf[slot, :, uc].astype(BF16), preferred_element_type=F32) + bgu_ref[0, :, uc]
            gate = jnp.minimum(gate, SWIGLU_LIMIT)
            up = jnp.clip(up, -SWIGLU_LIMIT, SWIGLU_LIMIT)
            act_ref[0:m, gc] = (gate * jax.nn.sigmoid(SWIGLU_ALPHA * gate) * (up + 1.0)).astype(BF16)
        for c in range(half // FF_CHUNK):
            lo_c = slice(c * FF_CHUNK, (c + 1) * FF_CHUNK)
            hi_c = slice(half + c * FF_CHUNK, half + (c + 1) * FF_CHUNK)
            y_lo = (jnp.dot(act_ref[0:m, :], wd_buf[slot, :, lo_c].astype(BF16), preferred_element_type=F32)
                    + bd_ref[0, :, lo_c])
            y_hi = (jnp.dot(act_ref[0:m, :], wd_buf[slot, :, hi_c].astype(BF16), preferred_element_type=F32)
                    + bd_ref[0, :, hi_c])
            y_ref[0:m, lo_c] = _pack_pair(y_lo, y_hi)

    for below, m in zip((0,) + TILE_SPANS[:-1], TILE_SPANS):
        pl.when(jnp.logical_and(n_rows > below, n_rows <= m))(functools.partial(tile_path, m))


def _experts(sched, xs, wgu, bgu, wd, bd):
    rows, half = xs.shape
    n_work = sched.shape[1]
    grid_spec = pltpu.PrefetchScalarGridSpec(
        num_scalar_prefetch=1,
        grid=(n_work,),
        in_specs=[
            pl.BlockSpec((EXPERT_TILE, half), lambda w, sc: (sc[S_TILE, w], 0)),
            pl.BlockSpec(memory_space=pl.ANY),
            pl.BlockSpec((1, 1, 2 * D_FF), lambda w, sc: (sc[S_EXPERT, w], 0, 0)),
            pl.BlockSpec(memory_space=pl.ANY),
            pl.BlockSpec((1, 1, D_MODEL), lambda w, sc: (sc[S_EXPERT, w], 0, 0)),
        ],
        out_specs=pl.BlockSpec((EXPERT_TILE, half), lambda w, sc: (sc[S_TILE, w], 0)),
        scratch_shapes=[pltpu.VMEM((EXPERT_TILE, D_FF), BF16),
                        pltpu.VMEM((2, D_MODEL, 2 * D_FF), F32),
                        pltpu.VMEM((2, D_FF, D_MODEL), F32),
                        pltpu.SemaphoreType.DMA((2, 2))],
    )
    return pl.pallas_call(
        _expert_kernel,
        grid_spec=grid_spec,
        out_shape=jax.ShapeDtypeStruct((rows, half), jnp.uint32),
        compiler_params=pltpu.CompilerParams(dimension_semantics=("arbitrary",),
                                             vmem_limit_bytes=VMEM_LIMIT),
        name="experts",
    )(sched, xs, wgu, bgu, wd, bd)


def _unpack_pair(packed):
    lo = lax.bitcast_convert_type(packed << 16, F32)
    hi = lax.bitcast_convert_type(packed & jnp.uint32(0xFFFF0000), F32)
    return lo, hi


def _pack_pair(lo, hi):
    lo_bits = lax.bitcast_convert_type(lo.astype(BF16).astype(F32), jnp.uint32)
    hi_bits = lax.bitcast_convert_type(hi.astype(BF16).astype(F32), jnp.uint32)
    return (lo_bits >> 16) | (hi_bits & jnp.uint32(0xFFFF0000))


def _combine_kernel(yg_ref, h2_ref, route_ref, o_ref):
    half = D_MODEL // 2
    by_token = jnp.transpose(route_ref[...])
    lo_sum = h2_ref[:, :half]
    hi_sum = h2_ref[:, half:]
    for k in range(TOP_K):
        gate = by_token[:, TOP_K + k:TOP_K + k + 1]
        lo, hi = _unpack_pair(yg_ref[k])
        lo_sum = lo_sum + gate * lo
        hi_sum = hi_sum + gate * hi
    o_ref[:, :half] = lo_sum
    o_ref[:, half:] = hi_sum


def _combine(yg, h2, route, part):
    n = h2.shape[0]
    half = D_MODEL // 2
    steps = yg.shape[1] // ROW_TILE
    row = lambda i: (i + part * steps, 0)
    return pl.pallas_call(
        _combine_kernel,
        grid=(steps,),
        in_specs=[
            pl.BlockSpec((TOP_K, ROW_TILE, half), lambda i: (0, i, 0)),
            pl.BlockSpec((ROW_TILE, D_MODEL), row),
            pl.BlockSpec((2 * TOP_K, ROW_TILE), lambda i: (0, i + part * steps)),
        ],
        out_specs=pl.BlockSpec((ROW_TILE, D_MODEL), row),
        out_shape=jax.ShapeDtypeStruct((n, D_MODEL), F32),
        input_output_aliases={1: 0},
        compiler_params=pltpu.CompilerParams(dimension_semantics=("arbitrary",),
                                             vmem_limit_bytes=VMEM_LIMIT),
        name="combine",
    )(yg, h2, route)


def _work_schedule(counts, n_rows):
    n_work = n_rows // EXPERT_TILE + N_EXPERTS
    tiles = (counts + (EXPERT_TILE - 1)) // EXPERT_TILE
    tile_end = jnp.cumsum(tiles)
    tile_start = tile_end - tiles
    total = tile_end[-1]
    w = jnp.arange(n_work, dtype=jnp.int32)
    wc = jnp.minimum(w, total - 1)
    expert_of = jnp.sum((tile_end[None, :] <= wc[:, None]).astype(jnp.int32), axis=1)
    mine = expert_of[:, None] == jnp.arange(N_EXPERTS, dtype=jnp.int32)[None, :]
    pick = lambda table: jnp.sum(jnp.where(mine, table[None, :], 0), axis=1)
    live = jnp.clip(pick(counts) - (wc - pick(tile_start)) * EXPERT_TILE, 0, EXPERT_TILE)
    live = jnp.where(w < total, live, 0)
    prev_e = jnp.concatenate([jnp.full((1,), -1, jnp.int32), expert_of[:-1]])
    new_e = jnp.logical_and(w < total, expert_of != prev_e)
    slot = (jnp.cumsum(new_e.astype(jnp.int32)) - 1) % 2
    later = jnp.logical_and(new_e[None, :], w[None, :] > w[:, None])
    next_w = jnp.min(jnp.where(later, w[None, :], n_work), axis=1)
    next_e = jnp.sum(jnp.where(next_w[:, None] == w[None, :], expert_of[None, :], 0), axis=1)
    next_e = jnp.where(next_w < n_work, next_e, -1)
    sched = jnp.stack([wc, expert_of, live, new_e.astype(jnp.int32), slot, next_e]).astype(jnp.int32)
    return sched, (tile_start * EXPERT_TILE).astype(jnp.int32)


def kernel(x, meta_tokens, norm1_g, w_in, q_norm_g, k_norm_g, attn_sink, dw_w, dw_b, conv_ln_g,
           conv_ln_b, w_out, norm2_g, router_w, router_b, w_gate_up, b_gate_up, w_down, b_down):
    assert norm1_g.shape[0] == 1, "single-layer trunk: meta-token query rows are not materialised"
    b, s, d = x.shape
    n = b * s
    x2d = x.reshape(n, d)

    scale = HEAD_DIM ** -0.5 * LOG2_E
    qg = (jnp.tile(q_norm_g[0], N_Q_HEADS) * scale).reshape(1, ATTN_W)
    kg = jnp.tile(k_norm_g[0], N_KV_HEADS).reshape(1, KV_W)
    head_of = np.arange(ATTN_W) // HEAD_DIM
    pq = jnp.asarray((head_of[:, None] == head_of[None, :]) / HEAD_DIM, BF16)
    pk = pq[:KV_W, :KV_W]
    g1 = norm1_g[0].reshape(1, d)

    q, kd, vd, hc = _inproj(x2d, g1, w_in[0], qg, kg, pq, pk, ROW_TILE)
    _, kmeta, vmeta, hc_meta = _inproj(meta_tokens, g1, w_in[0], qg, kg, pq, pk, N_META)

    meta_pad = ((0, BLOCK - N_META), (0, 0))
    attn = _attention(attn_sink[0] * LOG2_E, q.reshape(b, s, ATTN_W), kd.reshape(b, s, 2 * KV_W),
                      vd.reshape(b, s, 2 * KV_W), jnp.pad(kmeta, meta_pad), jnp.pad(vmeta, meta_pad),
                      jnp.asarray(_alibi_bias()))

    rw = router_w[0].T
    rwh = rw.astype(BF16)
    rwl = (rw - rwh.astype(F32)).astype(BF16)
    rb = router_b[0].reshape(N_EXPERTS, 1)
    tri = jnp.asarray(np.triu(np.ones((ROW_TILE, ROW_TILE), np.float32), 1), BF16)
    h2, xp, route, counts = _outproj(attn.reshape(n, ATTN_W), hc.reshape(b, s, CONV_W), hc_meta,
                                     dw_w[0].reshape(CONV_K, CONV_W), dw_b[0].reshape(1, CONV_W),
                                     conv_ln_g[0].reshape(1, CONV_W), conv_ln_b[0].reshape(1, CONV_W), x2d,
                                     w_out[0], w_out[0], norm2_g[0].reshape(1, d),
                                     rwh, rwl, rb, tri)

    counts_i = counts[:, 0].astype(jnp.int32)
    idx = route[:TOP_K].astype(jnp.int32)
    rank = route[2 * TOP_K:3 * TOP_K].astype(jnp.int32)
    sched, first_row = _work_schedule(counts_i, n * TOP_K)
    chosen = idx[:, :, None] == jnp.arange(N_EXPERTS, dtype=jnp.int32)
    pos = rank + jnp.sum(jnp.where(chosen, first_row, 0), axis=-1)
    pos3 = pos.reshape(TOP_K, n // SC_CHUNK, SC_CHUNK).transpose(1, 0, 2)

    xs = _sc_dispatch(pos3, xp, sched.shape[1] * EXPERT_TILE)
    y = _experts(sched, xs, w_gate_up[0],
                 b_gate_up[0].reshape(N_EXPERTS, 1, 2 * D_FF), w_down[0],
                 b_down[0].reshape(N_EXPERTS, 1, D_MODEL))
    out = h2
    chunks = pos3.shape[0] // COMBINE_PARTS
    for part in range(COMBINE_PARTS):
        out = _combine(_sc_collect(pos3[part * chunks:(part + 1) * chunks], y), out, route, part)
    return out.reshape(b, s, d)
```

```python
import functools

import numpy as np
import jax
import jax.numpy as jnp
from jax import lax
from jax.experimental import pallas as pl
from jax.experimental.pallas import tpu as pltpu
from jax.experimental.pallas import tpu_sc as plsc

F32 = jnp.float32
BF16 = jnp.bfloat16

D_MODEL = 1024
N_META = 16
HEAD_DIM = 64
N_Q_HEADS = 8
N_KV_HEADS = 2
GROUP = N_Q_HEADS // N_KV_HEADS
ATTN_W = N_Q_HEADS * HEAD_DIM
KV_W = N_KV_HEADS * HEAD_DIM
CONV_W = D_MODEL - ATTN_W
IN_W = ATTN_W + 2 * KV_W + 2 * CONV_W
WINDOW = 128
BLOCK = 128
CONV_K = 31
CONV_PAD = CONV_K // 2
N_EXPERTS = 32
TOP_K = 4
D_FF = D_MODEL
SWIGLU_LIMIT = 7.0
SWIGLU_ALPHA = 1.702
RMS_EPS = 1e-6
LN_EPS = 1e-5
NEG_INF = -1e30
LOG2_E = 1.4426950408889634

LANES = 128
ROW_TILE = 1024
EXPERT_TILE = 1024
SC_CORES = 2
SC_SUBCORES = 16
SC_WORKERS = SC_CORES * SC_SUBCORES
SC_CHUNK = 64
CONV_ROWS = 128
FF_CHUNK = 512
TILE_SPANS = (128, 256, 512, 768, 1024)
Q_BLOCKS = 16
COMBINE_PARTS = 8
ROUTE_ROWS = 16
VMEM_LIMIT = 56 * 1024 * 1024


def _rms(x, eps):
    return x * lax.rsqrt(jnp.mean(x * x, axis=-1, keepdims=True) + eps)


def _inproj_kernel(x_ref, g1_ref, w_ref, qg_ref, kg_ref, pq_ref, pk_ref,
                   q_ref, k_ref, v_ref, hc_ref, wb_ref):
    @pl.when(pl.program_id(0) == 0)
    def _():
        wb_ref[...] = w_ref[...].astype(BF16)

    x = x_ref[...]
    n = (_rms(x, RMS_EPS) * g1_ref[...]).astype(BF16)
    proj = jnp.dot(n, wb_ref[...], preferred_element_type=F32)
    q = proj[:, :ATTN_W]
    k = proj[:, ATTN_W:ATTN_W + KV_W]
    v = proj[:, ATTN_W + KV_W:ATTN_W + 2 * KV_W]
    a = proj[:, ATTN_W + 2 * KV_W:ATTN_W + 2 * KV_W + CONV_W]
    g = proj[:, ATTN_W + 2 * KV_W + CONV_W:]
    qms = jnp.dot((q * q).astype(BF16), pq_ref[...], preferred_element_type=F32)
    kms = jnp.dot((k * k).astype(BF16), pk_ref[...], preferred_element_type=F32)
    q_ref[...] = (q * lax.rsqrt(qms + RMS_EPS) * qg_ref[...]).astype(BF16)
    kn = k * lax.rsqrt(kms + RMS_EPS) * kg_ref[...]
    lo = lax.broadcasted_iota(jnp.int32, kn.shape, 1) < HEAD_DIM
    ksw = pltpu.roll(kn, HEAD_DIM, 1)
    k_ref[...] = jnp.concatenate([jnp.where(lo, kn, ksw), jnp.where(lo, ksw, kn)], axis=1).astype(BF16)
    vsw = pltpu.roll(v, HEAD_DIM, 1)
    v_ref[...] = jnp.concatenate([jnp.where(lo, v, vsw), jnp.where(lo, vsw, v)], axis=1).astype(BF16)
    hc_ref[...] = a * jax.nn.sigmoid(g)


def _inproj(x2d, g1, w_in, qg, kg, pq, pk, tile):
    n = x2d.shape[0]
    const = lambda i: (0, 0)
    row = lambda i: (i, 0)
    return pl.pallas_call(
        _inproj_kernel,
        grid=(n // tile,),
        in_specs=[
            pl.BlockSpec((tile, D_MODEL), row),
            pl.BlockSpec((1, D_MODEL), const),
            pl.BlockSpec((D_MODEL, IN_W), const),
            pl.BlockSpec((1, ATTN_W), const),
            pl.BlockSpec((1, KV_W), const),
            pl.BlockSpec((ATTN_W, ATTN_W), const),
            pl.BlockSpec((KV_W, KV_W), const),
        ],
        out_specs=[
            pl.BlockSpec((tile, ATTN_W), row),
            pl.BlockSpec((tile, 2 * KV_W), row),
            pl.BlockSpec((tile, 2 * KV_W), row),
            pl.BlockSpec((tile, CONV_W), row),
        ],
        out_shape=[
            jax.ShapeDtypeStruct((n, ATTN_W), BF16),
            jax.ShapeDtypeStruct((n, 2 * KV_W), BF16),
            jax.ShapeDtypeStruct((n, 2 * KV_W), BF16),
            jax.ShapeDtypeStruct((n, CONV_W), F32),
        ],
        scratch_shapes=[pltpu.VMEM((D_MODEL, IN_W), BF16)],
        compiler_params=pltpu.CompilerParams(dimension_semantics=("arbitrary",),
                                             vmem_limit_bytes=VMEM_LIMIT),
        name="inproj",
    )(x2d, g1, w_in, qg, kg, pq, pk)


def _conv_fill(pad_ref, hm_ref, hc_ref, seq):
    tail = pad_ref.shape[0] - (N_META + seq)
    pad_ref[0:N_META, :] = hm_ref[...]
    pad_ref[N_META:N_META + seq, :] = hc_ref[0]
    pad_ref[N_META + seq:, :] = jnp.zeros((tail, CONV_W), F32)


def _conv_chunk(pad_ref, y_ref, w_ref, b_ref, lg_ref, lb_ref, base):
    first = N_META - CONV_PAD
    span = CONV_ROWS + 8
    for lt in range(CONV_W // LANES):
        ls = slice(lt * LANES, (lt + 1) * LANES)
        win = pad_ref[pl.ds(base, CONV_ROWS + 32), ls]
        acc = None
        for sub in range(8):
            part = None
            for al in range(4):
                k = 8 * al + sub - first
                if 0 <= k < CONV_K:
                    term = win[8 * al:8 * al + span] * w_ref[k:k + 1, ls]
                    part = term if part is None else part + term
            if sub:
                part = pltpu.roll(part, span - sub, 0)
            acc = part[:CONV_ROWS] if acc is None else acc + part[:CONV_ROWS]
        y_ref[:, ls] = acc
    y = y_ref[...] + b_ref[...]
    mu = jnp.mean(y, axis=-1, keepdims=True)
    yc = y - mu
    var = jnp.mean(yc * yc, axis=-1, keepdims=True)
    z = yc * lax.rsqrt(var + LN_EPS) * lg_ref[...] + lb_ref[...]
    return (z * jax.nn.sigmoid(z)).astype(BF16)


def _alibi_bias():
    qi = np.arange(BLOCK)[:, None]
    kj = np.arange(BLOCK)[None, :]
    dists = [qi + BLOCK - kj, np.abs(qi - kj), kj + BLOCK - qi]
    out = np.zeros((N_KV_HEADS, 4, GROUP * BLOCK, BLOCK), np.float32)
    for h in range(N_KV_HEADS):
        for g in range(GROUP):
            slope = LOG2_E * 2.0 ** (-8.0 * (h * GROUP + g + 1) / N_Q_HEADS)
            for p, d in enumerate(dists):
                out[h, p, g * BLOCK:(g + 1) * BLOCK] = np.where(d <= WINDOW, -slope * d, NEG_INF)
    out[:, 3, :, N_META:] = NEG_INF
    return out


def _attn_kernel(sink_ref, q_ref, kp_ref, kc_ref, kn_ref, vp_ref, vc_ref, vn_ref,
                 km_ref, vm_ref, bias_ref, o_ref, *, n_steps):
    i = pl.program_id(1)
    lo = lax.broadcasted_iota(jnp.int32, (BLOCK, LANES), 1) < HEAD_DIM
    edge_first = jnp.where(i == 0, NEG_INF, 0.0).astype(F32)
    edge_last = jnp.where(i == n_steps - 1, NEG_INF, 0.0).astype(F32)
    nt = (((1,), (1,)), ((), ()))
    zero = jnp.zeros((BLOCK, LANES), BF16)

    def key_block(before_ref, here_ref, after_ref, idx, ks):
        if idx < 0:
            return before_ref[0, :, ks]
        if idx >= Q_BLOCKS:
            return after_ref[0, :, ks]
        return here_ref[0, idx * BLOCK:(idx + 1) * BLOCK, ks]

    for qb in range(Q_BLOCKS):
        qr = slice(qb * BLOCK, (qb + 1) * BLOCK)
        for h in range(N_KV_HEADS):
            ks = slice(h * LANES, (h + 1) * LANES)
            rows = []
            for j in range(2):
                pair = q_ref[0, qr, (2 * h + j) * LANES:(2 * h + j + 1) * LANES]
                rows.append(jnp.where(lo, pair, zero))
                rows.append(jnp.where(lo, zero, pair))
            qs = jnp.concatenate(rows, axis=0)
            k_p, k_c, k_n = (key_block(kp_ref, kc_ref, kn_ref, qb + rel, ks) for rel in (-1, 0, 1))
            v_p, v_c, v_n = (key_block(vp_ref, vc_ref, vn_ref, qb + rel, ks) for rel in (-1, 0, 1))
            s_p = lax.dot_general(qs, k_p, nt, preferred_element_type=F32) + bias_ref[h, 0]
            s_c = lax.dot_general(qs, k_c, nt, preferred_element_type=F32) + bias_ref[h, 1]
            s_n = lax.dot_general(qs, k_n, nt, preferred_element_type=F32) + bias_ref[h, 2]
            s_m = lax.dot_general(qs, km_ref[:, ks], nt, preferred_element_type=F32) + bias_ref[h, 3]
            if qb == 0:
                s_p = s_p + edge_first
            if qb == Q_BLOCKS - 1:
                s_n = s_n + edge_last
            sink = jnp.concatenate(
                [jnp.full((BLOCK, 1), sink_ref[h * GROUP + g], F32) for g in range(GROUP)], axis=0)
            m = jnp.max(jnp.maximum(jnp.maximum(s_p, s_c), jnp.maximum(s_n, s_m)), axis=-1, keepdims=True)
            m = jnp.maximum(m, sink)
            p_p = jnp.exp2(s_p - m)
            p_c = jnp.exp2(s_c - m)
            p_n = jnp.exp2(s_n - m)
            p_m = jnp.exp2(s_m - m)
            denom = jnp.sum((p_p + p_c) + (p_n + p_m), axis=-1, keepdims=True) + jnp.exp2(sink - m)
            o = (jnp.dot(p_p.astype(BF16), v_p, preferred_element_type=F32)
                 + jnp.dot(p_c.astype(BF16), v_c, preferred_element_type=F32)
                 + jnp.dot(p_n.astype(BF16), v_n, preferred_element_type=F32)
                 + jnp.dot(p_m.astype(BF16), vm_ref[:, ks], preferred_element_type=F32))
            o = o / denom
            for j in range(2):
                even = o[(2 * j) * BLOCK:(2 * j + 1) * BLOCK]
                odd = o[(2 * j + 1) * BLOCK:(2 * j + 2) * BLOCK]
                o_ref[0, qr, (2 * h + j) * LANES:(2 * h + j + 1) * LANES] = jnp.where(lo, even, odd).astype(BF16)


def _attention(sink, q, kd, vd, kmeta, vmeta, bias):
    b, s, _ = q.shape
    nb = s // BLOCK
    steps = nb // Q_BLOCKS
    here = lambda bi, i: (bi, i, 0)
    before = lambda bi, i: (bi, jnp.maximum(Q_BLOCKS * i - 1, 0), 0)
    after = lambda bi, i: (bi, jnp.minimum(Q_BLOCKS * i + Q_BLOCKS, nb - 1), 0)
    const2 = lambda bi, i: (0, 0)
    edge_blk = (1, BLOCK, 2 * KV_W)
    here_blk = (1, Q_BLOCKS * BLOCK, 2 * KV_W)
    return pl.pallas_call(
        functools.partial(_attn_kernel, n_steps=steps),
        grid=(b, steps),
        in_specs=[
            pl.BlockSpec(memory_space=pltpu.SMEM),
            pl.BlockSpec((1, Q_BLOCKS * BLOCK, ATTN_W), here),
            pl.BlockSpec(edge_blk, before), pl.BlockSpec(here_blk, here), pl.BlockSpec(edge_blk, after),
            pl.BlockSpec(edge_blk, before), pl.BlockSpec(here_blk, here), pl.BlockSpec(edge_blk, after),
            pl.BlockSpec((BLOCK, 2 * KV_W), const2),
            pl.BlockSpec((BLOCK, 2 * KV_W), const2),
            pl.BlockSpec(bias.shape, lambda bi, i: (0, 0, 0, 0)),
        ],
        out_specs=pl.BlockSpec((1, Q_BLOCKS * BLOCK, ATTN_W), here),
        out_shape=jax.ShapeDtypeStruct((b, s, ATTN_W), BF16),
        compiler_params=pltpu.CompilerParams(dimension_semantics=("arbitrary", "arbitrary"),
                                             vmem_limit_bytes=VMEM_LIMIT),
        name="attention",
    )(sink, q, kd, kd, kd, vd, vd, vd, kmeta, vmeta, bias)


def _outproj_kernel(attn_ref, hc_ref, hm_ref, cw_ref, cb_ref, lg_ref, lb_ref, x_ref, wa_ref, wc_ref, g2_ref,
                    rwh_ref, rwl_ref, rb_ref, tri_ref, h2_ref, xp_ref, route_ref, cnt_ref,
                    run_ref, pad_ref, y_ref, conv_ref, wab_ref, wcb_ref, *, seq):
    step = pl.program_id(0)
    tiles_per_seq = seq // ROW_TILE
    part = step % tiles_per_seq

    @pl.when(step == 0)
    def _():
        run_ref[...] = jnp.zeros_like(run_ref)
        wab_ref[...] = wa_ref[...].astype(BF16)
        wcb_ref[...] = wc_ref[...].astype(BF16)

    @pl.when(part == 0)
    def _():
        _conv_fill(pad_ref, hm_ref, hc_ref, seq)

    for c in range(ROW_TILE // CONV_ROWS):
        base = pl.multiple_of(part * ROW_TILE + c * CONV_ROWS, CONV_ROWS)
        conv_ref[c * CONV_ROWS:(c + 1) * CONV_ROWS, :] = _conv_chunk(pad_ref, y_ref, cw_ref, cb_ref, lg_ref,
                                                                     lb_ref, base)
    mix = (jnp.dot(attn_ref[...], wab_ref[...], preferred_element_type=F32)
           + jnp.dot(conv_ref[...], wcb_ref[...], preferred_element_type=F32))
    h2 = x_ref[...] + mix
    h2_ref[...] = h2
    hn = _rms(h2, RMS_EPS) * g2_ref[...]
    hn_hi = hn.astype(BF16)
    half = D_MODEL // 2
    xp_ref[...] = _pack_pair(hn[:, :half], hn[:, half:])
    hn_lo = (hn - hn_hi.astype(F32)).astype(BF16)
    nt = (((1,), (1,)), ((), ()))
    logits = (lax.dot_general(rwh_ref[...], hn_hi, nt, preferred_element_type=F32)
              + lax.dot_general(rwh_ref[...], hn_lo, nt, preferred_element_type=F32)
              + lax.dot_general(rwl_ref[...], hn_hi, nt, preferred_element_type=F32)) + rb_ref[...]
    expert = lax.broadcasted_iota(jnp.int32, logits.shape, 0).astype(F32)
    work = logits
    vals, sels = [], []
    for r in range(TOP_K):
        m = jnp.max(work, axis=0, keepdims=True)
        idx = jnp.min(jnp.where(work == m, expert, float(N_EXPERTS)), axis=0, keepdims=True)
        sel = expert == idx
        work = jnp.where(sel, -jnp.inf, work)
        route_ref[r:r + 1, :] = idx
        vals.append(m)
        sels.append(sel)
    onehot = jnp.where(jnp.logical_or(jnp.logical_or(sels[0], sels[1]), jnp.logical_or(sels[2], sels[3])),
                       1.0, 0.0)
    exps = [jnp.exp(v - vals[0]) for v in vals]
    tot = exps[0] + exps[1] + exps[2] + exps[3]
    before = jnp.dot(onehot.astype(BF16), tri_ref[...], preferred_element_type=F32) + run_ref[...]
    for r in range(TOP_K):
        route_ref[TOP_K + r:TOP_K + r + 1, :] = exps[r] / tot
        route_ref[2 * TOP_K + r:2 * TOP_K + r + 1, :] = jnp.sum(jnp.where(sels[r], before, 0.0), axis=0,
                                                               keepdims=True)
    route_ref[3 * TOP_K:, :] = jnp.zeros((ROUTE_ROWS - 3 * TOP_K, logits.shape[1]), F32)
    run_ref[...] = run_ref[...] + jnp.sum(onehot, axis=1, keepdims=True)
    cnt_ref[...] = run_ref[...]


def _outproj(attn2d, hc, hc_meta, dw_w, dw_b, ln_g, ln_b, x2d, wa, wc, g2, rwh, rwl, rb, tri):
    n = x2d.shape[0]
    seq = hc.shape[1]
    tile = ROW_TILE
    const = lambda i: (0, 0)
    row = lambda i: (i, 0)
    half = D_MODEL // 2
    return pl.pallas_call(
        functools.partial(_outproj_kernel, seq=seq),
        grid=(n // tile,),
        in_specs=[
            pl.BlockSpec((tile, ATTN_W), row),
            pl.BlockSpec((1, seq, CONV_W), lambda i: (i // (seq // tile), 0, 0)),
            pl.BlockSpec((N_META, CONV_W), const),
            pl.BlockSpec((CONV_K, CONV_W), const),
            pl.BlockSpec((1, CONV_W), const),
            pl.BlockSpec((1, CONV_W), const),
            pl.BlockSpec((1, CONV_W), const),
            pl.BlockSpec((tile, D_MODEL), row),
            pl.BlockSpec((ATTN_W, D_MODEL), const),
            pl.BlockSpec((CONV_W, D_MODEL), lambda i: (ATTN_W // CONV_W, 0)),
            pl.BlockSpec((1, D_MODEL), const),
            pl.BlockSpec((N_EXPERTS, D_MODEL), const),
            pl.BlockSpec((N_EXPERTS, D_MODEL), const),
            pl.BlockSpec((N_EXPERTS, 1), const),
            pl.BlockSpec((tile, tile), const),
        ],
        out_specs=[
            pl.BlockSpec((tile, D_MODEL), row),
            pl.BlockSpec((tile, half), row),
            pl.BlockSpec((ROUTE_ROWS, tile), lambda i: (0, i)),
            pl.BlockSpec((N_EXPERTS, 1), const),
        ],
        out_shape=[
            jax.ShapeDtypeStruct((n, D_MODEL), F32),
            jax.ShapeDtypeStruct((n, half), jnp.uint32),
            jax.ShapeDtypeStruct((ROUTE_ROWS, n), F32),
            jax.ShapeDtypeStruct((N_EXPERTS, 1), F32),
        ],
        scratch_shapes=[pltpu.VMEM((N_EXPERTS, 1), F32),
                        pltpu.VMEM((N_META + seq + 32, CONV_W), F32),
                        pltpu.VMEM((CONV_ROWS, CONV_W), F32),
                        pltpu.VMEM((tile, CONV_W), BF16),
                        pltpu.VMEM((ATTN_W, D_MODEL), BF16),
                        pltpu.VMEM((CONV_W, D_MODEL), BF16)],
        compiler_params=pltpu.CompilerParams(dimension_semantics=("arbitrary",),
                                             vmem_limit_bytes=VMEM_LIMIT),
        name="outproj",
    )(attn2d, hc, hc_meta, dw_w, dw_b, ln_g, ln_b, x2d, wa, wc, g2, rwh, rwl, rb, tri)


def _sc_mesh():
    return plsc.VectorSubcoreMesh(core_axis_name="c", subcore_axis_name="s",
                                  num_cores=SC_CORES, num_subcores=SC_SUBCORES)


def _sc_worker():
    return lax.axis_index("s") * SC_CORES + lax.axis_index("c")


def _sc_dispatch(pos3, xp, out_rows):
    n, half = xp.shape
    per_worker = n // SC_CHUNK // SC_WORKERS

    @functools.partial(
        pl.kernel, mesh=_sc_mesh(),
        out_type=jax.ShapeDtypeStruct((out_rows, half), jnp.uint32),
        scratch_types=[pltpu.VMEM((2, TOP_K, SC_CHUNK), jnp.int32),
                       pltpu.VMEM((2, SC_CHUNK, half), jnp.uint32),
                       pltpu.SemaphoreType.DMA((2,)),
                       pltpu.SemaphoreType.DMA],
        name="sc_dispatch")
    def run(pos_hbm, xp_hbm, xs_hbm, idx_v, rows_v, load_sem, scatter_sem):
        first = _sc_worker() * per_worker

        def load(j, buf):
            c = first + j
            return (pltpu.async_copy(pos_hbm.at[c], idx_v.at[buf], load_sem.at[buf]),
                    pltpu.async_copy(xp_hbm.at[pl.ds(c * SC_CHUNK, SC_CHUNK)], rows_v.at[buf], load_sem.at[buf]))

        loads = load(0, 0)
        for j in range(per_worker):
            buf = j % 2
            for cp in loads:
                cp.wait()
            if j + 1 < per_worker:
                loads = load(j + 1, 1 - buf)
            copies = [pltpu.async_copy(rows_v.at[buf], xs_hbm.at[idx_v.at[buf].at[k]], scatter_sem)
                      for k in range(TOP_K)]
            for cp in copies:
                cp.wait()

    return run(pos3, xp)


def _sc_collect(pos3, y):
    half = y.shape[1]
    n = pos3.shape[0] * SC_CHUNK
    per_worker = n // SC_CHUNK // SC_WORKERS

    @functools.partial(
        pl.kernel, mesh=_sc_mesh(),
        out_type=jax.ShapeDtypeStruct((TOP_K, n, half), jnp.uint32),
        scratch_types=[pltpu.VMEM((TOP_K, SC_CHUNK), jnp.int32),
                       pltpu.VMEM((2, SC_CHUNK, half), jnp.uint32),
                       pltpu.SemaphoreType.DMA((2,)),
                       pltpu.SemaphoreType.DMA((2,))],
        name="sc_collect")
    def run(pos_hbm, y_hbm, yg_hbm, idx_v, rows_v, gather_sem, store_sem):
        first = _sc_worker() * per_worker

        @pl.loop(0, per_worker)
        def _(j):
            c = first + j
            pltpu.sync_copy(pos_hbm.at[c], idx_v)

            def gather(k):
                return pltpu.async_copy(y_hbm.at[idx_v.at[k]], rows_v.at[k % 2], gather_sem.at[k % 2])

            gathers = [gather(0)]
            stores = []
            for k in range(TOP_K):
                if k + 1 < TOP_K:
                    if k >= 1:
                        stores[k - 1].wait()
                    gathers.append(gather(k + 1))
                gathers[k].wait()
                stores.append(pltpu.async_copy(rows_v.at[k % 2], yg_hbm.at[k, pl.ds(c * SC_CHUNK, SC_CHUNK)],
                                               store_sem.at[k % 2]))
            stores[TOP_K - 2].wait()
            stores[TOP_K - 1].wait()

    return run(pos3, y)


S_TILE, S_EXPERT, S_ROWS, S_NEW, S_SLOT, S_NEXT = range(6)


def _expert_kernel(sched_ref, xs_ref, wgu_hbm, bgu_ref, wd_hbm, bd_ref, y_ref,
                   act_ref, wgu_buf, wd_buf, sem):
    w = pl.program_id(0)
    e = sched_ref[S_EXPERT, w]
    slot = sched_ref[S_SLOT, w]
    n_rows = sched_ref[S_ROWS, w]

    def weight_copies(expert, s):
        return (pltpu.make_async_copy(wgu_hbm.at[expert], wgu_buf.at[s], sem.at[0, s]),
                pltpu.make_async_copy(wd_hbm.at[expert], wd_buf.at[s], sem.at[1, s]))

    @pl.when(sched_ref[S_NEW, w] == 1)
    def _():
        @pl.when(w == 0)
        def _():
            for cp in weight_copies(e, slot):
                cp.start()

        nxt = sched_ref[S_NEXT, w]

        @pl.when(nxt >= 0)
        def _():
            for cp in weight_copies(nxt, 1 - slot):
                cp.start()

        for cp in weight_copies(e, slot):
            cp.wait()

    def tile_path(m):
        half = D_MODEL // 2
        live = lax.broadcasted_iota(jnp.int32, (m, 1), 0) < n_rows
        x_lo, x_hi = _unpack_pair(jnp.where(live, xs_ref[0:m, :], jnp.uint32(0)))
        x = jnp.concatenate([x_lo.astype(BF16), x_hi.astype(BF16)], axis=1)
        for c in range(D_FF // FF_CHUNK):
            gc = slice(c * FF_CHUNK, (c + 1) * FF_CHUNK)
            uc = slice(D_FF + c * FF_CHUNK, D_FF + (c + 1) * FF_CHUNK)
            gate = jnp.dot(x, wgu_buf[slot, :, gc].astype(BF16), preferred_element_type=F32) + bgu_ref[0, :, gc]
            up = jnp.dot(x, wgu_buf[slot, :, uc].astype(BF16), preferred_element_type=F32) + bgu_ref[0, :, uc]
            gate = jnp.minimum(gate, SWIGLU_LIMIT)
            up = jnp.clip(up, -SWIGLU_LIMIT, SWIGLU_LIMIT)
            act_ref[0:m, gc] = (gate * jax.nn.sigmoid(SWIGLU_ALPHA * gate) * (up + 1.0)).astype(BF16)
        for c in range(half // FF_CHUNK):
            lo_c = slice(c * FF_CHUNK, (c + 1) * FF_CHUNK)
            hi_c = slice(half + c * FF_CHUNK, half + (c + 1) * FF_CHUNK)
            y_lo = (jnp.dot(act_ref[0:m, :], wd_buf[slot, :, lo_c].astype(BF16), preferred_element_type=F32)
                    + bd_ref[0, :, lo_c])
            y_hi = (jnp.dot(act_ref[0:m, :], wd_buf[slot, :, hi_c].astype(BF16), preferred_element_type=F32)
                    + bd_ref[0, :, hi_c])
            y_ref[0:m, lo_c] = _pack_pair(y_lo, y_hi)

    for below, m in zip((0,) + TILE_SPANS[:-1], TILE_SPANS):
        pl.when(jnp.logical_and(n_rows > below, n_rows <= m))(functools.partial(tile_path, m))


def _experts(sched, xs, wgu, bgu, wd, bd):
    rows, half = xs.shape
    n_work = sched.shape[1]
    grid_spec = pltpu.PrefetchScalarGridSpec(
        num_scalar_prefetch=1,
        grid=(n_work,),
        in_specs=[
            pl.BlockSpec((EXPERT_TILE, half), lambda w, sc: (sc[S_TILE, w], 0)),
            pl.BlockSpec(memory_space=pl.ANY),
            pl.BlockSpec((1, 1, 2 * D_FF), lambda w, sc: (sc[S_EXPERT, w], 0, 0)),
            pl.BlockSpec(memory_space=pl.ANY),
            pl.BlockSpec((1, 1, D_MODEL), lambda w, sc: (sc[S_EXPERT, w], 0, 0)),
        ],
        out_specs=pl.BlockSpec((EXPERT_TILE, half), lambda w, sc: (sc[S_TILE, w], 0)),
        scratch_shapes=[pltpu.VMEM((EXPERT_TILE, D_FF), BF16),
                        pltpu.VMEM((2, D_MODEL, 2 * D_FF), F32),
                        pltpu.VMEM((2, D_FF, D_MODEL), F32),
                        pltpu.SemaphoreType.DMA((2, 2))],
    )
    return pl.pallas_call(
        _expert_kernel,
        grid_spec=grid_spec,
        out_shape=jax.ShapeDtypeStruct((rows, half), jnp.uint32),
        compiler_params=pltpu.CompilerParams(dimension_semantics=("arbitrary",),
                                             vmem_limit_bytes=VMEM_LIMIT),
        name="experts",
    )(sched, xs, wgu, bgu, wd, bd)


def _unpack_pair(packed):
    lo = lax.bitcast_convert_type(packed << 16, F32)
    hi = lax.bitcast_convert_type(packed & jnp.uint32(0xFFFF0000), F32)
    return lo, hi


def _pack_pair(lo, hi):
    lo_bits = lax.bitcast_convert_type(lo.astype(BF16).astype(F32), jnp.uint32)
    hi_bits = lax.bitcast_convert_type(hi.astype(BF16).astype(F32), jnp.uint32)
    return (lo_bits >> 16) | (hi_bits & jnp.uint32(0xFFFF0000))


def _combine_kernel(yg_ref, h2_ref, route_ref, o_ref):
    half = D_MODEL // 2
    by_token = jnp.transpose(route_ref[...])
    lo_sum = h2_ref[:, :half]
    hi_sum = h2_ref[:, half:]
    for k in range(TOP_K):
        gate = by_token[:, TOP_K + k:TOP_K + k + 1]
        lo, hi = _unpack_pair(yg_ref[k])
        lo_sum = lo_sum + gate * lo
        hi_sum = hi_sum + gate * hi
    o_ref[:, :half] = lo_sum
    o_ref[:, half:] = hi_sum


def _combine(yg, h2, route, part):
    n = h2.shape[0]
    half = D_MODEL // 2
    steps = yg.shape[1] // ROW_TILE
    row = lambda i: (i + part * steps, 0)
    return pl.pallas_call(
        _combine_kernel,
        grid=(steps,),
        in_specs=[
            pl.BlockSpec((TOP_K, ROW_TILE, half), lambda i: (0, i, 0)),
            pl.BlockSpec((ROW_TILE, D_MODEL), row),
            pl.BlockSpec((2 * TOP_K, ROW_TILE), lambda i: (0, i + part * steps)),
        ],
        out_specs=pl.BlockSpec((ROW_TILE, D_MODEL), row),
        out_shape=jax.ShapeDtypeStruct((n, D_MODEL), F32),
        input_output_aliases={1: 0},
        compiler_params=pltpu.CompilerParams(dimension_semantics=("arbitrary",),
                                             vmem_limit_bytes=VMEM_LIMIT),
        name="combine",
    )(yg, h2, route)


def _work_schedule(counts, n_rows):
    n_work = n_rows // EXPERT_TILE + N_EXPERTS
    tiles = (counts + (EXPERT_TILE - 1)) // EXPERT_TILE
    tile_end = jnp.cumsum(tiles)
    tile_start = tile_end - tiles
    total = tile_end[-1]
    w = jnp.arange(n_work, dtype=jnp.int32)
    wc = jnp.minimum(w, total - 1)
    expert_of = jnp.sum((tile_end[None, :] <= wc[:, None]).astype(jnp.int32), axis=1)
    mine = expert_of[:, None] == jnp.arange(N_EXPERTS, dtype=jnp.int32)[None, :]
    pick = lambda table: jnp.sum(jnp.where(mine, table[None, :], 0), axis=1)
    live = jnp.clip(pick(counts) - (wc - pick(tile_start)) * EXPERT_TILE, 0, EXPERT_TILE)
    live = jnp.where(w < total, live, 0)
    prev_e = jnp.concatenate([jnp.full((1,), -1, jnp.int32), expert_of[:-1]])
    new_e = jnp.logical_and(w < total, expert_of != prev_e)
    slot = (jnp.cumsum(new_e.astype(jnp.int32)) - 1) % 2
    later = jnp.logical_and(new_e[None, :], w[None, :] > w[:, None])
    next_w = jnp.min(jnp.where(later, w[None, :], n_work), axis=1)
    next_e = jnp.sum(jnp.where(next_w[:, None] == w[None, :], expert_of[None, :], 0), axis=1)
    next_e = jnp.where(next_w < n_work, next_e, -1)
    sched = jnp.stack([wc, expert_of, live, new_e.astype(jnp.int32), slot, next_e]).astype(jnp.int32)
    return sched, (tile_start * EXPERT_TILE).astype(jnp.int32)


def kernel(x, meta_tokens, norm1_g, w_in, q_norm_g, k_norm_g, attn_sink, dw_w, dw_b, conv_ln_g,
           conv_ln_b, w_out, norm2_g, router_w, router_b, w_gate_up, b_gate_up, w_down, b_down):
    assert norm1_g.shape[0] == 1, "single-layer trunk: meta-token query rows are not materialised"
    b, s, d = x.shape
    n = b * s
    x2d = x.reshape(n, d)

    scale = HEAD_DIM ** -0.5 * LOG2_E
    qg = (jnp.tile(q_norm_g[0], N_Q_HEADS) * scale).reshape(1, ATTN_W)
    kg = jnp.tile(k_norm_g[0], N_KV_HEADS).reshape(1, KV_W)
    head_of = np.arange(ATTN_W) // HEAD_DIM
    pq = jnp.asarray((head_of[:, None] == head_of[None, :]) / HEAD_DIM, BF16)
    pk = pq[:KV_W, :KV_W]
    g1 = norm1_g[0].reshape(1, d)

    q, kd, vd, hc = _inproj(x2d, g1, w_in[0], qg, kg, pq, pk, ROW_TILE)
    meta_rows = jnp.pad(meta_tokens, ((0, BLOCK - N_META), (0, 0)))
    _, kmeta, vmeta, hc_meta = _inproj(meta_rows, g1, w_in[0], qg, kg, pq, pk, BLOCK)

    attn = _attention(attn_sink[0] * LOG2_E, q.reshape(b, s, ATTN_W), kd.reshape(b, s, 2 * KV_W),
                      vd.reshape(b, s, 2 * KV_W), kmeta, vmeta, jnp.asarray(_alibi_bias()))

    rw = router_w[0].T
    rwh = rw.astype(BF16)
    rwl = (rw - rwh.astype(F32)).astype(BF16)
    rb = router_b[0].reshape(N_EXPERTS, 1)
    tri = jnp.asarray(np.triu(np.ones((ROW_TILE, ROW_TILE), np.float32), 1), BF16)
    h2, xp, route, counts = _outproj(attn.reshape(n, ATTN_W), hc.reshape(b, s, CONV_W), hc_meta,
                                     dw_w[0].reshape(CONV_K, CONV_W), dw_b[0].reshape(1, CONV_W),
                                     conv_ln_g[0].reshape(1, CONV_W), conv_ln_b[0].reshape(1, CONV_W), x2d,
                                     w_out[0], w_out[0], norm2_g[0].reshape(1, d),
                                     rwh, rwl, rb, tri)

    counts_i = counts[:, 0].astype(jnp.int32)
    idx = route[:TOP_K].astype(jnp.int32)
    rank = route[2 * TOP_K:3 * TOP_K].astype(jnp.int32)
    sched, first_row = _work_schedule(counts_i, n * TOP_K)
    chosen = idx[:, :, None] == jnp.arange(N_EXPERTS, dtype=jnp.int32)
    pos = rank + jnp.sum(jnp.where(chosen, first_row, 0), axis=-1)
    pos3 = pos.reshape(TOP_K, n // SC_CHUNK, SC_CHUNK).transpose(1, 0, 2)

    xs = _sc_dispatch(pos3, xp, sched.shape[1] * EXPERT_TILE)
    y = _experts(sched, xs, w_gate_up[0],
                 b_gate_up[0].reshape(N_EXPERTS, 1, 2 * D_FF), w_down[0],
                 b_down[0].reshape(N_EXPERTS, 1, D_MODEL))
    out = h2
    chunks = pos3.shape[0] // COMBINE_PARTS
    for part in range(COMBINE_PARTS):
        out = _combine(_sc_collect(pos3[part * chunks:(part + 1) * chunks], y), out, route, part)
    return out.reshape(b, s, d)
```

```python
import functools

import numpy as np
import jax
import jax.numpy as jnp
from jax import lax
from jax.experimental import pallas as pl
from jax.experimental.pallas import tpu as pltpu
from jax.experimental.pallas import tpu_sc as plsc

F32 = jnp.float32
BF16 = jnp.bfloat16

D_MODEL = 1024
N_META = 16
HEAD_DIM = 64
N_Q_HEADS = 8
N_KV_HEADS = 2
GROUP = N_Q_HEADS // N_KV_HEADS
ATTN_W = N_Q_HEADS * HEAD_DIM
KV_W = N_KV_HEADS * HEAD_DIM
CONV_W = D_MODEL - ATTN_W
IN_W = ATTN_W + 2 * KV_W + 2 * CONV_W
WINDOW = 128
BLOCK = 128
CONV_K = 31
CONV_PAD = CONV_K // 2
N_EXPERTS = 32
TOP_K = 4
D_FF = D_MODEL
SWIGLU_LIMIT = 7.0
SWIGLU_ALPHA = 1.702
RMS_EPS = 1e-6
LN_EPS = 1e-5
NEG_INF = -1e30
LOG2_E = 1.4426950408889634

LANES = 128
ROW_TILE = 1024
EXPERT_TILE = 1024
SC_CORES = 2
SC_SUBCORES = 16
SC_WORKERS = SC_CORES * SC_SUBCORES
SC_CHUNK = 64
COLLECT_CHUNK = 32
CONV_ROWS = 128
FF_CHUNK = 512
TILE_SPANS = (128, 256, 512, 768, 1024)
Q_BLOCKS = 16
COMBINE_PARTS = 8
PLAN_LANES = 128
ROUTE_ROWS = 16
VMEM_LIMIT = 56 * 1024 * 1024


def _rms(x, eps):
    return x * lax.rsqrt(jnp.mean(x * x, axis=-1, keepdims=True) + eps)


def _inproj_kernel(x_ref, g1_ref, w_ref, qg_ref, kg_ref, pq_ref, pk_ref,
                   q_ref, k_ref, v_ref, hc_ref, wb_ref):
    @pl.when(pl.program_id(0) == 0)
    def _():
        wb_ref[...] = w_ref[...].astype(BF16)

    x = x_ref[...]
    n = (_rms(x, RMS_EPS) * g1_ref[...]).astype(BF16)
    proj = jnp.dot(n, wb_ref[...], preferred_element_type=F32)
    q = proj[:, :ATTN_W]
    k = proj[:, ATTN_W:ATTN_W + KV_W]
    v = proj[:, ATTN_W + KV_W:ATTN_W + 2 * KV_W]
    a = proj[:, ATTN_W + 2 * KV_W:ATTN_W + 2 * KV_W + CONV_W]
    g = proj[:, ATTN_W + 2 * KV_W + CONV_W:]
    qms = jnp.dot((q * q).astype(BF16), pq_ref[...], preferred_element_type=F32)
    kms = jnp.dot((k * k).astype(BF16), pk_ref[...], preferred_element_type=F32)
    q_ref[...] = (q * lax.rsqrt(qms + RMS_EPS) * qg_ref[...]).astype(BF16)
    kn = k * lax.rsqrt(kms + RMS_EPS) * kg_ref[...]
    lo = lax.broadcasted_iota(jnp.int32, kn.shape, 1) < HEAD_DIM
    ksw = pltpu.roll(kn, HEAD_DIM, 1)
    k_ref[...] = jnp.concatenate([jnp.where(lo, kn, ksw), jnp.where(lo, ksw, kn)], axis=1).astype(BF16)
    vsw = pltpu.roll(v, HEAD_DIM, 1)
    v_ref[...] = jnp.concatenate([jnp.where(lo, v, vsw), jnp.where(lo, vsw, v)], axis=1).astype(BF16)
    hc_ref[...] = a * jax.nn.sigmoid(g)


def _inproj(x2d, g1, w_in, qg, kg, pq, pk, tile):
    n = x2d.shape[0]
    const = lambda i: (0, 0)
    row = lambda i: (i, 0)
    return pl.pallas_call(
        _inproj_kernel,
        grid=(n // tile,),
        in_specs=[
            pl.BlockSpec((tile, D_MODEL), row),
            pl.BlockSpec((1, D_MODEL), const),
            pl.BlockSpec((D_MODEL, IN_W), const),
            pl.BlockSpec((1, ATTN_W), const),
            pl.BlockSpec((1, KV_W), const),
            pl.BlockSpec((ATTN_W, ATTN_W), const),
            pl.BlockSpec((KV_W, KV_W), const),
        ],
        out_specs=[
            pl.BlockSpec((tile, ATTN_W), row),
            pl.BlockSpec((tile, 2 * KV_W), row),
            pl.BlockSpec((tile, 2 * KV_W), row),
            pl.BlockSpec((tile, CONV_W), row),
        ],
        out_shape=[
            jax.ShapeDtypeStruct((n, ATTN_W), BF16),
            jax.ShapeDtypeStruct((n, 2 * KV_W), BF16),
            jax.ShapeDtypeStruct((n, 2 * KV_W), BF16),
            jax.ShapeDtypeStruct((n, CONV_W), F32),
        ],
        scratch_shapes=[pltpu.VMEM((D_MODEL, IN_W), BF16)],
        compiler_params=pltpu.CompilerParams(dimension_semantics=("arbitrary",),
                                             vmem_limit_bytes=VMEM_LIMIT),
        name="inproj",
    )(x2d, g1, w_in, qg, kg, pq, pk)


def _conv_fill(pad_ref, hm_ref, hc_ref, seq):
    tail = pad_ref.shape[0] - (N_META + seq)
    pad_ref[0:N_META, :] = hm_ref[...]
    pad_ref[N_META:N_META + seq, :] = hc_ref[0]
    pad_ref[N_META + seq:, :] = jnp.zeros((tail, CONV_W), F32)


def _conv_chunk(pad_ref, y_ref, w_ref, b_ref, lg_ref, lb_ref, base):
    first = N_META - CONV_PAD
    span = CONV_ROWS + 8
    for lt in range(CONV_W // LANES):
        ls = slice(lt * LANES, (lt + 1) * LANES)
        win = pad_ref[pl.ds(base, CONV_ROWS + 32), ls]
        acc = None
        for sub in range(8):
            part = None
            for al in range(4):
                k = 8 * al + sub - first
                if 0 <= k < CONV_K:
                    term = win[8 * al:8 * al + span] * w_ref[k:k + 1, ls]
                    part = term if part is None else part + term
            if sub:
                part = pltpu.roll(part, span - sub, 0)
            acc = part[:CONV_ROWS] if acc is None else acc + part[:CONV_ROWS]
        y_ref[:, ls] = acc
    y = y_ref[...] + b_ref[...]
    mu = jnp.mean(y, axis=-1, keepdims=True)
    yc = y - mu
    var = jnp.mean(yc * yc, axis=-1, keepdims=True)
    z = yc * lax.rsqrt(var + LN_EPS) * lg_ref[...] + lb_ref[...]
    return (z * jax.nn.sigmoid(z)).astype(BF16)


def _alibi_bias():
    qi = np.arange(BLOCK)[:, None]
    kj = np.arange(BLOCK)[None, :]
    dists = [qi + BLOCK - kj, np.abs(qi - kj), kj + BLOCK - qi]
    out = np.zeros((N_KV_HEADS, 4, GROUP * BLOCK, BLOCK), np.float32)
    for h in range(N_KV_HEADS):
        for g in range(GROUP):
            slope = LOG2_E * 2.0 ** (-8.0 * (h * GROUP + g + 1) / N_Q_HEADS)
            for p, d in enumerate(dists):
                out[h, p, g * BLOCK:(g + 1) * BLOCK] = np.where(d <= WINDOW, -slope * d, NEG_INF)
    out[:, 3, :, N_META:] = NEG_INF
    return out


def _attn_kernel(sink_ref, q_ref, kp_ref, kc_ref, kn_ref, vp_ref, vc_ref, vn_ref,
                 km_ref, vm_ref, bias_ref, o_ref, *, n_steps):
    i = pl.program_id(1)
    lo = lax.broadcasted_iota(jnp.int32, (BLOCK, LANES), 1) < HEAD_DIM
    edge_first = jnp.where(i == 0, NEG_INF, 0.0).astype(F32)
    edge_last = jnp.where(i == n_steps - 1, NEG_INF, 0.0).astype(F32)
    nt = (((1,), (1,)), ((), ()))
    zero = jnp.zeros((BLOCK, LANES), BF16)

    def key_block(before_ref, here_ref, after_ref, idx, ks):
        if idx < 0:
            return before_ref[0, :, ks]
        if idx >= Q_BLOCKS:
            return after_ref[0, :, ks]
        return here_ref[0, idx * BLOCK:(idx + 1) * BLOCK, ks]

    for qb in range(Q_BLOCKS):
        qr = slice(qb * BLOCK, (qb + 1) * BLOCK)
        for h in range(N_KV_HEADS):
            ks = slice(h * LANES, (h + 1) * LANES)
            rows = []
            for j in range(2):
                pair = q_ref[0, qr, (2 * h + j) * LANES:(2 * h + j + 1) * LANES]
                rows.append(jnp.where(lo, pair, zero))
                rows.append(jnp.where(lo, zero, pair))
            qs = jnp.concatenate(rows, axis=0)
            k_p, k_c, k_n = (key_block(kp_ref, kc_ref, kn_ref, qb + rel, ks) for rel in (-1, 0, 1))
            v_p, v_c, v_n = (key_block(vp_ref, vc_ref, vn_ref, qb + rel, ks) for rel in (-1, 0, 1))
            s_p = lax.dot_general(qs, k_p, nt, preferred_element_type=F32) + bias_ref[h, 0]
            s_c = lax.dot_general(qs, k_c, nt, preferred_element_type=F32) + bias_ref[h, 1]
            s_n = lax.dot_general(qs, k_n, nt, preferred_element_type=F32) + bias_ref[h, 2]
            s_m = lax.dot_general(qs, km_ref[:, ks], nt, preferred_element_type=F32) + bias_ref[h, 3]
            if qb == 0:
                s_p = s_p + edge_first
            if qb == Q_BLOCKS - 1:
                s_n = s_n + edge_last
            sink = jnp.concatenate(
                [jnp.full((BLOCK, 1), sink_ref[h * GROUP + g], F32) for g in range(GROUP)], axis=0)
            m = jnp.max(jnp.maximum(jnp.maximum(s_p, s_c), jnp.maximum(s_n, s_m)), axis=-1, keepdims=True)
            m = jnp.maximum(m, sink)
            p_p = jnp.exp2(s_p - m)
            p_c = jnp.exp2(s_c - m)
            p_n = jnp.exp2(s_n - m)
            p_m = jnp.exp2(s_m - m)
            denom = jnp.sum((p_p + p_c) + (p_n + p_m), axis=-1, keepdims=True) + jnp.exp2(sink - m)
            o = (jnp.dot(p_p.astype(BF16), v_p, preferred_element_type=F32)
                 + jnp.dot(p_c.astype(BF16), v_c, preferred_element_type=F32)
                 + jnp.dot(p_n.astype(BF16), v_n, preferred_element_type=F32)
                 + jnp.dot(p_m.astype(BF16), vm_ref[:, ks], preferred_element_type=F32))
            o = o / denom
            for j in range(2):
                even = o[(2 * j) * BLOCK:(2 * j + 1) * BLOCK]
                odd = o[(2 * j + 1) * BLOCK:(2 * j + 2) * BLOCK]
                o_ref[0, qr, (2 * h + j) * LANES:(2 * h + j + 1) * LANES] = jnp.where(lo, even, odd).astype(BF16)


def _attention(sink, q, kd, vd, kmeta, vmeta, bias):
    b, s, _ = q.shape
    nb = s // BLOCK
    steps = nb // Q_BLOCKS
    here = lambda bi, i: (bi, i, 0)
    before = lambda bi, i: (bi, jnp.maximum(Q_BLOCKS * i - 1, 0), 0)
    after = lambda bi, i: (bi, jnp.minimum(Q_BLOCKS * i + Q_BLOCKS, nb - 1), 0)
    const2 = lambda bi, i: (0, 0)
    edge_blk = (1, BLOCK, 2 * KV_W)
    here_blk = (1, Q_BLOCKS * BLOCK, 2 * KV_W)
    return pl.pallas_call(
        functools.partial(_attn_kernel, n_steps=steps),
        grid=(b, steps),
        in_specs=[
            pl.BlockSpec(memory_space=pltpu.SMEM),
            pl.BlockSpec((1, Q_BLOCKS * BLOCK, ATTN_W), here),
            pl.BlockSpec(edge_blk, before), pl.BlockSpec(here_blk, here), pl.BlockSpec(edge_blk, after),
            pl.BlockSpec(edge_blk, before), pl.BlockSpec(here_blk, here), pl.BlockSpec(edge_blk, after),
            pl.BlockSpec((BLOCK, 2 * KV_W), const2),
            pl.BlockSpec((BLOCK, 2 * KV_W), const2),
            pl.BlockSpec(bias.shape, lambda bi, i: (0, 0, 0, 0)),
        ],
        out_specs=pl.BlockSpec((1, Q_BLOCKS * BLOCK, ATTN_W), here),
        out_shape=jax.ShapeDtypeStruct((b, s, ATTN_W), BF16),
        compiler_params=pltpu.CompilerParams(dimension_semantics=("arbitrary", "arbitrary"),
                                             vmem_limit_bytes=VMEM_LIMIT),
        name="attention",
    )(sink, q, kd, kd, kd, vd, vd, vd, kmeta, vmeta, bias)


def _outproj_kernel(attn_ref, hc_ref, hm_ref, cw_ref, cb_ref, lg_ref, lb_ref, x_ref, wa_ref, wc_ref, g2_ref,
                    rwh_ref, rwl_ref, rb_ref, tri_ref, h2_ref, xp_ref, route_ref, cnt_ref,
                    run_ref, pad_ref, y_ref, conv_ref, wab_ref, wcb_ref, *, seq):
    step = pl.program_id(0)
    tiles_per_seq = seq // ROW_TILE
    part = step % tiles_per_seq

    @pl.when(step == 0)
    def _():
        run_ref[...] = jnp.zeros_like(run_ref)
        wab_ref[...] = wa_ref[...].astype(BF16)
        wcb_ref[...] = wc_ref[...].astype(BF16)

    @pl.when(part == 0)
    def _():
        _conv_fill(pad_ref, hm_ref, hc_ref, seq)

    for c in range(ROW_TILE // CONV_ROWS):
        base = pl.multiple_of(part * ROW_TILE + c * CONV_ROWS, CONV_ROWS)
        conv_ref[c * CONV_ROWS:(c + 1) * CONV_ROWS, :] = _conv_chunk(pad_ref, y_ref, cw_ref, cb_ref, lg_ref,
                                                                     lb_ref, base)
    mix = (jnp.dot(attn_ref[...], wab_ref[...], preferred_element_type=F32)
           + jnp.dot(conv_ref[...], wcb_ref[...], preferred_element_type=F32))
    h2 = x_ref[...] + mix
    h2_ref[...] = h2
    hn = _rms(h2, RMS_EPS) * g2_ref[...]
    hn_hi = hn.astype(BF16)
    half = D_MODEL // 2
    xp_ref[...] = _pack_pair(hn[:, :half], hn[:, half:])
    hn_lo = (hn - hn_hi.astype(F32)).astype(BF16)
    nt = (((1,), (1,)), ((), ()))
    logits = (lax.dot_general(rwh_ref[...], hn_hi, nt, preferred_element_type=F32)
              + lax.dot_general(rwh_ref[...], hn_lo, nt, preferred_element_type=F32)
              + lax.dot_general(rwl_ref[...], hn_hi, nt, preferred_element_type=F32)) + rb_ref[...]
    expert = lax.broadcasted_iota(jnp.int32, logits.shape, 0).astype(F32)
    work = logits
    vals, sels = [], []
    for r in range(TOP_K):
        m = jnp.max(work, axis=0, keepdims=True)
        idx = jnp.min(jnp.where(work == m, expert, float(N_EXPERTS)), axis=0, keepdims=True)
        sel = expert == idx
        work = jnp.where(sel, -jnp.inf, work)
        route_ref[r:r + 1, :] = idx
        vals.append(m)
        sels.append(sel)
    onehot = jnp.where(jnp.logical_or(jnp.logical_or(sels[0], sels[1]), jnp.logical_or(sels[2], sels[3])),
                       1.0, 0.0)
    exps = [jnp.exp(v - vals[0]) for v in vals]
    tot = exps[0] + exps[1] + exps[2] + exps[3]
    before = jnp.dot(onehot.astype(BF16), tri_ref[...], preferred_element_type=F32) + run_ref[...]
    for r in range(TOP_K):
        route_ref[TOP_K + r:TOP_K + r + 1, :] = exps[r] / tot
        route_ref[2 * TOP_K + r:2 * TOP_K + r + 1, :] = jnp.sum(jnp.where(sels[r], before, 0.0), axis=0,
                                                               keepdims=True)
    route_ref[3 * TOP_K:, :] = jnp.zeros((ROUTE_ROWS - 3 * TOP_K, logits.shape[1]), F32)
    run_ref[...] = run_ref[...] + jnp.sum(onehot, axis=1, keepdims=True)
    cnt_ref[...] = run_ref[...]


def _outproj(attn2d, hc, hc_meta, dw_w, dw_b, ln_g, ln_b, x2d, wa, wc, g2, rwh, rwl, rb, tri):
    n = x2d.shape[0]
    seq = hc.shape[1]
    tile = ROW_TILE
    const = lambda i: (0, 0)
    row = lambda i: (i, 0)
    half = D_MODEL // 2
    return pl.pallas_call(
        functools.partial(_outproj_kernel, seq=seq),
        grid=(n // tile,),
        in_specs=[
            pl.BlockSpec((tile, ATTN_W), row),
            pl.BlockSpec((1, seq, CONV_W), lambda i: (i // (seq // tile), 0, 0)),
            pl.BlockSpec((N_META, CONV_W), const),
            pl.BlockSpec((CONV_K, CONV_W), const),
            pl.BlockSpec((1, CONV_W), const),
            pl.BlockSpec((1, CONV_W), const),
            pl.BlockSpec((1, CONV_W), const),
            pl.BlockSpec((tile, D_MODEL), row),
            pl.BlockSpec((ATTN_W, D_MODEL), const),
            pl.BlockSpec((CONV_W, D_MODEL), lambda i: (ATTN_W // CONV_W, 0)),
            pl.BlockSpec((1, D_MODEL), const),
            pl.BlockSpec((N_EXPERTS, D_MODEL), const),
            pl.BlockSpec((N_EXPERTS, D_MODEL), const),
            pl.BlockSpec((N_EXPERTS, 1), const),
            pl.BlockSpec((tile, tile), const),
        ],
        out_specs=[
            pl.BlockSpec((tile, D_MODEL), row),
            pl.BlockSpec((tile, half), row),
            pl.BlockSpec((ROUTE_ROWS, tile), lambda i: (0, i)),
            pl.BlockSpec((N_EXPERTS, 1), const),
        ],
        out_shape=[
            jax.ShapeDtypeStruct((n, D_MODEL), F32),
            jax.ShapeDtypeStruct((n, half), jnp.uint32),
            jax.ShapeDtypeStruct((ROUTE_ROWS, n), F32),
            jax.ShapeDtypeStruct((N_EXPERTS, 1), F32),
        ],
        scratch_shapes=[pltpu.VMEM((N_EXPERTS, 1), F32),
                        pltpu.VMEM((N_META + seq + 32, CONV_W), F32),
                        pltpu.VMEM((CONV_ROWS, CONV_W), F32),
                        pltpu.VMEM((tile, CONV_W), BF16),
                        pltpu.VMEM((ATTN_W, D_MODEL), BF16),
                        pltpu.VMEM((CONV_W, D_MODEL), BF16)],
        compiler_params=pltpu.CompilerParams(dimension_semantics=("arbitrary",),
                                             vmem_limit_bytes=VMEM_LIMIT),
        name="outproj",
    )(attn2d, hc, hc_meta, dw_w, dw_b, ln_g, ln_b, x2d, wa, wc, g2, rwh, rwl, rb, tri)


def _plan_kernel(route_ref, cnt_ref, pos_ref, sched_ref):
    tile_f = float(EXPERT_TILE)
    cnt = jnp.broadcast_to(cnt_ref[...], (N_EXPERTS, PLAN_LANES))
    tiles = jnp.floor((cnt + (tile_f - 1.0)) / tile_f)
    e_row = lax.broadcasted_iota(jnp.int32, (N_EXPERTS, N_EXPERTS), 0)
    e_col = lax.broadcasted_iota(jnp.int32, (N_EXPERTS, N_EXPERTS), 1)
    lower = jnp.where(e_col <= e_row, 1.0, 0.0).astype(jnp.bfloat16)
    tile_end = jnp.dot(lower, tiles.astype(jnp.bfloat16), preferred_element_type=F32)
    tile_start = tile_end - tiles
    total = jnp.max(tile_end, axis=0, keepdims=True)
    w = lax.broadcasted_iota(jnp.int32, (1, PLAN_LANES), 1).astype(F32)
    wc = jnp.minimum(w, total - 1.0)
    expert_of = jnp.sum(jnp.where(tile_end <= wc, 1.0, 0.0), axis=0, keepdims=True)
    e_id = lax.broadcasted_iota(jnp.int32, (N_EXPERTS, PLAN_LANES), 0).astype(F32)
    mine = e_id == expert_of
    pick = lambda table: jnp.sum(jnp.where(mine, table, 0.0), axis=0, keepdims=True)
    live = jnp.clip(pick(cnt) - (wc - pick(tile_start)) * tile_f, 0.0, tile_f)
    in_list = w < total
    live = jnp.where(in_list, live, 0.0)
    prev_e = jnp.where(w == 0.0, -1.0, pltpu.roll(expert_of, 1, 1))
    new_e = jnp.where(jnp.logical_and(in_list, expert_of != prev_e), 1.0, 0.0)
    l_row = lax.broadcasted_iota(jnp.int32, (PLAN_LANES, PLAN_LANES), 0)
    l_col = lax.broadcasted_iota(jnp.int32, (PLAN_LANES, PLAN_LANES), 1)
    upper = jnp.where(l_row <= l_col, 1.0, 0.0).astype(jnp.bfloat16)
    new8 = jnp.broadcast_to(new_e, (8, PLAN_LANES))
    seen = jnp.dot(new8.astype(jnp.bfloat16), upper, preferred_element_type=F32)[0:1]
    slot = (seen - 1.0) - 2.0 * jnp.floor((seen - 1.0) * 0.5)
    exp8 = jnp.broadcast_to(expert_of, (8, PLAN_LANES))
    new_col = jnp.transpose(new8)[:, 0:1]
    exp_col = jnp.transpose(exp8)[:, 0:1]
    cand = l_row.astype(F32)
    later = jnp.logical_and(new_col > 0.0, cand > w)
    next_w = jnp.min(jnp.where(later, cand, float(PLAN_LANES)), axis=0, keepdims=True)
    next_e = jnp.sum(jnp.where(cand == next_w, exp_col, 0.0), axis=0, keepdims=True)
    next_e = jnp.where(next_w < float(PLAN_LANES), next_e, -1.0)
    rows = [wc, expert_of, live, new_e, slot, next_e, jnp.zeros_like(w), jnp.zeros_like(w)]
    sched_ref[...] = jnp.concatenate(rows, axis=0).astype(jnp.int32)
    idx = route_ref[0:TOP_K, :]
    posf = route_ref[2 * TOP_K:3 * TOP_K, :]
    for e in range(N_EXPERTS):
        first_row = tile_start[e:e + 1, 0:1] * tile_f
        posf = posf + jnp.where(idx == float(e), first_row, 0.0)
    pos_ref[...] = posf.astype(jnp.int32)


def _plan(route, counts):
    n = route.shape[1]
    assert n * TOP_K // EXPERT_TILE + N_EXPERTS <= PLAN_LANES
    return pl.pallas_call(
        _plan_kernel,
        out_shape=[jax.ShapeDtypeStruct((TOP_K, n), jnp.int32), jax.ShapeDtypeStruct((8, PLAN_LANES), jnp.int32)],
        compiler_params=pltpu.CompilerParams(vmem_limit_bytes=VMEM_LIMIT),
        name="plan",
    )(route, counts)


def _sc_mesh():
    return plsc.VectorSubcoreMesh(core_axis_name="c", subcore_axis_name="s",
                                  num_cores=SC_CORES, num_subcores=SC_SUBCORES)


def _sc_worker():
    return lax.axis_index("s") * SC_CORES + lax.axis_index("c")


def _sc_dispatch(pos3, xp, out_rows):
    n, half = xp.shape
    per_worker = n // SC_CHUNK // SC_WORKERS

    @functools.partial(
        pl.kernel, mesh=_sc_mesh(),
        out_type=jax.ShapeDtypeStruct((out_rows, half), jnp.uint32),
        scratch_types=[pltpu.VMEM((2, TOP_K, SC_CHUNK), jnp.int32),
                       pltpu.VMEM((2, SC_CHUNK, half), jnp.uint32),
                       pltpu.SemaphoreType.DMA((2,)),
                       pltpu.SemaphoreType.DMA],
        name="sc_dispatch")
    def run(pos_hbm, xp_hbm, xs_hbm, idx_v, rows_v, load_sem, scatter_sem):
        first = _sc_worker() * per_worker

        def load(j, buf):
            c = first + j
            return (pltpu.async_copy(pos_hbm.at[c], idx_v.at[buf], load_sem.at[buf]),
                    pltpu.async_copy(xp_hbm.at[pl.ds(c * SC_CHUNK, SC_CHUNK)], rows_v.at[buf], load_sem.at[buf]))

        loads = load(0, 0)
        for j in range(per_worker):
            buf = j % 2
            for cp in loads:
                cp.wait()
            if j + 1 < per_worker:
                loads = load(j + 1, 1 - buf)
            copies = [pltpu.async_copy(rows_v.at[buf], xs_hbm.at[idx_v.at[buf].at[k]], scatter_sem)
                      for k in range(TOP_K)]
            for cp in copies:
                cp.wait()

    return run(pos3, xp)


def _sc_collect(pos3, y):
    half = y.shape[1]
    chunk = pos3.shape[2]
    n = pos3.shape[0] * chunk
    per_worker = n // chunk // SC_WORKERS

    @functools.partial(
        pl.kernel, mesh=_sc_mesh(),
        out_type=jax.ShapeDtypeStruct((TOP_K, n, half), jnp.uint32),
        scratch_types=[pltpu.VMEM((TOP_K, chunk), jnp.int32),
                       pltpu.VMEM((TOP_K, chunk, half), jnp.uint32),
                       pltpu.SemaphoreType.DMA((TOP_K,)),
                       pltpu.SemaphoreType.DMA((TOP_K,))],
        name="sc_collect")
    def run(pos_hbm, y_hbm, yg_hbm, idx_v, rows_v, gather_sem, store_sem):
        first = _sc_worker() * per_worker

        @pl.loop(0, per_worker)
        def _(j):
            c = first + j
            pltpu.sync_copy(pos_hbm.at[c], idx_v)
            gathers = [pltpu.async_copy(y_hbm.at[idx_v.at[k]], rows_v.at[k], gather_sem.at[k])
                       for k in range(TOP_K)]
            stores = []
            for k in range(TOP_K):
                gathers[k].wait()
                stores.append(pltpu.async_copy(rows_v.at[k], yg_hbm.at[k, pl.ds(c * chunk, chunk)],
                                               store_sem.at[k]))
            for st in stores:
                st.wait()

    return run(pos3, y)


S_TILE, S_EXPERT, S_ROWS, S_NEW, S_SLOT, S_NEXT = range(6)


def _expert_kernel(sched_ref, xs_ref, wgu_hbm, bgu_ref, wd_hbm, bd_ref, y_ref,
                   act_ref, wgu_buf, wd_buf, sem):
    w = pl.program_id(0)
    e = sched_ref[S_EXPERT, w]
    slot = sched_ref[S_SLOT, w]
    n_rows = sched_ref[S_ROWS, w]

    def weight_copies(expert, s):
        return (pltpu.make_async_copy(wgu_hbm.at[expert], wgu_buf.at[s], sem.at[0, s]),
                pltpu.make_async_copy(wd_hbm.at[expert], wd_buf.at[s], sem.at[1, s]))

    @pl.when(sched_ref[S_NEW, w] == 1)
    def _():
        @pl.when(w == 0)
        def _():
            for cp in weight_copies(e, slot):
                cp.start()

        nxt = sched_ref[S_NEXT, w]

        @pl.when(nxt >= 0)
        def _():
            for cp in weight_copies(nxt, 1 - slot):
                cp.start()

        for cp in weight_copies(e, slot):
            cp.wait()

    def tile_path(m):
        half = D_MODEL // 2
        live = lax.broadcasted_iota(jnp.int32, (m, 1), 0) < n_rows
        x_lo, x_hi = _unpack_pair(jnp.where(live, xs_ref[0:m, :], jnp.uint32(0)))
        x = jnp.concatenate([x_lo.astype(BF16), x_hi.astype(BF16)], axis=1)
        for c in range(D_FF // FF_CHUNK):
            gc = slice(c * FF_CHUNK, (c + 1) * FF_CHUNK)
            uc = slice(D_FF + c * FF_CHUNK, D_FF + (c + 1) * FF_CHUNK)
            gate = jnp.dot(x, wgu_buf[slot, :, gc].astype(BF16), preferred_element_type=F32) + bgu_ref[0, :, gc]
            up = jnp.dot(x, wgu_buf[slot, :, uc].astype(BF16), preferred_element_type=F32) + bgu_ref[0, :, uc]
            gate = jnp.minimum(gate, SWIGLU_LIMIT)
            up = jnp.clip(up, -SWIGLU_LIMIT, SWIGLU_LIMIT)
            act_ref[0:m, gc] = (gate * jax.nn.sigmoid(SWIGLU_ALPHA * gate) * (up + 1.0)).astype(BF16)
        for c in range(half // FF_CHUNK):
            lo_c = slice(c * FF_CHUNK, (c + 1) * FF_CHUNK)
            hi_c = slice(half + c * FF_CHUNK, half + (c + 1) * FF_CHUNK)
            y_lo = (jnp.dot(act_ref[0:m, :], wd_buf[slot, :, lo_c].astype(BF16), preferred_element_type=F32)
                    + bd_ref[0, :, lo_c])
            y_hi = (jnp.dot(act_ref[0:m, :], wd_buf[slot, :, hi_c].astype(BF16), preferred_element_type=F32)
                    + bd_ref[0, :, hi_c])
            y_ref[0:m, lo_c] = _pack_pair(y_lo, y_hi)

    for below, m in zip((0,) + TILE_SPANS[:-1], TILE_SPANS):
        pl.when(jnp.logical_and(n_rows > below, n_rows <= m))(functools.partial(tile_path, m))


def _experts(sched, xs, wgu, bgu, wd, bd):
    rows, half = xs.shape
    n_work = rows // EXPERT_TILE
    grid_spec = pltpu.PrefetchScalarGridSpec(
        num_scalar_prefetch=1,
        grid=(n_work,),
        in_specs=[
            pl.BlockSpec((EXPERT_TILE, half), lambda w, sc: (sc[S_TILE, w], 0)),
            pl.BlockSpec(memory_space=pl.ANY),
            pl.BlockSpec((1, 1, 2 * D_FF), lambda w, sc: (sc[S_EXPERT, w], 0, 0)),
            pl.BlockSpec(memory_space=pl.ANY),
            pl.BlockSpec((1, 1, D_MODEL), lambda w, sc: (sc[S_EXPERT, w], 0, 0)),
        ],
        out_specs=pl.BlockSpec((EXPERT_TILE, half), lambda w, sc: (sc[S_TILE, w], 0)),
        scratch_shapes=[pltpu.VMEM((EXPERT_TILE, D_FF), BF16),
                        pltpu.VMEM((2, D_MODEL, 2 * D_FF), F32),
                        pltpu.VMEM((2, D_FF, D_MODEL), F32),
                        pltpu.SemaphoreType.DMA((2, 2))],
    )
    return pl.pallas_call(
        _expert_kernel,
        grid_spec=grid_spec,
        out_shape=jax.ShapeDtypeStruct((rows, half), jnp.uint32),
        compiler_params=pltpu.CompilerParams(dimension_semantics=("arbitrary",),
                                             vmem_limit_bytes=VMEM_LIMIT),
        name="experts",
    )(sched, xs, wgu, bgu, wd, bd)


def _unpack_pair(packed):
    lo = lax.bitcast_convert_type(packed << 16, F32)
    hi = lax.bitcast_convert_type(packed & jnp.uint32(0xFFFF0000), F32)
    return lo, hi


def _pack_pair(lo, hi):
    lo_bits = lax.bitcast_convert_type(lo.astype(BF16).astype(F32), jnp.uint32)
    hi_bits = lax.bitcast_convert_type(hi.astype(BF16).astype(F32), jnp.uint32)
    return (lo_bits >> 16) | (hi_bits & jnp.uint32(0xFFFF0000))


def _combine_kernel(yg_ref, h2_ref, route_ref, o_ref):
    half = D_MODEL // 2
    by_token = jnp.transpose(route_ref[...])
    lo_sum = h2_ref[:, :half]
    hi_sum = h2_ref[:, half:]
    for k in range(TOP_K):
        gate = by_token[:, TOP_K + k:TOP_K + k + 1]
        lo, hi = _unpack_pair(yg_ref[k])
        lo_sum = lo_sum + gate * lo
        hi_sum = hi_sum + gate * hi
    o_ref[:, :half] = lo_sum
    o_ref[:, half:] = hi_sum


def _combine(yg, h2, route, part):
    n = h2.shape[0]
    half = D_MODEL // 2
    steps = yg.shape[1] // ROW_TILE
    row = lambda i: (i + part * steps, 0)
    return pl.pallas_call(
        _combine_kernel,
        grid=(steps,),
        in_specs=[
            pl.BlockSpec((TOP_K, ROW_TILE, half), lambda i: (0, i, 0)),
            pl.BlockSpec((ROW_TILE, D_MODEL), row),
            pl.BlockSpec((2 * TOP_K, ROW_TILE), lambda i: (0, i + part * steps)),
        ],
        out_specs=pl.BlockSpec((ROW_TILE, D_MODEL), row),
        out_shape=jax.ShapeDtypeStruct((n, D_MODEL), F32),
        input_output_aliases={1: 0},
        compiler_params=pltpu.CompilerParams(dimension_semantics=("arbitrary",),
                                             vmem_limit_bytes=VMEM_LIMIT),
        name="combine",
    )(yg, h2, route)


def kernel(x, meta_tokens, norm1_g, w_in, q_norm_g, k_norm_g, attn_sink, dw_w, dw_b, conv_ln_g,
           conv_ln_b, w_out, norm2_g, router_w, router_b, w_gate_up, b_gate_up, w_down, b_down):
    assert norm1_g.shape[0] == 1, "single-layer trunk: meta-token query rows are not materialised"
    b, s, d = x.shape
    n = b * s
    x2d = x.reshape(n, d)

    scale = HEAD_DIM ** -0.5 * LOG2_E
    qg = (jnp.tile(q_norm_g[0], N_Q_HEADS) * scale).reshape(1, ATTN_W)
    kg = jnp.tile(k_norm_g[0], N_KV_HEADS).reshape(1, KV_W)
    head_of = np.arange(ATTN_W) // HEAD_DIM
    pq = jnp.asarray((head_of[:, None] == head_of[None, :]) / HEAD_DIM, BF16)
    pk = pq[:KV_W, :KV_W]
    g1 = norm1_g[0].reshape(1, d)

    q, kd, vd, hc = _inproj(x2d, g1, w_in[0], qg, kg, pq, pk, ROW_TILE)
    meta_rows = jnp.pad(meta_tokens, ((0, BLOCK - N_META), (0, 0)))
    _, kmeta, vmeta, hc_meta = _inproj(meta_rows, g1, w_in[0], qg, kg, pq, pk, BLOCK)

    attn = _attention(attn_sink[0] * LOG2_E, q.reshape(b, s, ATTN_W), kd.reshape(b, s, 2 * KV_W),
                      vd.reshape(b, s, 2 * KV_W), kmeta, vmeta, jnp.asarray(_alibi_bias()))

    rw = router_w[0].T
    rwh = rw.astype(BF16)
    rwl = (rw - rwh.astype(F32)).astype(BF16)
    rb = router_b[0].reshape(N_EXPERTS, 1)
    tri = jnp.asarray(np.triu(np.ones((ROW_TILE, ROW_TILE), np.float32), 1), BF16)
    h2, xp, route, counts = _outproj(attn.reshape(n, ATTN_W), hc.reshape(b, s, CONV_W), hc_meta,
                                     dw_w[0].reshape(CONV_K, CONV_W), dw_b[0].reshape(1, CONV_W),
                                     conv_ln_g[0].reshape(1, CONV_W), conv_ln_b[0].reshape(1, CONV_W), x2d,
                                     w_out[0], w_out[0], norm2_g[0].reshape(1, d),
                                     rwh, rwl, rb, tri)

    pos, sched = _plan(route, counts)
    n_work = n * TOP_K // EXPERT_TILE + N_EXPERTS
    pos3 = pos.reshape(TOP_K, n // SC_CHUNK, SC_CHUNK).transpose(1, 0, 2)

    xs = _sc_dispatch(pos3, xp, n_work * EXPERT_TILE)
    y = _experts(sched, xs, w_gate_up[0],
                 b_gate_up[0].reshape(N_EXPERTS, 1, 2 * D_FF), w_down[0],
                 b_down[0].reshape(N_EXPERTS, 1, D_MODEL))
    out = h2
    pos3c = pos.reshape(TOP_K, n // COLLECT_CHUNK, COLLECT_CHUNK).transpose(1, 0, 2)
    chunks = pos3c.shape[0] // COMBINE_PARTS
    for part in range(COMBINE_PARTS):
        out = _combine(_sc_collect(pos3c[part * chunks:(part + 1) * chunks], y), out, route, part)
    return out.reshape(b, s, d)
```

```python
import functools

import numpy as np
import jax
import jax.numpy as jnp
from jax import lax
from jax.experimental import pallas as pl
from jax.experimental.pallas import tpu as pltpu
from jax.experimental.pallas import tpu_sc as plsc

F32 = jnp.float32
BF16 = jnp.bfloat16

D_MODEL = 1024
N_META = 16
HEAD_DIM = 64
N_Q_HEADS = 8
N_KV_HEADS = 2
GROUP = N_Q_HEADS // N_KV_HEADS
ATTN_W = N_Q_HEADS * HEAD_DIM
KV_W = N_KV_HEADS * HEAD_DIM
CONV_W = D_MODEL - ATTN_W
IN_W = ATTN_W + 2 * KV_W + 2 * CONV_W
WINDOW = 128
BLOCK = 128
CONV_K = 31
CONV_PAD = CONV_K // 2
N_EXPERTS = 32
TOP_K = 4
D_FF = D_MODEL
SWIGLU_LIMIT = 7.0
SWIGLU_ALPHA = 1.702
RMS_EPS = 1e-6
LN_EPS = 1e-5
NEG_INF = -1e30
LOG2_E = 1.4426950408889634

LANES = 128
ROW_TILE = 1024
EXPERT_TILE = 1024
SC_CORES = 2
SC_SUBCORES = 16
SC_WORKERS = SC_CORES * SC_SUBCORES
SC_CHUNK = 64
COLLECT_CHUNK = 32
CONV_ROWS = 128
FF_CHUNK = 512
TILE_SPANS = (128, 256, 512, 768, 1024)
Q_BLOCKS = 16
COMBINE_PARTS = 8
PLAN_LANES = 128
ROUTE_ROWS = 16
VMEM_LIMIT = 56 * 1024 * 1024


def _rms(x, eps):
    return x * lax.rsqrt(jnp.mean(x * x, axis=-1, keepdims=True) + eps)


def _inproj_kernel(x_ref, g1_ref, w_ref, qg_ref, kg_ref, pq_ref, pk_ref,
                   q_ref, k_ref, v_ref, hc_ref, wb_ref):
    @pl.when(pl.program_id(0) == 0)
    def _():
        wb_ref[...] = w_ref[...].astype(BF16)

    x = x_ref[...]
    n = (_rms(x, RMS_EPS) * g1_ref[...]).astype(BF16)
    proj = jnp.dot(n, wb_ref[...], preferred_element_type=F32)
    q = proj[:, :ATTN_W]
    k = proj[:, ATTN_W:ATTN_W + KV_W]
    v = proj[:, ATTN_W + KV_W:ATTN_W + 2 * KV_W]
    a = proj[:, ATTN_W + 2 * KV_W:ATTN_W + 2 * KV_W + CONV_W]
    g = proj[:, ATTN_W + 2 * KV_W + CONV_W:]
    qms = jnp.dot((q * q).astype(BF16), pq_ref[...], preferred_element_type=F32)
    kms = jnp.dot((k * k).astype(BF16), pk_ref[...], preferred_element_type=F32)
    q_ref[...] = (q * lax.rsqrt(qms + RMS_EPS) * qg_ref[...]).astype(BF16)
    kn = k * lax.rsqrt(kms + RMS_EPS) * kg_ref[...]
    lo = lax.broadcasted_iota(jnp.int32, kn.shape, 1) < HEAD_DIM
    ksw = pltpu.roll(kn, HEAD_DIM, 1)
    k_ref[...] = jnp.concatenate([jnp.where(lo, kn, ksw), jnp.where(lo, ksw, kn)], axis=1).astype(BF16)
    vsw = pltpu.roll(v, HEAD_DIM, 1)
    v_ref[...] = jnp.concatenate([jnp.where(lo, v, vsw), jnp.where(lo, vsw, v)], axis=1).astype(BF16)
    hc_ref[...] = a * jax.nn.sigmoid(g)


def _inproj(x2d, g1, w_in, qg, kg, pq, pk, tile):
    n = x2d.shape[0]
    const = lambda i: (0, 0)
    row = lambda i: (i, 0)
    return pl.pallas_call(
        _inproj_kernel,
        grid=(n // tile,),
        in_specs=[
            pl.BlockSpec((tile, D_MODEL), row),
            pl.BlockSpec((1, D_MODEL), const),
            pl.BlockSpec((D_MODEL, IN_W), const),
            pl.BlockSpec((1, ATTN_W), const),
            pl.BlockSpec((1, KV_W), const),
            pl.BlockSpec((ATTN_W, ATTN_W), const),
            pl.BlockSpec((KV_W, KV_W), const),
        ],
        out_specs=[
            pl.BlockSpec((tile, ATTN_W), row),
            pl.BlockSpec((tile, 2 * KV_W), row),
            pl.BlockSpec((tile, 2 * KV_W), row),
            pl.BlockSpec((tile, CONV_W), row),
        ],
        out_shape=[
            jax.ShapeDtypeStruct((n, ATTN_W), BF16),
            jax.ShapeDtypeStruct((n, 2 * KV_W), BF16),
            jax.ShapeDtypeStruct((n, 2 * KV_W), BF16),
            jax.ShapeDtypeStruct((n, CONV_W), F32),
        ],
        scratch_shapes=[pltpu.VMEM((D_MODEL, IN_W), BF16)],
        compiler_params=pltpu.CompilerParams(dimension_semantics=("arbitrary",),
                                             vmem_limit_bytes=VMEM_LIMIT),
        name="inproj",
    )(x2d, g1, w_in, qg, kg, pq, pk)


def _conv_fill(pad_ref, hm_ref, hc_ref, seq):
    tail = pad_ref.shape[0] - (N_META + seq)
    pad_ref[0:N_META, :] = hm_ref[...]
    pad_ref[N_META:N_META + seq, :] = hc_ref[0]
    pad_ref[N_META + seq:, :] = jnp.zeros((tail, CONV_W), F32)


def _conv_chunk(pad_ref, y_ref, w_ref, b_ref, lg_ref, lb_ref, base):
    first = N_META - CONV_PAD
    span = CONV_ROWS + 8
    for lt in range(CONV_W // LANES):
        ls = slice(lt * LANES, (lt + 1) * LANES)
        win = pad_ref[pl.ds(base, CONV_ROWS + 32), ls]
        acc = None
        for sub in range(8):
            part = None
            for al in range(4):
                k = 8 * al + sub - first
                if 0 <= k < CONV_K:
                    term = win[8 * al:8 * al + span] * w_ref[k:k + 1, ls]
                    part = term if part is None else part + term
            if sub:
                part = pltpu.roll(part, span - sub, 0)
            acc = part[:CONV_ROWS] if acc is None else acc + part[:CONV_ROWS]
        y_ref[:, ls] = acc
    y = y_ref[...] + b_ref[...]
    mu = jnp.mean(y, axis=-1, keepdims=True)
    yc = y - mu
    var = jnp.mean(yc * yc, axis=-1, keepdims=True)
    z = yc * lax.rsqrt(var + LN_EPS) * lg_ref[...] + lb_ref[...]
    return (z * jax.nn.sigmoid(z)).astype(BF16)


def _alibi_bias():
    qi = np.arange(BLOCK)[:, None]
    kj = np.arange(BLOCK)[None, :]
    dists = [qi + BLOCK - kj, np.abs(qi - kj), kj + BLOCK - qi]
    out = np.zeros((N_KV_HEADS, 4, GROUP * BLOCK, BLOCK), np.float32)
    for h in range(N_KV_HEADS):
        for g in range(GROUP):
            slope = LOG2_E * 2.0 ** (-8.0 * (h * GROUP + g + 1) / N_Q_HEADS)
            for p, d in enumerate(dists):
                out[h, p, g * BLOCK:(g + 1) * BLOCK] = np.where(d <= WINDOW, -slope * d, NEG_INF)
    out[:, 3, :, N_META:] = NEG_INF
    return out


def _attn_kernel(sink_ref, q_ref, kp_ref, kc_ref, kn_ref, vp_ref, vc_ref, vn_ref,
                 km_ref, vm_ref, bias_ref, o_ref, *, n_steps):
    i = pl.program_id(1)
    lo = lax.broadcasted_iota(jnp.int32, (BLOCK, LANES), 1) < HEAD_DIM
    edge_first = jnp.where(i == 0, NEG_INF, 0.0).astype(F32)
    edge_last = jnp.where(i == n_steps - 1, NEG_INF, 0.0).astype(F32)
    nt = (((1,), (1,)), ((), ()))
    zero = jnp.zeros((BLOCK, LANES), BF16)

    def key_block(before_ref, here_ref, after_ref, idx, ks):
        if idx < 0:
            return before_ref[0, :, ks]
        if idx >= Q_BLOCKS:
            return after_ref[0, :, ks]
        return here_ref[0, idx * BLOCK:(idx + 1) * BLOCK, ks]

    for qb in range(Q_BLOCKS):
        qr = slice(qb * BLOCK, (qb + 1) * BLOCK)
        for h in range(N_KV_HEADS):
            ks = slice(h * LANES, (h + 1) * LANES)
            k_p, k_c, k_n = (key_block(kp_ref, kc_ref, kn_ref, qb + rel, ks) for rel in (-1, 0, 1))
            v_p, v_c, v_n = (key_block(vp_ref, vc_ref, vn_ref, qb + rel, ks) for rel in (-1, 0, 1))
            for j in range(2):
                pair = q_ref[0, qr, (2 * h + j) * LANES:(2 * h + j + 1) * LANES]
                qs = jnp.concatenate([jnp.where(lo, pair, zero), jnp.where(lo, zero, pair)], axis=0)
                br = slice(2 * j * BLOCK, (2 * j + 2) * BLOCK)
                s_p = lax.dot_general(qs, k_p, nt, preferred_element_type=F32) + bias_ref[h, 0, br, :]
                s_c = lax.dot_general(qs, k_c, nt, preferred_element_type=F32) + bias_ref[h, 1, br, :]
                s_n = lax.dot_general(qs, k_n, nt, preferred_element_type=F32) + bias_ref[h, 2, br, :]
                s_m = lax.dot_general(qs, km_ref[:, ks], nt, preferred_element_type=F32) + bias_ref[h, 3, br, :]
                if qb == 0:
                    s_p = s_p + edge_first
                if qb == Q_BLOCKS - 1:
                    s_n = s_n + edge_last
                sink = jnp.concatenate(
                    [jnp.full((BLOCK, 1), sink_ref[h * GROUP + 2 * j + g], F32) for g in range(2)], axis=0)
                m = jnp.max(jnp.maximum(jnp.maximum(s_p, s_c), jnp.maximum(s_n, s_m)), axis=-1, keepdims=True)
                m = jnp.maximum(m, sink)
                p_p = jnp.exp2(s_p - m)
                p_c = jnp.exp2(s_c - m)
                p_n = jnp.exp2(s_n - m)
                p_m = jnp.exp2(s_m - m)
                denom = jnp.sum((p_p + p_c) + (p_n + p_m), axis=-1, keepdims=True) + jnp.exp2(sink - m)
                o = (jnp.dot(p_p.astype(BF16), v_p, preferred_element_type=F32)
                     + jnp.dot(p_c.astype(BF16), v_c, preferred_element_type=F32)
                     + jnp.dot(p_n.astype(BF16), v_n, preferred_element_type=F32)
                     + jnp.dot(p_m.astype(BF16), vm_ref[:, ks], preferred_element_type=F32))
                o = o / denom
                o_ref[0, qr, (2 * h + j) * LANES:(2 * h + j + 1) * LANES] = jnp.where(
                    lo, o[:BLOCK], o[BLOCK:]).astype(BF16)


def _attention(sink, q, kd, vd, kmeta, vmeta, bias):
    b, s, _ = q.shape
    nb = s // BLOCK
    steps = nb // Q_BLOCKS
    here = lambda bi, i: (bi, i, 0)
    before = lambda bi, i: (bi, jnp.maximum(Q_BLOCKS * i - 1, 0), 0)
    after = lambda bi, i: (bi, jnp.minimum(Q_BLOCKS * i + Q_BLOCKS, nb - 1), 0)
    const2 = lambda bi, i: (0, 0)
    edge_blk = (1, BLOCK, 2 * KV_W)
    here_blk = (1, Q_BLOCKS * BLOCK, 2 * KV_W)
    return pl.pallas_call(
        functools.partial(_attn_kernel, n_steps=steps),
        grid=(b, steps),
        in_specs=[
            pl.BlockSpec(memory_space=pltpu.SMEM),
            pl.BlockSpec((1, Q_BLOCKS * BLOCK, ATTN_W), here),
            pl.BlockSpec(edge_blk, before), pl.BlockSpec(here_blk, here), pl.BlockSpec(edge_blk, after),
            pl.BlockSpec(edge_blk, before), pl.BlockSpec(here_blk, here), pl.BlockSpec(edge_blk, after),
            pl.BlockSpec((BLOCK, 2 * KV_W), const2),
            pl.BlockSpec((BLOCK, 2 * KV_W), const2),
            pl.BlockSpec(bias.shape, lambda bi, i: (0, 0, 0, 0)),
        ],
        out_specs=pl.BlockSpec((1, Q_BLOCKS * BLOCK, ATTN_W), here),
        out_shape=jax.ShapeDtypeStruct((b, s, ATTN_W), BF16),
        compiler_params=pltpu.CompilerParams(dimension_semantics=("arbitrary", "arbitrary"),
                                             vmem_limit_bytes=VMEM_LIMIT),
        name="attention",
    )(sink, q, kd, kd, kd, vd, vd, vd, kmeta, vmeta, bias)


def _outproj_kernel(attn_ref, hc_ref, hm_ref, cw_ref, cb_ref, lg_ref, lb_ref, x_ref, wa_ref, wc_ref, g2_ref,
                    rwh_ref, rwl_ref, rb_ref, tri_ref, h2_ref, xp_ref, route_ref, cnt_ref,
                    run_ref, pad_ref, y_ref, conv_ref, wab_ref, wcb_ref, *, seq):
    step = pl.program_id(0)
    tiles_per_seq = seq // ROW_TILE
    part = step % tiles_per_seq

    @pl.when(step == 0)
    def _():
        run_ref[...] = jnp.zeros_like(run_ref)
        wab_ref[...] = wa_ref[...].astype(BF16)
        wcb_ref[...] = wc_ref[...].astype(BF16)

    @pl.when(part == 0)
    def _():
        _conv_fill(pad_ref, hm_ref, hc_ref, seq)

    for c in range(ROW_TILE // CONV_ROWS):
        base = pl.multiple_of(part * ROW_TILE + c * CONV_ROWS, CONV_ROWS)
        conv_ref[c * CONV_ROWS:(c + 1) * CONV_ROWS, :] = _conv_chunk(pad_ref, y_ref, cw_ref, cb_ref, lg_ref,
                                                                     lb_ref, base)
    mix = (jnp.dot(attn_ref[...], wab_ref[...], preferred_element_type=F32)
           + jnp.dot(conv_ref[...], wcb_ref[...], preferred_element_type=F32))
    h2 = x_ref[...] + mix
    h2_ref[...] = h2
    hn = _rms(h2, RMS_EPS) * g2_ref[...]
    hn_hi = hn.astype(BF16)
    half = D_MODEL // 2
    xp_ref[...] = _pack_pair(hn[:, :half], hn[:, half:])
    hn_lo = (hn - hn_hi.astype(F32)).astype(BF16)
    nt = (((1,), (1,)), ((), ()))
    logits = (lax.dot_general(rwh_ref[...], hn_hi, nt, preferred_element_type=F32)
              + lax.dot_general(rwh_ref[...], hn_lo, nt, preferred_element_type=F32)
              + lax.dot_general(rwl_ref[...], hn_hi, nt, preferred_element_type=F32)) + rb_ref[...]
    expert = lax.broadcasted_iota(jnp.int32, logits.shape, 0).astype(F32)
    work = logits
    vals, sels = [], []
    for r in range(TOP_K):
        m = jnp.max(work, axis=0, keepdims=True)
        idx = jnp.min(jnp.where(work == m, expert, float(N_EXPERTS)), axis=0, keepdims=True)
        sel = expert == idx
        work = jnp.where(sel, -jnp.inf, work)
        route_ref[r:r + 1, :] = idx
        vals.append(m)
        sels.append(sel)
    onehot = jnp.where(jnp.logical_or(jnp.logical_or(sels[0], sels[1]), jnp.logical_or(sels[2], sels[3])),
                       1.0, 0.0)
    exps = [jnp.exp(v - vals[0]) for v in vals]
    tot = exps[0] + exps[1] + exps[2] + exps[3]
    before = jnp.dot(onehot.astype(BF16), tri_ref[...], preferred_element_type=F32) + run_ref[...]
    for r in range(TOP_K):
        route_ref[TOP_K + r:TOP_K + r + 1, :] = exps[r] / tot
        route_ref[2 * TOP_K + r:2 * TOP_K + r + 1, :] = jnp.sum(jnp.where(sels[r], before, 0.0), axis=0,
                                                               keepdims=True)
    route_ref[3 * TOP_K:, :] = jnp.zeros((ROUTE_ROWS - 3 * TOP_K, logits.shape[1]), F32)
    run_ref[...] = run_ref[...] + jnp.sum(onehot, axis=1, keepdims=True)
    cnt_ref[...] = run_ref[...]


def _outproj(attn2d, hc, hc_meta, dw_w, dw_b, ln_g, ln_b, x2d, wa, wc, g2, rwh, rwl, rb, tri):
    n = x2d.shape[0]
    seq = hc.shape[1]
    tile = ROW_TILE
    const = lambda i: (0, 0)
    row = lambda i: (i, 0)
    half = D_MODEL // 2
    return pl.pallas_call(
        functools.partial(_outproj_kernel, seq=seq),
        grid=(n // tile,),
        in_specs=[
            pl.BlockSpec((tile, ATTN_W), row),
            pl.BlockSpec((1, seq, CONV_W), lambda i: (i // (seq // tile), 0, 0)),
            pl.BlockSpec((N_META, CONV_W), const),
            pl.BlockSpec((CONV_K, CONV_W), const),
            pl.BlockSpec((1, CONV_W), const),
            pl.BlockSpec((1, CONV_W), const),
            pl.BlockSpec((1, CONV_W), const),
            pl.BlockSpec((tile, D_MODEL), row),
            pl.BlockSpec((ATTN_W, D_MODEL), const),
            pl.BlockSpec((CONV_W, D_MODEL), lambda i: (ATTN_W // CONV_W, 0)),
            pl.BlockSpec((1, D_MODEL), const),
            pl.BlockSpec((N_EXPERTS, D_MODEL), const),
            pl.BlockSpec((N_EXPERTS, D_MODEL), const),
            pl.BlockSpec((N_EXPERTS, 1), const),
            pl.BlockSpec((tile, tile), const),
        ],
        out_specs=[
            pl.BlockSpec((tile, D_MODEL), row),
            pl.BlockSpec((tile, half), row),
            pl.BlockSpec((ROUTE_ROWS, tile), lambda i: (0, i)),
            pl.BlockSpec((N_EXPERTS, 1), const),
        ],
        out_shape=[
            jax.ShapeDtypeStruct((n, D_MODEL), F32),
            jax.ShapeDtypeStruct((n, half), jnp.uint32),
            jax.ShapeDtypeStruct((ROUTE_ROWS, n), F32),
            jax.ShapeDtypeStruct((N_EXPERTS, 1), F32),
        ],
        scratch_shapes=[pltpu.VMEM((N_EXPERTS, 1), F32),
                        pltpu.VMEM((N_META + seq + 32, CONV_W), F32),
                        pltpu.VMEM((CONV_ROWS, CONV_W), F32),
                        pltpu.VMEM((tile, CONV_W), BF16),
                        pltpu.VMEM((ATTN_W, D_MODEL), BF16),
                        pltpu.VMEM((CONV_W, D_MODEL), BF16)],
        compiler_params=pltpu.CompilerParams(dimension_semantics=("arbitrary",),
                                             vmem_limit_bytes=VMEM_LIMIT),
        name="outproj",
    )(attn2d, hc, hc_meta, dw_w, dw_b, ln_g, ln_b, x2d, wa, wc, g2, rwh, rwl, rb, tri)


def _plan_kernel(route_ref, cnt_ref, pos_ref, sched_ref):
    tile_f = float(EXPERT_TILE)
    cnt = jnp.broadcast_to(cnt_ref[...], (N_EXPERTS, PLAN_LANES))
    tiles = jnp.floor((cnt + (tile_f - 1.0)) / tile_f)
    e_row = lax.broadcasted_iota(jnp.int32, (N_EXPERTS, N_EXPERTS), 0)
    e_col = lax.broadcasted_iota(jnp.int32, (N_EXPERTS, N_EXPERTS), 1)
    lower = jnp.where(e_col <= e_row, 1.0, 0.0).astype(jnp.bfloat16)
    tile_end = jnp.dot(lower, tiles.astype(jnp.bfloat16), preferred_element_type=F32)
    tile_start = tile_end - tiles
    total = jnp.max(tile_end, axis=0, keepdims=True)
    w = lax.broadcasted_iota(jnp.int32, (1, PLAN_LANES), 1).astype(F32)
    wc = jnp.minimum(w, total - 1.0)
    expert_of = jnp.sum(jnp.where(tile_end <= wc, 1.0, 0.0), axis=0, keepdims=True)
    e_id = lax.broadcasted_iota(jnp.int32, (N_EXPERTS, PLAN_LANES), 0).astype(F32)
    mine = e_id == expert_of
    pick = lambda table: jnp.sum(jnp.where(mine, table, 0.0), axis=0, keepdims=True)
    live = jnp.clip(pick(cnt) - (wc - pick(tile_start)) * tile_f, 0.0, tile_f)
    in_list = w < total
    live = jnp.where(in_list, live, 0.0)
    prev_e = jnp.where(w == 0.0, -1.0, pltpu.roll(expert_of, 1, 1))
    new_e = jnp.where(jnp.logical_and(in_list, expert_of != prev_e), 1.0, 0.0)
    l_row = lax.broadcasted_iota(jnp.int32, (PLAN_LANES, PLAN_LANES), 0)
    l_col = lax.broadcasted_iota(jnp.int32, (PLAN_LANES, PLAN_LANES), 1)
    upper = jnp.where(l_row <= l_col, 1.0, 0.0).astype(jnp.bfloat16)
    new8 = jnp.broadcast_to(new_e, (8, PLAN_LANES))
    seen = jnp.dot(new8.astype(jnp.bfloat16), upper, preferred_element_type=F32)[0:1]
    slot = (seen - 1.0) - 2.0 * jnp.floor((seen - 1.0) * 0.5)
    exp8 = jnp.broadcast_to(expert_of, (8, PLAN_LANES))
    new_col = jnp.transpose(new8)[:, 0:1]
    exp_col = jnp.transpose(exp8)[:, 0:1]
    cand = l_row.astype(F32)
    later = jnp.logical_and(new_col > 0.0, cand > w)
    next_w = jnp.min(jnp.where(later, cand, float(PLAN_LANES)), axis=0, keepdims=True)
    next_e = jnp.sum(jnp.where(cand == next_w, exp_col, 0.0), axis=0, keepdims=True)
    next_e = jnp.where(next_w < float(PLAN_LANES), next_e, -1.0)
    rows = [wc, expert_of, live, new_e, slot, next_e, jnp.zeros_like(w), jnp.zeros_like(w)]
    sched_ref[...] = jnp.concatenate(rows, axis=0).astype(jnp.int32)
    idx = route_ref[0:TOP_K, :]
    posf = route_ref[2 * TOP_K:3 * TOP_K, :]
    for e in range(N_EXPERTS):
        first_row = tile_start[e:e + 1, 0:1] * tile_f
        posf = posf + jnp.where(idx == float(e), first_row, 0.0)
    pos_ref[...] = posf.astype(jnp.int32)


def _plan(route, counts):
    n = route.shape[1]
    assert n * TOP_K // EXPERT_TILE + N_EXPERTS <= PLAN_LANES
    return pl.pallas_call(
        _plan_kernel,
        out_shape=[jax.ShapeDtypeStruct((TOP_K, n), jnp.int32), jax.ShapeDtypeStruct((8, PLAN_LANES), jnp.int32)],
        compiler_params=pltpu.CompilerParams(vmem_limit_bytes=VMEM_LIMIT),
        name="plan",
    )(route, counts)


def _sc_mesh():
    return plsc.VectorSubcoreMesh(core_axis_name="c", subcore_axis_name="s",
                                  num_cores=SC_CORES, num_subcores=SC_SUBCORES)


def _sc_worker():
    return lax.axis_index("s") * SC_CORES + lax.axis_index("c")


def _sc_dispatch(pos3, xp, out_rows):
    n, half = xp.shape
    per_worker = n // SC_CHUNK // SC_WORKERS

    @functools.partial(
        pl.kernel, mesh=_sc_mesh(),
        out_type=jax.ShapeDtypeStruct((out_rows, half), jnp.uint32),
        scratch_types=[pltpu.VMEM((2, TOP_K, SC_CHUNK), jnp.int32),
                       pltpu.VMEM((2, SC_CHUNK, half), jnp.uint32),
                       pltpu.SemaphoreType.DMA((2,)),
                       pltpu.SemaphoreType.DMA],
        name="sc_dispatch")
    def run(pos_hbm, xp_hbm, xs_hbm, idx_v, rows_v, load_sem, scatter_sem):
        first = _sc_worker() * per_worker

        def load(j, buf):
            c = first + j
            return (pltpu.async_copy(pos_hbm.at[c], idx_v.at[buf], load_sem.at[buf]),
                    pltpu.async_copy(xp_hbm.at[pl.ds(c * SC_CHUNK, SC_CHUNK)], rows_v.at[buf], load_sem.at[buf]))

        loads = load(0, 0)
        for j in range(per_worker):
            buf = j % 2
            for cp in loads:
                cp.wait()
            if j + 1 < per_worker:
                loads = load(j + 1, 1 - buf)
            copies = [pltpu.async_copy(rows_v.at[buf], xs_hbm.at[idx_v.at[buf].at[k]], scatter_sem)
                      for k in range(TOP_K)]
            for cp in copies:
                cp.wait()

    return run(pos3, xp)


def _sc_collect(pos3, y):
    half = y.shape[1]
    chunk = pos3.shape[2]
    n = pos3.shape[0] * chunk
    per_worker = n // chunk // SC_WORKERS

    @functools.partial(
        pl.kernel, mesh=_sc_mesh(),
        out_type=jax.ShapeDtypeStruct((TOP_K, n, half), jnp.uint32),
        scratch_types=[pltpu.VMEM((TOP_K, chunk), jnp.int32),
                       pltpu.VMEM((TOP_K, chunk, half), jnp.uint32),
                       pltpu.SemaphoreType.DMA((TOP_K,)),
                       pltpu.SemaphoreType.DMA((TOP_K,))],
        name="sc_collect")
    def run(pos_hbm, y_hbm, yg_hbm, idx_v, rows_v, gather_sem, store_sem):
        first = _sc_worker() * per_worker

        @pl.loop(0, per_worker)
        def _(j):
            c = first + j
            pltpu.sync_copy(pos_hbm.at[c], idx_v)
            gathers = [pltpu.async_copy(y_hbm.at[idx_v.at[k]], rows_v.at[k], gather_sem.at[k])
                       for k in range(TOP_K)]
            stores = []
            for k in range(TOP_K):
                gathers[k].wait()
                stores.append(pltpu.async_copy(rows_v.at[k], yg_hbm.at[k, pl.ds(c * chunk, chunk)],
                                               store_sem.at[k]))
            for st in stores:
                st.wait()

    return run(pos3, y)


S_TILE, S_EXPERT, S_ROWS, S_NEW, S_SLOT, S_NEXT = range(6)


def _expert_kernel(sched_ref, xs_ref, wgu_hbm, bgu_ref, wd_hbm, bd_ref, y_ref,
                   act_ref, wgu_buf, wd_buf, sem):
    w = pl.program_id(0)
    e = sched_ref[S_EXPERT, w]
    slot = sched_ref[S_SLOT, w]
    n_rows = sched_ref[S_ROWS, w]

    def weight_copies(expert, s):
        return (pltpu.make_async_copy(wgu_hbm.at[expert], wgu_buf.at[s], sem.at[0, s]),
                pltpu.make_async_copy(wd_hbm.at[expert], wd_buf.at[s], sem.at[1, s]))

    @pl.when(sched_ref[S_NEW, w] == 1)
    def _():
        @pl.when(w == 0)
        def _():
            for cp in weight_copies(e, slot):
                cp.start()

        nxt = sched_ref[S_NEXT, w]

        @pl.when(nxt >= 0)
        def _():
            for cp in weight_copies(nxt, 1 - slot):
                cp.start()

        for cp in weight_copies(e, slot):
            cp.wait()

    def tile_path(m):
        half = D_MODEL // 2
        live = lax.broadcasted_iota(jnp.int32, (m, 1), 0) < n_rows
        x_lo, x_hi = _unpack_pair(jnp.where(live, xs_ref[0:m, :], jnp.uint32(0)))
        x = jnp.concatenate([x_lo.astype(BF16), x_hi.astype(BF16)], axis=1)
        for c in range(D_FF // FF_CHUNK):
            gc = slice(c * FF_CHUNK, (c + 1) * FF_CHUNK)
            uc = slice(D_FF + c * FF_CHUNK, D_FF + (c + 1) * FF_CHUNK)
            gate = jnp.dot(x, wgu_buf[slot, :, gc].astype(BF16), preferred_element_type=F32) + bgu_ref[0, :, gc]
            up = jnp.dot(x, wgu_buf[slot, :, uc].astype(BF16), preferred_element_type=F32) + bgu_ref[0, :, uc]
            gate = jnp.minimum(gate, SWIGLU_LIMIT)
            up = jnp.clip(up, -SWIGLU_LIMIT, SWIGLU_LIMIT)
            act_ref[0:m, gc] = (gate * jax.nn.sigmoid(SWIGLU_ALPHA * gate) * (up + 1.0)).astype(BF16)
        for c in range(half // FF_CHUNK):
            lo_c = slice(c * FF_CHUNK, (c + 1) * FF_CHUNK)
            hi_c = slice(half + c * FF_CHUNK, half + (c + 1) * FF_CHUNK)
            y_lo = (jnp.dot(act_ref[0:m, :], wd_buf[slot, :, lo_c].astype(BF16), preferred_element_type=F32)
                    + bd_ref[0, :, lo_c])
            y_hi = (jnp.dot(act_ref[0:m, :], wd_buf[slot, :, hi_c].astype(BF16), preferred_element_type=F32)
                    + bd_ref[0, :, hi_c])
            y_ref[0:m, lo_c] = _pack_pair(y_lo, y_hi)

    for below, m in zip((0,) + TILE_SPANS[:-1], TILE_SPANS):
        pl.when(jnp.logical_and(n_rows > below, n_rows <= m))(functools.partial(tile_path, m))


def _experts(sched, xs, wgu, bgu, wd, bd):
    rows, half = xs.shape
    n_work = rows // EXPERT_TILE
    grid_spec = pltpu.PrefetchScalarGridSpec(
        num_scalar_prefetch=1,
        grid=(n_work,),
        in_specs=[
            pl.BlockSpec((EXPERT_TILE, half), lambda w, sc: (sc[S_TILE, w], 0)),
            pl.BlockSpec(memory_space=pl.ANY),
            pl.BlockSpec((1, 1, 2 * D_FF), lambda w, sc: (sc[S_EXPERT, w], 0, 0)),
            pl.BlockSpec(memory_space=pl.ANY),
            pl.BlockSpec((1, 1, D_MODEL), lambda w, sc: (sc[S_EXPERT, w], 0, 0)),
        ],
        out_specs=pl.BlockSpec((EXPERT_TILE, half), lambda w, sc: (sc[S_TILE, w], 0)),
        scratch_shapes=[pltpu.VMEM((EXPERT_TILE, D_FF), BF16),
                        pltpu.VMEM((2, D_MODEL, 2 * D_FF), F32),
                        pltpu.VMEM((2, D_FF, D_MODEL), F32),
                        pltpu.SemaphoreType.DMA((2, 2))],
    )
    return pl.pallas_call(
        _expert_kernel,
        grid_spec=grid_spec,
        out_shape=jax.ShapeDtypeStruct((rows, half), jnp.uint32),
        compiler_params=pltpu.CompilerParams(dimension_semantics=("arbitrary",),
                                             vmem_limit_bytes=VMEM_LIMIT),
        name="experts",
    )(sched, xs, wgu, bgu, wd, bd)


def _unpack_pair(packed):
    lo = lax.bitcast_convert_type(packed << 16, F32)
    hi = lax.bitcast_convert_type(packed & jnp.uint32(0xFFFF0000), F32)
    return lo, hi


def _pack_pair(lo, hi):
    lo_bits = lax.bitcast_convert_type(lo.astype(BF16).astype(F32), jnp.uint32)
    hi_bits = lax.bitcast_convert_type(hi.astype(BF16).astype(F32), jnp.uint32)
    return (lo_bits >> 16) | (hi_bits & jnp.uint32(0xFFFF0000))


def _combine_kernel(yg_ref, h2_ref, route_ref, o_ref):
    half = D_MODEL // 2
    by_token = jnp.transpose(route_ref[...])
    lo_sum = h2_ref[:, :half]
    hi_sum = h2_ref[:, half:]
    for k in range(TOP_K):
        gate = by_token[:, TOP_K + k:TOP_K + k + 1]
        lo, hi = _unpack_pair(yg_ref[k])
        lo_sum = lo_sum + gate * lo
        hi_sum = hi_sum + gate * hi
    o_ref[:, :half] = lo_sum
    o_ref[:, half:] = hi_sum


def _combine(yg, h2, route, part):
    n = h2.shape[0]
    half = D_MODEL // 2
    steps = yg.shape[1] // ROW_TILE
    row = lambda i: (i + part * steps, 0)
    return pl.pallas_call(
        _combine_kernel,
        grid=(steps,),
        in_specs=[
            pl.BlockSpec((TOP_K, ROW_TILE, half), lambda i: (0, i, 0)),
            pl.BlockSpec((ROW_TILE, D_MODEL), row),
            pl.BlockSpec((2 * TOP_K, ROW_TILE), lambda i: (0, i + part * steps)),
        ],
        out_specs=pl.BlockSpec((ROW_TILE, D_MODEL), row),
        out_shape=jax.ShapeDtypeStruct((n, D_MODEL), F32),
        input_output_aliases={1: 0},
        compiler_params=pltpu.CompilerParams(dimension_semantics=("arbitrary",),
                                             vmem_limit_bytes=VMEM_LIMIT),
        name="combine",
    )(yg, h2, route)


def kernel(x, meta_tokens, norm1_g, w_in, q_norm_g, k_norm_g, attn_sink, dw_w, dw_b, conv_ln_g,
           conv_ln_b, w_out, norm2_g, router_w, router_b, w_gate_up, b_gate_up, w_down, b_down):
    assert norm1_g.shape[0] == 1, "single-layer trunk: meta-token query rows are not materialised"
    b, s, d = x.shape
    n = b * s
    x2d = x.reshape(n, d)

    scale = HEAD_DIM ** -0.5 * LOG2_E
    qg = (jnp.tile(q_norm_g[0], N_Q_HEADS) * scale).reshape(1, ATTN_W)
    kg = jnp.tile(k_norm_g[0], N_KV_HEADS).reshape(1, KV_W)
    head_of = np.arange(ATTN_W) // HEAD_DIM
    pq = jnp.asarray((head_of[:, None] == head_of[None, :]) / HEAD_DIM, BF16)
    pk = pq[:KV_W, :KV_W]
    g1 = norm1_g[0].reshape(1, d)

    q, kd, vd, hc = _inproj(x2d, g1, w_in[0], qg, kg, pq, pk, ROW_TILE)
    meta_rows = jnp.pad(meta_tokens, ((0, BLOCK - N_META), (0, 0)))
    _, kmeta, vmeta, hc_meta = _inproj(meta_rows, g1, w_in[0], qg, kg, pq, pk, BLOCK)

    attn = _attention(attn_sink[0] * LOG2_E, q.reshape(b, s, ATTN_W), kd.reshape(b, s, 2 * KV_W),
                      vd.reshape(b, s, 2 * KV_W), kmeta, vmeta, jnp.asarray(_alibi_bias()))

    rw = router_w[0].T
    rwh = rw.astype(BF16)
    rwl = (rw - rwh.astype(F32)).astype(BF16)
    rb = router_b[0].reshape(N_EXPERTS, 1)
    tri = jnp.asarray(np.triu(np.ones((ROW_TILE, ROW_TILE), np.float32), 1), BF16)
    h2, xp, route, counts = _outproj(attn.reshape(n, ATTN_W), hc.reshape(b, s, CONV_W), hc_meta,
                                     dw_w[0].reshape(CONV_K, CONV_W), dw_b[0].reshape(1, CONV_W),
                                     conv_ln_g[0].reshape(1, CONV_W), conv_ln_b[0].reshape(1, CONV_W), x2d,
                                     w_out[0], w_out[0], norm2_g[0].reshape(1, d),
                                     rwh, rwl, rb, tri)

    pos, sched = _plan(route, counts)
    n_work = n * TOP_K // EXPERT_TILE + N_EXPERTS
    pos3 = pos.reshape(TOP_K, n // SC_CHUNK, SC_CHUNK).transpose(1, 0, 2)

    xs = _sc_dispatch(pos3, xp, n_work * EXPERT_TILE)
    y = _experts(sched, xs, w_gate_up[0],
                 b_gate_up[0].reshape(N_EXPERTS, 1, 2 * D_FF), w_down[0],
                 b_down[0].reshape(N_EXPERTS, 1, D_MODEL))
    out = h2
    pos3c = pos.reshape(TOP_K, n // COLLECT_CHUNK, COLLECT_CHUNK).transpose(1, 0, 2)
    chunks = pos3c.shape[0] // COMBINE_PARTS
    for part in range(COMBINE_PARTS):
        out = _combine(_sc_collect(pos3c[part * chunks:(part + 1) * chunks], y), out, route, part)
    return out.reshape(b, s, d)
```

```python
import functools

import numpy as np
import jax
import jax.numpy as jnp
from jax import lax
from jax.experimental import pallas as pl
from jax.experimental.pallas import tpu as pltpu
from jax.experimental.pallas import tpu_sc as plsc

F32 = jnp.float32
BF16 = jnp.bfloat16

D_MODEL = 1024
N_META = 16
HEAD_DIM = 64
N_Q_HEADS = 8
N_KV_HEADS = 2
GROUP = N_Q_HEADS // N_KV_HEADS
ATTN_W = N_Q_HEADS * HEAD_DIM
KV_W = N_KV_HEADS * HEAD_DIM
CONV_W = D_MODEL - ATTN_W
IN_W = ATTN_W + 2 * KV_W + 2 * CONV_W
WINDOW = 128
BLOCK = 128
CONV_K = 31
CONV_PAD = CONV_K // 2
N_EXPERTS = 32
TOP_K = 4
D_FF = D_MODEL
SWIGLU_LIMIT = 7.0
SWIGLU_ALPHA = 1.702
RMS_EPS = 1e-6
LN_EPS = 1e-5
NEG_INF = -1e30
LOG2_E = 1.4426950408889634

LANES = 128
ROW_TILE = 1024
EXPERT_TILE = 1024
SC_CORES = 2
SC_SUBCORES = 16
SC_WORKERS = SC_CORES * SC_SUBCORES
SC_CHUNK = 64
COLLECT_CHUNK = 32
CONV_ROWS = 128
FF_CHUNK = 512
TILE_SPANS = (128, 256, 512, 768, 1024)
Q_BLOCKS = 16
COMBINE_PARTS = 8
PLAN_LANES = 128
ROUTE_ROWS = 16
VMEM_LIMIT = 56 * 1024 * 1024


def _rms(x, eps):
    return x * lax.rsqrt(jnp.mean(x * x, axis=-1, keepdims=True) + eps)


def _inproj_kernel(x_ref, g1_ref, w_ref, qg_ref, kg_ref, pq_ref, pk_ref,
                   q_ref, k_ref, v_ref, hc_ref, wb_ref):
    @pl.when(pl.program_id(0) == 0)
    def _():
        wb_ref[...] = w_ref[...].astype(BF16)

    x = x_ref[...]
    n = (_rms(x, RMS_EPS) * g1_ref[...]).astype(BF16)
    proj = jnp.dot(n, wb_ref[...], preferred_element_type=F32)
    q = proj[:, :ATTN_W]
    k = proj[:, ATTN_W:ATTN_W + KV_W]
    v = proj[:, ATTN_W + KV_W:ATTN_W + 2 * KV_W]
    a = proj[:, ATTN_W + 2 * KV_W:ATTN_W + 2 * KV_W + CONV_W]
    g = proj[:, ATTN_W + 2 * KV_W + CONV_W:]
    qms = jnp.dot((q * q).astype(BF16), pq_ref[...], preferred_element_type=F32)
    kms = jnp.dot((k * k).astype(BF16), pk_ref[...], preferred_element_type=F32)
    q_ref[...] = (q * lax.rsqrt(qms + RMS_EPS) * qg_ref[...]).astype(BF16)
    kn = k * lax.rsqrt(kms + RMS_EPS) * kg_ref[...]
    lo = lax.broadcasted_iota(jnp.int32, kn.shape, 1) < HEAD_DIM
    ksw = pltpu.roll(kn, HEAD_DIM, 1)
    k_ref[...] = jnp.concatenate([jnp.where(lo, kn, ksw), jnp.where(lo, ksw, kn)], axis=1).astype(BF16)
    vsw = pltpu.roll(v, HEAD_DIM, 1)
    v_ref[...] = jnp.concatenate([jnp.where(lo, v, vsw), jnp.where(lo, vsw, v)], axis=1).astype(BF16)
    hc_ref[...] = a * jax.nn.sigmoid(g)


def _inproj(x2d, g1, w_in, qg, kg, pq, pk, tile):
    n = x2d.shape[0]
    const = lambda i: (0, 0)
    row = lambda i: (i, 0)
    return pl.pallas_call(
        _inproj_kernel,
        grid=(n // tile,),
        in_specs=[
            pl.BlockSpec((tile, D_MODEL), row),
            pl.BlockSpec((1, D_MODEL), const),
            pl.BlockSpec((D_MODEL, IN_W), const),
            pl.BlockSpec((1, ATTN_W), const),
            pl.BlockSpec((1, KV_W), const),
            pl.BlockSpec((ATTN_W, ATTN_W), const),
            pl.BlockSpec((KV_W, KV_W), const),
        ],
        out_specs=[
            pl.BlockSpec((tile, ATTN_W), row),
            pl.BlockSpec((tile, 2 * KV_W), row),
            pl.BlockSpec((tile, 2 * KV_W), row),
            pl.BlockSpec((tile, CONV_W), row),
        ],
        out_shape=[
            jax.ShapeDtypeStruct((n, ATTN_W), BF16),
            jax.ShapeDtypeStruct((n, 2 * KV_W), BF16),
            jax.ShapeDtypeStruct((n, 2 * KV_W), BF16),
            jax.ShapeDtypeStruct((n, CONV_W), F32),
        ],
        scratch_shapes=[pltpu.VMEM((D_MODEL, IN_W), BF16)],
        compiler_params=pltpu.CompilerParams(dimension_semantics=("arbitrary",),
                                             vmem_limit_bytes=VMEM_LIMIT),
        name="inproj",
    )(x2d, g1, w_in, qg, kg, pq, pk)


def _conv_fill(pad_ref, hm_ref, hc_ref, seq):
    tail = pad_ref.shape[0] - (N_META + seq)
    pad_ref[0:N_META, :] = hm_ref[...]
    pad_ref[N_META:N_META + seq, :] = hc_ref[0]
    pad_ref[N_META + seq:, :] = jnp.zeros((tail, CONV_W), F32)


def _conv_chunk(pad_ref, y_ref, w_ref, b_ref, lg_ref, lb_ref, base):
    first = N_META - CONV_PAD
    span = CONV_ROWS + 8
    for lt in range(CONV_W // LANES):
        ls = slice(lt * LANES, (lt + 1) * LANES)
        win = pad_ref[pl.ds(base, CONV_ROWS + 32), ls]
        acc = None
        for sub in range(8):
            part = None
            for al in range(4):
                k = 8 * al + sub - first
                if 0 <= k < CONV_K:
                    term = win[8 * al:8 * al + span] * w_ref[k:k + 1, ls]
                    part = term if part is None else part + term
            if sub:
                part = pltpu.roll(part, span - sub, 0)
            acc = part[:CONV_ROWS] if acc is None else acc + part[:CONV_ROWS]
        y_ref[:, ls] = acc
    y = y_ref[...] + b_ref[...]
    mu = jnp.mean(y, axis=-1, keepdims=True)
    yc = y - mu
    var = jnp.mean(yc * yc, axis=-1, keepdims=True)
    z = yc * lax.rsqrt(var + LN_EPS) * lg_ref[...] + lb_ref[...]
    return (z * jax.nn.sigmoid(z)).astype(BF16)


def _alibi_bias():
    qi = np.arange(BLOCK)[:, None]
    kj = np.arange(BLOCK)[None, :]
    dists = [qi + BLOCK - kj, np.abs(qi - kj), kj + BLOCK - qi]
    out = np.zeros((N_KV_HEADS, 4, GROUP * BLOCK, BLOCK), np.float32)
    for h in range(N_KV_HEADS):
        for g in range(GROUP):
            slope = LOG2_E * 2.0 ** (-8.0 * (h * GROUP + g + 1) / N_Q_HEADS)
            for p, d in enumerate(dists):
                out[h, p, g * BLOCK:(g + 1) * BLOCK] = np.where(d <= WINDOW, -slope * d, NEG_INF)
    out[:, 3, :, N_META:] = NEG_INF
    return out


def _attn_kernel(sink_ref, q_ref, kp_ref, kc_ref, kn_ref, vp_ref, vc_ref, vn_ref,
                 km_ref, vm_ref, bias_ref, o_ref, *, n_steps):
    i = pl.program_id(1)
    lo = lax.broadcasted_iota(jnp.int32, (BLOCK, LANES), 1) < HEAD_DIM
    edge_first = jnp.where(i == 0, NEG_INF, 0.0).astype(F32)
    edge_last = jnp.where(i == n_steps - 1, NEG_INF, 0.0).astype(F32)
    nt = (((1,), (1,)), ((), ()))
    zero = jnp.zeros((BLOCK, LANES), BF16)

    def key_block(before_ref, here_ref, after_ref, idx, ks):
        if idx < 0:
            return before_ref[0, :, ks]
        if idx >= Q_BLOCKS:
            return after_ref[0, :, ks]
        return here_ref[0, idx * BLOCK:(idx + 1) * BLOCK, ks]

    for qb in range(Q_BLOCKS):
        qr = slice(qb * BLOCK, (qb + 1) * BLOCK)
        for h in range(N_KV_HEADS):
            ks = slice(h * LANES, (h + 1) * LANES)
            rows = []
            for j in range(2):
                pair = q_ref[0, qr, (2 * h + j) * LANES:(2 * h + j + 1) * LANES]
                rows.append(jnp.where(lo, pair, zero))
                rows.append(jnp.where(lo, zero, pair))
            qs = jnp.concatenate(rows, axis=0)
            k_p, k_c, k_n = (key_block(kp_ref, kc_ref, kn_ref, qb + rel, ks) for rel in (-1, 0, 1))
            v_p, v_c, v_n = (key_block(vp_ref, vc_ref, vn_ref, qb + rel, ks) for rel in (-1, 0, 1))
            s_p = lax.dot_general(qs, k_p, nt, preferred_element_type=F32) + bias_ref[h, 0]
            s_c = lax.dot_general(qs, k_c, nt, preferred_element_type=F32) + bias_ref[h, 1]
            s_n = lax.dot_general(qs, k_n, nt, preferred_element_type=F32) + bias_ref[h, 2]
            s_m = lax.dot_general(qs, km_ref[:, ks], nt, preferred_element_type=F32) + bias_ref[h, 3]
            if qb == 0:
                s_p = s_p + edge_first
            if qb == Q_BLOCKS - 1:
                s_n = s_n + edge_last
            sink = jnp.concatenate(
                [jnp.full((BLOCK, 1), sink_ref[h * GROUP + g], F32) for g in range(GROUP)], axis=0)
            m = jnp.max(jnp.maximum(jnp.maximum(s_p, s_c), jnp.maximum(s_n, s_m)), axis=-1, keepdims=True)
            m = jnp.maximum(m, sink)
            p_p = jnp.exp2(s_p - m)
            p_c = jnp.exp2(s_c - m)
            p_n = jnp.exp2(s_n - m)
            p_m = jnp.exp2(s_m - m)
            denom = jnp.sum((p_p + p_c) + (p_n + p_m), axis=-1, keepdims=True) + jnp.exp2(sink - m)
            o = (jnp.dot(p_p.astype(BF16), v_p, preferred_element_type=F32)
                 + jnp.dot(p_c.astype(BF16), v_c, preferred_element_type=F32)
                 + jnp.dot(p_n.astype(BF16), v_n, preferred_element_type=F32)
                 + jnp.dot(p_m.astype(BF16), vm_ref[:, ks], preferred_element_type=F32))
            o = o / denom
            for j in range(2):
                even = o[(2 * j) * BLOCK:(2 * j + 1) * BLOCK]
                odd = o[(2 * j + 1) * BLOCK:(2 * j + 2) * BLOCK]
                o_ref[0, qr, (2 * h + j) * LANES:(2 * h + j + 1) * LANES] = jnp.where(lo, even, odd).astype(BF16)


def _attention(sink, q, kd, vd, kmeta, vmeta, bias):
    b, s, _ = q.shape
    nb = s // BLOCK
    steps = nb // Q_BLOCKS
    here = lambda bi, i: (bi, i, 0)
    before = lambda bi, i: (bi, jnp.maximum(Q_BLOCKS * i - 1, 0), 0)
    after = lambda bi, i: (bi, jnp.minimum(Q_BLOCKS * i + Q_BLOCKS, nb - 1), 0)
    const2 = lambda bi, i: (0, 0)
    edge_blk = (1, BLOCK, 2 * KV_W)
    here_blk = (1, Q_BLOCKS * BLOCK, 2 * KV_W)
    return pl.pallas_call(
        functools.partial(_attn_kernel, n_steps=steps),
        grid=(b, steps),
        in_specs=[
            pl.BlockSpec(memory_space=pltpu.SMEM),
            pl.BlockSpec((1, Q_BLOCKS * BLOCK, ATTN_W), here),
            pl.BlockSpec(edge_blk, before), pl.BlockSpec(here_blk, here), pl.BlockSpec(edge_blk, after),
            pl.BlockSpec(edge_blk, before), pl.BlockSpec(here_blk, here), pl.BlockSpec(edge_blk, after),
            pl.BlockSpec((BLOCK, 2 * KV_W), const2),
            pl.BlockSpec((BLOCK, 2 * KV_W), const2),
            pl.BlockSpec(bias.shape, lambda bi, i: (0, 0, 0, 0)),
        ],
        out_specs=pl.BlockSpec((1, Q_BLOCKS * BLOCK, ATTN_W), here),
        out_shape=jax.ShapeDtypeStruct((b, s, ATTN_W), BF16),
        compiler_params=pltpu.CompilerParams(dimension_semantics=("arbitrary", "arbitrary"),
                                             vmem_limit_bytes=VMEM_LIMIT),
        name="attention",
    )(sink, q, kd, kd, kd, vd, vd, vd, kmeta, vmeta, bias)


def _outproj_kernel(attn_ref, hc_ref, hm_ref, cw_ref, cb_ref, lg_ref, lb_ref, x_ref, wa_ref, wc_ref, g2_ref,
                    rwh_ref, rwl_ref, rb_ref, tri_ref, h2_ref, xp_ref, route_ref, cnt_ref,
                    run_ref, pad_ref, y_ref, conv_ref, wab_ref, wcb_ref, *, seq):
    step = pl.program_id(0)
    tiles_per_seq = seq // ROW_TILE
    part = step % tiles_per_seq

    @pl.when(step == 0)
    def _():
        run_ref[...] = jnp.zeros_like(run_ref)
        wab_ref[...] = wa_ref[...].astype(BF16)
        wcb_ref[...] = wc_ref[...].astype(BF16)

    @pl.when(part == 0)
    def _():
        _conv_fill(pad_ref, hm_ref, hc_ref, seq)

    for c in range(ROW_TILE // CONV_ROWS):
        base = pl.multiple_of(part * ROW_TILE + c * CONV_ROWS, CONV_ROWS)
        conv_ref[c * CONV_ROWS:(c + 1) * CONV_ROWS, :] = _conv_chunk(pad_ref, y_ref, cw_ref, cb_ref, lg_ref,
                                                                     lb_ref, base)
    mix = (jnp.dot(attn_ref[...], wab_ref[...], preferred_element_type=F32)
           + jnp.dot(conv_ref[...], wcb_ref[...], preferred_element_type=F32))
    h2 = x_ref[...] + mix
    h2_ref[...] = h2
    hn = _rms(h2, RMS_EPS) * g2_ref[...]
    hn_hi = hn.astype(BF16)
    half = D_MODEL // 2
    xp_ref[...] = _pack_pair(hn[:, :half], hn[:, half:])
    hn_lo = (hn - hn_hi.astype(F32)).astype(BF16)
    nt = (((1,), (1,)), ((), ()))
    logits = (lax.dot_general(rwh_ref[...], hn_hi, nt, preferred_element_type=F32)
              + lax.dot_general(rwh_ref[...], hn_lo, nt, preferred_element_type=F32)
              + lax.dot_general(rwl_ref[...], hn_hi, nt, preferred_element_type=F32)) + rb_ref[...]
    expert = lax.broadcasted_iota(jnp.int32, logits.shape, 0).astype(F32)
    work = logits
    vals, sels = [], []
    for r in range(TOP_K):
        m = jnp.max(work, axis=0, keepdims=True)
        idx = jnp.min(jnp.where(work == m, expert, float(N_EXPERTS)), axis=0, keepdims=True)
        sel = expert == idx
        work = jnp.where(sel, -jnp.inf, work)
        route_ref[r:r + 1, :] = idx
        vals.append(m)
        sels.append(sel)
    onehot = jnp.where(jnp.logical_or(jnp.logical_or(sels[0], sels[1]), jnp.logical_or(sels[2], sels[3])),
                       1.0, 0.0)
    exps = [jnp.exp(v - vals[0]) for v in vals]
    tot = exps[0] + exps[1] + exps[2] + exps[3]
    before = jnp.dot(onehot.astype(BF16), tri_ref[...], preferred_element_type=F32) + run_ref[...]
    for r in range(TOP_K):
        route_ref[TOP_K + r:TOP_K + r + 1, :] = exps[r] / tot
        route_ref[2 * TOP_K + r:2 * TOP_K + r + 1, :] = jnp.sum(jnp.where(sels[r], before, 0.0), axis=0,
                                                               keepdims=True)
    route_ref[3 * TOP_K:, :] = jnp.zeros((ROUTE_ROWS - 3 * TOP_K, logits.shape[1]), F32)
    run_ref[...] = run_ref[...] + jnp.sum(onehot, axis=1, keepdims=True)
    cnt_ref[...] = run_ref[...]


def _outproj(attn2d, hc, hc_meta, dw_w, dw_b, ln_g, ln_b, x2d, wa, wc, g2, rwh, rwl, rb, tri):
    n = x2d.shape[0]
    seq = hc.shape[1]
    tile = ROW_TILE
    const = lambda i: (0, 0)
    row = lambda i: (i, 0)
    half = D_MODEL // 2
    return pl.pallas_call(
        functools.partial(_outproj_kernel, seq=seq),
        grid=(n // tile,),
        in_specs=[
            pl.BlockSpec((tile, ATTN_W), row),
            pl.BlockSpec((1, seq, CONV_W), lambda i: (i // (seq // tile), 0, 0)),
            pl.BlockSpec((N_META, CONV_W), const),
            pl.BlockSpec((CONV_K, CONV_W), const),
            pl.BlockSpec((1, CONV_W), const),
            pl.BlockSpec((1, CONV_W), const),
            pl.BlockSpec((1, CONV_W), const),
            pl.BlockSpec((tile, D_MODEL), row),
            pl.BlockSpec((ATTN_W, D_MODEL), const),
            pl.BlockSpec((CONV_W, D_MODEL), lambda i: (ATTN_W // CONV_W, 0)),
            pl.BlockSpec((1, D_MODEL), const),
            pl.BlockSpec((N_EXPERTS, D_MODEL), const),
            pl.BlockSpec((N_EXPERTS, D_MODEL), const),
            pl.BlockSpec((N_EXPERTS, 1), const),
            pl.BlockSpec((tile, tile), const),
        ],
        out_specs=[
            pl.BlockSpec((tile, D_MODEL), row),
            pl.BlockSpec((tile, half), row),
            pl.BlockSpec((ROUTE_ROWS, tile), lambda i: (0, i)),
            pl.BlockSpec((N_EXPERTS, 1), const),
        ],
        out_shape=[
            jax.ShapeDtypeStruct((n, D_MODEL), F32),
            jax.ShapeDtypeStruct((n, half), jnp.uint32),
            jax.ShapeDtypeStruct((ROUTE_ROWS, n), F32),
            jax.ShapeDtypeStruct((N_EXPERTS, 1), F32),
        ],
        scratch_shapes=[pltpu.VMEM((N_EXPERTS, 1), F32),
                        pltpu.VMEM((N_META + seq + 32, CONV_W), F32),
                        pltpu.VMEM((CONV_ROWS, CONV_W), F32),
                        pltpu.VMEM((tile, CONV_W), BF16),
                        pltpu.VMEM((ATTN_W, D_MODEL), BF16),
                        pltpu.VMEM((CONV_W, D_MODEL), BF16)],
        compiler_params=pltpu.CompilerParams(dimension_semantics=("arbitrary",),
                                             vmem_limit_bytes=VMEM_LIMIT),
        name="outproj",
    )(attn2d, hc, hc_meta, dw_w, dw_b, ln_g, ln_b, x2d, wa, wc, g2, rwh, rwl, rb, tri)


def _plan_kernel(route_ref, cnt_ref, pos_ref, sched_ref):
    tile_f = float(EXPERT_TILE)
    cnt = jnp.broadcast_to(cnt_ref[...], (N_EXPERTS, PLAN_LANES))
    tiles = jnp.floor((cnt + (tile_f - 1.0)) / tile_f)
    e_row = lax.broadcasted_iota(jnp.int32, (N_EXPERTS, N_EXPERTS), 0)
    e_col = lax.broadcasted_iota(jnp.int32, (N_EXPERTS, N_EXPERTS), 1)
    lower = jnp.where(e_col <= e_row, 1.0, 0.0).astype(jnp.bfloat16)
    tile_end = jnp.dot(lower, tiles.astype(jnp.bfloat16), preferred_element_type=F32)
    tile_start = tile_end - tiles
    total = jnp.max(tile_end, axis=0, keepdims=True)
    w = lax.broadcasted_iota(jnp.int32, (1, PLAN_LANES), 1).astype(F32)
    wc = jnp.minimum(w, total - 1.0)
    expert_of = jnp.sum(jnp.where(tile_end <= wc, 1.0, 0.0), axis=0, keepdims=True)
    e_id = lax.broadcasted_iota(jnp.int32, (N_EXPERTS, PLAN_LANES), 0).astype(F32)
    mine = e_id == expert_of
    pick = lambda table: jnp.sum(jnp.where(mine, table, 0.0), axis=0, keepdims=True)
    live = jnp.clip(pick(cnt) - (wc - pick(tile_start)) * tile_f, 0.0, tile_f)
    in_list = w < total
    live = jnp.where(in_list, live, 0.0)
    prev_e = jnp.where(w == 0.0, -1.0, pltpu.roll(expert_of, 1, 1))
    new_e = jnp.where(jnp.logical_and(in_list, expert_of != prev_e), 1.0, 0.0)
    l_row = lax.broadcasted_iota(jnp.int32, (PLAN_LANES, PLAN_LANES), 0)
    l_col = lax.broadcasted_iota(jnp.int32, (PLAN_LANES, PLAN_LANES), 1)
    upper = jnp.where(l_row <= l_col, 1.0, 0.0).astype(jnp.bfloat16)
    new8 = jnp.broadcast_to(new_e, (8, PLAN_LANES))
    seen = jnp.dot(new8.astype(jnp.bfloat16), upper, preferred_element_type=F32)[0:1]
    slot = (seen - 1.0) - 2.0 * jnp.floor((seen - 1.0) * 0.5)
    exp8 = jnp.broadcast_to(expert_of, (8, PLAN_LANES))
    new_col = jnp.transpose(new8)[:, 0:1]
    exp_col = jnp.transpose(exp8)[:, 0:1]
    cand = l_row.astype(F32)
    later = jnp.logical_and(new_col > 0.0, cand > w)
    next_w = jnp.min(jnp.where(later, cand, float(PLAN_LANES)), axis=0, keepdims=True)
    next_e = jnp.sum(jnp.where(cand == next_w, exp_col, 0.0), axis=0, keepdims=True)
    next_e = jnp.where(next_w < float(PLAN_LANES), next_e, -1.0)
    rows = [wc, expert_of, live, new_e, slot, next_e, jnp.zeros_like(w), jnp.zeros_like(w)]
    sched_ref[...] = jnp.concatenate(rows, axis=0).astype(jnp.int32)
    idx = route_ref[0:TOP_K, :]
    posf = route_ref[2 * TOP_K:3 * TOP_K, :]
    for e in range(N_EXPERTS):
        first_row = tile_start[e:e + 1, 0:1] * tile_f
        posf = posf + jnp.where(idx == float(e), first_row, 0.0)
    pos_ref[...] = posf.astype(jnp.int32)


def _plan(route, counts):
    n = route.shape[1]
    assert n * TOP_K // EXPERT_TILE + N_EXPERTS <= PLAN_LANES
    return pl.pallas_call(
        _plan_kernel,
        out_shape=[jax.ShapeDtypeStruct((TOP_K, n), jnp.int32), jax.ShapeDtypeStruct((8, PLAN_LANES), jnp.int32)],
        compiler_params=pltpu.CompilerParams(vmem_limit_bytes=VMEM_LIMIT),
        name="plan",
    )(route, counts)


def _sc_mesh():
    return plsc.VectorSubcoreMesh(core_axis_name="c", subcore_axis_name="s",
                                  num_cores=SC_CORES, num_subcores=SC_SUBCORES)


def _sc_worker():
    return lax.axis_index("s") * SC_CORES + lax.axis_index("c")


def _sc_dispatch(pos3, xp, out_rows):
    n, half = xp.shape
    per_worker = n // SC_CHUNK // SC_WORKERS

    @functools.partial(
        pl.kernel, mesh=_sc_mesh(),
        out_type=jax.ShapeDtypeStruct((out_rows, half), jnp.uint32),
        scratch_types=[pltpu.VMEM((2, TOP_K, SC_CHUNK), jnp.int32),
                       pltpu.VMEM((2, SC_CHUNK, half), jnp.uint32),
                       pltpu.SemaphoreType.DMA((2,)),
                       pltpu.SemaphoreType.DMA],
        name="sc_dispatch")
    def run(pos_hbm, xp_hbm, xs_hbm, idx_v, rows_v, load_sem, scatter_sem):
        first = _sc_worker() * per_worker

        def load(j, buf):
            c = first + j
            return (pltpu.async_copy(pos_hbm.at[c], idx_v.at[buf], load_sem.at[buf]),
                    pltpu.async_copy(xp_hbm.at[pl.ds(c * SC_CHUNK, SC_CHUNK)], rows_v.at[buf], load_sem.at[buf]))

        loads = load(0, 0)
        for j in range(per_worker):
            buf = j % 2
            for cp in loads:
                cp.wait()
            if j + 1 < per_worker:
                loads = load(j + 1, 1 - buf)
            copies = [pltpu.async_copy(rows_v.at[buf], xs_hbm.at[idx_v.at[buf].at[k]], scatter_sem)
                      for k in range(TOP_K)]
            for cp in copies:
                cp.wait()

    return run(pos3, xp)


def _sc_collect(pos3, y):
    half = y.shape[1]
    chunk = pos3.shape[2]
    n = pos3.shape[0] * chunk
    per_worker = n // chunk // SC_WORKERS

    @functools.partial(
        pl.kernel, mesh=_sc_mesh(),
        out_type=jax.ShapeDtypeStruct((TOP_K, n, half), jnp.uint32),
        scratch_types=[pltpu.VMEM((TOP_K, chunk), jnp.int32),
                       pltpu.VMEM((TOP_K, chunk, half), jnp.uint32),
                       pltpu.SemaphoreType.DMA((TOP_K,)),
                       pltpu.SemaphoreType.DMA((TOP_K,))],
        name="sc_collect")
    def run(pos_hbm, y_hbm, yg_hbm, idx_v, rows_v, gather_sem, store_sem):
        first = _sc_worker() * per_worker

        @pl.loop(0, per_worker)
        def _(j):
            c = first + j
            pltpu.sync_copy(pos_hbm.at[c], idx_v)
            gathers = [pltpu.async_copy(y_hbm.at[idx_v.at[k]], rows_v.at[k], gather_sem.at[k])
                       for k in range(TOP_K)]
            stores = []
            for k in range(TOP_K):
                gathers[k].wait()
                stores.append(pltpu.async_copy(rows_v.at[k], yg_hbm.at[k, pl.ds(c * chunk, chunk)],
                                               store_sem.at[k]))
            for st in stores:
                st.wait()

    return run(pos3, y)


S_TILE, S_EXPERT, S_ROWS, S_NEW, S_SLOT, S_NEXT = range(6)


def _expert_kernel(sched_ref, xs_ref, wgu_hbm, bgu_ref, wd_hbm, bd_ref, y_ref,
                   act_ref, wgu_buf, wd_buf, wgu_b, wd_b, sem):
    w = pl.program_id(0)
    e = sched_ref[S_EXPERT, w]
    slot = sched_ref[S_SLOT, w]
    n_rows = sched_ref[S_ROWS, w]

    def weight_copies(expert, s):
        return (pltpu.make_async_copy(wgu_hbm.at[expert], wgu_buf.at[s], sem.at[0, s]),
                pltpu.make_async_copy(wd_hbm.at[expert], wd_buf.at[s], sem.at[1, s]))

    @pl.when(sched_ref[S_NEW, w] == 1)
    def _():
        @pl.when(w == 0)
        def _():
            for cp in weight_copies(e, slot):
                cp.start()

        nxt = sched_ref[S_NEXT, w]

        @pl.when(nxt >= 0)
        def _():
            for cp in weight_copies(nxt, 1 - slot):
                cp.start()

        for cp in weight_copies(e, slot):
            cp.wait()
        wgu_b[...] = wgu_buf[slot].astype(BF16)
        wd_b[...] = wd_buf[slot].astype(BF16)

    def tile_path(m):
        half = D_MODEL // 2
        live = lax.broadcasted_iota(jnp.int32, (m, 1), 0) < n_rows
        x_lo, x_hi = _unpack_pair(jnp.where(live, xs_ref[0:m, :], jnp.uint32(0)))
        x = jnp.concatenate([x_lo.astype(BF16), x_hi.astype(BF16)], axis=1)
        for c in range(D_FF // FF_CHUNK):
            gc = slice(c * FF_CHUNK, (c + 1) * FF_CHUNK)
            uc = slice(D_FF + c * FF_CHUNK, D_FF + (c + 1) * FF_CHUNK)
            gate = jnp.dot(x, wgu_b[:, gc], preferred_element_type=F32) + bgu_ref[0, :, gc]
            up = jnp.dot(x, wgu_b[:, uc], preferred_element_type=F32) + bgu_ref[0, :, uc]
            gate = jnp.minimum(gate, SWIGLU_LIMIT)
            up = jnp.clip(up, -SWIGLU_LIMIT, SWIGLU_LIMIT)
            act_ref[0:m, gc] = (gate * jax.nn.sigmoid(SWIGLU_ALPHA * gate) * (up + 1.0)).astype(BF16)
        for c in range(half // FF_CHUNK):
            lo_c = slice(c * FF_CHUNK, (c + 1) * FF_CHUNK)
            hi_c = slice(half + c * FF_CHUNK, half + (c + 1) * FF_CHUNK)
            y_lo = (jnp.dot(act_ref[0:m, :], wd_b[:, lo_c], preferred_element_type=F32)
                    + bd_ref[0, :, lo_c])
            y_hi = (jnp.dot(act_ref[0:m, :], wd_b[:, hi_c], preferred_element_type=F32)
                    + bd_ref[0, :, hi_c])
            y_ref[0:m, lo_c] = _pack_pair(y_lo, y_hi)

    for below, m in zip((0,) + TILE_SPANS[:-1], TILE_SPANS):
        pl.when(jnp.logical_and(n_rows > below, n_rows <= m))(functools.partial(tile_path, m))


def _experts(sched, xs, wgu, bgu, wd, bd):
    rows, half = xs.shape
    n_work = rows // EXPERT_TILE
    grid_spec = pltpu.PrefetchScalarGridSpec(
        num_scalar_prefetch=1,
        grid=(n_work,),
        in_specs=[
            pl.BlockSpec((EXPERT_TILE, half), lambda w, sc: (sc[S_TILE, w], 0)),
            pl.BlockSpec(memory_space=pl.ANY),
            pl.BlockSpec((1, 1, 2 * D_FF), lambda w, sc: (sc[S_EXPERT, w], 0, 0)),
            pl.BlockSpec(memory_space=pl.ANY),
            pl.BlockSpec((1, 1, D_MODEL), lambda w, sc: (sc[S_EXPERT, w], 0, 0)),
        ],
        out_specs=pl.BlockSpec((EXPERT_TILE, half), lambda w, sc: (sc[S_TILE, w], 0)),
        scratch_shapes=[pltpu.VMEM((EXPERT_TILE, D_FF), BF16),
                        pltpu.VMEM((2, D_MODEL, 2 * D_FF), F32),
                        pltpu.VMEM((2, D_FF, D_MODEL), F32),
                        pltpu.VMEM((D_MODEL, 2 * D_FF), BF16),
                        pltpu.VMEM((D_FF, D_MODEL), BF16),
                        pltpu.SemaphoreType.DMA((2, 2))],
    )
    return pl.pallas_call(
        _expert_kernel,
        grid_spec=grid_spec,
        out_shape=jax.ShapeDtypeStruct((rows, half), jnp.uint32),
        compiler_params=pltpu.CompilerParams(dimension_semantics=("arbitrary",),
                                             vmem_limit_bytes=VMEM_LIMIT),
        name="experts",
    )(sched, xs, wgu, bgu, wd, bd)


def _unpack_pair(packed):
    lo = lax.bitcast_convert_type(packed << 16, F32)
    hi = lax.bitcast_convert_type(packed & jnp.uint32(0xFFFF0000), F32)
    return lo, hi


def _pack_pair(lo, hi):
    lo_bits = lax.bitcast_convert_type(lo.astype(BF16).astype(F32), jnp.uint32)
    hi_bits = lax.bitcast_convert_type(hi.astype(BF16).astype(F32), jnp.uint32)
    return (lo_bits >> 16) | (hi_bits & jnp.uint32(0xFFFF0000))


def _combine_kernel(yg_ref, h2_ref, route_ref, o_ref):
    half = D_MODEL // 2
    by_token = jnp.transpose(route_ref[...])
    lo_sum = h2_ref[:, :half]
    hi_sum = h2_ref[:, half:]
    for k in range(TOP_K):
        gate = by_token[:, TOP_K + k:TOP_K + k + 1]
        lo, hi = _unpack_pair(yg_ref[k])
        lo_sum = lo_sum + gate * lo
        hi_sum = hi_sum + gate * hi
    o_ref[:, :half] = lo_sum
    o_ref[:, half:] = hi_sum


def _combine(yg, h2, route, part):
    n = h2.shape[0]
    half = D_MODEL // 2
    steps = yg.shape[1] // ROW_TILE
    row = lambda i: (i + part * steps, 0)
    return pl.pallas_call(
        _combine_kernel,
        grid=(steps,),
        in_specs=[
            pl.BlockSpec((TOP_K, ROW_TILE, half), lambda i: (0, i, 0)),
            pl.BlockSpec((ROW_TILE, D_MODEL), row),
            pl.BlockSpec((2 * TOP_K, ROW_TILE), lambda i: (0, i + part * steps)),
        ],
        out_specs=pl.BlockSpec((ROW_TILE, D_MODEL), row),
        out_shape=jax.ShapeDtypeStruct((n, D_MODEL), F32),
        input_output_aliases={1: 0},
        compiler_params=pltpu.CompilerParams(dimension_semantics=("arbitrary",),
                                             vmem_limit_bytes=VMEM_LIMIT),
        name="combine",
    )(yg, h2, route)


def kernel(x, meta_tokens, norm1_g, w_in, q_norm_g, k_norm_g, attn_sink, dw_w, dw_b, conv_ln_g,
           conv_ln_b, w_out, norm2_g, router_w, router_b, w_gate_up, b_gate_up, w_down, b_down):
    assert norm1_g.shape[0] == 1, "single-layer trunk: meta-token query rows are not materialised"
    b, s, d = x.shape
    n = b * s
    x2d = x.reshape(n, d)

    scale = HEAD_DIM ** -0.5 * LOG2_E
    qg = (jnp.tile(q_norm_g[0], N_Q_HEADS) * scale).reshape(1, ATTN_W)
    kg = jnp.tile(k_norm_g[0], N_KV_HEADS).reshape(1, KV_W)
    head_of = np.arange(ATTN_W) // HEAD_DIM
    pq = jnp.asarray((head_of[:, None] == head_of[None, :]) / HEAD_DIM, BF16)
    pk = pq[:KV_W, :KV_W]
    g1 = norm1_g[0].reshape(1, d)

    q, kd, vd, hc = _inproj(x2d, g1, w_in[0], qg, kg, pq, pk, ROW_TILE)
    meta_rows = jnp.pad(meta_tokens, ((0, BLOCK - N_META), (0, 0)))
    _, kmeta, vmeta, hc_meta = _inproj(meta_rows, g1, w_in[0], qg, kg, pq, pk, BLOCK)

    attn = _attention(attn_sink[0] * LOG2_E, q.reshape(b, s, ATTN_W), kd.reshape(b, s, 2 * KV_W),
                      vd.reshape(b, s, 2 * KV_W), kmeta, vmeta, jnp.asarray(_alibi_bias()))

    rw = router_w[0].T
    rwh = rw.astype(BF16)
    rwl = (rw - rwh.astype(F32)).astype(BF16)
    rb = router_b[0].reshape(N_EXPERTS, 1)
    tri = jnp.asarray(np.triu(np.ones((ROW_TILE, ROW_TILE), np.float32), 1), BF16)
    h2, xp, route, counts = _outproj(attn.reshape(n, ATTN_W), hc.reshape(b, s, CONV_W), hc_meta,
                                     dw_w[0].reshape(CONV_K, CONV_W), dw_b[0].reshape(1, CONV_W),
                                     conv_ln_g[0].reshape(1, CONV_W), conv_ln_b[0].reshape(1, CONV_W), x2d,
                                     w_out[0], w_out[0], norm2_g[0].reshape(1, d),
                                     rwh, rwl, rb, tri)

    pos, sched = _plan(route, counts)
    n_work = n * TOP_K // EXPERT_TILE + N_EXPERTS
    pos3 = pos.reshape(TOP_K, n // SC_CHUNK, SC_CHUNK).transpose(1, 0, 2)

    xs = _sc_dispatch(pos3, xp, n_work * EXPERT_TILE)
    y = _experts(sched, xs, w_gate_up[0],
                 b_gate_up[0].reshape(N_EXPERTS, 1, 2 * D_FF), w_down[0],
                 b_down[0].reshape(N_EXPERTS, 1, D_MODEL))
    out = h2
    pos3c = pos.reshape(TOP_K, n // COLLECT_CHUNK, COLLECT_CHUNK).transpose(1, 0, 2)
    chunks = pos3c.shape[0] // COMBINE_PARTS
    for part in range(COMBINE_PARTS):
        out = _combine(_sc_collect(pos3c[part * chunks:(part + 1) * chunks], y), out, route, part)
    return out.reshape(b, s, d)
```
